```python
import math, functools
import jax, jax.numpy as jnp
from jax import lax
import numpy as np

D_MODEL = 4096
BATCH = 4
SEQ = 2048
DEPTH = 2
DEC_BATCH = 8
DEC_SEQ = 4
PAST_LEN = 16384
PAGE_SIZE = 128

HEAD_DIM = 128
A_CHUNK = 128
A_GROUPS = 8
A_WIDTH = A_GROUPS * HEAD_DIM
B_HEADS = 12
B_WIDTH = B_HEADS * HEAD_DIM
CONV_W = 4
DN_CHUNK = 64
C_HEADS = 12
C_KV_HEADS = 4
C_Q = C_HEADS * HEAD_DIM
C_KV = C_KV_HEADS * HEAD_DIM
IDX_HEADS = 16
IDX_DIM = 128
TOPK_MAX = 256
Q_BLOCK = 128
N_BRANCH = 3
MIX_WIDTH = A_WIDTH + B_WIDTH + C_Q
D_FF = -(-8 * D_MODEL // (3 * 256)) * 256
IN_SPLITS = (A_WIDTH, A_WIDTH, 3 * B_WIDTH, B_WIDTH, B_HEADS, B_HEADS,
             C_Q, C_KV, C_KV, IDX_HEADS * IDX_DIM, IDX_HEADS, IDX_DIM, N_BRANCH * D_MODEL)
IN_WIDTH = sum(IN_SPLITS)
RMS_EPS = 1e-6
LN_EPS = 1e-5
F32 = jnp.float32

kernel_name = 'hybrid_gmlp_gdn_dsa_step'


def rmsnorm(x, g):
    xf = x.astype(F32)
    y = xf * lax.rsqrt(jnp.mean(xf * xf, -1, keepdims=True) + RMS_EPS)
    return (y * g.astype(F32)).astype(x.dtype)


def layernorm(x, g, b):
    xf = x.astype(F32)
    mu = jnp.mean(xf, -1, keepdims=True)
    var = jnp.mean(jnp.square(xf - mu), -1, keepdims=True)
    return ((xf - mu) * lax.rsqrt(var + LN_EPS) * g.astype(F32) + b.astype(F32)).astype(x.dtype)


def l2norm(x):
    return x * lax.rsqrt(jnp.sum(x * x, -1, keepdims=True) + 1e-6)


def split_cols(h):
    out, start = [], 0
    for w in IN_SPLITS:
        out.append(h[..., start:start + w])
        start += w
    return out


def causal_dwconv(x_ext, w):
    T = x_ext.shape[1] - (CONV_W - 1)
    out = x_ext[:, 0:T] * w[0]
    for j in range(1, CONV_W):
        out = out + x_ext[:, j:j + T] * w[j]
    return out


def chunk_mlp(u, v, w_s, b_s):
    Bt, T, _ = u.shape
    C = min(T, A_CHUNK)
    n = T // C
    vr = v.reshape(Bt, n, C, A_GROUPS, HEAD_DIM)
    ws = jnp.tril(w_s[:, :C, :C])
    s = jnp.einsum('gts,bnsgc->bntgc', ws, vr) + b_s[:, :C].T[None, None, :, :, None]
    return u * s.reshape(Bt, T, A_WIDTH)


def gated_delta_chunked(q, k, v, g, beta, S0, chunk):
    Bt, T, H, D = q.shape
    n = T // chunk

    def blk(x):
        x = x.reshape((Bt, n, chunk, H) + x.shape[3:])
        return jnp.moveaxis(x, (1, 3), (0, 2))

    q, k, v, g, beta = blk(q), blk(k), blk(v), blk(g), blk(beta)
    gc = jnp.cumsum(g, -1)
    pos = jnp.arange(chunk)
    incl = pos[:, None] >= pos[None, :]
    strict = pos[:, None] > pos[None, :]
    decay = jnp.exp(jnp.where(incl, gc[..., :, None] - gc[..., None, :], -jnp.inf))
    kb = k * beta[..., None]
    lmat = jnp.where(strict, jnp.einsum('nbhid,nbhjd->nbhij', kb, k) * decay, 0.0)
    eye = jnp.eye(chunk, dtype=F32)
    rhs = jnp.concatenate([v * beta[..., None], kb * jnp.exp(gc)[..., None]], -1)
    sol = lax.linalg.triangular_solve(lmat + eye, rhs, left_side=True, lower=True, unit_diagonal=True)
    value, kcd = sol[..., :D], sol[..., D:]
    attn = jnp.where(incl, jnp.einsum('nbhid,nbhjd->nbhij', q, k) * decay, 0.0)
    qg = q * jnp.exp(gc)[..., None]
    kend = k * jnp.exp(gc[..., -1:] - gc)[..., None]
    gend = jnp.exp(gc[..., -1])

    def step(S, xs):
        val_c, kcd_c, attn_c, qg_c, kend_c, gend_c = xs
        vn = val_c - jnp.einsum('bhcd,bhde->bhce', kcd_c, S)
        o = jnp.einsum('bhcd,bhde->bhce', qg_c, S) + jnp.einsum('bhij,bhje->bhie', attn_c, vn)
        S = S * gend_c[..., None, None] + jnp.einsum('bhcd,bhce->bhde', kend_c, vn)
        return S, o

    S, o = lax.scan(step, S0, (value, kcd, attn, qg, kend, gend))
    o = jnp.moveaxis(o, (0, 2), (1, 3)).reshape(Bt, T, H, D)
    return o, S


def gated_delta_branch(qkv, z, a, b, a_log, dt_bias, o_g, S0, chunk):
    Bt, T, _ = qkv.shape
    q, k, v = [t.reshape(Bt, T, B_HEADS, HEAD_DIM).astype(F32) for t in jnp.split(qkv, 3, -1)]
    q = l2norm(q) * (HEAD_DIM ** -0.5)
    k = l2norm(k)
    beta = jax.nn.sigmoid(b.astype(F32))
    g = -jnp.exp(a_log.astype(F32)) * jax.nn.softplus(a.astype(F32) + dt_bias.astype(F32))
    o, S = gated_delta_chunked(q, k, v, g, beta, S0.astype(F32), chunk)
    o = rmsnorm(o, o_g) * jax.nn.silu(z.reshape(Bt, T, B_HEADS, HEAD_DIM).astype(F32))
    return o.reshape(Bt, T, B_WIDTH).astype(qkv.dtype), S.astype(S0.dtype)


def indexer_scores(iq, iw, ik, mask):
    sc = jax.nn.relu(jnp.einsum('bthd,bsd->bths', iq.astype(F32), ik))
    w = iw.astype(F32) * (IDX_HEADS ** -0.5 * IDX_DIM ** -0.5)
    return jnp.where(mask, jnp.einsum('bth,bths->bts', w, sc), -jnp.inf)


def sparse_attend(q, ks, vs, valid):
    Bt, T, H, D = q.shape
    qg = q.reshape(Bt, T, C_KV_HEADS, H // C_KV_HEADS, D).astype(F32)
    s = jnp.einsum('btkgd,btskd->btkgs', qg, ks.astype(F32)) * (D ** -0.5)
    s = jnp.where(valid[:, :, None, None, :], s, -jnp.inf)
    p = jax.nn.softmax(s, -1)
    o = jnp.einsum('btkgs,btskd->btkgd', p, vs.astype(F32))
    return o.reshape(Bt, T, H * D).astype(q.dtype)


def dsa_prompt(q, k, v, iq, iw, ik):
    Bt, S, H, D = q.shape
    nblk = S // Q_BLOCK
    k_sel = min(TOPK_MAX, S // 4)
    key_pos = jnp.arange(S)
    bidx = jnp.arange(Bt)[:, None, None]
    ikf = ik.astype(F32)

    def to_blocks(a):
        return jnp.moveaxis(a.reshape((Bt, nblk, Q_BLOCK) + a.shape[2:]), 1, 0)

    def one_block(args):
        qb, iqb, iwb, start = args
        qpos = start + jnp.arange(Q_BLOCK)
        mask = (key_pos[None, :] <= qpos[:, None])[None]
        top_val, top_idx = lax.top_k(indexer_scores(iqb, iwb, ikf, mask), k_sel)
        return sparse_attend(qb, k[bidx, top_idx], v[bidx, top_idx], jnp.isfinite(top_val))

    out = lax.map(one_block, (to_blocks(q), to_blocks(iq), to_blocks(iw), jnp.arange(nblk) * Q_BLOCK))
    return jnp.moveaxis(out, 0, 1).reshape(Bt, S, H * D)


def dsa_sample(q, k, v, iq, iw, ik, k_pool, v_pool, kidx_pool, page_table):
    Bt, T, H, D = q.shape
    past = page_table.shape[1] * PAGE_SIZE
    L = past + T
    k_sel = min(TOPK_MAX, L // 4)
    ik_past = kidx_pool[page_table].reshape(Bt, past, IDX_DIM)
    ik_all = jnp.concatenate([ik_past.astype(F32), ik.astype(F32)], 1)
    mask = (jnp.arange(L)[None, :] <= past + jnp.arange(T)[:, None])[None]
    top_val, top_idx = lax.top_k(indexer_scores(iq, iw, ik_all, mask), k_sel)
    bidx = jnp.arange(Bt)[:, None, None]
    is_new = (top_idx >= past)[..., None, None]
    pidx = jnp.minimum(top_idx, past - 1)
    phys = page_table[bidx, pidx // PAGE_SIZE]
    off = pidx % PAGE_SIZE
    nidx = jnp.clip(top_idx - past, 0, T - 1)
    ks = jnp.where(is_new, k[bidx, nidx], k_pool[phys, off])
    vs = jnp.where(is_new, v[bidx, nidx], v_pool[phys, off])
    return sparse_attend(q, ks, vs, jnp.isfinite(top_val))


def trunk_layer(x, p, conv_prev, S0, dn_chunk, attend_c):
    (ln1, w_in, a_ln_g, a_ln_b, a_ws, a_bs, b_conv_w, b_a_log, b_dt_bias, b_out_g,
     w_br, w_o, ln2, w1, w3, w2) = p
    Bt, T, _ = x.shape
    xn = rmsnorm(x, ln1)
    (a_u, a_v, b_qkv, b_z, b_a, b_b, c_q, c_k, c_v, c_iq, c_iw, c_ik, gate_pre) = split_cols(xn @ w_in)
    a_v = layernorm(jax.nn.gelu(a_v), a_ln_g, a_ln_b)
    y_a = chunk_mlp(jax.nn.gelu(a_u), a_v, a_ws, a_bs)
    conv_ext = jnp.concatenate([conv_prev.astype(b_qkv.dtype), b_qkv], 1)
    new_conv = conv_ext[:, -(CONV_W - 1):]
    y_b, S_new = gated_delta_branch(jax.nn.silu(causal_dwconv(conv_ext, b_conv_w)), b_z, b_a, b_b,
                                    b_a_log, b_dt_bias, b_out_g, S0, dn_chunk)
    c_k = c_k.reshape(Bt, T, C_KV_HEADS, HEAD_DIM)
    c_v = c_v.reshape(Bt, T, C_KV_HEADS, HEAD_DIM)
    y_c = attend_c(c_q.reshape(Bt, T, C_HEADS, HEAD_DIM), c_k, c_v,
                   c_iq.reshape(Bt, T, IDX_HEADS, IDX_DIM), c_iw, c_ik)
    gates = jax.nn.sigmoid(gate_pre.astype(F32)).astype(x.dtype).reshape(Bt, T, N_BRANCH, D_MODEL)
    m = (gates[:, :, 0] * (y_a @ w_br[:A_WIDTH])
         + gates[:, :, 1] * (y_b @ w_br[A_WIDTH:A_WIDTH + B_WIDTH])
         + gates[:, :, 2] * (y_c @ w_br[A_WIDTH + B_WIDTH:]))
    x = x + m @ w_o
    hn = rmsnorm(x, ln2)
    x = x + (jax.nn.silu(hn @ w1) * (hn @ w3)) @ w2
    return x, a_v, new_conv, S_new, c_k, c_v, c_ik


def setup_inputs(seed: int = 0) -> dict:
    key = jax.random.key(seed)
    k = jax.random.split(key, 25)

    def nrm(kk, shape, scale):
        return scale * jax.random.normal(kk, shape, F32)

    n_pages = PAST_LEN // PAGE_SIZE
    n_phys = (DEC_BATCH * n_pages * 5) // 4
    page_table = jax.random.permutation(k[7], n_phys)[:DEC_BATCH * n_pages].reshape(DEC_BATCH, n_pages).astype(jnp.int32)
    dt = jnp.exp(jax.random.uniform(k[16], (DEPTH, B_HEADS), F32, math.log(1e-3), math.log(1e-1)))
    return {
        'x_prompt': nrm(k[0], (BATCH, SEQ, D_MODEL), 1.0),
        'x_sample': nrm(k[1], (DEC_BATCH, DEC_SEQ, D_MODEL), 1.0),
        'cache_k': nrm(k[2], (DEPTH, n_phys, PAGE_SIZE, C_KV_HEADS, HEAD_DIM), 1.0),
        'cache_v': nrm(k[3], (DEPTH, n_phys, PAGE_SIZE, C_KV_HEADS, HEAD_DIM), 1.0),
        'cache_kidx': nrm(k[4], (DEPTH, n_phys, PAGE_SIZE, IDX_DIM), 1.0),
        'state_conv': nrm(k[5], (DEPTH, DEC_BATCH, CONV_W - 1, 3 * B_WIDTH), 1.0),
        'state_delta': nrm(k[6], (DEPTH, DEC_BATCH, B_HEADS, HEAD_DIM, HEAD_DIM), 0.05),
        'page_table': page_table,
        'ln1': 1.0 + nrm(k[8], (DEPTH, D_MODEL), 0.02),
        'w_in': nrm(k[9], (DEPTH, D_MODEL, IN_WIDTH), D_MODEL ** -0.5),
        'a_ln_g': 1.0 + nrm(k[10], (DEPTH, A_WIDTH), 0.02),
        'a_ln_b': nrm(k[11], (DEPTH, A_WIDTH), 0.02),
        'a_ws': nrm(k[12], (DEPTH, A_GROUPS, A_CHUNK, A_CHUNK), 0.5 * A_CHUNK ** -0.5),
        'a_bs': 1.0 + nrm(k[13], (DEPTH, A_GROUPS, A_CHUNK), 0.1),
        'b_conv_w': nrm(k[14], (DEPTH, CONV_W, 3 * B_WIDTH), CONV_W ** -0.5),
        'b_a_log': jnp.log(jax.random.uniform(k[15], (DEPTH, B_HEADS), F32, 1.0, 16.0)),
        'b_dt_bias': dt + jnp.log(-jnp.expm1(-dt)),
        'b_out_g': 1.0 + nrm(k[17], (DEPTH, HEAD_DIM), 0.02),
        'w_br': nrm(k[18], (DEPTH, MIX_WIDTH, D_MODEL), (MIX_WIDTH // N_BRANCH) ** -0.5),
        'w_o': nrm(k[19], (DEPTH, D_MODEL, D_MODEL), D_MODEL ** -0.5),
        'ln2': 1.0 + nrm(k[20], (DEPTH, D_MODEL), 0.02),
        'ffn_w1': nrm(k[21], (DEPTH, D_MODEL, D_FF), D_MODEL ** -0.5),
        'ffn_w3': nrm(k[22], (DEPTH, D_MODEL, D_FF), D_MODEL ** -0.5),
        'ffn_w2': nrm(k[23], (DEPTH, D_FF, D_MODEL), D_FF ** -0.5),
        'ln_f': 1.0 + nrm(k[24], (D_MODEL,), 0.02),
    }


def reference(x_prompt, x_sample, cache_k, cache_v, cache_kidx, state_conv, state_delta, page_table,
              ln1, w_in, a_ln_g, a_ln_b, a_ws, a_bs, b_conv_w, b_a_log, b_dt_bias, b_out_g,
              w_br, w_o, ln2, ffn_w1, ffn_w3, ffn_w2, ln_f):
    xp, xs = x_prompt, x_sample
    Bp = xp.shape[0]
    pk, pv, pik, pconv, pdelta = [], [], [], [], []
    sk, sv, sik, sconv, sdelta, schunk = [], [], [], [], [], []
    for l in range(DEPTH):
        p = (ln1[l], w_in[l], a_ln_g[l], a_ln_b[l], a_ws[l], a_bs[l], b_conv_w[l], b_a_log[l],
             b_dt_bias[l], b_out_g[l], w_br[l], w_o[l], ln2[l], ffn_w1[l], ffn_w3[l], ffn_w2[l])
        conv0 = jnp.zeros((Bp, CONV_W - 1, 3 * B_WIDTH), xp.dtype)
        S0 = jnp.zeros((Bp, B_HEADS, HEAD_DIM, HEAD_DIM), F32)
        xp, _, c_new, S_new, k_new, v_new, ik_new = trunk_layer(
            xp, p, conv0, S0, min(DN_CHUNK, xp.shape[1]), dsa_prompt)
        pk.append(k_new); pv.append(v_new); pik.append(ik_new); pconv.append(c_new); pdelta.append(S_new)
        attend_s = functools.partial(dsa_sample, k_pool=cache_k[l], v_pool=cache_v[l],
                                     kidx_pool=cache_kidx[l], page_table=page_table)
        xs, av_new, c_new, S_new, k_new, v_new, ik_new = trunk_layer(
            xs, p, state_conv[l], state_delta[l], xs.shape[1], attend_s)
        sk.append(k_new); sv.append(v_new); sik.append(ik_new); sconv.append(c_new)
        sdelta.append(S_new); schunk.append(av_new)
    y_prompt = rmsnorm(xp, ln_f)
    y_sample = rmsnorm(xs, ln_f)
    return (y_prompt, y_sample,
            jnp.stack(pk), jnp.stack(pv), jnp.stack(pik), jnp.stack(pconv), jnp.stack(pdelta),
            jnp.stack(sk), jnp.stack(sv), jnp.stack(sik), jnp.stack(sconv), jnp.stack(sdelta),
            jnp.stack(schunk))
```

```python
import functools

import jax
import jax.numpy as jnp
from jax import lax
from jax.experimental import pallas as pl
from jax.experimental.pallas import tpu as pltpu

F32 = jnp.float32
BF16 = jnp.bfloat16
I32 = jnp.int32

HEAD_DIM = 128
A_GROUPS = 8
A_CHUNK = 128
A_WIDTH = A_GROUPS * HEAD_DIM
B_HEADS = 12
B_WIDTH = B_HEADS * HEAD_DIM
CONV_W = 4
C_HEADS = 12
C_KV_HEADS = 4
C_GROUP = C_HEADS // C_KV_HEADS
C_Q = C_HEADS * HEAD_DIM
C_KV = C_KV_HEADS * HEAD_DIM
IDX_HEADS = 16
IDX_DIM = 128
TOPK_MAX = 256
PAGE_SIZE = 128
N_BRANCH = 3
RMS_EPS = 1e-6
LN_EPS = 1e-5

LANE = 128
SUBLANE = 8
V7X_VMEM_BYTES = 64 * 1024 * 1024
VMEM_BUDGET = 56 * 1024 * 1024
COMPILER_SCRATCH_BYTES = 8 * 1024 * 1024

GDN_CHUNK = 128
GDN_INV_PASSES = 1
PAGES_PER_STEP = 8
MASK_NEG = -1e30
INT_MIN = -2147483648
KEY_OF_NEG_INF = -2139095041


def _cparams(semantics, vmem_bytes):
    return pltpu.CompilerParams(dimension_semantics=semantics,
                                vmem_limit_bytes=int(min(max(vmem_bytes, 16 * 1024 * 1024), VMEM_BUDGET)))


def _pick(dim, pref):
    t = pref
    while t >= SUBLANE:
        if dim % t == 0:
            return t
        t //= 2
    return dim


def _layout(d_model):
    segs = [("gate", N_BRANCH * d_model, d_model), ("bq", B_WIDTH, B_WIDTH), ("bk", B_WIDTH, B_WIDTH),
            ("bv", B_WIDTH, B_WIDTH), ("bz", B_WIDTH, B_WIDTH), ("cq", C_Q, C_Q), ("ck", C_KV, C_KV),
            ("au", A_WIDTH, A_WIDTH), ("av", A_WIDTH, A_WIDTH), ("cv", C_KV, C_KV),
            ("iq", IDX_HEADS * IDX_DIM, 4 * IDX_DIM), ("ik", IDX_DIM, IDX_DIM), ("small", LANE, LANE)]
    off, lay = 0, {}
    for name, width, align in segs:
        assert off % align == 0, (name, off, align)
        lay[name] = off
        off += width
    lay["total"] = off
    return lay


def _pack_w_in(w_in, d_model):
    widths = (A_WIDTH, A_WIDTH, 3 * B_WIDTH, B_WIDTH, B_HEADS, B_HEADS, C_Q, C_KV, C_KV,
              IDX_HEADS * IDX_DIM, IDX_HEADS, IDX_DIM, N_BRANCH * d_model)
    names = ("au", "av", "bqkv", "bz", "ba", "bb", "cq", "ck", "cv", "iq", "iw", "ik", "gate")
    src, start = {}, 0
    for n, w in zip(names, widths):
        src[n] = w_in[:, start:start + w].astype(BF16)
        start += w
    assert start == w_in.shape[1]
    pad = jnp.zeros((w_in.shape[0], LANE - 2 * B_HEADS - IDX_HEADS), BF16)
    return jnp.concatenate([src["gate"], src["bqkv"], src["bz"], src["cq"], src["ck"], src["au"], src["av"],
                            src["cv"], src["iq"], src["ik"], src["ba"], src["bb"], src["iw"], pad], axis=1)


def _rmsnorm_body(x_ref, g_ref, o_ref):
    x = x_ref[...]
    ms = jnp.mean(x * x, axis=-1, keepdims=True)
    o_ref[...] = (x * lax.rsqrt(ms + RMS_EPS) * g_ref[...]).astype(o_ref.dtype)


def _rmsnorm(x, g, out_dtype):
    m, d = x.shape
    tm = _pick(m, 256)
    return pl.pallas_call(
        _rmsnorm_body,
        out_shape=jax.ShapeDtypeStruct((m, d), out_dtype),
        grid=(m // tm,),
        in_specs=[pl.BlockSpec((tm, d), lambda i: (i, 0)), pl.BlockSpec((1, d), lambda i: (0, 0))],
        out_specs=pl.BlockSpec((tm, d), lambda i: (i, 0)),
        compiler_params=_cparams(("parallel",), 6 * tm * d * 4),
        name="rmsnorm",
    )(x, g.reshape(1, d))


def _mm_body(a_ref, w_ref, o_ref):
    o_ref[...] = jnp.dot(a_ref[...], w_ref[...], preferred_element_type=F32).astype(o_ref.dtype)


def _mm_res_body(a_ref, w_ref, r_ref, o_ref):
    o_ref[...] = (r_ref[...] + jnp.dot(a_ref[...], w_ref[...], preferred_element_type=F32)).astype(o_ref.dtype)


def _matmul(a, w, *, residual=None, out_dtype=F32, tm_pref=1024, tn_pref=1024, name="matmul"):
    m, k = a.shape
    n = w.shape[1]
    tm = _pick(m, tm_pref)
    tn = tn_pref if n % tn_pref == 0 else _pick(n, tn_pref)
    osz = jnp.dtype(out_dtype).itemsize
    vmem = 2 * (tm * k * 2 + k * tn * 2 + tm * tn * osz) + tm * tn * 4
    in_specs = [pl.BlockSpec((tm, k), lambda i, j: (i, 0)), pl.BlockSpec((k, tn), lambda i, j: (0, j))]
    args = [a, w]
    body = _mm_body
    if residual is not None:
        in_specs.append(pl.BlockSpec((tm, tn), lambda i, j: (i, j)))
        args.append(residual)
        body = _mm_res_body
        vmem += 2 * tm * tn * 4
    return pl.pallas_call(
        body,
        out_shape=jax.ShapeDtypeStruct((m, n), out_dtype),
        grid=(m // tm, n // tn),
        in_specs=in_specs,
        out_specs=pl.BlockSpec((tm, tn), lambda i, j: (i, j)),
        compiler_params=_cparams(("parallel", "arbitrary"), vmem + COMPILER_SCRATCH_BYTES),
        name=name,
    )(*args)


def _branch_a_body(u_ref, v_ref, w_ref, bs_ref, g_ref, b_ref, y_ref, *av_ref, chunk):
    tb = u_ref.shape[0]
    u = jax.nn.gelu(u_ref[...])
    v = jax.nn.gelu(v_ref[...])
    mu = jnp.mean(v, axis=-1, keepdims=True)
    var = jnp.mean(jnp.square(v - mu), axis=-1, keepdims=True)
    vn = (v - mu) * lax.rsqrt(var + LN_EPS) * g_ref[...] + b_ref[...]
    if av_ref:
        av_ref[0][...] = vn
    row = lax.broadcasted_iota(I32, (tb, tb), 0)
    col = lax.broadcasted_iota(I32, (tb, tb), 1)
    keep = (col <= row) & ((row // chunk) == (col // chunk))
    vb = vn.astype(BF16)
    for g in range(A_GROUPS):
        sl = slice(g * HEAD_DIM, (g + 1) * HEAD_DIM)
        wg = jnp.where(keep, w_ref[g], 0.0).astype(BF16)
        s = jnp.dot(wg, vb[:, sl], preferred_element_type=F32) + bs_ref[:, g:g + 1]
        y_ref[:, sl] = (u[:, sl] * s).astype(y_ref.dtype)


def _branch_a(h, lay, a_ws, a_bs, ln_g, ln_b, *, chunk, tb, emit_av):
    m = h.shape[0]
    reps = tb // chunk
    wfull = jnp.tile(a_ws[:, :chunk, :chunk], (1, reps, reps))
    bs_t = jnp.tile(a_bs[:, :chunk].T, (reps, 1))
    cu, cv = lay["au"] // A_WIDTH, lay["av"] // A_WIDTH
    out_shape = [jax.ShapeDtypeStruct((m, A_WIDTH), BF16)]
    out_specs = [pl.BlockSpec((tb, A_WIDTH), lambda i: (i, 0))]
    if emit_av:
        out_shape.append(jax.ShapeDtypeStruct((m, A_WIDTH), F32))
        out_specs.append(pl.BlockSpec((tb, A_WIDTH), lambda i: (i, 0)))
    res = pl.pallas_call(
        functools.partial(_branch_a_body, chunk=chunk),
        out_shape=out_shape,
        grid=(m // tb,),
        in_specs=[pl.BlockSpec((tb, A_WIDTH), lambda i: (i, cu)),
                  pl.BlockSpec((tb, A_WIDTH), lambda i: (i, cv)),
                  pl.BlockSpec((A_GROUPS, tb, tb), lambda i: (0, 0, 0)),
                  pl.BlockSpec((tb, A_GROUPS), lambda i: (0, 0)),
                  pl.BlockSpec((1, A_WIDTH), lambda i: (0, 0)),
                  pl.BlockSpec((1, A_WIDTH), lambda i: (0, 0))],
        out_specs=out_specs,
        compiler_params=_cparams(("parallel",), 32 << 20),
        name="branch_a_gmlp",
    )(h, h, wfull, bs_t, ln_g.reshape(1, A_WIDTH), ln_b.reshape(1, A_WIDTH))
    return res if emit_av else res[0]


def _dot(a, b):
    return jnp.dot(a.astype(BF16), b.astype(BF16), preferred_element_type=F32)


def _dot_nt(a, b):
    return lax.dot_general(a.astype(BF16), b.astype(BF16), (((1,), (1,)), ((), ())), preferred_element_type=F32)


def _split_bf16(x):
    hi = x.astype(BF16)
    return hi, (x - hi.astype(F32)).astype(BF16)


def _dot_inv(a, b):
    if GDN_INV_PASSES == 1:
        return _dot(a, b)
    ah, al = _split_bf16(a)
    bh, bl = _split_bf16(b)
    mm = lambda x, y: jnp.dot(x, y, preferred_element_type=F32)
    return mm(ah, bh) + (mm(ah, bl) + mm(al, bh))


def _transpose_rows(x):
    r = x.shape[0]
    if r < LANE:
        x = jnp.concatenate([x, jnp.zeros((LANE - r, LANE), x.dtype)], axis=0)
    return x.T[:, :r]


def _gdn_body(q_ref, k_ref, v_ref, z_ref, sm_ref, c0_ref, s0_ref, cw_ref, al_ref, dt_ref, og_ref,
              y_ref, sout_ref, ext_ref, s_ref, *, chunk, t_valid):
    c = pl.program_id(1)
    nc = pl.num_programs(1)
    hd = HEAD_DIM

    @pl.when(c == 0)
    def _():
        s_ref[...] = s0_ref[0]
        for j in range(3):
            ext_ref[j, 0:SUBLANE, :] = c0_ref[0, :, j * B_WIDTH:(j + 1) * B_WIDTH]

    acts = []
    for j, ref in enumerate((q_ref, k_ref, v_ref)):
        ext_ref[j, SUBLANE:SUBLANE + chunk, :] = ref[...]
        acc = None
        for i in range(CONV_W):
            lo = SUBLANE - (CONV_W - 1) + i
            term = ext_ref[j, lo:lo + chunk, :] * cw_ref[i:i + 1, j * B_WIDTH:(j + 1) * B_WIDTH]
            acc = term if acc is None else acc + term
        acts.append(jax.nn.silu(acc))
        ext_ref[j, 0:SUBLANE, :] = ext_ref[j, chunk:chunk + SUBLANE, :]
    qa, ka, va = acts

    sm = sm_ref[...]
    row1 = lax.broadcasted_iota(I32, (chunk, LANE), 0)
    g_all = -jnp.exp(al_ref[...]) * jax.nn.softplus(sm + dt_ref[...])
    beta_all = jax.nn.sigmoid(sm)
    if t_valid < chunk:
        g_all = jnp.where(row1 < t_valid, g_all, 0.0)
        beta_all = jnp.where(row1 < t_valid, beta_all, 0.0)
    gc_all = g_all
    d = 1
    while d < chunk:
        gc_all = gc_all + jnp.where(row1 >= d, pltpu.roll(gc_all, d, 0), 0.0)
        d *= 2
    gc_t = _transpose_rows(gc_all)

    row = lax.broadcasted_iota(I32, (chunk, chunk), 0)
    col = lax.broadcasted_iota(I32, (chunk, chunk), 1)
    incl = row >= col
    strict = row > col
    eye = jnp.where(row == col, 1.0, 0.0)
    pair_masks = []
    bs = 1
    while bs < chunk:
        pair_masks.append(((row // bs) % 2 == 1) & ((col // bs) == (row // bs) - 1))
        bs *= 2

    for h in range(B_HEADS):
        sl = slice(h * hd, (h + 1) * hd)
        qh, kh, vh = qa[:, sl], ka[:, sl], va[:, sl]
        qn = qh * lax.rsqrt(jnp.sum(qh * qh, axis=-1, keepdims=True) + 1e-6) * (hd ** -0.5)
        kn = kh * lax.rsqrt(jnp.sum(kh * kh, axis=-1, keepdims=True) + 1e-6)
        beta = beta_all[:, B_HEADS + h:B_HEADS + h + 1]
        gcol = gc_all[:, h:h + 1]
        grow = gc_t[h:h + 1, :]
        decay = jnp.exp(jnp.where(incl, gcol - grow, -jnp.inf))
        kb = kn * beta
        eg = jnp.exp(gcol)
        lmat = jnp.where(strict, _dot_nt(kb, kn) * decay, 0.0)
        attn = _dot_nt(qn, kn) * decay
        tinv = eye - jnp.where(pair_masks[0], lmat, 0.0)
        for pm in pair_masks[1:]:
            tinv = tinv - _dot_inv(tinv, _dot_inv(jnp.where(pm, lmat, 0.0), tinv))
        sol = _dot(tinv, jnp.concatenate([vh * beta, kb * eg], axis=-1))
        value, kcd = sol[:, :hd], sol[:, hd:]
        s_old = s_ref[h]
        vnew = value - _dot(kcd, s_old)
        o = _dot(qn * eg, s_old) + _dot(attn, vnew)
        glast = gc_all[chunk - 1:chunk, h:h + 1]
        kend = kn * jnp.exp(glast - gcol)
        s_ref[h] = s_old * jnp.exp(glast) + _dot(kend.T, vnew)
        on = o * lax.rsqrt(jnp.mean(o * o, axis=-1, keepdims=True) + RMS_EPS) * og_ref[...]
        y_ref[:, sl] = (on * jax.nn.silu(z_ref[:, sl])).astype(y_ref.dtype)

    @pl.when(c == nc - 1)
    def _():
        sout_ref[0] = s_ref[...]


def _gdn(srcs, conv0, s0, conv_w, a_log, dt_bias, o_g, *, n_b, t_pad, t_valid, chunk):
    nc = t_pad // chunk
    arrs = [a for a, _ in srcs]
    cbs = [cb for _, cb in srcs]
    widths = [B_WIDTH] * 4 + [LANE]

    def tok_spec(w, cb):
        return pl.BlockSpec((chunk, w), lambda b, c: (b * nc + c, cb))

    pad12 = lambda x: jnp.zeros((1, LANE), F32).at[0, :B_HEADS].set(x)
    return pl.pallas_call(
        functools.partial(_gdn_body, chunk=chunk, t_valid=t_valid),
        out_shape=[jax.ShapeDtypeStruct((n_b * t_pad, B_WIDTH), BF16),
                   jax.ShapeDtypeStruct((n_b, B_HEADS, HEAD_DIM, HEAD_DIM), F32)],
        grid=(n_b, nc),
        in_specs=[tok_spec(w, cb) for w, cb in zip(widths, cbs)] + [
            pl.BlockSpec((1, SUBLANE, 3 * B_WIDTH), lambda b, c: (b, 0, 0)),
            pl.BlockSpec((1, B_HEADS, HEAD_DIM, HEAD_DIM), lambda b, c: (b, 0, 0, 0)),
            pl.BlockSpec((CONV_W, 3 * B_WIDTH), lambda b, c: (0, 0)),
            pl.BlockSpec((1, LANE), lambda b, c: (0, 0)),
            pl.BlockSpec((1, LANE), lambda b, c: (0, 0)),
            pl.BlockSpec((1, HEAD_DIM), lambda b, c: (0, 0))],
        out_specs=[pl.BlockSpec((chunk, B_WIDTH), lambda b, c: (b * nc + c, 0)),
                   pl.BlockSpec((1, B_HEADS, HEAD_DIM, HEAD_DIM), lambda b, c: (b, 0, 0, 0))],
        scratch_shapes=[pltpu.VMEM((3, chunk + SUBLANE, B_WIDTH), F32),
                        pltpu.VMEM((B_HEADS, HEAD_DIM, HEAD_DIM), F32)],
        compiler_params=_cparams(("parallel", "arbitrary"), 40 << 20),
        name="branch_b_gated_delta",
    )(*arrs, conv0, s0, conv_w, pad12(a_log), pad12(dt_bias), o_g.reshape(1, HEAD_DIM))


def _sort_key(x):
    b = lax.bitcast_convert_type(x + 0.0, I32)
    return b ^ ((b >> 31) & 0x7FFFFFFF)


def _kth_largest_key(count_ge, k, rows):
    kf = jnp.float32(k)
    t0 = jnp.where(count_ge(jnp.zeros((rows, 1), I32)) >= kf, 0, INT_MIN).astype(I32)

    def step(i, t):
        cand = t + lax.shift_left(jnp.int32(1), 30 - i)
        return jnp.where(count_ge(cand) >= kf, cand, t)

    return lax.fori_loop(0, 31, step, t0)


def _tie_index(count_tie_le, r, nbits, rows):
    def step(i, j):
        cand = j + lax.shift_left(jnp.int32(1), nbits - 1 - i)
        return jnp.where(count_tie_le(cand - 1) < r, cand, j)

    return lax.fori_loop(0, nbits, step, jnp.zeros((rows, 1), I32))


def _topk_threshold(load_key, load_pos, nblk, k, rows, nbits, j_ref):
    def count(pred):
        def blk(b, acc):
            return acc + jnp.where(pred(load_key(b), load_pos(b)), 1.0, 0.0)
        acc = lax.fori_loop(0, nblk, blk, jnp.zeros((rows, LANE), F32))
        return jnp.sum(acc, axis=1, keepdims=True)

    def count_ge(cand):
        cb = jnp.broadcast_to(cand, (rows, LANE))
        return count(lambda key, pos: key >= cb)

    t = _kth_largest_key(count_ge, k, rows)
    tb = jnp.broadcast_to(t, (rows, LANE))
    n_ge = count_ge(t)
    j_ref[...] = jnp.full((rows, 1), 2 ** 30, I32)
    need = jnp.sum(jnp.where((n_ge > jnp.float32(k)) & (t > KEY_OF_NEG_INF), 1.0, 0.0))

    @pl.when(need > 0.0)
    def _():
        n_gt = count(lambda key, pos: key > tb)
        r = jnp.float32(k) - n_gt

        def count_tie_le(jc):
            jb = jnp.broadcast_to(jc, (rows, LANE))
            return count(lambda key, pos: (key == tb) & (pos <= jb))

        j_ref[...] = jnp.where(n_ge > jnp.float32(k), _tie_index(count_tie_le, r, nbits, rows), 2 ** 30)

    return t


def _selected(key, pos, t, j):
    return (key > t) | ((key == t) & (pos <= j))


def _dsa_prompt_body(cq_ref, iq0_ref, iq1_ref, iq2_ref, iq3_ref, sm_ref, ik_ref, ck_ref, cv_ref, y_ref,
                     key_ref, iqb_ref, qb_ref, m_ref, l_ref, acc_ref, j_ref, *, k_sel, nbits):
    qb = pl.program_id(1)
    nkb = qb + 1
    tq = LANE
    hd = HEAD_DIM
    row = lax.broadcasted_iota(I32, (tq, LANE), 0)
    col = lax.broadcasted_iota(I32, (tq, LANE), 1)
    qpos = qb * tq + row

    for i, ref in enumerate((iq0_ref, iq1_ref, iq2_ref, iq3_ref)):
        iqb_ref[:, i * 4 * IDX_DIM:(i + 1) * 4 * IDX_DIM] = ref[...].astype(BF16)
    w_all = sm_ref[...] * (IDX_HEADS ** -0.5 * IDX_DIM ** -0.5)
    w_off = 2 * B_HEADS

    def score_blk(kb, carry):
        ikb = ik_ref[pl.ds(pl.multiple_of(kb * LANE, LANE), LANE), :].astype(BF16)
        acc = jnp.zeros((tq, LANE), F32)
        for hh in range(IDX_HEADS):
            s = lax.dot_general(iqb_ref[:, hh * IDX_DIM:(hh + 1) * IDX_DIM], ikb, (((1,), (1,)), ((), ())),
                                preferred_element_type=F32)
            acc = acc + w_all[:, w_off + hh:w_off + hh + 1] * jnp.maximum(s, 0.0)
        kpos = kb * LANE + col
        key_ref[kb] = _sort_key(jnp.where(kpos <= qpos, acc, -jnp.inf))
        return carry

    lax.fori_loop(0, nkb, score_blk, 0)

    t = _topk_threshold(lambda b: key_ref[b], lambda b: b * LANE + col, nkb, k_sel, tq, nbits, j_ref)
    tb = jnp.broadcast_to(t, (tq, LANE))
    jb = jnp.broadcast_to(j_ref[...], (tq, LANE))

    for kvh in range(C_KV_HEADS):
        for g in range(C_GROUP):
            hsl = slice((kvh * C_GROUP + g) * hd, (kvh * C_GROUP + g + 1) * hd)
            qb_ref[kvh, g * tq:(g + 1) * tq, :] = cq_ref[:, hsl].astype(BF16)
    m_ref[...] = jnp.full(m_ref.shape, MASK_NEG, F32)
    l_ref[...] = jnp.zeros(l_ref.shape, F32)
    acc_ref[...] = jnp.zeros(acc_ref.shape, F32)

    def attend_blk(kb, carry):
        kpos = kb * LANE + col
        sel = _selected(key_ref[kb], kpos, tb, jb) & (kpos <= qpos)
        bias = jnp.where(sel, 0.0, MASK_NEG)
        bias3 = jnp.concatenate([bias] * C_GROUP, axis=0)
        start = pl.multiple_of(kb * LANE, LANE)
        kblk = ck_ref[pl.ds(start, LANE), :].astype(BF16)
        vblk = cv_ref[pl.ds(start, LANE), :].astype(BF16)
        for kvh in range(C_KV_HEADS):
            sl = slice(kvh * hd, (kvh + 1) * hd)
            s = lax.dot_general(qb_ref[kvh], kblk[:, sl], (((1,), (1,)), ((), ())),
                                preferred_element_type=F32) * (hd ** -0.5) + bias3
            m_old = m_ref[kvh]
            m_new = jnp.maximum(m_old, jnp.max(s, axis=-1, keepdims=True))
            alpha = jnp.exp(m_old - m_new)
            p = jnp.exp(s - m_new)
            l_ref[kvh] = alpha * l_ref[kvh] + jnp.sum(p, axis=-1, keepdims=True)
            acc_ref[kvh] = alpha * acc_ref[kvh] + jnp.dot(p.astype(BF16), vblk[:, sl], preferred_element_type=F32)
            m_ref[kvh] = m_new
        return carry

    lax.fori_loop(0, nkb, attend_blk, 0)

    for kvh in range(C_KV_HEADS):
        o = acc_ref[kvh] / l_ref[kvh]
        for g in range(C_GROUP):
            hsl = slice((kvh * C_GROUP + g) * hd, (kvh * C_GROUP + g + 1) * hd)
            y_ref[:, hsl] = o[g * tq:(g + 1) * tq, :].astype(y_ref.dtype)


def _dsa_prompt(h, lay, *, n_b, seq):
    tq = LANE
    nqb = seq // tq
    k_sel = min(TOPK_MAX, seq // 4)
    nbits = max(1, (seq - 1).bit_length())
    iq_cb = lay["iq"] // (4 * IDX_DIM)

    def q_spec(w, cb):
        return pl.BlockSpec((tq, w), lambda b, q: (b * nqb + q, cb))

    def kv_spec(w, cb):
        return pl.BlockSpec((seq, w), lambda b, q: (b, cb))

    return pl.pallas_call(
        functools.partial(_dsa_prompt_body, k_sel=k_sel, nbits=nbits),
        out_shape=jax.ShapeDtypeStruct((n_b * seq, C_Q), BF16),
        grid=(n_b, nqb),
        in_specs=[q_spec(C_Q, lay["cq"] // C_Q)]
                 + [q_spec(4 * IDX_DIM, iq_cb + i) for i in range(4)]
                 + [q_spec(LANE, lay["small"] // LANE),
                    kv_spec(IDX_DIM, lay["ik"] // IDX_DIM),
                    kv_spec(C_KV, lay["ck"] // C_KV),
                    kv_spec(C_KV, lay["cv"] // C_KV)],
        out_specs=pl.BlockSpec((tq, C_Q), lambda b, q: (b * nqb + q, 0)),
        scratch_shapes=[pltpu.VMEM((nqb, tq, LANE), I32),
                        pltpu.VMEM((tq, IDX_HEADS * IDX_DIM), BF16),
                        pltpu.VMEM((C_KV_HEADS, C_GROUP * tq, HEAD_DIM), BF16),
                        pltpu.VMEM((C_KV_HEADS, C_GROUP * tq, 1), F32),
                        pltpu.VMEM((C_KV_HEADS, C_GROUP * tq, 1), F32),
                        pltpu.VMEM((C_KV_HEADS, C_GROUP * tq, HEAD_DIM), F32),
                        pltpu.VMEM((tq, 1), I32)],
        compiler_params=_cparams(("parallel", "arbitrary"), 4 * seq * (2 * C_KV + IDX_DIM) * 2 + (16 << 20)),
        name="branch_c_prompt_dsa",
    )(h, h, h, h, h, h, h, h, h)


def _dsa_sample_select_body(pt_ref, *refs, n_pages, k_sel, nbits, t_new):
    pp = PAGES_PER_STEP
    page_refs = refs[:pp]
    iq_ref, w_ref, ikn_ref, bias_ref, key_ref, j_ref = refs[pp:]
    j = pl.program_id(1)
    n_steps = n_pages // pp
    rows = t_new
    col = lax.broadcasted_iota(I32, (rows, LANE), 1)
    trow = lax.broadcasted_iota(I32, (rows, LANE), 0)
    iq = iq_ref[0].astype(BF16)

    def scores(keys_f32):
        s = lax.dot_general(iq, keys_f32.astype(BF16), (((1,), (1,)), ((), ())), preferred_element_type=F32)
        r = jnp.maximum(s, 0.0) * w_ref[0]
        return jnp.sum(r.reshape(rows, IDX_HEADS, LANE), axis=1)

    @pl.when(j < n_steps)
    def _():
        for i in range(pp):
            key_ref[j * pp + i] = _sort_key(scores(page_refs[i][0]))

    @pl.when(j == n_steps)
    def _():
        new_ok = (col <= trow) & (col < t_new)
        key_ref[n_pages] = _sort_key(jnp.where(new_ok, scores(ikn_ref[0]), -jnp.inf))
        nblk = n_pages + 1
        t = _topk_threshold(lambda b: key_ref[b], lambda b: b * LANE + col, nblk, k_sel, rows, nbits, j_ref)
        tb = jnp.broadcast_to(t, (rows, LANE))
        jb = jnp.broadcast_to(j_ref[...], (rows, LANE))

        def write_blk(b, carry):
            kpos = b * LANE + col
            sel = _selected(key_ref[b], kpos, tb, jb) & ((kpos < n_pages * PAGE_SIZE) | new_ok)
            bias_ref[0, b] = jnp.where(sel, 0.0, MASK_NEG)
            return carry

        lax.fori_loop(0, nblk, write_blk, 0)


def _page_index_map(i, n_pages, n_steps):
    def index_map(b, j, pt):
        step = jnp.minimum(j, n_steps - 1)
        return (pt[b * n_pages + step * PAGES_PER_STEP + i], 0, 0)
    return index_map


def _dsa_sample_select(pt_flat, kidx_pool, iq_s, iw_s, ik_new, *, n_b, n_pages, t_new):
    pp = PAGES_PER_STEP
    assert n_pages % pp == 0
    n_steps = n_pages // pp
    total = n_pages * PAGE_SIZE + t_new
    k_sel = min(TOPK_MAX, total // 4)
    nbits = max(1, ((n_pages + 1) * PAGE_SIZE - 1).bit_length())
    rows_q = t_new * IDX_HEADS
    grid_spec = pltpu.PrefetchScalarGridSpec(
        num_scalar_prefetch=1,
        grid=(n_b, n_steps + 1),
        in_specs=[pl.BlockSpec((1, PAGE_SIZE, IDX_DIM), _page_index_map(i, n_pages, n_steps)) for i in range(pp)]
                 + [pl.BlockSpec((1, rows_q, IDX_DIM), lambda b, j, pt: (b, 0, 0)),
                    pl.BlockSpec((1, rows_q, 1), lambda b, j, pt: (b, 0, 0)),
                    pl.BlockSpec((1, PAGE_SIZE, IDX_DIM), lambda b, j, pt: (b, 0, 0))],
        out_specs=pl.BlockSpec((1, n_pages + 1, t_new, LANE), lambda b, j, pt: (b, 0, 0, 0)),
        scratch_shapes=[pltpu.VMEM((n_pages + 1, t_new, LANE), I32), pltpu.VMEM((t_new, 1), I32)])
    return pl.pallas_call(
        functools.partial(_dsa_sample_select_body, n_pages=n_pages, k_sel=k_sel, nbits=nbits, t_new=t_new),
        out_shape=jax.ShapeDtypeStruct((n_b, n_pages + 1, t_new, LANE), F32),
        grid_spec=grid_spec,
        compiler_params=_cparams(("parallel", "arbitrary"), 24 << 20),
        name="branch_c_decode_select",
    )(pt_flat, *([kidx_pool] * pp), iq_s, iw_s, ik_new)


def _dsa_sample_attend_body(pt_ref, *refs, n_pages, t_new):
    pp = PAGES_PER_STEP
    k_refs, v_refs = refs[:pp], refs[pp:2 * pp]
    q_ref, bias_ref, kn_ref, vn_ref, o_ref, m_ref, l_ref, acc_ref = refs[2 * pp:]
    j = pl.program_id(1)
    n_steps = n_pages // pp
    hd = HEAD_DIM
    rq = q_ref.shape[1] // C_KV_HEADS
    reps = rq // t_new

    @pl.when(j == 0)
    def _():
        m_ref[...] = jnp.full(m_ref.shape, MASK_NEG, F32)
        l_ref[...] = jnp.zeros(l_ref.shape, F32)
        acc_ref[...] = jnp.zeros(acc_ref.shape, F32)

    def attend(kpage, vpage, bias_t):
        bias = jnp.concatenate([bias_t] * reps, axis=0)
        kb_, vb_ = kpage.astype(BF16), vpage.astype(BF16)
        for kvh in range(C_KV_HEADS):
            rs = slice(kvh * rq, (kvh + 1) * rq)
            sl = slice(kvh * hd, (kvh + 1) * hd)
            s = lax.dot_general(q_ref[0, rs, :].astype(BF16), kb_[:, sl], (((1,), (1,)), ((), ())),
                                preferred_element_type=F32) * (hd ** -0.5) + bias
            m_old = m_ref[rs, :]
            m_new = jnp.maximum(m_old, jnp.max(s, axis=-1, keepdims=True))
            alpha = jnp.exp(m_old - m_new)
            p = jnp.exp(s - m_new)
            l_ref[rs, :] = alpha * l_ref[rs, :] + jnp.sum(p, axis=-1, keepdims=True)
            acc_ref[rs, :] = alpha * acc_ref[rs, :] + jnp.dot(p.astype(BF16), vb_[:, sl], preferred_element_type=F32)
            m_ref[rs, :] = m_new

    @pl.when(j < n_steps)
    def _():
        for i in range(pp):
            attend(k_refs[i][0], v_refs[i][0], bias_ref[0, j * pp + i])

    @pl.when(j == n_steps)
    def _():
        attend(kn_ref[0], vn_ref[0], bias_ref[0, n_pages])
        o_ref[0] = acc_ref[...] / l_ref[...]


def _dsa_sample_attend(pt_flat, k_pool, v_pool, q_s, bias, k_new, v_new, *, n_b, n_pages, t_new):
    pp = PAGES_PER_STEP
    n_steps = n_pages // pp
    rows = q_s.shape[1]
    page_specs = [pl.BlockSpec((1, PAGE_SIZE, C_KV), _page_index_map(i, n_pages, n_steps)) for i in range(pp)]
    grid_spec = pltpu.PrefetchScalarGridSpec(
        num_scalar_prefetch=1,
        grid=(n_b, n_steps + 1),
        in_specs=page_specs + page_specs
                 + [pl.BlockSpec((1, rows, HEAD_DIM), lambda b, j, pt: (b, 0, 0)),
                    pl.BlockSpec((1, n_pages + 1, t_new, LANE), lambda b, j, pt: (b, 0, 0, 0)),
                    pl.BlockSpec((1, PAGE_SIZE, C_KV), lambda b, j, pt: (b, 0, 0)),
                    pl.BlockSpec((1, PAGE_SIZE, C_KV), lambda b, j, pt: (b, 0, 0))],
        out_specs=pl.BlockSpec((1, rows, HEAD_DIM), lambda b, j, pt: (b, 0, 0)),
        scratch_shapes=[pltpu.VMEM((rows, 1), F32), pltpu.VMEM((rows, 1), F32), pltpu.VMEM((rows, HEAD_DIM), F32)])
    return pl.pallas_call(
        functools.partial(_dsa_sample_attend_body, n_pages=n_pages, t_new=t_new),
        out_shape=jax.ShapeDtypeStruct((n_b, rows, HEAD_DIM), F32),
        grid_spec=grid_spec,
        compiler_params=_cparams(("parallel", "arbitrary"), 32 << 20),
        name="branch_c_decode_attend",
    )(pt_flat, *([k_pool] * pp), *([v_pool] * pp), q_s, bias, k_new, v_new)


def _dsa_sample(h_s, lay, pt_flat, k_pool, v_pool, kidx_pool, *, n_b, t_new, n_pages):
    hd = HEAD_DIM
    seg = lambda name, w: h_s[:, lay[name]:lay[name] + w]
    rq = -(-C_GROUP * t_new // 16) * 16
    q = seg("cq", C_Q).reshape(n_b, t_new, C_KV_HEADS, C_GROUP, hd).transpose(0, 2, 3, 1, 4)
    q = q.reshape(n_b, C_KV_HEADS, C_GROUP * t_new, hd)
    q_s = jnp.pad(q, ((0, 0), (0, 0), (0, rq - C_GROUP * t_new), (0, 0))).reshape(n_b, C_KV_HEADS * rq, hd)
    iq_s = seg("iq", IDX_HEADS * IDX_DIM).reshape(n_b, t_new * IDX_HEADS, IDX_DIM)
    iw = h_s[:, lay["small"] + 2 * B_HEADS:lay["small"] + 2 * B_HEADS + IDX_HEADS]
    iw_s = (iw * (IDX_HEADS ** -0.5 * IDX_DIM ** -0.5)).reshape(n_b, t_new * IDX_HEADS, 1)
    padrows = lambda x: jnp.pad(x.reshape(n_b, t_new, -1), ((0, 0), (0, PAGE_SIZE - t_new), (0, 0)))
    ik_new, k_new, v_new = padrows(seg("ik", IDX_DIM)), padrows(seg("ck", C_KV)), padrows(seg("cv", C_KV))
    bias = _dsa_sample_select(pt_flat, kidx_pool, iq_s, iw_s, ik_new, n_b=n_b, n_pages=n_pages, t_new=t_new)
    o = _dsa_sample_attend(pt_flat, k_pool, v_pool, q_s, bias, k_new, v_new, n_b=n_b, n_pages=n_pages, t_new=t_new)
    o = o.reshape(n_b, C_KV_HEADS, rq, hd)[:, :, :C_GROUP * t_new].reshape(n_b, C_KV_HEADS, C_GROUP, t_new, hd)
    return o.transpose(0, 3, 1, 2, 4).reshape(n_b * t_new, C_Q).astype(BF16)


def _merge_body(ya_ref, yb_ref, yc_ref, wa_ref, wb_ref, wc_ref, g0_ref, g1_ref, g2_ref, o_ref):
    acc = jax.nn.sigmoid(g0_ref[...]) * jnp.dot(ya_ref[...], wa_ref[...], preferred_element_type=F32)
    acc = acc + jax.nn.sigmoid(g1_ref[...]) * jnp.dot(yb_ref[...], wb_ref[...], preferred_element_type=F32)
    acc = acc + jax.nn.sigmoid(g2_ref[...]) * jnp.dot(yc_ref[...], wc_ref[...], preferred_element_type=F32)
    o_ref[...] = acc.astype(o_ref.dtype)


def _merge(ya, yb, yc, wa, wb, wc, h, lay, d_model):
    m = ya.shape[0]
    tm = _pick(m, 1024)
    tn = _pick(d_model, 512)
    gcb = lay["gate"] // tn
    per = d_model // tn

    def gate_spec(i):
        return pl.BlockSpec((tm, tn), lambda r, c: (r, gcb + i * per + c))

    kw = A_WIDTH + B_WIDTH + C_Q
    vmem = 2 * (tm * kw * 2 + kw * tn * 2 + 3 * tm * tn * 4 + tm * tn * 2) + 4 * tm * tn * 4
    return pl.pallas_call(
        _merge_body,
        out_shape=jax.ShapeDtypeStruct((m, d_model), BF16),
        grid=(m // tm, d_model // tn),
        in_specs=[pl.BlockSpec((tm, A_WIDTH), lambda r, c: (r, 0)),
                  pl.BlockSpec((tm, B_WIDTH), lambda r, c: (r, 0)),
                  pl.BlockSpec((tm, C_Q), lambda r, c: (r, 0)),
                  pl.BlockSpec((A_WIDTH, tn), lambda r, c: (0, c)),
                  pl.BlockSpec((B_WIDTH, tn), lambda r, c: (0, c)),
                  pl.BlockSpec((C_Q, tn), lambda r, c: (0, c)),
                  gate_spec(0), gate_spec(1), gate_spec(2)],
        out_specs=pl.BlockSpec((tm, tn), lambda r, c: (r, c)),
        compiler_params=_cparams(("parallel", "arbitrary"), vmem + COMPILER_SCRATCH_BYTES),
        name="gated_merge",
    )(ya, yb, yc, wa, wb, wc, h, h, h)


def _ffn_body(hn_ref, w1_ref, w3_ref, w2_ref, x_ref, o_ref):
    f = pl.program_id(1)
    hn = hn_ref[...]
    a = jnp.dot(hn, w1_ref[...], preferred_element_type=F32)
    b = jnp.dot(hn, w3_ref[...], preferred_element_type=F32)
    act = (jax.nn.silu(a) * b).astype(BF16)
    part = jnp.dot(act, w2_ref[...], preferred_element_type=F32)

    @pl.when(f == 0)
    def _():
        o_ref[...] = x_ref[...] + part

    @pl.when(f != 0)
    def _():
        o_ref[...] += part


def _ffn(hn, w1, w3, w2, x):
    m, d = hn.shape
    d_ff = w1.shape[1]
    tm = _pick(m, 512)
    tf = _pick(d_ff, 256)
    once = pl.Buffered(1)
    vmem = tm * d * 2 + tm * d * 4 + 2 * tm * d * 4 + 2 * 3 * d * tf * 2 + 4 * tm * tf * 4 + tm * d * 4
    return pl.pallas_call(
        _ffn_body,
        out_shape=jax.ShapeDtypeStruct((m, d), F32),
        grid=(m // tm, d_ff // tf),
        in_specs=[pl.BlockSpec((tm, d), lambda i, f: (i, 0), pipeline_mode=once),
                  pl.BlockSpec((d, tf), lambda i, f: (0, f)),
                  pl.BlockSpec((d, tf), lambda i, f: (0, f)),
                  pl.BlockSpec((tf, d), lambda i, f: (f, 0)),
                  pl.BlockSpec((tm, d), lambda i, f: (i, 0), pipeline_mode=once)],
        out_specs=pl.BlockSpec((tm, d), lambda i, f: (i, 0)),
        compiler_params=_cparams(("parallel", "arbitrary"), vmem + COMPILER_SCRATCH_BYTES),
        name="swiglu_ffn",
    )(hn, w1, w3, w2, x)


def _dense_front(x, ln1, w_in_p):
    xn = _rmsnorm(x, ln1, BF16)
    return _matmul(xn, w_in_p, tn_pref=768, name="in_proj")


def _dense_back(x, ya, yb, yc, h, lay, wts, d_model):
    wa, wb, wc, w_o, ln2, w1, w3, w2 = wts
    mix = _merge(ya, yb, yc, wa, wb, wc, h, lay, d_model)
    x = _matmul(mix, w_o, residual=x, name="out_proj")
    hn = _rmsnorm(x, ln2, BF16)
    return _ffn(hn, w1, w3, w2, x)


def kernel(x_prompt, x_sample, cache_k, cache_v, cache_kidx, state_conv, state_delta, page_table, ln1, w_in,
           a_ln_g, a_ln_b, a_ws, a_bs, b_conv_w, b_a_log, b_dt_bias, b_out_g, w_br, w_o, ln2, ffn_w1, ffn_w3,
           ffn_w2, ln_f):
    n_bp, seq, d_model = x_prompt.shape
    n_bs, t_new, _ = x_sample.shape
    depth = ln1.shape[0]
    n_phys = cache_k.shape[1]
    n_pages = page_table.shape[1]
    assert seq % A_CHUNK == 0 and seq % GDN_CHUNK == 0 and CONV_W - 1 <= t_new <= min(A_CHUNK, GDN_CHUNK)
    lay = _layout(d_model)
    hd = HEAD_DIM

    xp = x_prompt.reshape(n_bp * seq, d_model)
    xs = x_sample.reshape(n_bs * t_new, d_model)
    k_pool = cache_k.reshape(depth * n_phys, PAGE_SIZE, C_KV)
    v_pool = cache_v.reshape(depth * n_phys, PAGE_SIZE, C_KV)
    kidx_pool = cache_kidx.reshape(depth * n_phys, PAGE_SIZE, IDX_DIM)
    conv0_p = jnp.zeros((n_bp, SUBLANE, 3 * B_WIDTH), F32)
    s0_p = jnp.zeros((n_bp, B_HEADS, hd, hd), F32)
    qkv_off = lay["bq"]

    outs = {k: [] for k in ("pk", "pv", "pik", "pconv", "pdelta", "sk", "sv", "sik", "sconv", "sdelta", "schunk")}
    for l in range(depth):
        w_in_p = _pack_w_in(w_in[l], d_model)
        wts = (w_br[l, :A_WIDTH].astype(BF16), w_br[l, A_WIDTH:A_WIDTH + B_WIDTH].astype(BF16),
               w_br[l, A_WIDTH + B_WIDTH:].astype(BF16), w_o[l].astype(BF16), ln2[l],
               ffn_w1[l].astype(BF16), ffn_w3[l].astype(BF16), ffn_w2[l].astype(BF16))

        h = _dense_front(xp, ln1[l], w_in_p)
        ya = _branch_a(h, lay, a_ws[l], a_bs[l], a_ln_g[l], a_ln_b[l], chunk=A_CHUNK, tb=A_CHUNK, emit_av=False)
        srcs = [(h, lay[n] // B_WIDTH) for n in ("bq", "bk", "bv", "bz")] + [(h, lay["small"] // LANE)]
        yb, s_new = _gdn(srcs, conv0_p, s0_p, b_conv_w[l], b_a_log[l], b_dt_bias[l], b_out_g[l],
                         n_b=n_bp, t_pad=seq, t_valid=GDN_CHUNK, chunk=GDN_CHUNK)
        yc = _dsa_prompt(h, lay, n_b=n_bp, seq=seq)
        h3 = h.reshape(n_bp, seq, lay["total"])
        outs["pk"].append(h3[:, :, lay["ck"]:lay["ck"] + C_KV].reshape(n_bp, seq, C_KV_HEADS, hd))
        outs["pv"].append(h3[:, :, lay["cv"]:lay["cv"] + C_KV].reshape(n_bp, seq, C_KV_HEADS, hd))
        outs["pik"].append(h3[:, :, lay["ik"]:lay["ik"] + IDX_DIM])
        outs["pconv"].append(h3[:, seq - (CONV_W - 1):, qkv_off:qkv_off + 3 * B_WIDTH])
        outs["pdelta"].append(s_new)
        xp = _dense_back(xp, ya, yb, yc, h, lay, wts, d_model)

        hs = _dense_front(xs, ln1[l], w_in_p)
        ya, av = _branch_a(hs, lay, a_ws[l], a_bs[l], a_ln_g[l], a_ln_b[l], chunk=t_new, tb=n_bs * t_new,
                           emit_av=True)
        hs3 = hs.reshape(n_bs, t_new, lay["total"])
        padt = lambda x: jnp.pad(x, ((0, 0), (0, GDN_CHUNK - t_new), (0, 0))).reshape(n_bs * GDN_CHUNK, -1)
        srcs = [(padt(hs3[:, :, lay[n]:lay[n] + B_WIDTH]), 0) for n in ("bq", "bk", "bv", "bz")]
        srcs.append((padt(hs3[:, :, lay["small"]:lay["small"] + LANE]), 0))
        conv0_s = jnp.pad(state_conv[l], ((0, 0), (SUBLANE - (CONV_W - 1), 0), (0, 0)))
        yb, s_new = _gdn(srcs, conv0_s, state_delta[l], b_conv_w[l], b_a_log[l], b_dt_bias[l], b_out_g[l],
                         n_b=n_bs, t_pad=GDN_CHUNK, t_valid=t_new, chunk=GDN_CHUNK)
        yb = yb.reshape(n_bs, GDN_CHUNK, B_WIDTH)[:, :t_new].reshape(n_bs * t_new, B_WIDTH)
        pt_flat = (page_table + l * n_phys).reshape(-1).astype(I32)
        yc = _dsa_sample(hs, lay, pt_flat, k_pool, v_pool, kidx_pool, n_b=n_bs, t_new=t_new, n_pages=n_pages)
        outs["sk"].append(hs3[:, :, lay["ck"]:lay["ck"] + C_KV].reshape(n_bs, t_new, C_KV_HEADS, hd))
        outs["sv"].append(hs3[:, :, lay["cv"]:lay["cv"] + C_KV].reshape(n_bs, t_new, C_KV_HEADS, hd))
        outs["sik"].append(hs3[:, :, lay["ik"]:lay["ik"] + IDX_DIM])
        outs["sconv"].append(hs3[:, t_new - (CONV_W - 1):, qkv_off:qkv_off + 3 * B_WIDTH])
        outs["sdelta"].append(s_new)
        outs["schunk"].append(av.reshape(n_bs, t_new, A_WIDTH))
        xs = _dense_back(xs, ya, yb, yc, hs, lay, wts, d_model)

    y_prompt = _rmsnorm(xp, ln_f, F32).reshape(n_bp, seq, d_model)
    y_sample = _rmsnorm(xs, ln_f, F32).reshape(n_bs, t_new, d_model)
    st = lambda k: jnp.stack(outs[k])
    return (y_prompt, y_sample, st("pk"), st("pv"), st("pik"), st("pconv"), st("pdelta"),
            st("sk"), st("sv"), st("sik"), st("sconv"), st("sdelta"), st("schunk"))
```

```python
import functools

import jax
import jax.numpy as jnp
from jax import lax
from jax.experimental import pallas as pl
from jax.experimental.pallas import tpu as pltpu

F32 = jnp.float32
BF16 = jnp.bfloat16
I32 = jnp.int32

HEAD_DIM = 128
A_GROUPS = 8
A_CHUNK = 128
A_WIDTH = A_GROUPS * HEAD_DIM
B_HEADS = 12
B_WIDTH = B_HEADS * HEAD_DIM
CONV_W = 4
C_HEADS = 12
C_KV_HEADS = 4
C_GROUP = C_HEADS // C_KV_HEADS
C_Q = C_HEADS * HEAD_DIM
C_KV = C_KV_HEADS * HEAD_DIM
IDX_HEADS = 16
IDX_DIM = 128
TOPK_MAX = 256
PAGE_SIZE = 128
N_BRANCH = 3
RMS_EPS = 1e-6
LN_EPS = 1e-5

LANE = 128
SUBLANE = 8
V7X_VMEM_BYTES = 64 * 1024 * 1024
VMEM_BUDGET = 56 * 1024 * 1024
COMPILER_SCRATCH_BYTES = 8 * 1024 * 1024

GDN_CHUNK = 128
DSA_TQ = 256
DSA_TK = 512
COUNT_ROWS = 128
PAGES_PER_STEP = 8
MASK_NEG = -1e30
INT_MIN = -2147483648
KEY_OF_NEG_INF = -2139095041


def _cparams(semantics, vmem_bytes):
    return pltpu.CompilerParams(dimension_semantics=semantics,
                                vmem_limit_bytes=int(min(max(vmem_bytes, 16 * 1024 * 1024), VMEM_BUDGET)))


def _pick(dim, pref):
    t = pref
    while t >= SUBLANE:
        if dim % t == 0:
            return t
        t //= 2
    return dim


def _layout(d_model):
    segs = [("gate", N_BRANCH * d_model, d_model), ("bq", B_WIDTH, B_WIDTH), ("bk", B_WIDTH, B_WIDTH),
            ("bv", B_WIDTH, B_WIDTH), ("bz", B_WIDTH, B_WIDTH), ("cq", C_Q, C_Q), ("ck", C_KV, C_KV),
            ("au", A_WIDTH, A_WIDTH), ("av", A_WIDTH, A_WIDTH), ("cv", C_KV, C_KV),
            ("iq", IDX_HEADS * IDX_DIM, 4 * IDX_DIM), ("ik", IDX_DIM, IDX_DIM), ("small", LANE, LANE)]
    off, lay = 0, {}
    for name, width, align in segs:
        assert off % align == 0, (name, off, align)
        lay[name] = off
        off += width
    lay["total"] = off
    return lay


def _pack_w_in(w_in, d_model):
    widths = (A_WIDTH, A_WIDTH, 3 * B_WIDTH, B_WIDTH, B_HEADS, B_HEADS, C_Q, C_KV, C_KV,
              IDX_HEADS * IDX_DIM, IDX_HEADS, IDX_DIM, N_BRANCH * d_model)
    names = ("au", "av", "bqkv", "bz", "ba", "bb", "cq", "ck", "cv", "iq", "iw", "ik", "gate")
    src, start = {}, 0
    for n, w in zip(names, widths):
        src[n] = w_in[:, start:start + w].astype(BF16)
        start += w
    assert start == w_in.shape[1]
    pad = jnp.zeros((w_in.shape[0], LANE - 2 * B_HEADS - IDX_HEADS), BF16)
    return jnp.concatenate([src["gate"], src["bqkv"], src["bz"], src["cq"], src["ck"], src["au"], src["av"],
                            src["cv"], src["iq"], src["ik"], src["ba"], src["bb"], src["iw"], pad], axis=1)


def _rmsnorm_body(x_ref, g_ref, o_ref):
    x = x_ref[...]
    ms = jnp.mean(x * x, axis=-1, keepdims=True)
    o_ref[...] = (x * lax.rsqrt(ms + RMS_EPS) * g_ref[...]).astype(o_ref.dtype)


def _rmsnorm(x, g, out_dtype):
    m, d = x.shape
    tm = _pick(m, 256)
    return pl.pallas_call(
        _rmsnorm_body,
        out_shape=jax.ShapeDtypeStruct((m, d), out_dtype),
        grid=(m // tm,),
        in_specs=[pl.BlockSpec((tm, d), lambda i: (i, 0)), pl.BlockSpec((1, d), lambda i: (0, 0))],
        out_specs=pl.BlockSpec((tm, d), lambda i: (i, 0)),
        compiler_params=_cparams(("parallel",), 6 * tm * d * 4),
        name="rmsnorm",
    )(x, g.reshape(1, d))


def _mm_body(a_ref, w_ref, o_ref):
    o_ref[...] = jnp.dot(a_ref[...], w_ref[...], preferred_element_type=F32).astype(o_ref.dtype)


def _mm_res_body(a_ref, w_ref, r_ref, o_ref):
    o_ref[...] = (r_ref[...] + jnp.dot(a_ref[...], w_ref[...], preferred_element_type=F32)).astype(o_ref.dtype)


def _matmul(a, w, *, residual=None, out_dtype=F32, tm_pref=1024, tn_pref=1024, name="matmul"):
    m, k = a.shape
    n = w.shape[1]
    tm = _pick(m, tm_pref)
    tn = tn_pref if n % tn_pref == 0 else _pick(n, tn_pref)
    osz = jnp.dtype(out_dtype).itemsize
    vmem = 2 * (tm * k * 2 + k * tn * 2 + tm * tn * osz) + tm * tn * 4
    in_specs = [pl.BlockSpec((tm, k), lambda i, j: (i, 0)), pl.BlockSpec((k, tn), lambda i, j: (0, j))]
    args = [a, w]
    body = _mm_body
    if residual is not None:
        in_specs.append(pl.BlockSpec((tm, tn), lambda i, j: (i, j)))
        args.append(residual)
        body = _mm_res_body
        vmem += 2 * tm * tn * 4
    return pl.pallas_call(
        body,
        out_shape=jax.ShapeDtypeStruct((m, n), out_dtype),
        grid=(m // tm, n // tn),
        in_specs=in_specs,
        out_specs=pl.BlockSpec((tm, tn), lambda i, j: (i, j)),
        compiler_params=_cparams(("parallel", "arbitrary"), vmem + COMPILER_SCRATCH_BYTES),
        name=name,
    )(*args)


def _branch_a_body(u_ref, v_ref, w_ref, bs_ref, g_ref, b_ref, y_ref, *av_ref, chunk):
    tb = u_ref.shape[0]
    u = jax.nn.gelu(u_ref[...])
    v = jax.nn.gelu(v_ref[...])
    mu = jnp.mean(v, axis=-1, keepdims=True)
    var = jnp.mean(jnp.square(v - mu), axis=-1, keepdims=True)
    vn = (v - mu) * lax.rsqrt(var + LN_EPS) * g_ref[...] + b_ref[...]
    if av_ref:
        av_ref[0][...] = vn
    row = lax.broadcasted_iota(I32, (tb, tb), 0)
    col = lax.broadcasted_iota(I32, (tb, tb), 1)
    keep = (col <= row) & ((row // chunk) == (col // chunk))
    vb = vn.astype(BF16)
    for g in range(A_GROUPS):
        sl = slice(g * HEAD_DIM, (g + 1) * HEAD_DIM)
        wg = jnp.where(keep, w_ref[g], 0.0).astype(BF16)
        s = jnp.dot(wg, vb[:, sl], preferred_element_type=F32) + bs_ref[:, g:g + 1]
        y_ref[:, sl] = (u[:, sl] * s).astype(y_ref.dtype)


def _branch_a(h, lay, a_ws, a_bs, ln_g, ln_b, *, chunk, tb, emit_av):
    m = h.shape[0]
    reps = tb // chunk
    wfull = jnp.tile(a_ws[:, :chunk, :chunk], (1, reps, reps))
    bs_t = jnp.tile(a_bs[:, :chunk].T, (reps, 1))
    cu, cv = lay["au"] // A_WIDTH, lay["av"] // A_WIDTH
    out_shape = [jax.ShapeDtypeStruct((m, A_WIDTH), BF16)]
    out_specs = [pl.BlockSpec((tb, A_WIDTH), lambda i: (i, 0))]
    if emit_av:
        out_shape.append(jax.ShapeDtypeStruct((m, A_WIDTH), F32))
        out_specs.append(pl.BlockSpec((tb, A_WIDTH), lambda i: (i, 0)))
    res = pl.pallas_call(
        functools.partial(_branch_a_body, chunk=chunk),
        out_shape=out_shape,
        grid=(m // tb,),
        in_specs=[pl.BlockSpec((tb, A_WIDTH), lambda i: (i, cu)),
                  pl.BlockSpec((tb, A_WIDTH), lambda i: (i, cv)),
                  pl.BlockSpec((A_GROUPS, tb, tb), lambda i: (0, 0, 0)),
                  pl.BlockSpec((tb, A_GROUPS), lambda i: (0, 0)),
                  pl.BlockSpec((1, A_WIDTH), lambda i: (0, 0)),
                  pl.BlockSpec((1, A_WIDTH), lambda i: (0, 0))],
        out_specs=out_specs,
        compiler_params=_cparams(("parallel",), 32 << 20),
        name="branch_a_gmlp",
    )(h, h, wfull, bs_t, ln_g.reshape(1, A_WIDTH), ln_b.reshape(1, A_WIDTH))
    return res if emit_av else res[0]


def _bdot(a, b):
    return lax.dot_general(a.astype(BF16), b.astype(BF16), (((2,), (1,)), ((0,), (0,))),
                           preferred_element_type=F32)


def _bdot_nt(a, b):
    return lax.dot_general(a.astype(BF16), b.astype(BF16), (((2,), (2,)), ((0,), (0,))),
                           preferred_element_type=F32)


def _transpose_rows(x):
    r = x.shape[0]
    if r < LANE:
        x = jnp.concatenate([x, jnp.zeros((LANE - r, LANE), x.dtype)], axis=0)
    return x.T[:, :r]


def _gdn_body(q_ref, k_ref, v_ref, z_ref, sm_ref, c0_ref, s0_ref, cw_ref, al_ref, dt_ref, og_ref,
              y_ref, sout_ref, ext_ref, s_ref, *, chunk, t_valid):
    c = pl.program_id(1)
    nc = pl.num_programs(1)
    hd = HEAD_DIM

    @pl.when(c == 0)
    def _():
        s_ref[...] = s0_ref[0]
        for j in range(3):
            ext_ref[j, 0:SUBLANE, :] = c0_ref[0, :, j * B_WIDTH:(j + 1) * B_WIDTH]

    acts = []
    for j, ref in enumerate((q_ref, k_ref, v_ref)):
        ext_ref[j, SUBLANE:SUBLANE + chunk, :] = ref[...]
        acc = None
        for i in range(CONV_W):
            lo = SUBLANE - (CONV_W - 1) + i
            term = ext_ref[j, lo:lo + chunk, :] * cw_ref[i:i + 1, j * B_WIDTH:(j + 1) * B_WIDTH]
            acc = term if acc is None else acc + term
        acts.append(jax.nn.silu(acc))
        ext_ref[j, 0:SUBLANE, :] = ext_ref[j, chunk:chunk + SUBLANE, :]
    qa, ka, va = acts

    sm = sm_ref[...]
    row1 = lax.broadcasted_iota(I32, (chunk, LANE), 0)
    g_all = -jnp.exp(al_ref[...]) * jax.nn.softplus(sm + dt_ref[...])
    beta_all = jax.nn.sigmoid(sm)
    if t_valid < chunk:
        g_all = jnp.where(row1 < t_valid, g_all, 0.0)
        beta_all = jnp.where(row1 < t_valid, beta_all, 0.0)
    gc_all = g_all
    d = 1
    while d < chunk:
        gc_all = gc_all + jnp.where(row1 >= d, pltpu.roll(gc_all, d, 0), 0.0)
        d *= 2
    gc_t = _transpose_rows(gc_all)

    row = lax.broadcasted_iota(I32, (chunk, chunk), 0)
    col = lax.broadcasted_iota(I32, (chunk, chunk), 1)
    incl = row >= col
    strict = row > col
    eye = jnp.where(row == col, 1.0, 0.0)
    pair_masks = []
    bs = 1
    while bs < chunk:
        pair_masks.append(((row // bs) % 2 == 1) & ((col // bs) == (row // bs) - 1))
        bs *= 2

    heads = range(B_HEADS)
    per_head = lambda x: jnp.stack([x[:, h * hd:(h + 1) * hd] for h in heads], axis=0)
    q3, k3, v3 = per_head(qa), per_head(ka), per_head(va)
    qn = q3 * lax.rsqrt(jnp.sum(q3 * q3, axis=-1, keepdims=True) + 1e-6) * (hd ** -0.5)
    kn = k3 * lax.rsqrt(jnp.sum(k3 * k3, axis=-1, keepdims=True) + 1e-6)
    beta = jnp.stack([beta_all[:, B_HEADS + h:B_HEADS + h + 1] for h in heads], axis=0)
    gcol = jnp.stack([gc_all[:, h:h + 1] for h in heads], axis=0)
    grow = jnp.stack([gc_t[h:h + 1, :] for h in heads], axis=0)
    decay = jnp.exp(jnp.where(incl[None], gcol - grow, -jnp.inf))
    kb = kn * beta
    eg = jnp.exp(gcol)
    lmat = jnp.where(strict[None], _bdot_nt(kb, kn) * decay, 0.0)
    attn = _bdot_nt(qn, kn) * decay
    tinv = eye[None] - jnp.where(pair_masks[0][None], lmat, 0.0)
    for pm in pair_masks[1:]:
        tinv = tinv - _bdot(tinv, _bdot(jnp.where(pm[None], lmat, 0.0), tinv))
    sol = _bdot(tinv, jnp.concatenate([v3 * beta, kb * eg], axis=-1))
    value, kcd = sol[:, :, :hd], sol[:, :, hd:]
    s_old = s_ref[...]
    vnew = value - _bdot(kcd, s_old)
    o = _bdot(qn * eg, s_old) + _bdot(attn, vnew)
    glast = gcol[:, chunk - 1:chunk, :]
    kend = kn * jnp.exp(glast - gcol)
    s_ref[...] = s_old * jnp.exp(glast) + _bdot(jnp.swapaxes(kend, 1, 2), vnew)
    on = o * lax.rsqrt(jnp.mean(o * o, axis=-1, keepdims=True) + RMS_EPS) * og_ref[...]
    for h in heads:
        sl = slice(h * hd, (h + 1) * hd)
        y_ref[:, sl] = (on[h] * jax.nn.silu(z_ref[:, sl])).astype(y_ref.dtype)

    @pl.when(c == nc - 1)
    def _():
        sout_ref[0] = s_ref[...]


def _gdn(srcs, conv0, s0, conv_w, a_log, dt_bias, o_g, *, n_b, t_pad, t_valid, chunk):
    nc = t_pad // chunk
    arrs = [a for a, _ in srcs]
    cbs = [cb for _, cb in srcs]
    widths = [B_WIDTH] * 4 + [LANE]

    def tok_spec(w, cb):
        return pl.BlockSpec((chunk, w), lambda b, c: (b * nc + c, cb))

    pad12 = lambda x: jnp.zeros((1, LANE), F32).at[0, :B_HEADS].set(x)
    return pl.pallas_call(
        functools.partial(_gdn_body, chunk=chunk, t_valid=t_valid),
        out_shape=[jax.ShapeDtypeStruct((n_b * t_pad, B_WIDTH), BF16),
                   jax.ShapeDtypeStruct((n_b, B_HEADS, HEAD_DIM, HEAD_DIM), F32)],
        grid=(n_b, nc),
        in_specs=[tok_spec(w, cb) for w, cb in zip(widths, cbs)] + [
            pl.BlockSpec((1, SUBLANE, 3 * B_WIDTH), lambda b, c: (b, 0, 0)),
            pl.BlockSpec((1, B_HEADS, HEAD_DIM, HEAD_DIM), lambda b, c: (b, 0, 0, 0)),
            pl.BlockSpec((CONV_W, 3 * B_WIDTH), lambda b, c: (0, 0)),
            pl.BlockSpec((1, LANE), lambda b, c: (0, 0)),
            pl.BlockSpec((1, LANE), lambda b, c: (0, 0)),
            pl.BlockSpec((1, HEAD_DIM), lambda b, c: (0, 0))],
        out_specs=[pl.BlockSpec((chunk, B_WIDTH), lambda b, c: (b * nc + c, 0)),
                   pl.BlockSpec((1, B_HEADS, HEAD_DIM, HEAD_DIM), lambda b, c: (b, 0, 0, 0))],
        scratch_shapes=[pltpu.VMEM((3, chunk + SUBLANE, B_WIDTH), F32),
                        pltpu.VMEM((B_HEADS, HEAD_DIM, HEAD_DIM), F32)],
        compiler_params=_cparams(("parallel", "arbitrary"), 40 << 20),
        name="branch_b_gated_delta",
    )(*arrs, conv0, s0, conv_w, pad12(a_log), pad12(dt_bias), o_g.reshape(1, HEAD_DIM))


def _sort_key(x):
    b = lax.bitcast_convert_type(x + 0.0, I32)
    return b ^ ((b >> 31) & 0x7FFFFFFF)


def _kth_largest_key(count_ge, k, rows):
    kf = jnp.float32(k)
    t0 = jnp.where(count_ge(jnp.zeros((rows, 1), I32)) >= kf, 0, INT_MIN).astype(I32)

    def step(i, t):
        cand = t + lax.shift_left(jnp.int32(1), 30 - i)
        return jnp.where(count_ge(cand) >= kf, cand, t)

    return lax.fori_loop(0, 31, step, t0)


def _tie_index(count_tie_le, r, nbits, rows):
    def step(i, j):
        cand = j + lax.shift_left(jnp.int32(1), nbits - 1 - i)
        return jnp.where(count_tie_le(cand - 1) < r, cand, j)

    return lax.fori_loop(0, nbits, step, jnp.zeros((rows, 1), I32))


def _topk_threshold(load_key, nblk, k, rows, nbits, j_ref, t_ref, tmp_ref, cnt_ref):
    rc = min(rows, COUNT_ROWS)
    col = lax.broadcasted_iota(I32, (rc, LANE), 1)

    def count(make_pred):
        for r0 in range(0, rows, rc):
            pred = make_pred(lambda ref, r0=r0: jnp.broadcast_to(ref[r0:r0 + rc, :], (rc, LANE)))

            def blk(b, acc, pred=pred, r0=r0):
                return acc + jnp.where(pred(load_key(b, r0, rc), b * LANE + col), 1.0, 0.0)

            acc = lax.fori_loop(0, nblk, blk, jnp.zeros((rc, LANE), F32))
            cnt_ref[r0:r0 + rc, :] = jnp.sum(acc, axis=1, keepdims=True)
        return cnt_ref[...]

    def count_ge(cand):
        tmp_ref[...] = cand

        def make(bc):
            cb = bc(tmp_ref)
            return lambda key, pos: key >= cb
        return count(make)

    t = _kth_largest_key(count_ge, k, rows)
    t_ref[...] = t
    n_ge = count_ge(t)
    j_ref[...] = jnp.full((rows, 1), 2 ** 30, I32)
    need = jnp.sum(jnp.where((n_ge > jnp.float32(k)) & (t > KEY_OF_NEG_INF), 1.0, 0.0))

    @pl.when(need > 0.0)
    def _():
        def make_gt(bc):
            tb = bc(t_ref)
            return lambda key, pos: key > tb

        r = jnp.float32(k) - count(make_gt)

        def count_tie_le(jc):
            tmp_ref[...] = jc

            def make(bc):
                tb, jb = bc(t_ref), bc(tmp_ref)
                return lambda key, pos: (key == tb) & (pos <= jb)
            return count(make)

        j_ref[...] = jnp.where(n_ge > jnp.float32(k), _tie_index(count_tie_le, r, nbits, rows), 2 ** 30)

    return t


def _topk_threshold_lanes(load_key, kofs3, nblk, k, nq, nbits):
    kf = jnp.float32(k)

    def count(pred):
        def blk(b, acc):
            return acc + jnp.sum(jnp.where(pred(load_key(b), b * LANE + kofs3), 1.0, 0.0), axis=0)
        acc = lax.fori_loop(0, nblk, blk, jnp.zeros((SUBLANE, nq), F32))
        return jnp.broadcast_to(jnp.sum(acc, axis=0, keepdims=True), (SUBLANE, nq))

    count_ge = lambda cand: count(lambda key, pos: key >= cand)
    t0 = jnp.where(count_ge(jnp.zeros((SUBLANE, nq), I32)) >= kf, 0, INT_MIN).astype(I32)

    def bit_step(i, t):
        cand = t + lax.shift_left(jnp.int32(1), 30 - i)
        return jnp.where(count_ge(cand) >= kf, cand, t)

    t = lax.fori_loop(0, 31, bit_step, t0)
    n_ge = count_ge(t)
    tied = (n_ge > kf) & (t > KEY_OF_NEG_INF)

    def tie_index():
        r = kf - count(lambda key, pos: key > t)

        def idx_step(i, j):
            cand = j + lax.shift_left(jnp.int32(1), nbits - 1 - i)
            below = count(lambda key, pos: (key == t) & (pos <= cand - 1))
            return jnp.where(below < r, cand, j)

        return lax.fori_loop(0, nbits, idx_step, jnp.zeros((SUBLANE, nq), I32))

    no_limit = jnp.full((SUBLANE, nq), 2 ** 30, I32)
    j = lax.cond(jnp.sum(jnp.where(tied, 1.0, 0.0)) > 0.0,
                 lambda: jnp.where(tied, tie_index(), no_limit), lambda: no_limit)
    return t, j


def _selected(key, pos, t, j):
    return (key > t) | ((key == t) & (pos <= j))


def _dsa_prompt_body(cq_ref, iq0_ref, iq1_ref, iq2_ref, iq3_ref, sm_ref, ik_ref, ck_ref, cv_ref, y_ref,
                     key_ref, iqb_ref, wb_ref, qb_ref, m_ref, l_ref, acc_ref, *, k_sel, nbits, tq, tk):
    qi = pl.program_id(1)
    kpg = tk // LANE
    nkb = (qi + 1) * (tq // LANE)
    ngrp = ((qi + 1) * tq + tk - 1) // tk
    hd = HEAD_DIM
    nt = (((1,), (1,)), ((), ()))
    sub = LANE // SUBLANE
    qpos3 = qi * tq + lax.broadcasted_iota(I32, (sub, SUBLANE, tq), 2)
    kofs3 = lax.broadcasted_iota(I32, (sub, SUBLANE, tq), 0) * SUBLANE + lax.broadcasted_iota(I32, (sub, SUBLANE, tq), 1)

    w_all = sm_ref[...] * (IDX_HEADS ** -0.5 * IDX_DIM ** -0.5)
    w_t = jnp.concatenate([w_all[r:r + LANE].T for r in range(0, tq, LANE)], axis=1)
    w_off = 2 * B_HEADS
    for hh in range(IDX_HEADS):
        ref = (iq0_ref, iq1_ref, iq2_ref, iq3_ref)[hh // 4]
        iqb_ref[hh] = ref[:, (hh % 4) * IDX_DIM:(hh % 4 + 1) * IDX_DIM].astype(BF16)
        wb_ref[hh] = jnp.broadcast_to(w_t[w_off + hh:w_off + hh + 1, :], (SUBLANE, tq))

    def score_grp(g, carry):
        ikg = ik_ref[pl.ds(pl.multiple_of(g * tk, tk), tk), :].astype(BF16)
        acc = jnp.zeros((tk // SUBLANE, SUBLANE, tq), F32)
        for hh in range(IDX_HEADS):
            s = lax.dot_general(ikg, iqb_ref[hh], nt, preferred_element_type=F32)
            acc = acc + wb_ref[hh] * jnp.maximum(s, 0.0).reshape(tk // SUBLANE, SUBLANE, tq)
        for t in range(kpg):
            kpos3 = (g * kpg + t) * LANE + kofs3
            blk = acc[t * sub:(t + 1) * sub]
            key_ref[g * kpg + t] = _sort_key(jnp.where(kpos3 <= qpos3, blk, -jnp.inf))
        return carry

    lax.fori_loop(0, ngrp, score_grp, 0)

    t8, j8 = _topk_threshold_lanes(lambda b: key_ref[b], kofs3, nkb, k_sel, tq, nbits)

    for kvh in range(C_KV_HEADS):
        for g in range(C_GROUP):
            hsl = slice((kvh * C_GROUP + g) * hd, (kvh * C_GROUP + g + 1) * hd)
            qb_ref[kvh, g * tq:(g + 1) * tq, :] = cq_ref[:, hsl].astype(BF16)
    m_ref[...] = jnp.full(m_ref.shape, MASK_NEG, F32)
    l_ref[...] = jnp.zeros(l_ref.shape, F32)
    acc_ref[...] = jnp.zeros(acc_ref.shape, F32)
    ones = jnp.ones((tk, hd), BF16)

    def attend_grp(g, carry):
        bias_t = []
        for t in range(kpg):
            kpos3 = (g * kpg + t) * LANE + kofs3
            sel = _selected(key_ref[g * kpg + t], kpos3, t8, j8) & (kpos3 <= qpos3)
            kq = jnp.where(sel, 0.0, MASK_NEG).reshape(LANE, tq)
            bias_t.append(jnp.concatenate([kq[:, r:r + LANE].T for r in range(0, tq, LANE)], axis=0))
        bias = jnp.concatenate(bias_t, axis=1)
        bias3 = jnp.concatenate([bias] * C_GROUP, axis=0)
        start = pl.multiple_of(g * tk, tk)
        kgrp = ck_ref[pl.ds(start, tk), :].astype(BF16)
        vgrp = cv_ref[pl.ds(start, tk), :].astype(BF16)
        for kvh in range(C_KV_HEADS):
            sl = slice(kvh * hd, (kvh + 1) * hd)
            s = lax.dot_general(qb_ref[kvh], kgrp[:, sl], nt, preferred_element_type=F32) * (hd ** -0.5) + bias3
            m_old = m_ref[kvh]
            m_new = jnp.maximum(m_old, jnp.max(s, axis=-1, keepdims=True))
            alpha = jnp.exp(m_old - m_new)
            p = jnp.exp(s - m_new).astype(BF16)
            pv = jnp.dot(p, jnp.concatenate([vgrp[:, sl], ones], axis=1), preferred_element_type=F32)
            l_ref[kvh] = alpha * l_ref[kvh] + pv[:, hd:hd + 1]
            acc_ref[kvh] = alpha * acc_ref[kvh] + pv[:, :hd]
            m_ref[kvh] = m_new
        return carry

    lax.fori_loop(0, ngrp, attend_grp, 0)

    for kvh in range(C_KV_HEADS):
        o = acc_ref[kvh] / l_ref[kvh]
        for g in range(C_GROUP):
            hsl = slice((kvh * C_GROUP + g) * hd, (kvh * C_GROUP + g + 1) * hd)
            y_ref[:, hsl] = o[g * tq:(g + 1) * tq, :].astype(y_ref.dtype)


def _dsa_prompt(h, lay, *, n_b, seq):
    tq, tk = DSA_TQ, DSA_TK
    assert seq % tq == 0 and seq % tk == 0
    nqb = seq // tq
    k_sel = min(TOPK_MAX, seq // 4)
    nbits = max(1, (seq - 1).bit_length())
    iq_cb = lay["iq"] // (4 * IDX_DIM)
    once = pl.Buffered(1)

    def q_spec(w, cb):
        return pl.BlockSpec((tq, w), lambda b, q: (b * nqb + q, cb))

    def kv_spec(w, cb):
        return pl.BlockSpec((seq, w), lambda b, q: (b, cb), pipeline_mode=once)

    rows3 = C_GROUP * tq
    lane_pad = lambda r: r * LANE * 4
    vmem = (seq * (2 * C_KV + IDX_DIM) * 4 + 2 * tq * (C_Q + IDX_HEADS * IDX_DIM + LANE) * 4 + 2 * tq * C_Q * 2
            + (seq // LANE) * tq * LANE * 4 + IDX_HEADS * tq * LANE * 6 + C_KV_HEADS * rows3 * HEAD_DIM * 6
            + 2 * C_KV_HEADS * lane_pad(rows3) + 4 * rows3 * tk * 4)
    return pl.pallas_call(
        functools.partial(_dsa_prompt_body, k_sel=k_sel, nbits=nbits, tq=tq, tk=tk),
        out_shape=jax.ShapeDtypeStruct((n_b * seq, C_Q), BF16),
        grid=(n_b, nqb),
        in_specs=[q_spec(C_Q, lay["cq"] // C_Q)]
                 + [q_spec(4 * IDX_DIM, iq_cb + i) for i in range(4)]
                 + [q_spec(LANE, lay["small"] // LANE),
                    kv_spec(IDX_DIM, lay["ik"] // IDX_DIM),
                    kv_spec(C_KV, lay["ck"] // C_KV),
                    kv_spec(C_KV, lay["cv"] // C_KV)],
        out_specs=pl.BlockSpec((tq, C_Q), lambda b, q: (b * nqb + q, 0)),
        scratch_shapes=[pltpu.VMEM((seq // LANE, LANE // SUBLANE, SUBLANE, tq), I32),
                        pltpu.VMEM((IDX_HEADS, tq, IDX_DIM), BF16),
                        pltpu.VMEM((IDX_HEADS, SUBLANE, tq), F32),
                        pltpu.VMEM((C_KV_HEADS, rows3, HEAD_DIM), BF16),
                        pltpu.VMEM((C_KV_HEADS, rows3, 1), F32),
                        pltpu.VMEM((C_KV_HEADS, rows3, 1), F32),
                        pltpu.VMEM((C_KV_HEADS, rows3, HEAD_DIM), F32)],
        compiler_params=_cparams(("parallel", "arbitrary"), vmem + COMPILER_SCRATCH_BYTES),
        name="branch_c_prompt_dsa",
    )(h, h, h, h, h, h, h, h, h)


def _dsa_sample_select_body(pt_ref, *refs, n_pages, k_sel, nbits, t_new):
    pp = PAGES_PER_STEP
    page_refs = refs[:pp]
    iq_ref, w_ref, ikn_ref, bias_ref, key_ref, j_ref, t_ref, tmp_ref, cnt_ref = refs[pp:]
    j = pl.program_id(1)
    n_steps = n_pages // pp
    rows = t_new
    col = lax.broadcasted_iota(I32, (rows, LANE), 1)
    trow = lax.broadcasted_iota(I32, (rows, LANE), 0)
    iq = iq_ref[0].astype(BF16)

    def scores(keys_f32):
        n = keys_f32.shape[0]
        s = lax.dot_general(iq, keys_f32.astype(BF16), (((1,), (1,)), ((), ())), preferred_element_type=F32)
        r = jnp.maximum(s, 0.0) * w_ref[0]
        return jnp.sum(r.reshape(rows, IDX_HEADS, n), axis=1)

    @pl.when(j < n_steps)
    def _():
        keys = _sort_key(scores(jnp.concatenate([r[0, 0] for r in page_refs], axis=0)))
        for i in range(pp):
            key_ref[j * pp + i] = keys[:, i * LANE:(i + 1) * LANE]

    @pl.when(j == n_steps)
    def _():
        new_ok = (col <= trow) & (col < t_new)
        key_ref[n_pages] = _sort_key(jnp.where(new_ok, scores(ikn_ref[0]), -jnp.inf))
        nblk = n_pages + 1
        t = _topk_threshold(lambda b, r0, rc: key_ref[b, r0:r0 + rc, :], nblk, k_sel, rows, nbits,
                            j_ref, t_ref, tmp_ref, cnt_ref)
        tb = jnp.broadcast_to(t, (rows, LANE))
        jb = jnp.broadcast_to(j_ref[...], (rows, LANE))

        def write_blk(b, carry):
            kpos = b * LANE + col
            sel = _selected(key_ref[b], kpos, tb, jb) & ((kpos < n_pages * PAGE_SIZE) | new_ok)
            bias_ref[0, b] = jnp.where(sel, 0.0, MASK_NEG)
            return carry

        lax.fori_loop(0, nblk, write_blk, 0)


def _page_index_map(layer, i, n_pages, n_steps, trailing):
    def index_map(b, j, pt):
        step = jnp.minimum(j, n_steps - 1)
        return (layer, pt[b * n_pages + step * PAGES_PER_STEP + i]) + (0,) * trailing
    return index_map


def _dsa_sample_select(pt_flat, kidx_pool, layer, iq_s, iw_s, ik_new, *, n_b, n_pages, t_new):
    pp = PAGES_PER_STEP
    assert n_pages % pp == 0
    n_steps = n_pages // pp
    total = n_pages * PAGE_SIZE + t_new
    k_sel = min(TOPK_MAX, total // 4)
    nbits = max(1, ((n_pages + 1) * PAGE_SIZE - 1).bit_length())
    rows_q = t_new * IDX_HEADS
    grid_spec = pltpu.PrefetchScalarGridSpec(
        num_scalar_prefetch=1,
        grid=(n_b, n_steps + 1),
        in_specs=[pl.BlockSpec((1, 1, PAGE_SIZE, IDX_DIM), _page_index_map(layer, i, n_pages, n_steps, 2))
                  for i in range(pp)]
                 + [pl.BlockSpec((1, rows_q, IDX_DIM), lambda b, j, pt: (b, 0, 0)),
                    pl.BlockSpec((1, rows_q, 1), lambda b, j, pt: (b, 0, 0)),
                    pl.BlockSpec((1, PAGE_SIZE, IDX_DIM), lambda b, j, pt: (b, 0, 0))],
        out_specs=pl.BlockSpec((1, n_pages + 1, t_new, LANE), lambda b, j, pt: (b, 0, 0, 0)),
        scratch_shapes=[pltpu.VMEM((n_pages + 1, t_new, LANE), I32), pltpu.VMEM((t_new, 1), I32),
                        pltpu.VMEM((t_new, 1), I32), pltpu.VMEM((t_new, 1), I32), pltpu.VMEM((t_new, 1), F32)])
    return pl.pallas_call(
        functools.partial(_dsa_sample_select_body, n_pages=n_pages, k_sel=k_sel, nbits=nbits, t_new=t_new),
        out_shape=jax.ShapeDtypeStruct((n_b, n_pages + 1, t_new, LANE), F32),
        grid_spec=grid_spec,
        compiler_params=_cparams(("parallel", "arbitrary"), 24 << 20),
        name="branch_c_decode_select",
    )(pt_flat, *([kidx_pool] * pp), iq_s, iw_s, ik_new)


def _dsa_sample_attend_body(pt_ref, *refs, n_pages, t_new):
    pp = PAGES_PER_STEP
    k_refs, v_refs = refs[:pp], refs[pp:2 * pp]
    q_ref, bias_ref, kn_ref, vn_ref, o_ref, m_ref, l_ref, acc_ref = refs[2 * pp:]
    j = pl.program_id(1)
    n_steps = n_pages // pp
    hd = HEAD_DIM
    rq = q_ref.shape[1] // C_KV_HEADS
    reps = rq // t_new

    @pl.when(j == 0)
    def _():
        m_ref[...] = jnp.full(m_ref.shape, MASK_NEG, F32)
        l_ref[...] = jnp.zeros(l_ref.shape, F32)
        acc_ref[...] = jnp.zeros(acc_ref.shape, F32)

    def attend(k_of, v_of, bias_t):
        bias = jnp.concatenate([bias_t] * reps, axis=0)
        for kvh in range(C_KV_HEADS):
            rs = slice(kvh * rq, (kvh + 1) * rq)
            s = lax.dot_general(q_ref[0, rs, :].astype(BF16), k_of(kvh).astype(BF16), (((1,), (1,)), ((), ())),
                                preferred_element_type=F32) * (hd ** -0.5) + bias
            m_old = m_ref[rs, :]
            m_new = jnp.maximum(m_old, jnp.max(s, axis=-1, keepdims=True))
            alpha = jnp.exp(m_old - m_new)
            p = jnp.exp(s - m_new)
            l_ref[rs, :] = alpha * l_ref[rs, :] + jnp.sum(p, axis=-1, keepdims=True)
            acc_ref[rs, :] = alpha * acc_ref[rs, :] + jnp.dot(p.astype(BF16), v_of(kvh).astype(BF16),
                                                              preferred_element_type=F32)
            m_ref[rs, :] = m_new

    @pl.when(j < n_steps)
    def _():
        pages = lambda prefs: (lambda kvh: jnp.concatenate([r[0, 0, :, kvh, :] for r in prefs], axis=0))
        bias_t = jnp.concatenate([bias_ref[0, j * pp + i] for i in range(pp)], axis=1)
        attend(pages(k_refs), pages(v_refs), bias_t)

    @pl.when(j == n_steps)
    def _():
        new = lambda ref: (lambda kvh: ref[0, :, kvh * hd:(kvh + 1) * hd])
        attend(new(kn_ref), new(vn_ref), bias_ref[0, n_pages])
        o_ref[0] = acc_ref[...] / l_ref[...]


def _dsa_sample_attend(pt_flat, k_pool, v_pool, layer, q_s, bias, k_new, v_new, *, n_b, n_pages, t_new):
    pp = PAGES_PER_STEP
    n_steps = n_pages // pp
    rows = q_s.shape[1]
    page_specs = [pl.BlockSpec((1, 1, PAGE_SIZE, C_KV_HEADS, HEAD_DIM), _page_index_map(layer, i, n_pages, n_steps, 3))
                  for i in range(pp)]
    grid_spec = pltpu.PrefetchScalarGridSpec(
        num_scalar_prefetch=1,
        grid=(n_b, n_steps + 1),
        in_specs=page_specs + page_specs
                 + [pl.BlockSpec((1, rows, HEAD_DIM), lambda b, j, pt: (b, 0, 0)),
                    pl.BlockSpec((1, n_pages + 1, t_new, LANE), lambda b, j, pt: (b, 0, 0, 0)),
                    pl.BlockSpec((1, PAGE_SIZE, C_KV), lambda b, j, pt: (b, 0, 0)),
                    pl.BlockSpec((1, PAGE_SIZE, C_KV), lambda b, j, pt: (b, 0, 0))],
        out_specs=pl.BlockSpec((1, rows, HEAD_DIM), lambda b, j, pt: (b, 0, 0)),
        scratch_shapes=[pltpu.VMEM((rows, 1), F32), pltpu.VMEM((rows, 1), F32), pltpu.VMEM((rows, HEAD_DIM), F32)])
    return pl.pallas_call(
        functools.partial(_dsa_sample_attend_body, n_pages=n_pages, t_new=t_new),
        out_shape=jax.ShapeDtypeStruct((n_b, rows, HEAD_DIM), F32),
        grid_spec=grid_spec,
        compiler_params=_cparams(("parallel", "arbitrary"), 32 << 20),
        name="branch_c_decode_attend",
    )(pt_flat, *([k_pool] * pp), *([v_pool] * pp), q_s, bias, k_new, v_new)


def _dsa_sample(h_s, lay, pt_flat, k_pool, v_pool, kidx_pool, layer, *, n_b, t_new, n_pages):
    hd = HEAD_DIM
    seg = lambda name, w: h_s[:, lay[name]:lay[name] + w]
    rq = -(-C_GROUP * t_new // 16) * 16
    q = seg("cq", C_Q).reshape(n_b, t_new, C_KV_HEADS, C_GROUP, hd).transpose(0, 2, 3, 1, 4)
    q = q.reshape(n_b, C_KV_HEADS, C_GROUP * t_new, hd)
    q_s = jnp.pad(q, ((0, 0), (0, 0), (0, rq - C_GROUP * t_new), (0, 0))).reshape(n_b, C_KV_HEADS * rq, hd)
    iq_s = seg("iq", IDX_HEADS * IDX_DIM).reshape(n_b, t_new * IDX_HEADS, IDX_DIM)
    iw = h_s[:, lay["small"] + 2 * B_HEADS:lay["small"] + 2 * B_HEADS + IDX_HEADS]
    iw_s = (iw * (IDX_HEADS ** -0.5 * IDX_DIM ** -0.5)).reshape(n_b, t_new * IDX_HEADS, 1)
    padrows = lambda x: jnp.pad(x.reshape(n_b, t_new, -1), ((0, 0), (0, PAGE_SIZE - t_new), (0, 0)))
    ik_new, k_new, v_new = padrows(seg("ik", IDX_DIM)), padrows(seg("ck", C_KV)), padrows(seg("cv", C_KV))
    bias = _dsa_sample_select(pt_flat, kidx_pool, layer, iq_s, iw_s, ik_new, n_b=n_b, n_pages=n_pages, t_new=t_new)
    o = _dsa_sample_attend(pt_flat, k_pool, v_pool, layer, q_s, bias, k_new, v_new,
                           n_b=n_b, n_pages=n_pages, t_new=t_new)
    o = o.reshape(n_b, C_KV_HEADS, rq, hd)[:, :, :C_GROUP * t_new].reshape(n_b, C_KV_HEADS, C_GROUP, t_new, hd)
    return o.transpose(0, 3, 1, 2, 4).reshape(n_b * t_new, C_Q).astype(BF16)


def _merge_body(ya_ref, yb_ref, yc_ref, wa_ref, wb_ref, wc_ref, g0_ref, g1_ref, g2_ref, o_ref):
    acc = jax.nn.sigmoid(g0_ref[...]) * jnp.dot(ya_ref[...], wa_ref[...], preferred_element_type=F32)
    acc = acc + jax.nn.sigmoid(g1_ref[...]) * jnp.dot(yb_ref[...], wb_ref[...], preferred_element_type=F32)
    acc = acc + jax.nn.sigmoid(g2_ref[...]) * jnp.dot(yc_ref[...], wc_ref[...], preferred_element_type=F32)
    o_ref[...] = acc.astype(o_ref.dtype)


def _merge(ya, yb, yc, wa, wb, wc, h, lay, d_model):
    m = ya.shape[0]
    tm = _pick(m, 1024)
    tn = _pick(d_model, 512)
    gcb = lay["gate"] // tn
    per = d_model // tn

    def gate_spec(i):
        return pl.BlockSpec((tm, tn), lambda r, c: (r, gcb + i * per + c))

    kw = A_WIDTH + B_WIDTH + C_Q
    vmem = 2 * (tm * kw * 2 + kw * tn * 2 + 3 * tm * tn * 4 + tm * tn * 2) + 4 * tm * tn * 4
    return pl.pallas_call(
        _merge_body,
        out_shape=jax.ShapeDtypeStruct((m, d_model), BF16),
        grid=(m // tm, d_model // tn),
        in_specs=[pl.BlockSpec((tm, A_WIDTH), lambda r, c: (r, 0)),
                  pl.BlockSpec((tm, B_WIDTH), lambda r, c: (r, 0)),
                  pl.BlockSpec((tm, C_Q), lambda r, c: (r, 0)),
                  pl.BlockSpec((A_WIDTH, tn), lambda r, c: (0, c)),
                  pl.BlockSpec((B_WIDTH, tn), lambda r, c: (0, c)),
                  pl.BlockSpec((C_Q, tn), lambda r, c: (0, c)),
                  gate_spec(0), gate_spec(1), gate_spec(2)],
        out_specs=pl.BlockSpec((tm, tn), lambda r, c: (r, c)),
        compiler_params=_cparams(("parallel", "arbitrary"), vmem + COMPILER_SCRATCH_BYTES),
        name="gated_merge",
    )(ya, yb, yc, wa, wb, wc, h, h, h)


def _ffn_body(hn_ref, w1_ref, w3_ref, w2_ref, x_ref, o_ref):
    @pl.when(pl.program_id(1) == 0)
    def _():
        o_ref[...] = x_ref[...]

    hn = hn_ref[...]
    a = jnp.dot(hn, w1_ref[...], preferred_element_type=F32)
    b = jnp.dot(hn, w3_ref[...], preferred_element_type=F32)
    act = (jax.nn.silu(a) * b).astype(BF16)
    o_ref[...] += jnp.dot(act, w2_ref[...], preferred_element_type=F32)


def _ffn(hn, w1, w3, w2, x):
    m, d = hn.shape
    d_ff = w1.shape[1]
    tm = _pick(m, 512)
    tf = _pick(d_ff, 256)
    once = pl.Buffered(1)
    vmem = tm * d * 2 + tm * d * 4 + 2 * tm * d * 4 + 2 * 3 * d * tf * 2 + 4 * tm * tf * 4 + tm * d * 4
    return pl.pallas_call(
        _ffn_body,
        out_shape=jax.ShapeDtypeStruct((m, d), F32),
        grid=(m // tm, d_ff // tf),
        in_specs=[pl.BlockSpec((tm, d), lambda i, f: (i, 0), pipeline_mode=once),
                  pl.BlockSpec((d, tf), lambda i, f: (0, f)),
                  pl.BlockSpec((d, tf), lambda i, f: (0, f)),
                  pl.BlockSpec((tf, d), lambda i, f: (f, 0)),
                  pl.BlockSpec((tm, d), lambda i, f: (i, 0), pipeline_mode=once)],
        out_specs=pl.BlockSpec((tm, d), lambda i, f: (i, 0)),
        compiler_params=_cparams(("parallel", "arbitrary"), vmem + COMPILER_SCRATCH_BYTES),
        name="swiglu_ffn",
    )(hn, w1, w3, w2, x)


def _dense_front(x, ln1, w_in_p):
    xn = _rmsnorm(x, ln1, BF16)
    return _matmul(xn, w_in_p, tn_pref=768, name="in_proj")


def _dense_back(x, ya, yb, yc, h, lay, wts, d_model):
    wa, wb, wc, w_o, ln2, w1, w3, w2 = wts
    mix = _merge(ya, yb, yc, wa, wb, wc, h, lay, d_model)
    x = _matmul(mix, w_o, residual=x, name="out_proj")
    hn = _rmsnorm(x, ln2, BF16)
    return _ffn(hn, w1, w3, w2, x)


def kernel(x_prompt, x_sample, cache_k, cache_v, cache_kidx, state_conv, state_delta, page_table, ln1, w_in,
           a_ln_g, a_ln_b, a_ws, a_bs, b_conv_w, b_a_log, b_dt_bias, b_out_g, w_br, w_o, ln2, ffn_w1, ffn_w3,
           ffn_w2, ln_f):
    n_bp, seq, d_model = x_prompt.shape
    n_bs, t_new, _ = x_sample.shape
    depth = ln1.shape[0]
    n_phys = cache_k.shape[1]
    n_pages = page_table.shape[1]
    assert seq % A_CHUNK == 0 and seq % GDN_CHUNK == 0 and CONV_W - 1 <= t_new <= min(A_CHUNK, GDN_CHUNK)
    lay = _layout(d_model)
    hd = HEAD_DIM

    xp = x_prompt.reshape(n_bp * seq, d_model)
    xs = x_sample.reshape(n_bs * t_new, d_model)
    pt_flat = page_table.reshape(-1).astype(I32)
    conv0_p = jnp.zeros((n_bp, SUBLANE, 3 * B_WIDTH), F32)
    s0_p = jnp.zeros((n_bp, B_HEADS, hd, hd), F32)
    qkv_off = lay["bq"]

    outs = {k: [] for k in ("pk", "pv", "pik", "pconv", "pdelta", "sk", "sv", "sik", "sconv", "sdelta", "schunk")}
    for l in range(depth):
        w_in_p = _pack_w_in(w_in[l], d_model)
        wts = (w_br[l, :A_WIDTH].astype(BF16), w_br[l, A_WIDTH:A_WIDTH + B_WIDTH].astype(BF16),
               w_br[l, A_WIDTH + B_WIDTH:].astype(BF16), w_o[l].astype(BF16), ln2[l],
               ffn_w1[l].astype(BF16), ffn_w3[l].astype(BF16), ffn_w2[l].astype(BF16))

        h = _dense_front(xp, ln1[l], w_in_p)
        ya = _branch_a(h, lay, a_ws[l], a_bs[l], a_ln_g[l], a_ln_b[l], chunk=A_CHUNK, tb=A_CHUNK, emit_av=False)
        srcs = [(h, lay[n] // B_WIDTH) for n in ("bq", "bk", "bv", "bz")] + [(h, lay["small"] // LANE)]
        yb, s_new = _gdn(srcs, conv0_p, s0_p, b_conv_w[l], b_a_log[l], b_dt_bias[l], b_out_g[l],
                         n_b=n_bp, t_pad=seq, t_valid=GDN_CHUNK, chunk=GDN_CHUNK)
        yc = _dsa_prompt(h, lay, n_b=n_bp, seq=seq)
        h3 = h.reshape(n_bp, seq, lay["total"])
        outs["pk"].append(h3[:, :, lay["ck"]:lay["ck"] + C_KV].reshape(n_bp, seq, C_KV_HEADS, hd))
        outs["pv"].append(h3[:, :, lay["cv"]:lay["cv"] + C_KV].reshape(n_bp, seq, C_KV_HEADS, hd))
        outs["pik"].append(h3[:, :, lay["ik"]:lay["ik"] + IDX_DIM])
        outs["pconv"].append(h3[:, seq - (CONV_W - 1):, qkv_off:qkv_off + 3 * B_WIDTH])
        outs["pdelta"].append(s_new)
        xp = _dense_back(xp, ya, yb, yc, h, lay, wts, d_model)

        hs = _dense_front(xs, ln1[l], w_in_p)
        ya, av = _branch_a(hs, lay, a_ws[l], a_bs[l], a_ln_g[l], a_ln_b[l], chunk=t_new, tb=n_bs * t_new,
                           emit_av=True)
        hs3 = hs.reshape(n_bs, t_new, lay["total"])
        padt = lambda x: jnp.pad(x, ((0, 0), (0, GDN_CHUNK - t_new), (0, 0))).reshape(n_bs * GDN_CHUNK, -1)
        srcs = [(padt(hs3[:, :, lay[n]:lay[n] + B_WIDTH]), 0) for n in ("bq", "bk", "bv", "bz")]
        srcs.append((padt(hs3[:, :, lay["small"]:lay["small"] + LANE]), 0))
        conv0_s = jnp.pad(state_conv[l], ((0, 0), (SUBLANE - (CONV_W - 1), 0), (0, 0)))
        yb, s_new = _gdn(srcs, conv0_s, state_delta[l], b_conv_w[l], b_a_log[l], b_dt_bias[l], b_out_g[l],
                         n_b=n_bs, t_pad=GDN_CHUNK, t_valid=t_new, chunk=GDN_CHUNK)
        yb = yb.reshape(n_bs, GDN_CHUNK, B_WIDTH)[:, :t_new].reshape(n_bs * t_new, B_WIDTH)
        yc = _dsa_sample(hs, lay, pt_flat, cache_k, cache_v, cache_kidx, l, n_b=n_bs, t_new=t_new, n_pages=n_pages)
        outs["sk"].append(hs3[:, :, lay["ck"]:lay["ck"] + C_KV].reshape(n_bs, t_new, C_KV_HEADS, hd))
        outs["sv"].append(hs3[:, :, lay["cv"]:lay["cv"] + C_KV].reshape(n_bs, t_new, C_KV_HEADS, hd))
        outs["sik"].append(hs3[:, :, lay["ik"]:lay["ik"] + IDX_DIM])
        outs["sconv"].append(hs3[:, t_new - (CONV_W - 1):, qkv_off:qkv_off + 3 * B_WIDTH])
        outs["sdelta"].append(s_new)
        outs["schunk"].append(av.reshape(n_bs, t_new, A_WIDTH))
        xs = _dense_back(xs, ya, yb, yc, hs, lay, wts, d_model)

    y_prompt = _rmsnorm(xp, ln_f, F32).reshape(n_bp, seq, d_model)
    y_sample = _rmsnorm(xs, ln_f, F32).reshape(n_bs, t_new, d_model)
    st = lambda k: jnp.stack(outs[k])
    return (y_prompt, y_sample, st("pk"), st("pv"), st("pik"), st("pconv"), st("pdelta"),
            st("sk"), st("sv"), st("sik"), st("sconv"), st("sdelta"), st("schunk"))
```

```python
import functools

import jax
import jax.numpy as jnp
from jax import lax
from jax.experimental import pallas as pl
from jax.experimental.pallas import tpu as pltpu

F32 = jnp.float32
BF16 = jnp.bfloat16
I32 = jnp.int32

HEAD_DIM = 128
A_GROUPS = 8
A_CHUNK = 128
A_WIDTH = A_GROUPS * HEAD_DIM
B_HEADS = 12
B_WIDTH = B_HEADS * HEAD_DIM
CONV_W = 4
C_HEADS = 12
C_KV_HEADS = 4
C_GROUP = C_HEADS // C_KV_HEADS
C_Q = C_HEADS * HEAD_DIM
C_KV = C_KV_HEADS * HEAD_DIM
IDX_HEADS = 16
IDX_DIM = 128
TOPK_MAX = 256
PAGE_SIZE = 128
N_BRANCH = 3
RMS_EPS = 1e-6
LN_EPS = 1e-5

LANE = 128
SUBLANE = 8
V7X_VMEM_BYTES = 64 * 1024 * 1024
VMEM_BUDGET = 56 * 1024 * 1024
COMPILER_SCRATCH_BYTES = 8 * 1024 * 1024

GDN_CHUNK = 128
DSA_TQ = 256
DSA_TK = 512
PAGES_PER_STEP = 8
MASK_NEG = -1e30
INT_MIN = -2147483648
KEY_OF_NEG_INF = -2139095041


def _cparams(semantics, vmem_bytes):
    return pltpu.CompilerParams(dimension_semantics=semantics,
                                vmem_limit_bytes=int(min(max(vmem_bytes, 16 * 1024 * 1024), VMEM_BUDGET)))


def _pick(dim, pref):
    t = pref
    while t >= SUBLANE:
        if dim % t == 0:
            return t
        t //= 2
    return dim


def _layout(d_model):
    segs = [("gate", N_BRANCH * d_model, d_model), ("bq", B_WIDTH, B_WIDTH), ("bk", B_WIDTH, B_WIDTH),
            ("bv", B_WIDTH, B_WIDTH), ("bz", B_WIDTH, B_WIDTH), ("cq", C_Q, C_Q), ("ck", C_KV, C_KV),
            ("au", A_WIDTH, A_WIDTH), ("av", A_WIDTH, A_WIDTH), ("cv", C_KV, C_KV),
            ("iq", IDX_HEADS * IDX_DIM, 4 * IDX_DIM), ("ik", IDX_DIM, IDX_DIM), ("small", LANE, LANE)]
    off, lay = 0, {}
    for name, width, align in segs:
        assert off % align == 0, (name, off, align)
        lay[name] = off
        off += width
    lay["total"] = off
    return lay


def _pack_body(w_ref, o_ref, *, moves, used):
    for src, dst, width in moves:
        o_ref[:, dst:dst + width] = w_ref[:, src:src + width].astype(o_ref.dtype)
    o_ref[:, used:] = jnp.zeros((o_ref.shape[0], o_ref.shape[1] - used), o_ref.dtype)


def _pack_w_in(w_in, d_model):
    depth, d, in_width = w_in.shape
    lay = _layout(d_model)
    widths = (A_WIDTH, A_WIDTH, 3 * B_WIDTH, B_WIDTH, B_HEADS, B_HEADS, C_Q, C_KV, C_KV,
              IDX_HEADS * IDX_DIM, IDX_HEADS, IDX_DIM, N_BRANCH * d_model)
    dsts = (lay["au"], lay["av"], lay["bq"], lay["bz"], lay["small"], lay["small"] + B_HEADS, lay["cq"], lay["ck"],
            lay["cv"], lay["iq"], lay["small"] + 2 * B_HEADS, lay["ik"], lay["gate"])
    moves, start = [], 0
    for width, dst in zip(widths, dsts):
        moves.append((start, dst, width))
        start += width
    assert start == in_width
    used = lay["small"] + 2 * B_HEADS + IDX_HEADS
    tr = _pick(d, 32)
    return pl.pallas_call(
        functools.partial(_pack_body, moves=tuple(moves), used=used),
        out_shape=jax.ShapeDtypeStruct((depth, d, lay["total"]), BF16),
        grid=(depth, d // tr),
        in_specs=[pl.BlockSpec((None, tr, in_width), lambda l, i: (l, i, 0))],
        out_specs=pl.BlockSpec((None, tr, lay["total"]), lambda l, i: (l, i, 0)),
        compiler_params=_cparams(("parallel", "parallel"), 4 * tr * lay["total"] * 6),
        name="pack_in_proj_weight",
    )(w_in)


def _rmsnorm_body(x_ref, g_ref, o_ref):
    x = x_ref[...]
    ms = jnp.mean(x * x, axis=-1, keepdims=True)
    o_ref[...] = (x * lax.rsqrt(ms + RMS_EPS) * g_ref[...]).astype(o_ref.dtype)


def _rmsnorm(x, g, out_dtype):
    m, d = x.shape
    tm = _pick(m, 256)
    return pl.pallas_call(
        _rmsnorm_body,
        out_shape=jax.ShapeDtypeStruct((m, d), out_dtype),
        grid=(m // tm,),
        in_specs=[pl.BlockSpec((tm, d), lambda i: (i, 0)), pl.BlockSpec((1, d), lambda i: (0, 0))],
        out_specs=pl.BlockSpec((tm, d), lambda i: (i, 0)),
        compiler_params=_cparams(("parallel",), 6 * tm * d * 4),
        name="rmsnorm",
    )(x, g.reshape(1, d))


def _mm_body(a_ref, w_ref, o_ref):
    o_ref[...] = jnp.dot(a_ref[...], w_ref[...], preferred_element_type=F32).astype(o_ref.dtype)


def _mm_res_body(a_ref, w_ref, r_ref, o_ref):
    o_ref[...] = (r_ref[...] + jnp.dot(a_ref[...], w_ref[...], preferred_element_type=F32)).astype(o_ref.dtype)


def _layer_spec(block, index_map, layer):
    if layer is None:
        return pl.BlockSpec(block, index_map)
    return pl.BlockSpec((None,) + block, lambda *g: (layer,) + index_map(*g))


def _matmul(a, w, *, layer=None, residual=None, out_dtype=F32, tm_pref=1024, tn_pref=1024, name="matmul"):
    m, k = a.shape
    n = w.shape[-1]
    tm = _pick(m, tm_pref)
    tn = tn_pref if n % tn_pref == 0 else _pick(n, tn_pref)
    osz = jnp.dtype(out_dtype).itemsize
    vmem = 2 * (tm * k * 2 + k * tn * 2 + tm * tn * osz) + tm * tn * 4
    in_specs = [pl.BlockSpec((tm, k), lambda i, j: (i, 0)), _layer_spec((k, tn), lambda i, j: (0, j), layer)]
    args = [a, w]
    body = _mm_body
    if residual is not None:
        in_specs.append(pl.BlockSpec((tm, tn), lambda i, j: (i, j)))
        args.append(residual)
        body = _mm_res_body
        vmem += 2 * tm * tn * 4
    return pl.pallas_call(
        body,
        out_shape=jax.ShapeDtypeStruct((m, n), out_dtype),
        grid=(m // tm, n // tn),
        in_specs=in_specs,
        out_specs=pl.BlockSpec((tm, tn), lambda i, j: (i, j)),
        compiler_params=_cparams(("parallel", "arbitrary"), vmem + COMPILER_SCRATCH_BYTES),
        name=name,
    )(*args)


def _branch_a_body(u_ref, v_ref, w_ref, bs_ref, g_ref, b_ref, y_ref, *av_ref, chunk):
    tb = u_ref.shape[0]
    u = jax.nn.gelu(u_ref[...])
    v = jax.nn.gelu(v_ref[...])
    mu = jnp.mean(v, axis=-1, keepdims=True)
    var = jnp.mean(jnp.square(v - mu), axis=-1, keepdims=True)
    vn = (v - mu) * lax.rsqrt(var + LN_EPS) * g_ref[...] + b_ref[...]
    if av_ref:
        av_ref[0][...] = vn
    row = lax.broadcasted_iota(I32, (tb, tb), 0)
    col = lax.broadcasted_iota(I32, (tb, tb), 1)
    keep = (col <= row) & ((row // chunk) == (col // chunk))
    vb = vn.astype(BF16)
    for g in range(A_GROUPS):
        sl = slice(g * HEAD_DIM, (g + 1) * HEAD_DIM)
        wg = jnp.where(keep, w_ref[g], 0.0).astype(BF16)
        s = jnp.dot(wg, vb[:, sl], preferred_element_type=F32) + bs_ref[:, g:g + 1]
        y_ref[:, sl] = (u[:, sl] * s).astype(y_ref.dtype)


def _branch_a(h, lay, a_ws, a_bs, ln_g, ln_b, *, chunk, tb, emit_av):
    m = h.shape[0]
    reps = tb // chunk
    wfull = jnp.tile(a_ws[:, :chunk, :chunk], (1, reps, reps))
    bs_t = jnp.tile(a_bs[:, :chunk].T, (reps, 1))
    cu, cv = lay["au"] // A_WIDTH, lay["av"] // A_WIDTH
    out_shape = [jax.ShapeDtypeStruct((m, A_WIDTH), BF16)]
    out_specs = [pl.BlockSpec((tb, A_WIDTH), lambda i: (i, 0))]
    if emit_av:
        out_shape.append(jax.ShapeDtypeStruct((m, A_WIDTH), F32))
        out_specs.append(pl.BlockSpec((tb, A_WIDTH), lambda i: (i, 0)))
    res = pl.pallas_call(
        functools.partial(_branch_a_body, chunk=chunk),
        out_shape=out_shape,
        grid=(m // tb,),
        in_specs=[pl.BlockSpec((tb, A_WIDTH), lambda i: (i, cu)),
                  pl.BlockSpec((tb, A_WIDTH), lambda i: (i, cv)),
                  pl.BlockSpec((A_GROUPS, tb, tb), lambda i: (0, 0, 0)),
                  pl.BlockSpec((tb, A_GROUPS), lambda i: (0, 0)),
                  pl.BlockSpec((1, A_WIDTH), lambda i: (0, 0)),
                  pl.BlockSpec((1, A_WIDTH), lambda i: (0, 0))],
        out_specs=out_specs,
        compiler_params=_cparams(("parallel",), 32 << 20),
        name="branch_a_gmlp",
    )(h, h, wfull, bs_t, ln_g.reshape(1, A_WIDTH), ln_b.reshape(1, A_WIDTH))
    return res if emit_av else res[0]


def _bdot(a, b):
    return lax.dot_general(a.astype(BF16), b.astype(BF16), (((2,), (1,)), ((0,), (0,))),
                           preferred_element_type=F32)


def _bdot_nt(a, b):
    return lax.dot_general(a.astype(BF16), b.astype(BF16), (((2,), (2,)), ((0,), (0,))),
                           preferred_element_type=F32)


def _transpose_rows(x):
    r = x.shape[0]
    if r < LANE:
        x = jnp.concatenate([x, jnp.zeros((LANE - r, LANE), x.dtype)], axis=0)
    return x.T[:, :r]


def _gdn_body(q_ref, k_ref, v_ref, z_ref, sm_ref, c0_ref, s0_ref, cw_ref, al_ref, dt_ref, og_ref,
              y_ref, sout_ref, ext_ref, s_ref, *, chunk, t_valid):
    c = pl.program_id(1)
    nc = pl.num_programs(1)
    hd = HEAD_DIM

    @pl.when(c == 0)
    def _():
        s_ref[...] = s0_ref[0]
        for j in range(3):
            ext_ref[j, 0:SUBLANE, :] = c0_ref[0, :, j * B_WIDTH:(j + 1) * B_WIDTH]

    acts = []
    for j, ref in enumerate((q_ref, k_ref, v_ref)):
        ext_ref[j, SUBLANE:SUBLANE + chunk, :] = ref[...]
        acc = None
        for i in range(CONV_W):
            lo = SUBLANE - (CONV_W - 1) + i
            term = ext_ref[j, lo:lo + chunk, :] * cw_ref[i:i + 1, j * B_WIDTH:(j + 1) * B_WIDTH]
            acc = term if acc is None else acc + term
        acts.append(jax.nn.silu(acc))
        ext_ref[j, 0:SUBLANE, :] = ext_ref[j, chunk:chunk + SUBLANE, :]
    qa, ka, va = acts

    sm = sm_ref[...]
    row1 = lax.broadcasted_iota(I32, (chunk, LANE), 0)
    g_all = -jnp.exp(al_ref[...]) * jax.nn.softplus(sm + dt_ref[...])
    beta_all = jax.nn.sigmoid(sm)
    if t_valid < chunk:
        g_all = jnp.where(row1 < t_valid, g_all, 0.0)
        beta_all = jnp.where(row1 < t_valid, beta_all, 0.0)
    gc_all = g_all
    d = 1
    while d < chunk:
        gc_all = gc_all + jnp.where(row1 >= d, pltpu.roll(gc_all, d, 0), 0.0)
        d *= 2
    gc_t = _transpose_rows(gc_all)

    row = lax.broadcasted_iota(I32, (chunk, chunk), 0)
    col = lax.broadcasted_iota(I32, (chunk, chunk), 1)
    incl = row >= col
    strict = row > col
    eye = jnp.where(row == col, 1.0, 0.0)
    pair_masks = []
    bs = 1
    while bs < chunk:
        pair_masks.append(((row // bs) % 2 == 1) & ((col // bs) == (row // bs) - 1))
        bs *= 2

    heads = range(B_HEADS)
    per_head = lambda x: jnp.stack([x[:, h * hd:(h + 1) * hd] for h in heads], axis=0)
    q3, k3, v3 = per_head(qa), per_head(ka), per_head(va)
    qn = q3 * lax.rsqrt(jnp.sum(q3 * q3, axis=-1, keepdims=True) + 1e-6) * (hd ** -0.5)
    kn = k3 * lax.rsqrt(jnp.sum(k3 * k3, axis=-1, keepdims=True) + 1e-6)
    beta = jnp.stack([beta_all[:, B_HEADS + h:B_HEADS + h + 1] for h in heads], axis=0)
    gcol = jnp.stack([gc_all[:, h:h + 1] for h in heads], axis=0)
    grow = jnp.stack([gc_t[h:h + 1, :] for h in heads], axis=0)
    decay = jnp.exp(jnp.where(incl[None], gcol - grow, -jnp.inf))
    kb = kn * beta
    eg = jnp.exp(gcol)
    lmat = jnp.where(strict[None], _bdot_nt(kb, kn) * decay, 0.0)
    attn = _bdot_nt(qn, kn) * decay
    tinv = eye[None] - jnp.where(pair_masks[0][None], lmat, 0.0)
    for pm in pair_masks[1:]:
        tinv = tinv - _bdot(tinv, _bdot(jnp.where(pm[None], lmat, 0.0), tinv))
    sol = _bdot(tinv, jnp.concatenate([v3 * beta, kb * eg], axis=-1))
    value, kcd = sol[:, :, :hd], sol[:, :, hd:]
    s_old = s_ref[...]
    vnew = value - _bdot(kcd, s_old)
    o = _bdot(qn * eg, s_old) + _bdot(attn, vnew)
    glast = gcol[:, chunk - 1:chunk, :]
    kend = kn * jnp.exp(glast - gcol)
    s_ref[...] = s_old * jnp.exp(glast) + _bdot(jnp.swapaxes(kend, 1, 2), vnew)
    on = o * lax.rsqrt(jnp.mean(o * o, axis=-1, keepdims=True) + RMS_EPS) * og_ref[...]
    for h in heads:
        sl = slice(h * hd, (h + 1) * hd)
        y_ref[:, sl] = (on[h] * jax.nn.silu(z_ref[:, sl])).astype(y_ref.dtype)

    @pl.when(c == nc - 1)
    def _():
        sout_ref[0] = s_ref[...]


def _gdn(srcs, conv0, s0, conv_w, a_log, dt_bias, o_g, *, n_b, t_pad, t_valid, chunk):
    nc = t_pad // chunk
    arrs = [a for a, _ in srcs]
    cbs = [cb for _, cb in srcs]
    widths = [B_WIDTH] * 4 + [LANE]

    def tok_spec(w, cb):
        return pl.BlockSpec((chunk, w), lambda b, c: (b * nc + c, cb))

    pad12 = lambda x: jnp.zeros((1, LANE), F32).at[0, :B_HEADS].set(x)
    return pl.pallas_call(
        functools.partial(_gdn_body, chunk=chunk, t_valid=t_valid),
        out_shape=[jax.ShapeDtypeStruct((n_b * t_pad, B_WIDTH), BF16),
                   jax.ShapeDtypeStruct((n_b, B_HEADS, HEAD_DIM, HEAD_DIM), F32)],
        grid=(n_b, nc),
        in_specs=[tok_spec(w, cb) for w, cb in zip(widths, cbs)] + [
            pl.BlockSpec((1, SUBLANE, 3 * B_WIDTH), lambda b, c: (b, 0, 0)),
            pl.BlockSpec((1, B_HEADS, HEAD_DIM, HEAD_DIM), lambda b, c: (b, 0, 0, 0)),
            pl.BlockSpec((CONV_W, 3 * B_WIDTH), lambda b, c: (0, 0)),
            pl.BlockSpec((1, LANE), lambda b, c: (0, 0)),
            pl.BlockSpec((1, LANE), lambda b, c: (0, 0)),
            pl.BlockSpec((1, HEAD_DIM), lambda b, c: (0, 0))],
        out_specs=[pl.BlockSpec((chunk, B_WIDTH), lambda b, c: (b * nc + c, 0)),
                   pl.BlockSpec((1, B_HEADS, HEAD_DIM, HEAD_DIM), lambda b, c: (b, 0, 0, 0))],
        scratch_shapes=[pltpu.VMEM((3, chunk + SUBLANE, B_WIDTH), F32),
                        pltpu.VMEM((B_HEADS, HEAD_DIM, HEAD_DIM), F32)],
        compiler_params=_cparams(("parallel", "arbitrary"), 40 << 20),
        name="branch_b_gated_delta",
    )(*arrs, conv0, s0, conv_w, pad12(a_log), pad12(dt_bias), o_g.reshape(1, HEAD_DIM))


def _sort_key(x):
    b = lax.bitcast_convert_type(x + 0.0, I32)
    return b ^ ((b >> 31) & 0x7FFFFFFF)


def _select_rule(count, shape, k, nbits):
    kf = jnp.float32(k)
    count_ge = lambda cand: count(lambda key, pos: key >= cand)
    t0 = jnp.where(count_ge(jnp.zeros(shape, I32)) >= kf, 0, INT_MIN).astype(I32)

    def bit_step(i, t):
        cand = t + lax.shift_left(jnp.int32(1), 30 - i)
        return jnp.where(count_ge(cand) >= kf, cand, t)

    t = lax.fori_loop(0, 31, bit_step, t0)
    n_ge = count_ge(t)
    tied = (n_ge > kf) & (t > KEY_OF_NEG_INF)

    def tie_index():
        r = kf - count(lambda key, pos: key > t)

        def idx_step(i, j):
            cand = j + lax.shift_left(jnp.int32(1), nbits - 1 - i)
            below = count(lambda key, pos: (key == t) & (pos <= cand - 1))
            return jnp.where(below < r, cand, j)

        return lax.fori_loop(0, nbits, idx_step, jnp.zeros(shape, I32))

    no_limit = jnp.full(shape, 2 ** 30, I32)
    j = lax.cond(jnp.sum(jnp.where(tied, 1.0, 0.0)) > 0.0,
                 lambda: jnp.where(tied, tie_index(), no_limit), lambda: no_limit)
    return t, j


def _selected(key, pos, t, j):
    return (key > t) | ((key == t) & (pos <= j))


def _dsa_prompt_body(cq_ref, iq0_ref, iq1_ref, iq2_ref, iq3_ref, sm_ref, ik_ref, ck_ref, cv_ref, y_ref,
                     key_ref, iqb_ref, wb_ref, qb_ref, m_ref, l_ref, acc_ref, *, k_sel, nbits, tq, tk):
    qi = pl.program_id(1)
    kpg = tk // LANE
    nkb = (qi + 1) * (tq // LANE)
    ngrp = ((qi + 1) * tq + tk - 1) // tk
    hd = HEAD_DIM
    nt = (((1,), (1,)), ((), ()))
    sub = LANE // SUBLANE
    qpos3 = qi * tq + lax.broadcasted_iota(I32, (sub, SUBLANE, tq), 2)
    kofs3 = lax.broadcasted_iota(I32, (sub, SUBLANE, tq), 0) * SUBLANE + lax.broadcasted_iota(I32, (sub, SUBLANE, tq), 1)

    w_all = sm_ref[...] * (IDX_HEADS ** -0.5 * IDX_DIM ** -0.5)
    w_t = jnp.concatenate([w_all[r:r + LANE].T for r in range(0, tq, LANE)], axis=1)
    w_off = 2 * B_HEADS
    for hh in range(IDX_HEADS):
        ref = (iq0_ref, iq1_ref, iq2_ref, iq3_ref)[hh // 4]
        iqb_ref[hh] = ref[:, (hh % 4) * IDX_DIM:(hh % 4 + 1) * IDX_DIM].astype(BF16)
        wb_ref[hh] = jnp.broadcast_to(w_t[w_off + hh:w_off + hh + 1, :], (SUBLANE, tq))

    def score_grp(g, carry):
        ikg = ik_ref[pl.ds(pl.multiple_of(g * tk, tk), tk), :].astype(BF16)
        acc = jnp.zeros((tk // SUBLANE, SUBLANE, tq), F32)
        for hh in range(IDX_HEADS):
            s = lax.dot_general(ikg, iqb_ref[hh], nt, preferred_element_type=F32)
            acc = acc + wb_ref[hh] * jnp.maximum(s, 0.0).reshape(tk // SUBLANE, SUBLANE, tq)
        for t in range(kpg):
            kpos3 = (g * kpg + t) * LANE + kofs3
            blk = acc[t * sub:(t + 1) * sub]
            key_ref[g * kpg + t] = _sort_key(jnp.where(kpos3 <= qpos3, blk, -jnp.inf))
        return carry

    lax.fori_loop(0, ngrp, score_grp, 0)

    def count(pred):
        def blk(b, acc):
            return acc + jnp.sum(jnp.where(pred(key_ref[b], b * LANE + kofs3), 1.0, 0.0), axis=0)
        acc = lax.fori_loop(0, nkb, blk, jnp.zeros((SUBLANE, tq), F32))
        return jnp.broadcast_to(jnp.sum(acc, axis=0, keepdims=True), (SUBLANE, tq))

    t8, j8 = _select_rule(count, (SUBLANE, tq), k_sel, nbits)

    for kvh in range(C_KV_HEADS):
        for g in range(C_GROUP):
            hsl = slice((kvh * C_GROUP + g) * hd, (kvh * C_GROUP + g + 1) * hd)
            qb_ref[kvh, g * tq:(g + 1) * tq, :] = cq_ref[:, hsl].astype(BF16)
    m_ref[...] = jnp.full(m_ref.shape, MASK_NEG, F32)
    l_ref[...] = jnp.zeros(l_ref.shape, F32)
    acc_ref[...] = jnp.zeros(acc_ref.shape, F32)
    ones = jnp.ones((tk, hd), BF16)

    def attend_grp(g, carry):
        bias_t = []
        for t in range(kpg):
            kpos3 = (g * kpg + t) * LANE + kofs3
            sel = _selected(key_ref[g * kpg + t], kpos3, t8, j8) & (kpos3 <= qpos3)
            kq = jnp.where(sel, 0.0, MASK_NEG).reshape(LANE, tq)
            bias_t.append(jnp.concatenate([kq[:, r:r + LANE].T for r in range(0, tq, LANE)], axis=0))
        bias = jnp.concatenate(bias_t, axis=1)
        bias3 = jnp.concatenate([bias] * C_GROUP, axis=0)
        start = pl.multiple_of(g * tk, tk)
        kgrp = ck_ref[pl.ds(start, tk), :].astype(BF16)
        vgrp = cv_ref[pl.ds(start, tk), :].astype(BF16)
        for kvh in range(C_KV_HEADS):
            sl = slice(kvh * hd, (kvh + 1) * hd)
            s = lax.dot_general(qb_ref[kvh], kgrp[:, sl], nt, preferred_element_type=F32) * (hd ** -0.5) + bias3
            m_old = m_ref[kvh]
            m_new = jnp.maximum(m_old, jnp.max(s, axis=-1, keepdims=True))
            alpha = jnp.exp(m_old - m_new)
            p = jnp.exp(s - m_new).astype(BF16)
            pv = jnp.dot(p, jnp.concatenate([vgrp[:, sl], ones], axis=1), preferred_element_type=F32)
            l_ref[kvh] = alpha * l_ref[kvh] + pv[:, hd:hd + 1]
            acc_ref[kvh] = alpha * acc_ref[kvh] + pv[:, :hd]
            m_ref[kvh] = m_new
        return carry

    lax.fori_loop(0, ngrp, attend_grp, 0)

    for kvh in range(C_KV_HEADS):
        o = acc_ref[kvh] / l_ref[kvh]
        for g in range(C_GROUP):
            hsl = slice((kvh * C_GROUP + g) * hd, (kvh * C_GROUP + g + 1) * hd)
            y_ref[:, hsl] = o[g * tq:(g + 1) * tq, :].astype(y_ref.dtype)


def _dsa_prompt(h, lay, *, n_b, seq):
    tq, tk = DSA_TQ, DSA_TK
    assert seq % tq == 0 and seq % tk == 0
    nqb = seq // tq
    k_sel = min(TOPK_MAX, seq // 4)
    nbits = max(1, (seq - 1).bit_length())
    iq_cb = lay["iq"] // (4 * IDX_DIM)
    once = pl.Buffered(1)

    def q_spec(w, cb):
        return pl.BlockSpec((tq, w), lambda b, q: (b * nqb + q, cb))

    def kv_spec(w, cb):
        return pl.BlockSpec((seq, w), lambda b, q: (b, cb), pipeline_mode=once)

    rows3 = C_GROUP * tq
    lane_pad = lambda r: r * LANE * 4
    vmem = (seq * (2 * C_KV + IDX_DIM) * 4 + 2 * tq * (C_Q + IDX_HEADS * IDX_DIM + LANE) * 4 + 2 * tq * C_Q * 2
            + (seq // LANE) * tq * LANE * 4 + IDX_HEADS * tq * LANE * 6 + C_KV_HEADS * rows3 * HEAD_DIM * 6
            + 2 * C_KV_HEADS * lane_pad(rows3) + 4 * rows3 * tk * 4)
    return pl.pallas_call(
        functools.partial(_dsa_prompt_body, k_sel=k_sel, nbits=nbits, tq=tq, tk=tk),
        out_shape=jax.ShapeDtypeStruct((n_b * seq, C_Q), BF16),
        grid=(n_b, nqb),
        in_specs=[q_spec(C_Q, lay["cq"] // C_Q)]
                 + [q_spec(4 * IDX_DIM, iq_cb + i) for i in range(4)]
                 + [q_spec(LANE, lay["small"] // LANE),
                    kv_spec(IDX_DIM, lay["ik"] // IDX_DIM),
                    kv_spec(C_KV, lay["ck"] // C_KV),
                    kv_spec(C_KV, lay["cv"] // C_KV)],
        out_specs=pl.BlockSpec((tq, C_Q), lambda b, q: (b * nqb + q, 0)),
        scratch_shapes=[pltpu.VMEM((seq // LANE, LANE // SUBLANE, SUBLANE, tq), I32),
                        pltpu.VMEM((IDX_HEADS, tq, IDX_DIM), BF16),
                        pltpu.VMEM((IDX_HEADS, SUBLANE, tq), F32),
                        pltpu.VMEM((C_KV_HEADS, rows3, HEAD_DIM), BF16),
                        pltpu.VMEM((C_KV_HEADS, rows3, 1), F32),
                        pltpu.VMEM((C_KV_HEADS, rows3, 1), F32),
                        pltpu.VMEM((C_KV_HEADS, rows3, HEAD_DIM), F32)],
        compiler_params=_cparams(("parallel", "arbitrary"), vmem + COMPILER_SCRATCH_BYTES),
        name="branch_c_prompt_dsa",
    )(h, h, h, h, h, h, h, h, h)


def _dsa_sample_select_body(pt_ref, *refs, n_pages, k_sel, nbits, t_new):
    pp = PAGES_PER_STEP
    page_refs = refs[:pp]
    iq_ref, w_ref, ikn_ref, bias_ref, key_ref = refs[pp:]
    j = pl.program_id(1)
    n_steps = n_pages // pp
    rows = t_new
    col = lax.broadcasted_iota(I32, (rows, LANE), 1)
    trow = lax.broadcasted_iota(I32, (rows, LANE), 0)
    iq = iq_ref[0].astype(BF16)

    def scores(keys_f32):
        n = keys_f32.shape[0]
        s = lax.dot_general(iq, keys_f32.astype(BF16), (((1,), (1,)), ((), ())), preferred_element_type=F32)
        r = jnp.maximum(s, 0.0) * w_ref[0]
        return jnp.sum(r.reshape(rows, IDX_HEADS, n), axis=1)

    @pl.when(j < n_steps)
    def _():
        keys = _sort_key(scores(jnp.concatenate([r[0, 0] for r in page_refs], axis=0)))
        for i in range(pp):
            key_ref[j * pp + i] = keys[:, i * LANE:(i + 1) * LANE]

    @pl.when(j == n_steps)
    def _():
        new_ok = (col <= trow) & (col < t_new)
        key_ref[n_pages] = _sort_key(jnp.where(new_ok, scores(ikn_ref[0]), -jnp.inf))
        nblk = n_pages + 1
        grp = SUBLANE
        nblk_pad = key_ref.shape[0]
        if nblk_pad > nblk:
            key_ref[nblk:nblk_pad] = jnp.full((nblk_pad - nblk, rows, LANE), KEY_OF_NEG_INF, I32)
        blk3 = lax.broadcasted_iota(I32, (grp, rows, LANE), 0)
        col3 = lax.broadcasted_iota(I32, (grp, rows, LANE), 2)

        def count(pred):
            def trip(i, acc):
                b0 = pl.multiple_of(i * grp, grp)
                hit = pred(key_ref[pl.ds(b0, grp)], (b0 + blk3) * LANE + col3)
                return acc + jnp.sum(jnp.where(hit, 1.0, 0.0), axis=0)
            acc = lax.fori_loop(0, nblk_pad // grp, trip, jnp.zeros((rows, LANE), F32))
            return jnp.broadcast_to(jnp.sum(acc, axis=1, keepdims=True), (rows, LANE))

        tb, jb = _select_rule(count, (rows, LANE), k_sel, nbits)

        def write_blk(b, carry):
            kpos = b * LANE + col
            sel = _selected(key_ref[b], kpos, tb, jb) & ((kpos < n_pages * PAGE_SIZE) | new_ok)
            bias_ref[0, b] = jnp.where(sel, 0.0, MASK_NEG)
            return carry

        lax.fori_loop(0, nblk, write_blk, 0)


def _page_index_map(layer, i, n_pages, n_steps, trailing):
    def index_map(b, j, pt):
        step = jnp.minimum(j, n_steps - 1)
        return (layer, pt[b * n_pages + step * PAGES_PER_STEP + i]) + (0,) * trailing
    return index_map


def _dsa_sample_select(pt_flat, kidx_pool, layer, iq_s, iw_s, ik_new, *, n_b, n_pages, t_new):
    pp = PAGES_PER_STEP
    assert n_pages % pp == 0
    n_steps = n_pages // pp
    total = n_pages * PAGE_SIZE + t_new
    k_sel = min(TOPK_MAX, total // 4)
    nbits = max(1, ((n_pages + 1) * PAGE_SIZE - 1).bit_length())
    rows_q = t_new * IDX_HEADS
    grid_spec = pltpu.PrefetchScalarGridSpec(
        num_scalar_prefetch=1,
        grid=(n_b, n_steps + 1),
        in_specs=[pl.BlockSpec((1, 1, PAGE_SIZE, IDX_DIM), _page_index_map(layer, i, n_pages, n_steps, 2))
                  for i in range(pp)]
                 + [pl.BlockSpec((1, rows_q, IDX_DIM), lambda b, j, pt: (b, 0, 0)),
                    pl.BlockSpec((1, rows_q, 1), lambda b, j, pt: (b, 0, 0)),
                    pl.BlockSpec((1, PAGE_SIZE, IDX_DIM), lambda b, j, pt: (b, 0, 0))],
        out_specs=pl.BlockSpec((1, n_pages + 1, t_new, LANE), lambda b, j, pt: (b, 0, 0, 0)),
        scratch_shapes=[pltpu.VMEM((-(-(n_pages + 1) // SUBLANE) * SUBLANE, t_new, LANE), I32)])
    return pl.pallas_call(
        functools.partial(_dsa_sample_select_body, n_pages=n_pages, k_sel=k_sel, nbits=nbits, t_new=t_new),
        out_shape=jax.ShapeDtypeStruct((n_b, n_pages + 1, t_new, LANE), F32),
        grid_spec=grid_spec,
        compiler_params=_cparams(("parallel", "arbitrary"), 24 << 20),
        name="branch_c_decode_select",
    )(pt_flat, *([kidx_pool] * pp), iq_s, iw_s, ik_new)


def _dsa_sample_attend_body(pt_ref, *refs, n_pages, t_new):
    pp = PAGES_PER_STEP
    k_refs, v_refs = refs[:pp], refs[pp:2 * pp]
    q_ref, bias_ref, kn_ref, vn_ref, o_ref, m_ref, l_ref, acc_ref = refs[2 * pp:]
    j = pl.program_id(1)
    n_steps = n_pages // pp
    hd = HEAD_DIM
    rq = q_ref.shape[1] // C_KV_HEADS
    reps = rq // t_new

    @pl.when(j == 0)
    def _():
        m_ref[...] = jnp.full(m_ref.shape, MASK_NEG, F32)
        l_ref[...] = jnp.zeros(l_ref.shape, F32)
        acc_ref[...] = jnp.zeros(acc_ref.shape, F32)

    def attend(k_of, v_of, bias_t):
        bias = jnp.concatenate([bias_t] * reps, axis=0)
        for kvh in range(C_KV_HEADS):
            rs = slice(kvh * rq, (kvh + 1) * rq)
            s = lax.dot_general(q_ref[0, rs, :].astype(BF16), k_of(kvh).astype(BF16), (((1,), (1,)), ((), ())),
                                preferred_element_type=F32) * (hd ** -0.5) + bias
            m_old = m_ref[rs, :]
            m_new = jnp.maximum(m_old, jnp.max(s, axis=-1, keepdims=True))
            alpha = jnp.exp(m_old - m_new)
            p = jnp.exp(s - m_new)
            l_ref[rs, :] = alpha * l_ref[rs, :] + jnp.sum(p, axis=-1, keepdims=True)
            acc_ref[rs, :] = alpha * acc_ref[rs, :] + jnp.dot(p.astype(BF16), v_of(kvh).astype(BF16),
                                                              preferred_element_type=F32)
            m_ref[rs, :] = m_new

    @pl.when(j < n_steps)
    def _():
        pages = lambda prefs: (lambda kvh: jnp.concatenate([r[0, 0, :, kvh, :] for r in prefs], axis=0))
        bias_t = jnp.concatenate([bias_ref[0, j * pp + i] for i in range(pp)], axis=1)
        attend(pages(k_refs), pages(v_refs), bias_t)

    @pl.when(j == n_steps)
    def _():
        new = lambda ref: (lambda kvh: ref[0, :, kvh * hd:(kvh + 1) * hd])
        attend(new(kn_ref), new(vn_ref), bias_ref[0, n_pages])
        o_ref[0] = acc_ref[...] / l_ref[...]


def _dsa_sample_attend(pt_flat, k_pool, v_pool, layer, q_s, bias, k_new, v_new, *, n_b, n_pages, t_new):
    pp = PAGES_PER_STEP
    n_steps = n_pages // pp
    rows = q_s.shape[1]
    page_specs = [pl.BlockSpec((1, 1, PAGE_SIZE, C_KV_HEADS, HEAD_DIM), _page_index_map(layer, i, n_pages, n_steps, 3))
                  for i in range(pp)]
    grid_spec = pltpu.PrefetchScalarGridSpec(
        num_scalar_prefetch=1,
        grid=(n_b, n_steps + 1),
        in_specs=page_specs + page_specs
                 + [pl.BlockSpec((1, rows, HEAD_DIM), lambda b, j, pt: (b, 0, 0)),
                    pl.BlockSpec((1, n_pages + 1, t_new, LANE), lambda b, j, pt: (b, 0, 0, 0)),
                    pl.BlockSpec((1, PAGE_SIZE, C_KV), lambda b, j, pt: (b, 0, 0)),
                    pl.BlockSpec((1, PAGE_SIZE, C_KV), lambda b, j, pt: (b, 0, 0))],
        out_specs=pl.BlockSpec((1, rows, HEAD_DIM), lambda b, j, pt: (b, 0, 0)),
        scratch_shapes=[pltpu.VMEM((rows, 1), F32), pltpu.VMEM((rows, 1), F32), pltpu.VMEM((rows, HEAD_DIM), F32)])
    return pl.pallas_call(
        functools.partial(_dsa_sample_attend_body, n_pages=n_pages, t_new=t_new),
        out_shape=jax.ShapeDtypeStruct((n_b, rows, HEAD_DIM), F32),
        grid_spec=grid_spec,
        compiler_params=_cparams(("parallel", "arbitrary"), 32 << 20),
        name="branch_c_decode_attend",
    )(pt_flat, *([k_pool] * pp), *([v_pool] * pp), q_s, bias, k_new, v_new)


def _dsa_sample(h_s, lay, pt_flat, k_pool, v_pool, kidx_pool, layer, *, n_b, t_new, n_pages):
    hd = HEAD_DIM
    seg = lambda name, w: h_s[:, lay[name]:lay[name] + w]
    rq = -(-C_GROUP * t_new // 16) * 16
    q = seg("cq", C_Q).reshape(n_b, t_new, C_KV_HEADS, C_GROUP, hd).transpose(0, 2, 3, 1, 4)
    q = q.reshape(n_b, C_KV_HEADS, C_GROUP * t_new, hd)
    q_s = jnp.pad(q, ((0, 0), (0, 0), (0, rq - C_GROUP * t_new), (0, 0))).reshape(n_b, C_KV_HEADS * rq, hd)
    iq_s = seg("iq", IDX_HEADS * IDX_DIM).reshape(n_b, t_new * IDX_HEADS, IDX_DIM)
    iw = h_s[:, lay["small"] + 2 * B_HEADS:lay["small"] + 2 * B_HEADS + IDX_HEADS]
    iw_s = (iw * (IDX_HEADS ** -0.5 * IDX_DIM ** -0.5)).reshape(n_b, t_new * IDX_HEADS, 1)
    padrows = lambda x: jnp.pad(x.reshape(n_b, t_new, -1), ((0, 0), (0, PAGE_SIZE - t_new), (0, 0)))
    ik_new, k_new, v_new = padrows(seg("ik", IDX_DIM)), padrows(seg("ck", C_KV)), padrows(seg("cv", C_KV))
    bias = _dsa_sample_select(pt_flat, kidx_pool, layer, iq_s, iw_s, ik_new, n_b=n_b, n_pages=n_pages, t_new=t_new)
    o = _dsa_sample_attend(pt_flat, k_pool, v_pool, layer, q_s, bias, k_new, v_new,
                           n_b=n_b, n_pages=n_pages, t_new=t_new)
    o = o.reshape(n_b, C_KV_HEADS, rq, hd)[:, :, :C_GROUP * t_new].reshape(n_b, C_KV_HEADS, C_GROUP, t_new, hd)
    return o.transpose(0, 3, 1, 2, 4).reshape(n_b * t_new, C_Q).astype(BF16)


def _merge_body(ya_ref, yb_ref, yc_ref, w_ref, g0_ref, g1_ref, g2_ref, o_ref):
    b0, c0 = A_WIDTH, A_WIDTH + B_WIDTH
    acc = jax.nn.sigmoid(g0_ref[...]) * jnp.dot(ya_ref[...], w_ref[0:b0, :], preferred_element_type=F32)
    acc = acc + jax.nn.sigmoid(g1_ref[...]) * jnp.dot(yb_ref[...], w_ref[b0:c0, :], preferred_element_type=F32)
    acc = acc + jax.nn.sigmoid(g2_ref[...]) * jnp.dot(yc_ref[...], w_ref[c0:, :], preferred_element_type=F32)
    o_ref[...] = acc.astype(o_ref.dtype)


def _merge(ya, yb, yc, w_br, layer, h, lay, d_model):
    m = ya.shape[0]
    tm = _pick(m, 1024)
    tn = _pick(d_model, 512)
    gcb = lay["gate"] // tn
    per = d_model // tn

    def gate_spec(i):
        return pl.BlockSpec((tm, tn), lambda r, c: (r, gcb + i * per + c))

    kw = A_WIDTH + B_WIDTH + C_Q
    vmem = 2 * (tm * kw * 2 + kw * tn * 2 + 3 * tm * tn * 4 + tm * tn * 2) + 4 * tm * tn * 4
    return pl.pallas_call(
        _merge_body,
        out_shape=jax.ShapeDtypeStruct((m, d_model), BF16),
        grid=(m // tm, d_model // tn),
        in_specs=[pl.BlockSpec((tm, A_WIDTH), lambda r, c: (r, 0)),
                  pl.BlockSpec((tm, B_WIDTH), lambda r, c: (r, 0)),
                  pl.BlockSpec((tm, C_Q), lambda r, c: (r, 0)),
                  _layer_spec((kw, tn), lambda r, c: (0, c), layer),
                  gate_spec(0), gate_spec(1), gate_spec(2)],
        out_specs=pl.BlockSpec((tm, tn), lambda r, c: (r, c)),
        compiler_params=_cparams(("parallel", "arbitrary"), vmem + COMPILER_SCRATCH_BYTES),
        name="gated_merge",
    )(ya, yb, yc, w_br, h, h, h)


def _ffn_body(hn_ref, w1_ref, w3_ref, w2_ref, x_ref, o_ref):
    @pl.when(pl.program_id(1) == 0)
    def _():
        o_ref[...] = x_ref[...]

    hn = hn_ref[...]
    a = jnp.dot(hn, w1_ref[...], preferred_element_type=F32)
    b = jnp.dot(hn, w3_ref[...], preferred_element_type=F32)
    act = (jax.nn.silu(a) * b).astype(BF16)
    o_ref[...] += jnp.dot(act, w2_ref[...], preferred_element_type=F32)


def _ffn(hn, w1, w3, w2, layer, x):
    m, d = hn.shape
    d_ff = w1.shape[-1]
    tm = _pick(m, 512)
    tf = _pick(d_ff, 256)
    once = pl.Buffered(1)
    vmem = tm * d * 2 + tm * d * 4 + 2 * tm * d * 4 + 2 * 3 * d * tf * 2 + 4 * tm * tf * 4 + tm * d * 4
    return pl.pallas_call(
        _ffn_body,
        out_shape=jax.ShapeDtypeStruct((m, d), F32),
        grid=(m // tm, d_ff // tf),
        in_specs=[pl.BlockSpec((tm, d), lambda i, f: (i, 0), pipeline_mode=once),
                  _layer_spec((d, tf), lambda i, f: (0, f), layer),
                  _layer_spec((d, tf), lambda i, f: (0, f), layer),
                  _layer_spec((tf, d), lambda i, f: (f, 0), layer),
                  pl.BlockSpec((tm, d), lambda i, f: (i, 0), pipeline_mode=once)],
        out_specs=pl.BlockSpec((tm, d), lambda i, f: (i, 0)),
        compiler_params=_cparams(("parallel", "arbitrary"), vmem + COMPILER_SCRATCH_BYTES),
        name="swiglu_ffn",
    )(hn, w1, w3, w2, x)


def _dense_front(x, ln1, w_in_p, layer):
    xn = _rmsnorm(x, ln1, BF16)
    return _matmul(xn, w_in_p, layer=layer, tn_pref=768, name="in_proj")


def _dense_back(x, ya, yb, yc, h, lay, wts, layer, d_model):
    w_br, w_o, w1, w3, w2, ln2 = wts
    mix = _merge(ya, yb, yc, w_br, layer, h, lay, d_model)
    x = _matmul(mix, w_o, layer=layer, residual=x, name="out_proj")
    hn = _rmsnorm(x, ln2, BF16)
    return _ffn(hn, w1, w3, w2, layer, x)


def kernel(x_prompt, x_sample, cache_k, cache_v, cache_kidx, state_conv, state_delta, page_table, ln1, w_in,
           a_ln_g, a_ln_b, a_ws, a_bs, b_conv_w, b_a_log, b_dt_bias, b_out_g, w_br, w_o, ln2, ffn_w1, ffn_w3,
           ffn_w2, ln_f):
    n_bp, seq, d_model = x_prompt.shape
    n_bs, t_new, _ = x_sample.shape
    depth = ln1.shape[0]
    n_phys = cache_k.shape[1]
    n_pages = page_table.shape[1]
    assert seq % A_CHUNK == 0 and seq % GDN_CHUNK == 0 and CONV_W - 1 <= t_new <= min(A_CHUNK, GDN_CHUNK)
    lay = _layout(d_model)
    hd = HEAD_DIM

    xp = x_prompt.reshape(n_bp * seq, d_model)
    xs = x_sample.reshape(n_bs * t_new, d_model)
    pt_flat = page_table.reshape(-1).astype(I32)
    conv0_p = jnp.zeros((n_bp, SUBLANE, 3 * B_WIDTH), F32)
    s0_p = jnp.zeros((n_bp, B_HEADS, hd, hd), F32)
    qkv_off = lay["bq"]

    outs = {k: [] for k in ("pk", "pv", "pik", "pconv", "pdelta", "sk", "sv", "sik", "sconv", "sdelta", "schunk")}
    dense_w = tuple(w.astype(BF16) for w in (w_br, w_o, ffn_w1, ffn_w3, ffn_w2))
    w_in_p = _pack_w_in(w_in, d_model)
    for l in range(depth):
        wts = dense_w + (ln2[l],)

        h = _dense_front(xp, ln1[l], w_in_p, l)
        ya = _branch_a(h, lay, a_ws[l], a_bs[l], a_ln_g[l], a_ln_b[l], chunk=A_CHUNK, tb=A_CHUNK, emit_av=False)
        srcs = [(h, lay[n] // B_WIDTH) for n in ("bq", "bk", "bv", "bz")] + [(h, lay["small"] // LANE)]
        yb, s_new = _gdn(srcs, conv0_p, s0_p, b_conv_w[l], b_a_log[l], b_dt_bias[l], b_out_g[l],
                         n_b=n_bp, t_pad=seq, t_valid=GDN_CHUNK, chunk=GDN_CHUNK)
        yc = _dsa_prompt(h, lay, n_b=n_bp, seq=seq)
        h3 = h.reshape(n_bp, seq, lay["total"])
        outs["pk"].append(h3[:, :, lay["ck"]:lay["ck"] + C_KV].reshape(n_bp, seq, C_KV_HEADS, hd))
        outs["pv"].append(h3[:, :, lay["cv"]:lay["cv"] + C_KV].reshape(n_bp, seq, C_KV_HEADS, hd))
        outs["pik"].append(h3[:, :, lay["ik"]:lay["ik"] + IDX_DIM])
        outs["pconv"].append(h3[:, seq - (CONV_W - 1):, qkv_off:qkv_off + 3 * B_WIDTH])
        outs["pdelta"].append(s_new)
        xp = _dense_back(xp, ya, yb, yc, h, lay, wts, l, d_model)

        hs = _dense_front(xs, ln1[l], w_in_p, l)
        ya, av = _branch_a(hs, lay, a_ws[l], a_bs[l], a_ln_g[l], a_ln_b[l], chunk=t_new, tb=n_bs * t_new,
                           emit_av=True)
        hs3 = hs.reshape(n_bs, t_new, lay["total"])
        padt = lambda x: jnp.pad(x, ((0, 0), (0, GDN_CHUNK - t_new), (0, 0))).reshape(n_bs * GDN_CHUNK, -1)
        srcs = [(padt(hs3[:, :, lay[n]:lay[n] + B_WIDTH]), 0) for n in ("bq", "bk", "bv", "bz")]
        srcs.append((padt(hs3[:, :, lay["small"]:lay["small"] + LANE]), 0))
        conv0_s = jnp.pad(state_conv[l], ((0, 0), (SUBLANE - (CONV_W - 1), 0), (0, 0)))
        yb, s_new = _gdn(srcs, conv0_s, state_delta[l], b_conv_w[l], b_a_log[l], b_dt_bias[l], b_out_g[l],
                         n_b=n_bs, t_pad=GDN_CHUNK, t_valid=t_new, chunk=GDN_CHUNK)
        yb = yb.reshape(n_bs, GDN_CHUNK, B_WIDTH)[:, :t_new].reshape(n_bs * t_new, B_WIDTH)
        yc = _dsa_sample(hs, lay, pt_flat, cache_k, cache_v, cache_kidx, l, n_b=n_bs, t_new=t_new, n_pages=n_pages)
        outs["sk"].append(hs3[:, :, lay["ck"]:lay["ck"] + C_KV].reshape(n_bs, t_new, C_KV_HEADS, hd))
        outs["sv"].append(hs3[:, :, lay["cv"]:lay["cv"] + C_KV].reshape(n_bs, t_new, C_KV_HEADS, hd))
        outs["sik"].append(hs3[:, :, lay["ik"]:lay["ik"] + IDX_DIM])
        outs["sconv"].append(hs3[:, t_new - (CONV_W - 1):, qkv_off:qkv_off + 3 * B_WIDTH])
        outs["sdelta"].append(s_new)
        outs["schunk"].append(av.reshape(n_bs, t_new, A_WIDTH))
        xs = _dense_back(xs, ya, yb, yc, hs, lay, wts, l, d_model)

    y_prompt = _rmsnorm(xp, ln_f, F32).reshape(n_bp, seq, d_model)
    y_sample = _rmsnorm(xs, ln_f, F32).reshape(n_bs, t_new, d_model)
    st = lambda k: jnp.stack(outs[k])
    return (y_prompt, y_sample, st("pk"), st("pv"), st("pik"), st("pconv"), st("pdelta"),
            st("sk"), st("sv"), st("sik"), st("sconv"), st("sdelta"), st("schunk"))
```

```python
import functools

import jax
import jax.numpy as jnp
from jax import lax
from jax.experimental import pallas as pl
from jax.experimental.pallas import tpu as pltpu

F32 = jnp.float32
BF16 = jnp.bfloat16
I32 = jnp.int32

HEAD_DIM = 128
A_GROUPS = 8
A_CHUNK = 128
A_WIDTH = A_GROUPS * HEAD_DIM
B_HEADS = 12
B_WIDTH = B_HEADS * HEAD_DIM
CONV_W = 4
C_HEADS = 12
C_KV_HEADS = 4
C_GROUP = C_HEADS // C_KV_HEADS
C_Q = C_HEADS * HEAD_DIM
C_KV = C_KV_HEADS * HEAD_DIM
IDX_HEADS = 16
IDX_DIM = 128
TOPK_MAX = 256
PAGE_SIZE = 128
N_BRANCH = 3
RMS_EPS = 1e-6
LN_EPS = 1e-5

LANE = 128
SUBLANE = 8
V7X_VMEM_BYTES = 64 * 1024 * 1024
VMEM_BUDGET = 60 * 1024 * 1024
COMPILER_SCRATCH_BYTES = 8 * 1024 * 1024

GDN_CHUNK = 128
DSA_TQ = 256
DSA_TK = 512
FFN_TF = 512
PAGES_PER_STEP = 8
MASK_NEG = -1e30
LOG2E = 1.4426950408889634
INT_MIN = -2147483648
KEY_OF_NEG_INF = -2139095041


def _cparams(semantics, vmem_bytes):
    return pltpu.CompilerParams(dimension_semantics=semantics,
                                vmem_limit_bytes=int(min(max(vmem_bytes, 16 * 1024 * 1024), VMEM_BUDGET)))


def _pick(dim, pref):
    t = pref
    while t >= SUBLANE:
        if dim % t == 0:
            return t
        t //= 2
    return dim


def _layout(d_model):
    segs = [("gate", N_BRANCH * d_model, d_model), ("bq", B_WIDTH, B_WIDTH), ("bk", B_WIDTH, B_WIDTH),
            ("bv", B_WIDTH, B_WIDTH), ("bz", B_WIDTH, B_WIDTH), ("cq", C_Q, C_Q), ("ck", C_KV, C_KV),
            ("au", A_WIDTH, A_WIDTH), ("av", A_WIDTH, A_WIDTH), ("cv", C_KV, C_KV),
            ("iq", IDX_HEADS * IDX_DIM, 4 * IDX_DIM), ("ik", IDX_DIM, IDX_DIM), ("small", LANE, LANE)]
    off, lay = 0, {}
    for name, width, align in segs:
        assert off % align == 0, (name, off, align)
        lay[name] = off
        off += width
    lay["total"] = off
    return lay


def _pack_body(tab_ref, main_ref, ab_ref, iw_ref, o_ref):
    j = pl.program_id(1)
    last = pl.num_programs(1) - 1

    @pl.when(j != last)
    def _():
        o_ref[...] = main_ref[0].astype(o_ref.dtype)

    @pl.when(j == last)
    def _():
        n_ab, n_iw = 2 * B_HEADS, IDX_HEADS
        rest = jnp.zeros((o_ref.shape[0] - n_ab - n_iw, o_ref.shape[1]), F32)
        o_ref[...] = jnp.concatenate([ab_ref[0, 0:n_ab, :], iw_ref[0, 0:n_iw, :], rest], axis=0).astype(o_ref.dtype)


def _pack_w_in(w_in, d_model):
    depth, d, in_width = w_in.shape
    lay = _layout(d_model)
    w_t = jnp.swapaxes(w_in, 1, 2)
    widths = (A_WIDTH, A_WIDTH, 3 * B_WIDTH, B_WIDTH, 2 * B_HEADS, C_Q, C_KV, C_KV,
              IDX_HEADS * IDX_DIM, IDX_HEADS, IDX_DIM, N_BRANCH * d_model)
    dsts = (lay["au"], lay["av"], lay["bq"], lay["bz"], None, lay["cq"], lay["ck"], lay["cv"], lay["iq"], None,
            lay["ik"], lay["gate"])
    n_blocks = lay["total"] // LANE
    table, start, special = [0] * n_blocks, 0, []
    for width, dst in zip(widths, dsts):
        if dst is None:
            special.append(start)
        else:
            assert width % LANE == 0 and dst % LANE == 0 and start % SUBLANE == 0
            for blk in range(width // LANE):
                table[dst // LANE + blk] = (start + blk * LANE) // SUBLANE
        start += width
    assert start == in_width and lay["small"] // LANE == n_blocks - 1
    ab_row, iw_row = special
    assert ab_row % SUBLANE == 0 and iw_row % SUBLANE == 0 and max(ab_row, iw_row) + LANE <= in_width
    window = lambda index_map: pl.BlockSpec((pl.Element(1), pl.Element(LANE), pl.Element(d)), index_map)
    grid_spec = pltpu.PrefetchScalarGridSpec(
        num_scalar_prefetch=1,
        grid=(depth, n_blocks),
        in_specs=[window(lambda l, j, tab: (l, tab[j] * SUBLANE, 0)),
                  window(lambda l, j, tab: (l, ab_row, 0)),
                  window(lambda l, j, tab: (l, iw_row, 0))],
        out_specs=pl.BlockSpec((None, LANE, d), lambda l, j, tab: (l, j, 0)))
    return pl.pallas_call(
        _pack_body,
        out_shape=jax.ShapeDtypeStruct((depth, lay["total"], d), BF16),
        grid_spec=grid_spec,
        compiler_params=_cparams(("parallel", "arbitrary"), 8 * LANE * d * 4),
        name="pack_in_proj_weight",
    )(jnp.asarray(table, I32), w_t, w_t, w_t)


def _rmsnorm_body(x_ref, g_ref, o_ref):
    x = x_ref[...]
    ms = jnp.mean(x * x, axis=-1, keepdims=True)
    o_ref[...] = (x * lax.rsqrt(ms + RMS_EPS) * g_ref[...]).astype(o_ref.dtype)


def _rmsnorm(x, g, out_dtype):
    m, d = x.shape
    tm = _pick(m, 256)
    return pl.pallas_call(
        _rmsnorm_body,
        out_shape=jax.ShapeDtypeStruct((m, d), out_dtype),
        grid=(m // tm,),
        in_specs=[pl.BlockSpec((tm, d), lambda i: (i, 0)), pl.BlockSpec((1, d), lambda i: (0, 0))],
        out_specs=pl.BlockSpec((tm, d), lambda i: (i, 0)),
        compiler_params=_cparams(("parallel",), 6 * tm * d * 4),
        name="rmsnorm",
    )(x, g.reshape(1, d))


def _mm_body(a_ref, w_ref, o_ref):
    o_ref[...] = jnp.dot(a_ref[...], w_ref[...], preferred_element_type=F32).astype(o_ref.dtype)


def _mm_res_body(a_ref, w_ref, r_ref, o_ref):
    o_ref[...] = (r_ref[...] + jnp.dot(a_ref[...], w_ref[...], preferred_element_type=F32)).astype(o_ref.dtype)


def _mm_wt_body(a_ref, wt_ref, o_ref):
    o_ref[...] = lax.dot_general(a_ref[...], wt_ref[...], (((1,), (1,)), ((), ())),
                                 preferred_element_type=F32).astype(o_ref.dtype)


def _layer_spec(block, index_map, layer):
    if layer is None:
        return pl.BlockSpec(block, index_map)
    return pl.BlockSpec((None,) + block, lambda *g: (layer,) + index_map(*g))


def _matmul(a, w, *, layer=None, w_transposed=False, residual=None, out_dtype=F32, tm_pref=1024, tn_pref=1024,
            name="matmul"):
    m, k = a.shape
    n = w.shape[-2] if w_transposed else w.shape[-1]
    tm = _pick(m, tm_pref)
    tn = tn_pref if n % tn_pref == 0 else _pick(n, tn_pref)
    osz = jnp.dtype(out_dtype).itemsize
    vmem = 2 * (tm * k * 2 + k * tn * 2 + tm * tn * osz) + tm * tn * 4
    w_spec = (_layer_spec((tn, k), lambda i, j: (j, 0), layer) if w_transposed
              else _layer_spec((k, tn), lambda i, j: (0, j), layer))
    in_specs = [pl.BlockSpec((tm, k), lambda i, j: (i, 0)), w_spec]
    args = [a, w]
    body = _mm_wt_body if w_transposed else _mm_body
    assert residual is None or not w_transposed
    if residual is not None:
        in_specs.append(pl.BlockSpec((tm, tn), lambda i, j: (i, j)))
        args.append(residual)
        body = _mm_res_body
        vmem += 2 * tm * tn * 4
    return pl.pallas_call(
        body,
        out_shape=jax.ShapeDtypeStruct((m, n), out_dtype),
        grid=(m // tm, n // tn),
        in_specs=in_specs,
        out_specs=pl.BlockSpec((tm, tn), lambda i, j: (i, j)),
        compiler_params=_cparams(("parallel", "arbitrary"), vmem + COMPILER_SCRATCH_BYTES),
        name=name,
    )(*args)


def _branch_a_body(u_ref, v_ref, w_ref, bs_ref, g_ref, b_ref, y_ref, *av_ref, chunk):
    tb = u_ref.shape[0]
    u = jax.nn.gelu(u_ref[...])
    v = jax.nn.gelu(v_ref[...])
    mu = jnp.mean(v, axis=-1, keepdims=True)
    var = jnp.mean(jnp.square(v - mu), axis=-1, keepdims=True)
    vn = (v - mu) * lax.rsqrt(var + LN_EPS) * g_ref[...] + b_ref[...]
    if av_ref:
        av_ref[0][...] = vn
    row = lax.broadcasted_iota(I32, (tb, tb), 0)
    col = lax.broadcasted_iota(I32, (tb, tb), 1)
    keep = (col <= row) & ((row // chunk) == (col // chunk))
    vb = vn.astype(BF16)
    for g in range(A_GROUPS):
        sl = slice(g * HEAD_DIM, (g + 1) * HEAD_DIM)
        wg = jnp.where(keep, w_ref[g], 0.0).astype(BF16)
        s = jnp.dot(wg, vb[:, sl], preferred_element_type=F32) + bs_ref[:, g:g + 1]
        y_ref[:, sl] = (u[:, sl] * s).astype(y_ref.dtype)


def _branch_a(h, lay, a_ws, a_bs, ln_g, ln_b, *, chunk, tb, emit_av):
    m = h.shape[0]
    reps = tb // chunk
    wfull = jnp.tile(a_ws[:, :chunk, :chunk], (1, reps, reps))
    bs_t = jnp.tile(a_bs[:, :chunk].T, (reps, 1))
    cu, cv = lay["au"] // A_WIDTH, lay["av"] // A_WIDTH
    out_shape = [jax.ShapeDtypeStruct((m, A_WIDTH), BF16)]
    out_specs = [pl.BlockSpec((tb, A_WIDTH), lambda i: (i, 0))]
    if emit_av:
        out_shape.append(jax.ShapeDtypeStruct((m, A_WIDTH), F32))
        out_specs.append(pl.BlockSpec((tb, A_WIDTH), lambda i: (i, 0)))
    res = pl.pallas_call(
        functools.partial(_branch_a_body, chunk=chunk),
        out_shape=out_shape,
        grid=(m // tb,),
        in_specs=[pl.BlockSpec((tb, A_WIDTH), lambda i: (i, cu)),
                  pl.BlockSpec((tb, A_WIDTH), lambda i: (i, cv)),
                  pl.BlockSpec((A_GROUPS, tb, tb), lambda i: (0, 0, 0)),
                  pl.BlockSpec((tb, A_GROUPS), lambda i: (0, 0)),
                  pl.BlockSpec((1, A_WIDTH), lambda i: (0, 0)),
                  pl.BlockSpec((1, A_WIDTH), lambda i: (0, 0))],
        out_specs=out_specs,
        compiler_params=_cparams(("parallel",), 32 << 20),
        name="branch_a_gmlp",
    )(h, h, wfull, bs_t, ln_g.reshape(1, A_WIDTH), ln_b.reshape(1, A_WIDTH))
    return res if emit_av else res[0]


def _bdot(a, b):
    return lax.dot_general(a.astype(BF16), b.astype(BF16), (((2,), (1,)), ((0,), (0,))),
                           preferred_element_type=F32)


def _bdot_nt(a, b):
    return lax.dot_general(a.astype(BF16), b.astype(BF16), (((2,), (2,)), ((0,), (0,))),
                           preferred_element_type=F32)


def _transpose_rows(x):
    r = x.shape[0]
    if r < LANE:
        x = jnp.concatenate([x, jnp.zeros((LANE - r, LANE), x.dtype)], axis=0)
    return x.T[:, :r]


def _gdn_body(q_ref, k_ref, v_ref, z_ref, sm_ref, c0_ref, s0_ref, cw_ref, al_ref, dt_ref, og_ref,
              y_ref, sout_ref, ext_ref, s_ref, *, chunk, t_valid):
    c = pl.program_id(1)
    nc = pl.num_programs(1)
    hd = HEAD_DIM

    @pl.when(c == 0)
    def _():
        s_ref[...] = s0_ref[0]
        for j in range(3):
            ext_ref[j, 0:SUBLANE, :] = c0_ref[0, :, j * B_WIDTH:(j + 1) * B_WIDTH]

    acts = []
    for j, ref in enumerate((q_ref, k_ref, v_ref)):
        ext_ref[j, SUBLANE:SUBLANE + chunk, :] = ref[...]
        acc = None
        for i in range(CONV_W):
            lo = SUBLANE - (CONV_W - 1) + i
            term = ext_ref[j, lo:lo + chunk, :] * cw_ref[i:i + 1, j * B_WIDTH:(j + 1) * B_WIDTH]
            acc = term if acc is None else acc + term
        acts.append(jax.nn.silu(acc))
        ext_ref[j, 0:SUBLANE, :] = ext_ref[j, chunk:chunk + SUBLANE, :]
    qa, ka, va = acts

    sm = sm_ref[...]
    row1 = lax.broadcasted_iota(I32, (chunk, LANE), 0)
    g_all = -jnp.exp(al_ref[...]) * jax.nn.softplus(sm + dt_ref[...])
    beta_all = jax.nn.sigmoid(sm)
    if t_valid < chunk:
        g_all = jnp.where(row1 < t_valid, g_all, 0.0)
        beta_all = jnp.where(row1 < t_valid, beta_all, 0.0)
    gc_all = g_all
    d = 1
    while d < chunk:
        gc_all = gc_all + jnp.where(row1 >= d, pltpu.roll(gc_all, d, 0), 0.0)
        d *= 2
    gc_t = _transpose_rows(gc_all)

    row = lax.broadcasted_iota(I32, (chunk, chunk), 0)
    col = lax.broadcasted_iota(I32, (chunk, chunk), 1)
    incl = row >= col
    strict = row > col
    eye = jnp.where(row == col, 1.0, 0.0)
    pair_masks = []
    bs = 1
    while bs < chunk:
        pair_masks.append(((row // bs) % 2 == 1) & ((col // bs) == (row // bs) - 1))
        bs *= 2

    heads = range(B_HEADS)
    per_head = lambda x: jnp.stack([x[:, h * hd:(h + 1) * hd] for h in heads], axis=0)
    q3, k3, v3 = per_head(qa), per_head(ka), per_head(va)
    qn = q3 * lax.rsqrt(jnp.sum(q3 * q3, axis=-1, keepdims=True) + 1e-6) * (hd ** -0.5)
    kn = k3 * lax.rsqrt(jnp.sum(k3 * k3, axis=-1, keepdims=True) + 1e-6)
    beta = jnp.stack([beta_all[:, B_HEADS + h:B_HEADS + h + 1] for h in heads], axis=0)
    gcol = jnp.stack([gc_all[:, h:h + 1] for h in heads], axis=0)
    grow = jnp.stack([gc_t[h:h + 1, :] for h in heads], axis=0)
    decay = jnp.exp(jnp.where(incl[None], gcol - grow, -jnp.inf))
    kb = kn * beta
    eg = jnp.exp(gcol)
    lmat = jnp.where(strict[None], _bdot_nt(kb, kn) * decay, 0.0)
    attn = _bdot_nt(qn, kn) * decay
    tinv = eye[None] - jnp.where(pair_masks[0][None], lmat, 0.0)
    for pm in pair_masks[1:]:
        tinv = tinv - _bdot(tinv, _bdot(jnp.where(pm[None], lmat, 0.0), tinv))
    sol = _bdot(tinv, jnp.concatenate([v3 * beta, kb * eg], axis=-1))
    value, kcd = sol[:, :, :hd], sol[:, :, hd:]
    s_old = s_ref[...]
    vnew = value - _bdot(kcd, s_old)
    o = _bdot(qn * eg, s_old) + _bdot(attn, vnew)
    glast = gcol[:, chunk - 1:chunk, :]
    kend = kn * jnp.exp(glast - gcol)
    s_ref[...] = s_old * jnp.exp(glast) + _bdot(jnp.swapaxes(kend, 1, 2), vnew)
    on = o * lax.rsqrt(jnp.mean(o * o, axis=-1, keepdims=True) + RMS_EPS) * og_ref[...]
    for h in heads:
        sl = slice(h * hd, (h + 1) * hd)
        y_ref[:, sl] = (on[h] * jax.nn.silu(z_ref[:, sl])).astype(y_ref.dtype)

    @pl.when(c == nc - 1)
    def _():
        sout_ref[0] = s_ref[...]


def _gdn(srcs, conv0, s0, conv_w, a_log, dt_bias, o_g, *, n_b, t_pad, t_valid, chunk):
    nc = t_pad // chunk
    arrs = [a for a, _ in srcs]
    cbs = [cb for _, cb in srcs]
    widths = [B_WIDTH] * 4 + [LANE]

    def tok_spec(w, cb):
        return pl.BlockSpec((chunk, w), lambda b, c: (b * nc + c, cb))

    pad12 = lambda x: jnp.zeros((1, LANE), F32).at[0, :B_HEADS].set(x)
    return pl.pallas_call(
        functools.partial(_gdn_body, chunk=chunk, t_valid=t_valid),
        out_shape=[jax.ShapeDtypeStruct((n_b * t_pad, B_WIDTH), BF16),
                   jax.ShapeDtypeStruct((n_b, B_HEADS, HEAD_DIM, HEAD_DIM), F32)],
        grid=(n_b, nc),
        in_specs=[tok_spec(w, cb) for w, cb in zip(widths, cbs)] + [
            pl.BlockSpec((1, SUBLANE, 3 * B_WIDTH), lambda b, c: (b, 0, 0)),
            pl.BlockSpec((1, B_HEADS, HEAD_DIM, HEAD_DIM), lambda b, c: (b, 0, 0, 0)),
            pl.BlockSpec((CONV_W, 3 * B_WIDTH), lambda b, c: (0, 0)),
            pl.BlockSpec((1, LANE), lambda b, c: (0, 0)),
            pl.BlockSpec((1, LANE), lambda b, c: (0, 0)),
            pl.BlockSpec((1, HEAD_DIM), lambda b, c: (0, 0))],
        out_specs=[pl.BlockSpec((chunk, B_WIDTH), lambda b, c: (b * nc + c, 0)),
                   pl.BlockSpec((1, B_HEADS, HEAD_DIM, HEAD_DIM), lambda b, c: (b, 0, 0, 0))],
        scratch_shapes=[pltpu.VMEM((3, chunk + SUBLANE, B_WIDTH), F32),
                        pltpu.VMEM((B_HEADS, HEAD_DIM, HEAD_DIM), F32)],
        compiler_params=_cparams(("parallel", "arbitrary"), 40 << 20),
        name="branch_b_gated_delta",
    )(*arrs, conv0, s0, conv_w, pad12(a_log), pad12(dt_bias), o_g.reshape(1, HEAD_DIM))


def _sort_key(x):
    b = lax.bitcast_convert_type(x + 0.0, I32)
    return b ^ ((b >> 31) & 0x7FFFFFFF)


def _select_rule(count, shape, k, nbits):
    kf = jnp.float32(k)
    count_ge = lambda cand: count(lambda key, pos: key >= cand)
    t0 = jnp.where(count_ge(jnp.zeros(shape, I32)) >= kf, 0, INT_MIN).astype(I32)

    def bit_step(i, t):
        cand = t + lax.shift_left(jnp.int32(1), 30 - i)
        return jnp.where(count_ge(cand) >= kf, cand, t)

    t = lax.fori_loop(0, 31, bit_step, t0)
    n_ge = count_ge(t)
    tied = (n_ge > kf) & (t > KEY_OF_NEG_INF)

    def tie_index():
        r = kf - count(lambda key, pos: key > t)

        def idx_step(i, j):
            cand = j + lax.shift_left(jnp.int32(1), nbits - 1 - i)
            below = count(lambda key, pos: (key == t) & (pos <= cand - 1))
            return jnp.where(below < r, cand, j)

        return lax.fori_loop(0, nbits, idx_step, jnp.zeros(shape, I32))

    no_limit = jnp.full(shape, 2 ** 30, I32)
    j = lax.cond(jnp.sum(jnp.where(tied, 1.0, 0.0)) > 0.0,
                 lambda: jnp.where(tied, tie_index(), no_limit), lambda: no_limit)
    return t, j


def _selected(key, pos, t, j):
    return (key > t) | ((key == t) & (pos <= j))


def _dsa_prompt_body(cq_ref, iq0_ref, iq1_ref, iq2_ref, iq3_ref, sm_ref, ik_ref, ck_ref, cv_ref, y_ref,
                     key_ref, iqb_ref, wb_ref, qb_ref, m_ref, l_ref, acc_ref, *, k_sel, nbits, tq, tk):
    qi = pl.program_id(1)
    kpg = tk // LANE
    nkb = (qi + 1) * (tq // LANE)
    ngrp = ((qi + 1) * tq + tk - 1) // tk
    hd = HEAD_DIM
    nt = (((1,), (1,)), ((), ()))
    sub = LANE // SUBLANE
    qpos3 = qi * tq + lax.broadcasted_iota(I32, (sub, SUBLANE, tq), 2)
    kofs3 = lax.broadcasted_iota(I32, (sub, SUBLANE, tq), 0) * SUBLANE + lax.broadcasted_iota(I32, (sub, SUBLANE, tq), 1)

    w_all = sm_ref[...] * (IDX_HEADS ** -0.5 * IDX_DIM ** -0.5)
    w_t = jnp.concatenate([w_all[r:r + LANE].T for r in range(0, tq, LANE)], axis=1)
    w_off = 2 * B_HEADS
    for hh in range(IDX_HEADS):
        ref = (iq0_ref, iq1_ref, iq2_ref, iq3_ref)[hh // 4]
        iqb_ref[hh] = ref[:, (hh % 4) * IDX_DIM:(hh % 4 + 1) * IDX_DIM].astype(BF16)
        wb_ref[hh] = jnp.broadcast_to(w_t[w_off + hh:w_off + hh + 1, :], (SUBLANE, tq))

    def score_grp(g, carry):
        ikg = ik_ref[pl.ds(pl.multiple_of(g * tk, tk), tk), :].astype(BF16)
        acc = jnp.zeros((tk // SUBLANE, SUBLANE, tq), F32)
        for hh in range(IDX_HEADS):
            s = lax.dot_general(ikg, iqb_ref[hh], nt, preferred_element_type=F32)
            acc = acc + wb_ref[hh] * jnp.maximum(s, 0.0).reshape(tk // SUBLANE, SUBLANE, tq)
        for t in range(kpg):
            kpos3 = (g * kpg + t) * LANE + kofs3
            blk = acc[t * sub:(t + 1) * sub]
            key_ref[g * kpg + t] = _sort_key(jnp.where(kpos3 <= qpos3, blk, -jnp.inf))
        return carry

    lax.fori_loop(0, ngrp, score_grp, 0)

    def count(pred):
        def blk(b, acc):
            return acc + jnp.sum(jnp.where(pred(key_ref[b], b * LANE + kofs3), 1.0, 0.0), axis=0)
        acc = lax.fori_loop(0, nkb, blk, jnp.zeros((SUBLANE, tq), F32))
        return jnp.broadcast_to(jnp.sum(acc, axis=0, keepdims=True), (SUBLANE, tq))

    t8, j8 = _select_rule(count, (SUBLANE, tq), k_sel, nbits)

    for kvh in range(C_KV_HEADS):
        for g in range(C_GROUP):
            hsl = slice((kvh * C_GROUP + g) * hd, (kvh * C_GROUP + g + 1) * hd)
            qb_ref[kvh, g * tq:(g + 1) * tq, :] = cq_ref[:, hsl].astype(BF16)
    m_ref[...] = jnp.full(m_ref.shape, MASK_NEG, F32)
    l_ref[...] = jnp.zeros(l_ref.shape, F32)
    acc_ref[...] = jnp.zeros(acc_ref.shape, F32)
    ones = jnp.ones((tk, hd), BF16)

    def attend_grp(g, carry):
        bias_t = []
        for t in range(kpg):
            kpos3 = (g * kpg + t) * LANE + kofs3
            sel = _selected(key_ref[g * kpg + t], kpos3, t8, j8) & (kpos3 <= qpos3)
            kq = jnp.where(sel, 0.0, MASK_NEG).reshape(LANE, tq)
            bias_t.append(jnp.concatenate([kq[:, r:r + LANE].T for r in range(0, tq, LANE)], axis=0))
        bias = jnp.concatenate(bias_t, axis=1)
        bias3 = jnp.concatenate([bias] * C_GROUP, axis=0)
        start = pl.multiple_of(g * tk, tk)
        kgrp = ck_ref[pl.ds(start, tk), :].astype(BF16)
        vgrp = cv_ref[pl.ds(start, tk), :].astype(BF16)
        for kvh in range(C_KV_HEADS):
            sl = slice(kvh * hd, (kvh + 1) * hd)
            s = lax.dot_general(qb_ref[kvh], kgrp[:, sl], nt, preferred_element_type=F32) * (hd ** -0.5 * LOG2E) + bias3
            m_old = m_ref[kvh]
            m_new = jnp.maximum(m_old, jnp.max(s, axis=-1, keepdims=True))
            alpha = jnp.exp2(m_old - m_new)
            p = jnp.exp2(s - m_new).astype(BF16)
            pv = jnp.dot(p, jnp.concatenate([vgrp[:, sl], ones], axis=1), preferred_element_type=F32)
            l_ref[kvh] = alpha * l_ref[kvh] + pv[:, hd:hd + 1]
            acc_ref[kvh] = alpha * acc_ref[kvh] + pv[:, :hd]
            m_ref[kvh] = m_new
        return carry

    lax.fori_loop(0, ngrp, attend_grp, 0)

    for kvh in range(C_KV_HEADS):
        o = acc_ref[kvh] / l_ref[kvh]
        for g in range(C_GROUP):
            hsl = slice((kvh * C_GROUP + g) * hd, (kvh * C_GROUP + g + 1) * hd)
            y_ref[:, hsl] = o[g * tq:(g + 1) * tq, :].astype(y_ref.dtype)


def _dsa_prompt(h, lay, *, n_b, seq):
    tq, tk = DSA_TQ, DSA_TK
    assert seq % tq == 0 and seq % tk == 0
    nqb = seq // tq
    k_sel = min(TOPK_MAX, seq // 4)
    nbits = max(1, (seq - 1).bit_length())
    iq_cb = lay["iq"] // (4 * IDX_DIM)
    once = pl.Buffered(1)

    def q_spec(w, cb):
        return pl.BlockSpec((tq, w), lambda b, q: (b * nqb + q, cb))

    def kv_spec(w, cb):
        return pl.BlockSpec((seq, w), lambda b, q: (b, cb), pipeline_mode=once)

    rows3 = C_GROUP * tq
    lane_pad = lambda r: r * LANE * 4
    vmem = (seq * (2 * C_KV + IDX_DIM) * 4 + 2 * tq * (C_Q + IDX_HEADS * IDX_DIM + LANE) * 4 + 2 * tq * C_Q * 2
            + (seq // LANE) * tq * LANE * 4 + IDX_HEADS * tq * LANE * 6 + C_KV_HEADS * rows3 * HEAD_DIM * 6
            + 2 * C_KV_HEADS * lane_pad(rows3) + 4 * rows3 * tk * 4)
    return pl.pallas_call(
        functools.partial(_dsa_prompt_body, k_sel=k_sel, nbits=nbits, tq=tq, tk=tk),
        out_shape=jax.ShapeDtypeStruct((n_b * seq, C_Q), BF16),
        grid=(n_b, nqb),
        in_specs=[q_spec(C_Q, lay["cq"] // C_Q)]
                 + [q_spec(4 * IDX_DIM, iq_cb + i) for i in range(4)]
                 + [q_spec(LANE, lay["small"] // LANE),
                    kv_spec(IDX_DIM, lay["ik"] // IDX_DIM),
                    kv_spec(C_KV, lay["ck"] // C_KV),
                    kv_spec(C_KV, lay["cv"] // C_KV)],
        out_specs=pl.BlockSpec((tq, C_Q), lambda b, q: (b * nqb + q, 0)),
        scratch_shapes=[pltpu.VMEM((seq // LANE, LANE // SUBLANE, SUBLANE, tq), I32),
                        pltpu.VMEM((IDX_HEADS, tq, IDX_DIM), BF16),
                        pltpu.VMEM((IDX_HEADS, SUBLANE, tq), F32),
                        pltpu.VMEM((C_KV_HEADS, rows3, HEAD_DIM), BF16),
                        pltpu.VMEM((C_KV_HEADS, rows3, 1), F32),
                        pltpu.VMEM((C_KV_HEADS, rows3, 1), F32),
                        pltpu.VMEM((C_KV_HEADS, rows3, HEAD_DIM), F32)],
        compiler_params=_cparams(("parallel", "arbitrary"), vmem + COMPILER_SCRATCH_BYTES),
        name="branch_c_prompt_dsa",
    )(h, h, h, h, h, h, h, h, h)


def _dsa_sample_select_body(pt_ref, *refs, n_pages, k_sel, nbits, t_new):
    pp = PAGES_PER_STEP
    page_refs = refs[:pp]
    iq_ref, w_ref, ikn_ref, bias_ref, key_ref = refs[pp:]
    j = pl.program_id(1)
    n_steps = n_pages // pp
    rows = t_new
    col = lax.broadcasted_iota(I32, (rows, LANE), 1)
    trow = lax.broadcasted_iota(I32, (rows, LANE), 0)
    iq = iq_ref[0].astype(BF16)

    def scores(keys_f32):
        n = keys_f32.shape[0]
        s = lax.dot_general(iq, keys_f32.astype(BF16), (((1,), (1,)), ((), ())), preferred_element_type=F32)
        r = jnp.maximum(s, 0.0) * w_ref[0]
        return jnp.sum(r.reshape(rows, IDX_HEADS, n), axis=1)

    @pl.when(j < n_steps)
    def _():
        keys = _sort_key(scores(jnp.concatenate([r[0, 0] for r in page_refs], axis=0)))
        for i in range(pp):
            key_ref[j * pp + i] = keys[:, i * LANE:(i + 1) * LANE]

    @pl.when(j == n_steps)
    def _():
        new_ok = (col <= trow) & (col < t_new)
        key_ref[n_pages] = _sort_key(jnp.where(new_ok, scores(ikn_ref[0]), -jnp.inf))
        nblk = n_pages + 1
        grp = SUBLANE
        nblk_pad = key_ref.shape[0]
        if nblk_pad > nblk:
            key_ref[nblk:nblk_pad] = jnp.full((nblk_pad - nblk, rows, LANE), KEY_OF_NEG_INF, I32)
        blk3 = lax.broadcasted_iota(I32, (grp, rows, LANE), 0)
        col3 = lax.broadcasted_iota(I32, (grp, rows, LANE), 2)

        def count(pred):
            def trip(i, acc):
                b0 = pl.multiple_of(i * grp, grp)
                hit = pred(key_ref[pl.ds(b0, grp)], (b0 + blk3) * LANE + col3)
                return acc + jnp.sum(jnp.where(hit, 1.0, 0.0), axis=0)
            acc = lax.fori_loop(0, nblk_pad // grp, trip, jnp.zeros((rows, LANE), F32))
            return jnp.broadcast_to(jnp.sum(acc, axis=1, keepdims=True), (rows, LANE))

        tb, jb = _select_rule(count, (rows, LANE), k_sel, nbits)

        def write_blk(b, carry):
            kpos = b * LANE + col
            sel = _selected(key_ref[b], kpos, tb, jb) & ((kpos < n_pages * PAGE_SIZE) | new_ok)
            bias_ref[0, b] = jnp.where(sel, 0.0, MASK_NEG)
            return carry

        lax.fori_loop(0, nblk, write_blk, 0)


def _page_index_map(layer, i, n_pages, n_steps, trailing):
    def index_map(b, j, pt):
        step = jnp.minimum(j, n_steps - 1)
        return (layer, pt[b * n_pages + step * PAGES_PER_STEP + i]) + (0,) * trailing
    return index_map


def _dsa_sample_select(pt_flat, kidx_pool, layer, iq_s, iw_s, ik_new, *, n_b, n_pages, t_new):
    pp = PAGES_PER_STEP
    assert n_pages % pp == 0
    n_steps = n_pages // pp
    total = n_pages * PAGE_SIZE + t_new
    k_sel = min(TOPK_MAX, total // 4)
    nbits = max(1, ((n_pages + 1) * PAGE_SIZE - 1).bit_length())
    rows_q = t_new * IDX_HEADS
    grid_spec = pltpu.PrefetchScalarGridSpec(
        num_scalar_prefetch=1,
        grid=(n_b, n_steps + 1),
        in_specs=[pl.BlockSpec((1, 1, PAGE_SIZE, IDX_DIM), _page_index_map(layer, i, n_pages, n_steps, 2))
                  for i in range(pp)]
                 + [pl.BlockSpec((1, rows_q, IDX_DIM), lambda b, j, pt: (b, 0, 0)),
                    pl.BlockSpec((1, rows_q, 1), lambda b, j, pt: (b, 0, 0)),
                    pl.BlockSpec((1, PAGE_SIZE, IDX_DIM), lambda b, j, pt: (b, 0, 0))],
        out_specs=pl.BlockSpec((1, n_pages + 1, t_new, LANE), lambda b, j, pt: (b, 0, 0, 0)),
        scratch_shapes=[pltpu.VMEM((-(-(n_pages + 1) // SUBLANE) * SUBLANE, t_new, LANE), I32)])
    return pl.pallas_call(
        functools.partial(_dsa_sample_select_body, n_pages=n_pages, k_sel=k_sel, nbits=nbits, t_new=t_new),
        out_shape=jax.ShapeDtypeStruct((n_b, n_pages + 1, t_new, LANE), F32),
        grid_spec=grid_spec,
        compiler_params=_cparams(("parallel", "arbitrary"), 24 << 20),
        name="branch_c_decode_select",
    )(pt_flat, *([kidx_pool] * pp), iq_s, iw_s, ik_new)


def _dsa_sample_attend_body(pt_ref, *refs, n_pages, t_new):
    pp = PAGES_PER_STEP
    k_refs, v_refs = refs[:pp], refs[pp:2 * pp]
    q_ref, bias_ref, kn_ref, vn_ref, o_ref, m_ref, l_ref, acc_ref = refs[2 * pp:]
    j = pl.program_id(1)
    n_steps = n_pages // pp
    hd = HEAD_DIM
    rq = q_ref.shape[1] // C_KV_HEADS
    reps = rq // t_new

    @pl.when(j == 0)
    def _():
        m_ref[...] = jnp.full(m_ref.shape, MASK_NEG, F32)
        l_ref[...] = jnp.zeros(l_ref.shape, F32)
        acc_ref[...] = jnp.zeros(acc_ref.shape, F32)

    def attend(k_of, v_of, bias_t):
        bias = jnp.concatenate([bias_t] * reps, axis=0)
        for kvh in range(C_KV_HEADS):
            rs = slice(kvh * rq, (kvh + 1) * rq)
            s = lax.dot_general(q_ref[0, rs, :].astype(BF16), k_of(kvh).astype(BF16), (((1,), (1,)), ((), ())),
                                preferred_element_type=F32) * (hd ** -0.5) + bias
            m_old = m_ref[rs, :]
            m_new = jnp.maximum(m_old, jnp.max(s, axis=-1, keepdims=True))
            alpha = jnp.exp(m_old - m_new)
            p = jnp.exp(s - m_new)
            l_ref[rs, :] = alpha * l_ref[rs, :] + jnp.sum(p, axis=-1, keepdims=True)
            acc_ref[rs, :] = alpha * acc_ref[rs, :] + jnp.dot(p.astype(BF16), v_of(kvh).astype(BF16),
                                                              preferred_element_type=F32)
            m_ref[rs, :] = m_new

    @pl.when(j < n_steps)
    def _():
        head_rows = lambda kvh: pl.ds(kvh, PAGE_SIZE, stride=C_KV_HEADS)
        pages = lambda prefs: (lambda kvh: jnp.concatenate([r[0, 0, head_rows(kvh), :] for r in prefs], axis=0))
        bias_t = jnp.concatenate([bias_ref[0, j * pp + i] for i in range(pp)], axis=1)
        attend(pages(k_refs), pages(v_refs), bias_t)

    @pl.when(j == n_steps)
    def _():
        new = lambda ref: (lambda kvh: ref[0, :, kvh * hd:(kvh + 1) * hd])
        attend(new(kn_ref), new(vn_ref), bias_ref[0, n_pages])
        o_ref[0] = acc_ref[...] / l_ref[...]


def _dsa_sample_attend(pt_flat, k_pool, v_pool, layer, q_s, bias, k_new, v_new, *, n_b, n_pages, t_new):
    pp = PAGES_PER_STEP
    n_steps = n_pages // pp
    rows = q_s.shape[1]
    page_specs = [pl.BlockSpec((1, 1, PAGE_SIZE * C_KV_HEADS, HEAD_DIM), _page_index_map(layer, i, n_pages, n_steps, 2))
                  for i in range(pp)]
    grid_spec = pltpu.PrefetchScalarGridSpec(
        num_scalar_prefetch=1,
        grid=(n_b, n_steps + 1),
        in_specs=page_specs + page_specs
                 + [pl.BlockSpec((1, rows, HEAD_DIM), lambda b, j, pt: (b, 0, 0)),
                    pl.BlockSpec((1, n_pages + 1, t_new, LANE), lambda b, j, pt: (b, 0, 0, 0)),
                    pl.BlockSpec((1, PAGE_SIZE, C_KV), lambda b, j, pt: (b, 0, 0)),
                    pl.BlockSpec((1, PAGE_SIZE, C_KV), lambda b, j, pt: (b, 0, 0))],
        out_specs=pl.BlockSpec((1, rows, HEAD_DIM), lambda b, j, pt: (b, 0, 0)),
        scratch_shapes=[pltpu.VMEM((rows, 1), F32), pltpu.VMEM((rows, 1), F32), pltpu.VMEM((rows, HEAD_DIM), F32)])
    return pl.pallas_call(
        functools.partial(_dsa_sample_attend_body, n_pages=n_pages, t_new=t_new),
        out_shape=jax.ShapeDtypeStruct((n_b, rows, HEAD_DIM), F32),
        grid_spec=grid_spec,
        compiler_params=_cparams(("parallel", "arbitrary"), 32 << 20),
        name="branch_c_decode_attend",
    )(pt_flat, *([k_pool] * pp), *([v_pool] * pp), q_s, bias, k_new, v_new)


def _dsa_sample(h_s, lay, pt_flat, k_pool, v_pool, kidx_pool, layer, *, n_b, t_new, n_pages):
    hd = HEAD_DIM
    seg = lambda name, w: h_s[:, lay[name]:lay[name] + w]
    rq = -(-C_GROUP * t_new // 16) * 16
    q = seg("cq", C_Q).reshape(n_b, t_new, C_KV_HEADS, C_GROUP, hd).transpose(0, 2, 3, 1, 4)
    q = q.reshape(n_b, C_KV_HEADS, C_GROUP * t_new, hd)
    q_s = jnp.pad(q, ((0, 0), (0, 0), (0, rq - C_GROUP * t_new), (0, 0))).reshape(n_b, C_KV_HEADS * rq, hd)
    iq_s = seg("iq", IDX_HEADS * IDX_DIM).reshape(n_b, t_new * IDX_HEADS, IDX_DIM)
    iw = h_s[:, lay["small"] + 2 * B_HEADS:lay["small"] + 2 * B_HEADS + IDX_HEADS]
    iw_s = (iw * (IDX_HEADS ** -0.5 * IDX_DIM ** -0.5)).reshape(n_b, t_new * IDX_HEADS, 1)
    padrows = lambda x: jnp.pad(x.reshape(n_b, t_new, -1), ((0, 0), (0, PAGE_SIZE - t_new), (0, 0)))
    ik_new, k_new, v_new = padrows(seg("ik", IDX_DIM)), padrows(seg("ck", C_KV)), padrows(seg("cv", C_KV))
    bias = _dsa_sample_select(pt_flat, kidx_pool, layer, iq_s, iw_s, ik_new, n_b=n_b, n_pages=n_pages, t_new=t_new)
    o = _dsa_sample_attend(pt_flat, k_pool, v_pool, layer, q_s, bias, k_new, v_new,
                           n_b=n_b, n_pages=n_pages, t_new=t_new)
    o = o.reshape(n_b, C_KV_HEADS, rq, hd)[:, :, :C_GROUP * t_new].reshape(n_b, C_KV_HEADS, C_GROUP, t_new, hd)
    return o.transpose(0, 3, 1, 2, 4).reshape(n_b * t_new, C_Q).astype(BF16)


def _merge_body(ya_ref, yb_ref, yc_ref, w_ref, g0_ref, g1_ref, g2_ref, o_ref):
    b0, c0 = A_WIDTH, A_WIDTH + B_WIDTH
    acc = jax.nn.sigmoid(g0_ref[...]) * jnp.dot(ya_ref[...], w_ref[0:b0, :], preferred_element_type=F32)
    acc = acc + jax.nn.sigmoid(g1_ref[...]) * jnp.dot(yb_ref[...], w_ref[b0:c0, :], preferred_element_type=F32)
    acc = acc + jax.nn.sigmoid(g2_ref[...]) * jnp.dot(yc_ref[...], w_ref[c0:, :], preferred_element_type=F32)
    o_ref[...] = acc.astype(o_ref.dtype)


def _merge(ya, yb, yc, w_br, layer, h, lay, d_model):
    m = ya.shape[0]
    tm = _pick(m, 1024)
    tn = _pick(d_model, 512)
    gcb = lay["gate"] // tn
    per = d_model // tn

    def gate_spec(i):
        return pl.BlockSpec((tm, tn), lambda r, c: (r, gcb + i * per + c))

    kw = A_WIDTH + B_WIDTH + C_Q
    vmem = 2 * (tm * kw * 2 + kw * tn * 2 + 3 * tm * tn * 4 + tm * tn * 2) + 4 * tm * tn * 4
    return pl.pallas_call(
        _merge_body,
        out_shape=jax.ShapeDtypeStruct((m, d_model), BF16),
        grid=(m // tm, d_model // tn),
        in_specs=[pl.BlockSpec((tm, A_WIDTH), lambda r, c: (r, 0)),
                  pl.BlockSpec((tm, B_WIDTH), lambda r, c: (r, 0)),
                  pl.BlockSpec((tm, C_Q), lambda r, c: (r, 0)),
                  _layer_spec((kw, tn), lambda r, c: (0, c), layer),
                  gate_spec(0), gate_spec(1), gate_spec(2)],
        out_specs=pl.BlockSpec((tm, tn), lambda r, c: (r, c)),
        compiler_params=_cparams(("parallel", "arbitrary"), vmem + COMPILER_SCRATCH_BYTES),
        name="gated_merge",
    )(ya, yb, yc, w_br, h, h, h)


def _ffn_body(hn_ref, w1_ref, w3_ref, w2_ref, x_ref, o_ref):
    @pl.when(pl.program_id(1) == 0)
    def _():
        o_ref[...] = x_ref[...]

    hn = hn_ref[...]
    a = jnp.dot(hn, w1_ref[...], preferred_element_type=F32)
    b = jnp.dot(hn, w3_ref[...], preferred_element_type=F32)
    act = (jax.nn.silu(a) * b).astype(BF16)
    o_ref[...] += jnp.dot(act, w2_ref[...], preferred_element_type=F32)


def _ffn(hn, w1, w3, w2, layer, x):
    m, d = hn.shape
    d_ff = w1.shape[-1]
    tm = _pick(m, 512)
    tf = FFN_TF
    assert d_ff % tf == 0
    once = pl.Buffered(1)
    vmem = tm * d * 2 + tm * d * 4 + 2 * tm * d * 4 + 2 * 3 * d * tf * 2 + 3 * tm * tf * 4
    return pl.pallas_call(
        _ffn_body,
        out_shape=jax.ShapeDtypeStruct((m, d), F32),
        grid=(m // tm, d_ff // tf),
        in_specs=[pl.BlockSpec((tm, d), lambda i, f: (i, 0), pipeline_mode=once),
                  _layer_spec((d, tf), lambda i, f: (0, f), layer),
                  _layer_spec((d, tf), lambda i, f: (0, f), layer),
                  _layer_spec((tf, d), lambda i, f: (f, 0), layer),
                  pl.BlockSpec((tm, d), lambda i, f: (i, 0), pipeline_mode=once)],
        out_specs=pl.BlockSpec((tm, d), lambda i, f: (i, 0)),
        compiler_params=_cparams(("parallel", "arbitrary"), vmem + COMPILER_SCRATCH_BYTES),
        name="swiglu_ffn",
    )(hn, w1, w3, w2, x)


def _dense_front(x, ln1, w_in_p, layer):
    xn = _rmsnorm(x, ln1, BF16)
    return _matmul(xn, w_in_p, layer=layer, w_transposed=True, tn_pref=768, name="in_proj")


def _dense_back(x, ya, yb, yc, h, lay, wts, layer, d_model):
    w_br, w_o, w1, w3, w2, ln2 = wts
    mix = _merge(ya, yb, yc, w_br, layer, h, lay, d_model)
    x = _matmul(mix, w_o, layer=layer, residual=x, name="out_proj")
    hn = _rmsnorm(x, ln2, BF16)
    return _ffn(hn, w1, w3, w2, layer, x)


def kernel(x_prompt, x_sample, cache_k, cache_v, cache_kidx, state_conv, state_delta, page_table, ln1, w_in,
           a_ln_g, a_ln_b, a_ws, a_bs, b_conv_w, b_a_log, b_dt_bias, b_out_g, w_br, w_o, ln2, ffn_w1, ffn_w3,
           ffn_w2, ln_f):
    n_bp, seq, d_model = x_prompt.shape
    n_bs, t_new, _ = x_sample.shape
    depth = ln1.shape[0]
    n_phys = cache_k.shape[1]
    n_pages = page_table.shape[1]
    assert seq % A_CHUNK == 0 and seq % GDN_CHUNK == 0 and CONV_W - 1 <= t_new <= min(A_CHUNK, GDN_CHUNK)
    lay = _layout(d_model)
    hd = HEAD_DIM

    xp = x_prompt.reshape(n_bp * seq, d_model)
    xs = x_sample.reshape(n_bs * t_new, d_model)
    pt_flat = page_table.reshape(-1).astype(I32)
    k_pool = cache_k.reshape(depth, n_phys, PAGE_SIZE * C_KV_HEADS, hd)
    v_pool = cache_v.reshape(depth, n_phys, PAGE_SIZE * C_KV_HEADS, hd)
    conv0_p = jnp.zeros((n_bp, SUBLANE, 3 * B_WIDTH), F32)
    s0_p = jnp.zeros((n_bp, B_HEADS, hd, hd), F32)
    qkv_off = lay["bq"]

    outs = {k: [] for k in ("pk", "pv", "pik", "pconv", "pdelta", "sk", "sv", "sik", "sconv", "sdelta", "schunk")}
    ff_pad = -ffn_w1.shape[-1] % FFN_TF
    dense_w = (w_br.astype(BF16), w_o.astype(BF16),
               jnp.pad(ffn_w1.astype(BF16), ((0, 0), (0, 0), (0, ff_pad))),
               jnp.pad(ffn_w3.astype(BF16), ((0, 0), (0, 0), (0, ff_pad))),
               jnp.pad(ffn_w2.astype(BF16), ((0, 0), (0, ff_pad), (0, 0))))
    w_in_p = _pack_w_in(w_in, d_model)
    for l in range(depth):
        wts = dense_w + (ln2[l],)

        h = _dense_front(xp, ln1[l], w_in_p, l)
        ya = _branch_a(h, lay, a_ws[l], a_bs[l], a_ln_g[l], a_ln_b[l], chunk=A_CHUNK, tb=A_CHUNK, emit_av=False)
        srcs = [(h, lay[n] // B_WIDTH) for n in ("bq", "bk", "bv", "bz")] + [(h, lay["small"] // LANE)]
        yb, s_new = _gdn(srcs, conv0_p, s0_p, b_conv_w[l], b_a_log[l], b_dt_bias[l], b_out_g[l],
                         n_b=n_bp, t_pad=seq, t_valid=GDN_CHUNK, chunk=GDN_CHUNK)
        yc = _dsa_prompt(h, lay, n_b=n_bp, seq=seq)
        h3 = h.reshape(n_bp, seq, lay["total"])
        outs["pk"].append(h3[:, :, lay["ck"]:lay["ck"] + C_KV].reshape(n_bp, seq, C_KV_HEADS, hd))
        outs["pv"].append(h3[:, :, lay["cv"]:lay["cv"] + C_KV].reshape(n_bp, seq, C_KV_HEADS, hd))
        outs["pik"].append(h3[:, :, lay["ik"]:lay["ik"] + IDX_DIM])
        outs["pconv"].append(h3[:, seq - (CONV_W - 1):, qkv_off:qkv_off + 3 * B_WIDTH])
        outs["pdelta"].append(s_new)
        xp = _dense_back(xp, ya, yb, yc, h, lay, wts, l, d_model)

        hs = _dense_front(xs, ln1[l], w_in_p, l)
        ya, av = _branch_a(hs, lay, a_ws[l], a_bs[l], a_ln_g[l], a_ln_b[l], chunk=t_new, tb=n_bs * t_new,
                           emit_av=True)
        hs3 = hs.reshape(n_bs, t_new, lay["total"])
        padt = lambda x: jnp.pad(x, ((0, 0), (0, GDN_CHUNK - t_new), (0, 0))).reshape(n_bs * GDN_CHUNK, -1)
        srcs = [(padt(hs3[:, :, lay[n]:lay[n] + B_WIDTH]), 0) for n in ("bq", "bk", "bv", "bz")]
        srcs.append((padt(hs3[:, :, lay["small"]:lay["small"] + LANE]), 0))
        conv0_s = jnp.pad(state_conv[l], ((0, 0), (SUBLANE - (CONV_W - 1), 0), (0, 0)))
        yb, s_new = _gdn(srcs, conv0_s, state_delta[l], b_conv_w[l], b_a_log[l], b_dt_bias[l], b_out_g[l],
                         n_b=n_bs, t_pad=GDN_CHUNK, t_valid=t_new, chunk=GDN_CHUNK)
        yb = yb.reshape(n_bs, GDN_CHUNK, B_WIDTH)[:, :t_new].reshape(n_bs * t_new, B_WIDTH)
        yc = _dsa_sample(hs, lay, pt_flat, k_pool, v_pool, cache_kidx, l, n_b=n_bs, t_new=t_new, n_pages=n_pages)
        outs["sk"].append(hs3[:, :, lay["ck"]:lay["ck"] + C_KV].reshape(n_bs, t_new, C_KV_HEADS, hd))
        outs["sv"].append(hs3[:, :, lay["cv"]:lay["cv"] + C_KV].reshape(n_bs, t_new, C_KV_HEADS, hd))
        outs["sik"].append(hs3[:, :, lay["ik"]:lay["ik"] + IDX_DIM])
        outs["sconv"].append(hs3[:, t_new - (CONV_W - 1):, qkv_off:qkv_off + 3 * B_WIDTH])
        outs["sdelta"].append(s_new)
        outs["schunk"].append(av.reshape(n_bs, t_new, A_WIDTH))
        xs = _dense_back(xs, ya, yb, yc, hs, lay, wts, l, d_model)

    y_prompt = _rmsnorm(xp, ln_f, F32).reshape(n_bp, seq, d_model)
    y_sample = _rmsnorm(xs, ln_f, F32).reshape(n_bs, t_new, d_model)
    st = lambda k: jnp.stack(outs[k])
    return (y_prompt, y_sample, st("pk"), st("pv"), st("pik"), st("pconv"), st("pdelta"),
            st("sk"), st("sv"), st("sik"), st("sconv"), st("sdelta"), st("schunk"))
```

```python
import functools

import jax
import jax.numpy as jnp
from jax import lax
from jax.experimental import pallas as pl
from jax.experimental.pallas import tpu as pltpu

F32 = jnp.float32
BF16 = jnp.bfloat16
I32 = jnp.int32

HEAD_DIM = 128
A_GROUPS = 8
A_CHUNK = 128
A_WIDTH = A_GROUPS * HEAD_DIM
B_HEADS = 12
B_WIDTH = B_HEADS * HEAD_DIM
CONV_W = 4
C_HEADS = 12
C_KV_HEADS = 4
C_GROUP = C_HEADS // C_KV_HEADS
C_Q = C_HEADS * HEAD_DIM
C_KV = C_KV_HEADS * HEAD_DIM
IDX_HEADS = 16
IDX_DIM = 128
TOPK_MAX = 256
PAGE_SIZE = 128
N_BRANCH = 3
RMS_EPS = 1e-6
LN_EPS = 1e-5

LANE = 128
SUBLANE = 8
V7X_VMEM_BYTES = 64 * 1024 * 1024
VMEM_BUDGET = 56 * 1024 * 1024
COMPILER_SCRATCH_BYTES = 8 * 1024 * 1024

GDN_CHUNK = 128
DSA_TQ = 256
DSA_TK = 512
DSA_ROW_TILE = 32
FFN_TF = 256
PAGES_PER_STEP = 16
MASK_NEG = -1e30
LOG2E = 1.4426950408889634
INT_MIN = -2147483648
KEY_OF_NEG_INF = -2139095041


def _cparams(semantics, vmem_bytes):
    return pltpu.CompilerParams(dimension_semantics=semantics,
                                vmem_limit_bytes=int(min(max(vmem_bytes, 16 * 1024 * 1024), VMEM_BUDGET)))


def _pick(dim, pref):
    t = pref
    while t >= SUBLANE:
        if dim % t == 0:
            return t
        t //= 2
    return dim


def _layout(d_model):
    segs = [("gate", N_BRANCH * d_model, d_model), ("bq", B_WIDTH, B_WIDTH), ("bk", B_WIDTH, B_WIDTH),
            ("bv", B_WIDTH, B_WIDTH), ("bz", B_WIDTH, B_WIDTH), ("cq", C_Q, C_Q), ("ck", C_KV, C_KV),
            ("au", A_WIDTH, A_WIDTH), ("av", A_WIDTH, A_WIDTH), ("cv", C_KV, C_KV),
            ("iq", IDX_HEADS * IDX_DIM, 4 * IDX_DIM), ("ik", IDX_DIM, IDX_DIM), ("small", LANE, LANE)]
    off, lay = 0, {}
    for name, width, align in segs:
        assert off % align == 0, (name, off, align)
        lay[name] = off
        off += width
    lay["total"] = off
    return lay


def _pack_body(tab_ref, main_ref, ab_ref, iw_ref, o_ref):
    j = pl.program_id(1)
    last = pl.num_programs(1) - 1

    @pl.when(j != last)
    def _():
        o_ref[...] = main_ref[0].astype(o_ref.dtype)

    @pl.when(j == last)
    def _():
        n_ab, n_iw = 2 * B_HEADS, IDX_HEADS
        rest = jnp.zeros((o_ref.shape[0] - n_ab - n_iw, o_ref.shape[1]), F32)
        o_ref[...] = jnp.concatenate([ab_ref[0, 0:n_ab, :], iw_ref[0, 0:n_iw, :], rest], axis=0).astype(o_ref.dtype)


def _pack_w_in(w_in, d_model):
    depth, d, in_width = w_in.shape
    lay = _layout(d_model)
    w_t = jnp.swapaxes(w_in, 1, 2)
    widths = (A_WIDTH, A_WIDTH, 3 * B_WIDTH, B_WIDTH, 2 * B_HEADS, C_Q, C_KV, C_KV,
              IDX_HEADS * IDX_DIM, IDX_HEADS, IDX_DIM, N_BRANCH * d_model)
    dsts = (lay["au"], lay["av"], lay["bq"], lay["bz"], None, lay["cq"], lay["ck"], lay["cv"], lay["iq"], None,
            lay["ik"], lay["gate"])
    n_blocks = lay["total"] // LANE
    table, start, special = [0] * n_blocks, 0, []
    for width, dst in zip(widths, dsts):
        if dst is None:
            special.append(start)
        else:
            assert width % LANE == 0 and dst % LANE == 0 and start % SUBLANE == 0
            for blk in range(width // LANE):
                table[dst // LANE + blk] = (start + blk * LANE) // SUBLANE
        start += width
    assert start == in_width and lay["small"] // LANE == n_blocks - 1
    ab_row, iw_row = special
    assert ab_row % SUBLANE == 0 and iw_row % SUBLANE == 0 and max(ab_row, iw_row) + LANE <= in_width
    window = lambda index_map: pl.BlockSpec((pl.Element(1), pl.Element(LANE), pl.Element(d)), index_map)
    grid_spec = pltpu.PrefetchScalarGridSpec(
        num_scalar_prefetch=1,
        grid=(depth, n_blocks),
        in_specs=[window(lambda l, j, tab: (l, tab[j] * SUBLANE, 0)),
                  window(lambda l, j, tab: (l, ab_row, 0)),
                  window(lambda l, j, tab: (l, iw_row, 0))],
        out_specs=pl.BlockSpec((None, LANE, d), lambda l, j, tab: (l, j, 0)))
    return pl.pallas_call(
        _pack_body,
        out_shape=jax.ShapeDtypeStruct((depth, lay["total"], d), BF16),
        grid_spec=grid_spec,
        compiler_params=_cparams(("parallel", "arbitrary"), 8 * LANE * d * 4),
        name="pack_in_proj_weight",
    )(jnp.asarray(table, I32), w_t, w_t, w_t)


def _rmsnorm_body(x_ref, g_ref, o_ref):
    x = x_ref[...]
    ms = jnp.mean(x * x, axis=-1, keepdims=True)
    o_ref[...] = (x * lax.rsqrt(ms + RMS_EPS) * g_ref[...]).astype(o_ref.dtype)


def _rmsnorm(x, g, out_dtype):
    m, d = x.shape
    tm = _pick(m, 256)
    return pl.pallas_call(
        _rmsnorm_body,
        out_shape=jax.ShapeDtypeStruct((m, d), out_dtype),
        grid=(m // tm,),
        in_specs=[pl.BlockSpec((tm, d), lambda i: (i, 0)), pl.BlockSpec((1, d), lambda i: (0, 0))],
        out_specs=pl.BlockSpec((tm, d), lambda i: (i, 0)),
        compiler_params=_cparams(("parallel",), 6 * tm * d * 4),
        name="rmsnorm",
    )(x, g.reshape(1, d))


def _mm_body(a_ref, w_ref, o_ref):
    o_ref[...] = jnp.dot(a_ref[...], w_ref[...], preferred_element_type=F32).astype(o_ref.dtype)


def _mm_res_body(a_ref, w_ref, r_ref, o_ref):
    o_ref[...] = (r_ref[...] + jnp.dot(a_ref[...], w_ref[...], preferred_element_type=F32)).astype(o_ref.dtype)


def _mm_wt_body(a_ref, wt_ref, o_ref):
    o_ref[...] = lax.dot_general(a_ref[...], wt_ref[...], (((1,), (1,)), ((), ())),
                                 preferred_element_type=F32).astype(o_ref.dtype)


def _layer_spec(block, index_map, layer):
    if layer is None:
        return pl.BlockSpec(block, index_map)
    return pl.BlockSpec((None,) + block, lambda *g: (layer,) + index_map(*g))


def _matmul(a, w, *, layer=None, w_transposed=False, residual=None, out_dtype=F32, tm_pref=1024, tn_pref=1024,
            name="matmul"):
    m, k = a.shape
    n = w.shape[-2] if w_transposed else w.shape[-1]
    tm = _pick(m, tm_pref)
    tn = tn_pref if n % tn_pref == 0 else _pick(n, tn_pref)
    osz = jnp.dtype(out_dtype).itemsize
    vmem = 2 * (tm * k * 2 + k * tn * 2 + tm * tn * osz) + tm * tn * 4
    w_spec = (_layer_spec((tn, k), lambda i, j: (j, 0), layer) if w_transposed
              else _layer_spec((k, tn), lambda i, j: (0, j), layer))
    in_specs = [pl.BlockSpec((tm, k), lambda i, j: (i, 0)), w_spec]
    args = [a, w]
    body = _mm_wt_body if w_transposed else _mm_body
    assert residual is None or not w_transposed
    if residual is not None:
        in_specs.append(pl.BlockSpec((tm, tn), lambda i, j: (i, j)))
        args.append(residual)
        body = _mm_res_body
        vmem += 2 * tm * tn * 4
    return pl.pallas_call(
        body,
        out_shape=jax.ShapeDtypeStruct((m, n), out_dtype),
        grid=(m // tm, n // tn),
        in_specs=in_specs,
        out_specs=pl.BlockSpec((tm, tn), lambda i, j: (i, j)),
        compiler_params=_cparams(("parallel", "arbitrary"), vmem + COMPILER_SCRATCH_BYTES),
        name=name,
    )(*args)


def _branch_a_body(u_ref, v_ref, w_ref, bs_ref, g_ref, b_ref, y_ref, *av_ref, chunk):
    tb = u_ref.shape[0]
    u = jax.nn.gelu(u_ref[...])
    v = jax.nn.gelu(v_ref[...])
    mu = jnp.mean(v, axis=-1, keepdims=True)
    var = jnp.mean(jnp.square(v - mu), axis=-1, keepdims=True)
    vn = (v - mu) * lax.rsqrt(var + LN_EPS) * g_ref[...] + b_ref[...]
    if av_ref:
        av_ref[0][...] = vn
    row = lax.broadcasted_iota(I32, (tb, tb), 0)
    col = lax.broadcasted_iota(I32, (tb, tb), 1)
    keep = (col <= row) & ((row // chunk) == (col // chunk))
    vb = vn.astype(BF16)
    for g in range(A_GROUPS):
        sl = slice(g * HEAD_DIM, (g + 1) * HEAD_DIM)
        wg = jnp.where(keep, w_ref[g], 0.0).astype(BF16)
        s = jnp.dot(wg, vb[:, sl], preferred_element_type=F32) + bs_ref[:, g:g + 1]
        y_ref[:, sl] = (u[:, sl] * s).astype(y_ref.dtype)


def _branch_a(h, lay, a_ws, a_bs, ln_g, ln_b, *, chunk, tb, emit_av):
    m = h.shape[0]
    reps = tb // chunk
    wfull = jnp.tile(a_ws[:, :chunk, :chunk], (1, reps, reps))
    bs_t = jnp.tile(a_bs[:, :chunk].T, (reps, 1))
    cu, cv = lay["au"] // A_WIDTH, lay["av"] // A_WIDTH
    out_shape = [jax.ShapeDtypeStruct((m, A_WIDTH), BF16)]
    out_specs = [pl.BlockSpec((tb, A_WIDTH), lambda i: (i, 0))]
    if emit_av:
        out_shape.append(jax.ShapeDtypeStruct((m, A_WIDTH), F32))
        out_specs.append(pl.BlockSpec((tb, A_WIDTH), lambda i: (i, 0)))
    res = pl.pallas_call(
        functools.partial(_branch_a_body, chunk=chunk),
        out_shape=out_shape,
        grid=(m // tb,),
        in_specs=[pl.BlockSpec((tb, A_WIDTH), lambda i: (i, cu)),
                  pl.BlockSpec((tb, A_WIDTH), lambda i: (i, cv)),
                  pl.BlockSpec((A_GROUPS, tb, tb), lambda i: (0, 0, 0)),
                  pl.BlockSpec((tb, A_GROUPS), lambda i: (0, 0)),
                  pl.BlockSpec((1, A_WIDTH), lambda i: (0, 0)),
                  pl.BlockSpec((1, A_WIDTH), lambda i: (0, 0))],
        out_specs=out_specs,
        compiler_params=_cparams(("parallel",), 32 << 20),
        name="branch_a_gmlp",
    )(h, h, wfull, bs_t, ln_g.reshape(1, A_WIDTH), ln_b.reshape(1, A_WIDTH))
    return res if emit_av else res[0]


def _bdot(a, b):
    return lax.dot_general(a.astype(BF16), b.astype(BF16), (((2,), (1,)), ((0,), (0,))),
                           preferred_element_type=F32)


def _bdot_nt(a, b):
    return lax.dot_general(a.astype(BF16), b.astype(BF16), (((2,), (2,)), ((0,), (0,))),
                           preferred_element_type=F32)


def _transpose_rows(x):
    r = x.shape[0]
    if r < LANE:
        x = jnp.concatenate([x, jnp.zeros((LANE - r, LANE), x.dtype)], axis=0)
    return x.T[:, :r]


def _gdn_body(q_ref, k_ref, v_ref, z_ref, sm_ref, c0_ref, s0_ref, cw_ref, al_ref, dt_ref, og_ref,
              y_ref, sout_ref, ext_ref, s_ref, *, chunk, t_valid):
    c = pl.program_id(1)
    nc = pl.num_programs(1)
    hd = HEAD_DIM

    @pl.when(c == 0)
    def _():
        s_ref[...] = s0_ref[0]
        for j in range(3):
            ext_ref[j, 0:SUBLANE, :] = c0_ref[0, :, j * B_WIDTH:(j + 1) * B_WIDTH]

    acts = []
    for j, ref in enumerate((q_ref, k_ref, v_ref)):
        ext_ref[j, SUBLANE:SUBLANE + chunk, :] = ref[...]
        acc = None
        for i in range(CONV_W):
            lo = SUBLANE - (CONV_W - 1) + i
            term = ext_ref[j, lo:lo + chunk, :] * cw_ref[i:i + 1, j * B_WIDTH:(j + 1) * B_WIDTH]
            acc = term if acc is None else acc + term
        acts.append(jax.nn.silu(acc))
        ext_ref[j, 0:SUBLANE, :] = ext_ref[j, chunk:chunk + SUBLANE, :]
    qa, ka, va = acts

    sm = sm_ref[...]
    row1 = lax.broadcasted_iota(I32, (chunk, LANE), 0)
    g_all = -jnp.exp(al_ref[...]) * jax.nn.softplus(sm + dt_ref[...])
    beta_all = jax.nn.sigmoid(sm)
    if t_valid < chunk:
        g_all = jnp.where(row1 < t_valid, g_all, 0.0)
        beta_all = jnp.where(row1 < t_valid, beta_all, 0.0)
    gc_all = g_all
    d = 1
    while d < chunk:
        gc_all = gc_all + jnp.where(row1 >= d, pltpu.roll(gc_all, d, 0), 0.0)
        d *= 2
    gc_t = _transpose_rows(gc_all)

    row = lax.broadcasted_iota(I32, (chunk, chunk), 0)
    col = lax.broadcasted_iota(I32, (chunk, chunk), 1)
    incl = row >= col
    strict = row > col
    eye = jnp.where(row == col, 1.0, 0.0)
    pair_masks = []
    bs = 1
    while bs < chunk:
        pair_masks.append(((row // bs) % 2 == 1) & ((col // bs) == (row // bs) - 1))
        bs *= 2

    heads = range(B_HEADS)
    per_head = lambda x: jnp.stack([x[:, h * hd:(h + 1) * hd] for h in heads], axis=0)
    q3, k3, v3 = per_head(qa), per_head(ka), per_head(va)
    qn = q3 * lax.rsqrt(jnp.sum(q3 * q3, axis=-1, keepdims=True) + 1e-6) * (hd ** -0.5)
    kn = k3 * lax.rsqrt(jnp.sum(k3 * k3, axis=-1, keepdims=True) + 1e-6)
    beta = jnp.stack([beta_all[:, B_HEADS + h:B_HEADS + h + 1] for h in heads], axis=0)
    gcol = jnp.stack([gc_all[:, h:h + 1] for h in heads], axis=0)
    grow = jnp.stack([gc_t[h:h + 1, :] for h in heads], axis=0)
    decay = jnp.exp(jnp.where(incl[None], gcol - grow, -jnp.inf))
    kb = kn * beta
    eg = jnp.exp(gcol)
    lmat = jnp.where(strict[None], _bdot_nt(kb, kn) * decay, 0.0)
    attn = _bdot_nt(qn, kn) * decay
    tinv = eye[None] - jnp.where(pair_masks[0][None], lmat, 0.0)
    for pm in pair_masks[1:]:
        tinv = tinv - _bdot(tinv, _bdot(jnp.where(pm[None], lmat, 0.0), tinv))
    sol = _bdot(tinv, jnp.concatenate([v3 * beta, kb * eg], axis=-1))
    value, kcd = sol[:, :, :hd], sol[:, :, hd:]
    s_old = s_ref[...]
    vnew = value - _bdot(kcd, s_old)
    o = _bdot(qn * eg, s_old) + _bdot(attn, vnew)
    glast = gcol[:, chunk - 1:chunk, :]
    kend = kn * jnp.exp(glast - gcol)
    s_ref[...] = s_old * jnp.exp(glast) + _bdot(jnp.swapaxes(kend, 1, 2), vnew)
    on = o * lax.rsqrt(jnp.mean(o * o, axis=-1, keepdims=True) + RMS_EPS) * og_ref[...]
    for h in heads:
        sl = slice(h * hd, (h + 1) * hd)
        y_ref[:, sl] = (on[h] * jax.nn.silu(z_ref[:, sl])).astype(y_ref.dtype)

    @pl.when(c == nc - 1)
    def _():
        sout_ref[0] = s_ref[...]


def _gdn(srcs, conv0, s0, conv_w, a_log, dt_bias, o_g, *, n_b, t_pad, t_valid, chunk):
    nc = t_pad // chunk
    arrs = [a for a, _ in srcs]
    cbs = [cb for _, cb in srcs]
    widths = [B_WIDTH] * 4 + [LANE]

    def tok_spec(w, cb):
        return pl.BlockSpec((chunk, w), lambda b, c: (b * nc + c, cb))

    pad12 = lambda x: jnp.zeros((1, LANE), F32).at[0, :B_HEADS].set(x)
    return pl.pallas_call(
        functools.partial(_gdn_body, chunk=chunk, t_valid=t_valid),
        out_shape=[jax.ShapeDtypeStruct((n_b * t_pad, B_WIDTH), BF16),
                   jax.ShapeDtypeStruct((n_b, B_HEADS, HEAD_DIM, HEAD_DIM), F32)],
        grid=(n_b, nc),
        in_specs=[tok_spec(w, cb) for w, cb in zip(widths, cbs)] + [
            pl.BlockSpec((1, SUBLANE, 3 * B_WIDTH), lambda b, c: (b, 0, 0)),
            pl.BlockSpec((1, B_HEADS, HEAD_DIM, HEAD_DIM), lambda b, c: (b, 0, 0, 0)),
            pl.BlockSpec((CONV_W, 3 * B_WIDTH), lambda b, c: (0, 0)),
            pl.BlockSpec((1, LANE), lambda b, c: (0, 0)),
            pl.BlockSpec((1, LANE), lambda b, c: (0, 0)),
            pl.BlockSpec((1, HEAD_DIM), lambda b, c: (0, 0))],
        out_specs=[pl.BlockSpec((chunk, B_WIDTH), lambda b, c: (b * nc + c, 0)),
                   pl.BlockSpec((1, B_HEADS, HEAD_DIM, HEAD_DIM), lambda b, c: (b, 0, 0, 0))],
        scratch_shapes=[pltpu.VMEM((3, chunk + SUBLANE, B_WIDTH), F32),
                        pltpu.VMEM((B_HEADS, HEAD_DIM, HEAD_DIM), F32)],
        compiler_params=_cparams(("parallel", "arbitrary"), 40 << 20),
        name="branch_b_gated_delta",
    )(*arrs, conv0, s0, conv_w, pad12(a_log), pad12(dt_bias), o_g.reshape(1, HEAD_DIM))


def _sort_key(x):
    b = lax.bitcast_convert_type(x + 0.0, I32)
    return b ^ ((b >> 31) & 0x7FFFFFFF)


def _select_rule(count, shape, k, nbits):
    kf = jnp.float32(k)
    count_ge = lambda cand: count(lambda key, pos: key >= cand)
    t0 = jnp.where(count_ge(jnp.zeros(shape, I32)) >= kf, 0, INT_MIN).astype(I32)

    def bit_step(i, t):
        cand = t + lax.shift_left(jnp.int32(1), 30 - i)
        return jnp.where(count_ge(cand) >= kf, cand, t)

    t = lax.fori_loop(0, 31, bit_step, t0)
    n_ge = count_ge(t)
    tied = (n_ge > kf) & (t > KEY_OF_NEG_INF)

    def tie_index():
        r = kf - count(lambda key, pos: key > t)

        def idx_step(i, j):
            cand = j + lax.shift_left(jnp.int32(1), nbits - 1 - i)
            below = count(lambda key, pos: (key == t) & (pos <= cand - 1))
            return jnp.where(below < r, cand, j)

        return lax.fori_loop(0, nbits, idx_step, jnp.zeros(shape, I32))

    no_limit = jnp.full(shape, 2 ** 30, I32)
    j = lax.cond(jnp.sum(jnp.where(tied, 1.0, 0.0)) > 0.0,
                 lambda: jnp.where(tied, tie_index(), no_limit), lambda: no_limit)
    return t, j


def _selected(key, pos, t, j):
    return (key > t) | ((key == t) & (pos <= j))


def _dsa_prompt_body(cq_ref, iq0_ref, iq1_ref, iq2_ref, iq3_ref, sm_ref, ik_ref, ck_ref, cv_ref, y_ref,
                     key_ref, iqb_ref, wb_ref, qb_ref, m_ref, l_ref, acc_ref, p_ref, al_ref,
                     *, k_sel, nbits, tq, tk):
    rows3 = C_GROUP * tq
    qi = pl.program_id(1)
    kpg = tk // LANE
    nkb = (qi + 1) * (tq // LANE)
    ngrp = ((qi + 1) * tq + tk - 1) // tk
    hd = HEAD_DIM
    nt = (((1,), (1,)), ((), ()))
    sub = LANE // SUBLANE
    qpos3 = qi * tq + lax.broadcasted_iota(I32, (sub, SUBLANE, tq), 2)
    kofs3 = lax.broadcasted_iota(I32, (sub, SUBLANE, tq), 0) * SUBLANE + lax.broadcasted_iota(I32, (sub, SUBLANE, tq), 1)

    w_all = sm_ref[...] * (IDX_HEADS ** -0.5 * IDX_DIM ** -0.5)
    w_t = jnp.concatenate([w_all[r:r + LANE].T for r in range(0, tq, LANE)], axis=1)
    w_off = 2 * B_HEADS
    for hh in range(IDX_HEADS):
        ref = (iq0_ref, iq1_ref, iq2_ref, iq3_ref)[hh // 4]
        iqb_ref[hh] = ref[:, (hh % 4) * IDX_DIM:(hh % 4 + 1) * IDX_DIM].astype(BF16)
        wb_ref[hh] = jnp.broadcast_to(w_t[w_off + hh:w_off + hh + 1, :], (SUBLANE, tq))

    def score_grp(g, carry):
        ikg = ik_ref[pl.ds(pl.multiple_of(g * tk, tk), tk), :].astype(BF16)
        acc = jnp.zeros((tk // SUBLANE, SUBLANE, tq), F32)
        for hh in range(IDX_HEADS):
            s = lax.dot_general(ikg, iqb_ref[hh], nt, preferred_element_type=F32)
            acc = acc + wb_ref[hh] * jnp.maximum(s, 0.0).reshape(tk // SUBLANE, SUBLANE, tq)
        for t in range(kpg):
            kpos3 = (g * kpg + t) * LANE + kofs3
            blk = acc[t * sub:(t + 1) * sub]
            key_ref[g * kpg + t] = _sort_key(jnp.where(kpos3 <= qpos3, blk, -jnp.inf))
        return carry

    lax.fori_loop(0, ngrp, score_grp, 0)

    def count(pred):
        def blk(b, acc):
            return acc + jnp.sum(jnp.where(pred(key_ref[b], b * LANE + kofs3), 1.0, 0.0), axis=0)
        acc = lax.fori_loop(0, nkb, blk, jnp.zeros((SUBLANE, tq), F32))
        return jnp.broadcast_to(jnp.sum(acc, axis=0, keepdims=True), (SUBLANE, tq))

    t8, j8 = _select_rule(count, (SUBLANE, tq), k_sel, nbits)

    for kvh in range(C_KV_HEADS):
        for g in range(C_GROUP):
            hsl = slice((kvh * C_GROUP + g) * hd, (kvh * C_GROUP + g + 1) * hd)
            qb_ref[kvh, g * tq:(g + 1) * tq, :] = cq_ref[:, hsl].astype(BF16)
    m_ref[...] = jnp.full(m_ref.shape, MASK_NEG, F32)
    l_ref[...] = jnp.zeros(l_ref.shape, F32)
    acc_ref[...] = jnp.zeros(acc_ref.shape, F32)
    ones = jnp.ones((tk, hd), BF16)

    def attend_grp(g, carry):
        bias_t = []
        for t in range(kpg):
            kpos3 = (g * kpg + t) * LANE + kofs3
            sel = _selected(key_ref[g * kpg + t], kpos3, t8, j8) & (kpos3 <= qpos3)
            kq = jnp.where(sel, 0.0, MASK_NEG).reshape(LANE, tq)
            bias_t.append(jnp.concatenate([kq[:, r:r + LANE].T for r in range(0, tq, LANE)], axis=0))
        bias = jnp.concatenate(bias_t, axis=1)
        start = pl.multiple_of(g * tk, tk)
        kgrp = ck_ref[pl.ds(start, tk), :].astype(BF16)
        vgrp = cv_ref[pl.ds(start, tk), :].astype(BF16)
        for kvh in range(C_KV_HEADS):
            sl = slice(kvh * hd, (kvh + 1) * hd)
            s = lax.dot_general(qb_ref[kvh], kgrp[:, sl], nt, preferred_element_type=F32)
            for r0 in range(0, rows3, DSA_ROW_TILE):
                rs = slice(r0, r0 + DSA_ROW_TILE)
                x = s[rs] * (hd ** -0.5 * LOG2E) + bias[r0 % tq:r0 % tq + DSA_ROW_TILE]
                m_old = m_ref[kvh, rs, :]
                m_new = jnp.maximum(m_old, jnp.max(x, axis=-1, keepdims=True))
                p_ref[rs, :] = jnp.exp2(x - m_new).astype(BF16)
                al_ref[rs, :] = jnp.exp2(m_old - m_new)
                m_ref[kvh, rs, :] = m_new
            pv = jnp.dot(p_ref[...], jnp.concatenate([vgrp[:, sl], ones], axis=1), preferred_element_type=F32)
            alpha = al_ref[...]
            l_ref[kvh] = alpha * l_ref[kvh] + pv[:, hd:hd + 1]
            acc_ref[kvh] = alpha * acc_ref[kvh] + pv[:, :hd]
        return carry

    lax.fori_loop(0, ngrp, attend_grp, 0)

    for kvh in range(C_KV_HEADS):
        o = acc_ref[kvh] / l_ref[kvh]
        for g in range(C_GROUP):
            hsl = slice((kvh * C_GROUP + g) * hd, (kvh * C_GROUP + g + 1) * hd)
            y_ref[:, hsl] = o[g * tq:(g + 1) * tq, :].astype(y_ref.dtype)


def _dsa_prompt(h, lay, *, n_b, seq):
    tq, tk = DSA_TQ, DSA_TK
    assert seq % tq == 0 and seq % tk == 0
    nqb = seq // tq
    k_sel = min(TOPK_MAX, seq // 4)
    nbits = max(1, (seq - 1).bit_length())
    iq_cb = lay["iq"] // (4 * IDX_DIM)
    once = pl.Buffered(1)

    def q_spec(w, cb):
        return pl.BlockSpec((tq, w), lambda b, q: (b * nqb + q, cb))

    def kv_spec(w, cb):
        return pl.BlockSpec((seq, w), lambda b, q: (b, cb), pipeline_mode=once)

    rows3 = C_GROUP * tq
    lane_pad = lambda r: r * LANE * 4
    vmem = (seq * (2 * C_KV + IDX_DIM) * 4 + 2 * tq * (C_Q + IDX_HEADS * IDX_DIM + LANE) * 4 + 2 * tq * C_Q * 2
            + (seq // LANE) * tq * LANE * 4 + IDX_HEADS * tq * LANE * 6 + C_KV_HEADS * rows3 * HEAD_DIM * 6
            + 2 * C_KV_HEADS * lane_pad(rows3) + 4 * rows3 * tk * 4)
    return pl.pallas_call(
        functools.partial(_dsa_prompt_body, k_sel=k_sel, nbits=nbits, tq=tq, tk=tk),
        out_shape=jax.ShapeDtypeStruct((n_b * seq, C_Q), BF16),
        grid=(n_b, nqb),
        in_specs=[q_spec(C_Q, lay["cq"] // C_Q)]
                 + [q_spec(4 * IDX_DIM, iq_cb + i) for i in range(4)]
                 + [q_spec(LANE, lay["small"] // LANE),
                    kv_spec(IDX_DIM, lay["ik"] // IDX_DIM),
                    kv_spec(C_KV, lay["ck"] // C_KV),
                    kv_spec(C_KV, lay["cv"] // C_KV)],
        out_specs=pl.BlockSpec((tq, C_Q), lambda b, q: (b * nqb + q, 0)),
        scratch_shapes=[pltpu.VMEM((seq // LANE, LANE // SUBLANE, SUBLANE, tq), I32),
                        pltpu.VMEM((IDX_HEADS, tq, IDX_DIM), BF16),
                        pltpu.VMEM((IDX_HEADS, SUBLANE, tq), F32),
                        pltpu.VMEM((C_KV_HEADS, rows3, HEAD_DIM), BF16),
                        pltpu.VMEM((C_KV_HEADS, rows3, 1), F32),
                        pltpu.VMEM((C_KV_HEADS, rows3, 1), F32),
                        pltpu.VMEM((C_KV_HEADS, rows3, HEAD_DIM), F32),
                        pltpu.VMEM((rows3, tk), BF16),
                        pltpu.VMEM((rows3, 1), F32)],
        compiler_params=_cparams(("parallel", "arbitrary"), vmem + COMPILER_SCRATCH_BYTES),
        name="branch_c_prompt_dsa",
    )(h, h, h, h, h, h, h, h, h)


def _dsa_sample_select_body(pt_ref, *refs, n_pages, k_sel, nbits, t_new):
    pp = PAGES_PER_STEP
    page_refs = refs[:pp]
    iq_ref, w_ref, ikn_ref, bias_ref, key_ref = refs[pp:]
    j = pl.program_id(1)
    n_steps = n_pages // pp
    rows = t_new
    col = lax.broadcasted_iota(I32, (rows, LANE), 1)
    trow = lax.broadcasted_iota(I32, (rows, LANE), 0)
    iq = iq_ref[0].astype(BF16)

    def scores(keys_f32):
        n = keys_f32.shape[0]
        s = lax.dot_general(iq, keys_f32.astype(BF16), (((1,), (1,)), ((), ())), preferred_element_type=F32)
        r = jnp.maximum(s, 0.0) * w_ref[0]
        return jnp.sum(r.reshape(rows, IDX_HEADS, n), axis=1)

    @pl.when(j < n_steps)
    def _():
        keys = _sort_key(scores(jnp.concatenate([r[0, 0] for r in page_refs], axis=0)))
        for i in range(pp):
            key_ref[j * pp + i] = keys[:, i * LANE:(i + 1) * LANE]

    @pl.when(j == n_steps)
    def _():
        new_ok = (col <= trow) & (col < t_new)
        key_ref[n_pages] = _sort_key(jnp.where(new_ok, scores(ikn_ref[0]), -jnp.inf))
        nblk = n_pages + 1
        grp = SUBLANE
        nblk_pad = key_ref.shape[0]
        if nblk_pad > nblk:
            key_ref[nblk:nblk_pad] = jnp.full((nblk_pad - nblk, rows, LANE), KEY_OF_NEG_INF, I32)
        blk3 = lax.broadcasted_iota(I32, (grp, rows, LANE), 0)
        col3 = lax.broadcasted_iota(I32, (grp, rows, LANE), 2)

        def count(pred):
            def trip(i, acc):
                b0 = pl.multiple_of(i * grp, grp)
                hit = pred(key_ref[pl.ds(b0, grp)], (b0 + blk3) * LANE + col3)
                return acc + jnp.sum(jnp.where(hit, 1.0, 0.0), axis=0)
            acc = lax.fori_loop(0, nblk_pad // grp, trip, jnp.zeros((rows, LANE), F32))
            return jnp.broadcast_to(jnp.sum(acc, axis=1, keepdims=True), (rows, LANE))

        tb, jb = _select_rule(count, (rows, LANE), k_sel, nbits)

        def write_blk(b, carry):
            kpos = b * LANE + col
            sel = _selected(key_ref[b], kpos, tb, jb) & ((kpos < n_pages * PAGE_SIZE) | new_ok)
            bias_ref[0, b] = jnp.where(sel, 0.0, MASK_NEG)
            return carry

        lax.fori_loop(0, nblk, write_blk, 0)


def _page_index_map(layer, i, n_pages, n_steps, trailing):
    def index_map(b, j, pt):
        step = jnp.minimum(j, n_steps - 1)
        return (layer, pt[b * n_pages + step * PAGES_PER_STEP + i]) + (0,) * trailing
    return index_map


def _dsa_sample_select(pt_flat, kidx_pool, layer, iq_s, iw_s, ik_new, *, n_b, n_pages, t_new):
    pp = PAGES_PER_STEP
    assert n_pages % pp == 0
    n_steps = n_pages // pp
    total = n_pages * PAGE_SIZE + t_new
    k_sel = min(TOPK_MAX, total // 4)
    nbits = max(1, ((n_pages + 1) * PAGE_SIZE - 1).bit_length())
    rows_q = t_new * IDX_HEADS
    grid_spec = pltpu.PrefetchScalarGridSpec(
        num_scalar_prefetch=1,
        grid=(n_b, n_steps + 1),
        in_specs=[pl.BlockSpec((1, 1, PAGE_SIZE, IDX_DIM), _page_index_map(layer, i, n_pages, n_steps, 2))
                  for i in range(pp)]
                 + [pl.BlockSpec((1, rows_q, IDX_DIM), lambda b, j, pt: (b, 0, 0)),
                    pl.BlockSpec((1, rows_q, 1), lambda b, j, pt: (b, 0, 0)),
                    pl.BlockSpec((1, PAGE_SIZE, IDX_DIM), lambda b, j, pt: (b, 0, 0))],
        out_specs=pl.BlockSpec((1, n_pages + 1, t_new, LANE), lambda b, j, pt: (b, 0, 0, 0)),
        scratch_shapes=[pltpu.VMEM((-(-(n_pages + 1) // SUBLANE) * SUBLANE, t_new, LANE), I32)])
    return pl.pallas_call(
        functools.partial(_dsa_sample_select_body, n_pages=n_pages, k_sel=k_sel, nbits=nbits, t_new=t_new),
        out_shape=jax.ShapeDtypeStruct((n_b, n_pages + 1, t_new, LANE), F32),
        grid_spec=grid_spec,
        compiler_params=_cparams(("parallel", "arbitrary"), 24 << 20),
        name="branch_c_decode_select",
    )(pt_flat, *([kidx_pool] * pp), iq_s, iw_s, ik_new)


def _dsa_sample_attend_body(pt_ref, *refs, n_pages, t_new):
    pp = PAGES_PER_STEP
    k_refs, v_refs = refs[:pp], refs[pp:2 * pp]
    q_ref, bias_ref, kn_ref, vn_ref, o_ref, m_ref, l_ref, acc_ref = refs[2 * pp:]
    j = pl.program_id(1)
    n_steps = n_pages // pp
    hd = HEAD_DIM
    rq = q_ref.shape[1] // C_KV_HEADS
    reps = rq // t_new

    @pl.when(j == 0)
    def _():
        m_ref[...] = jnp.full(m_ref.shape, MASK_NEG, F32)
        l_ref[...] = jnp.zeros(l_ref.shape, F32)
        acc_ref[...] = jnp.zeros(acc_ref.shape, F32)

    def attend(k_of, v_of, bias_t):
        bias = jnp.concatenate([bias_t] * reps, axis=0)
        for kvh in range(C_KV_HEADS):
            rs = slice(kvh * rq, (kvh + 1) * rq)
            s = lax.dot_general(q_ref[0, rs, :].astype(BF16), k_of(kvh).astype(BF16), (((1,), (1,)), ((), ())),
                                preferred_element_type=F32) * (hd ** -0.5) + bias
            m_old = m_ref[rs, :]
            m_new = jnp.maximum(m_old, jnp.max(s, axis=-1, keepdims=True))
            alpha = jnp.exp(m_old - m_new)
            p = jnp.exp(s - m_new)
            l_ref[rs, :] = alpha * l_ref[rs, :] + jnp.sum(p, axis=-1, keepdims=True)
            acc_ref[rs, :] = alpha * acc_ref[rs, :] + jnp.dot(p.astype(BF16), v_of(kvh).astype(BF16),
                                                              preferred_element_type=F32)
            m_ref[rs, :] = m_new

    @pl.when(j < n_steps)
    def _():
        head_rows = lambda kvh: pl.ds(kvh, PAGE_SIZE, stride=C_KV_HEADS)
        pages = lambda prefs: (lambda kvh: jnp.concatenate([r[0, 0, head_rows(kvh), :] for r in prefs], axis=0))
        bias_t = jnp.concatenate([bias_ref[0, j * pp + i] for i in range(pp)], axis=1)
        attend(pages(k_refs), pages(v_refs), bias_t)

    @pl.when(j == n_steps)
    def _():
        new = lambda ref: (lambda kvh: ref[0, :, kvh * hd:(kvh + 1) * hd])
        attend(new(kn_ref), new(vn_ref), bias_ref[0, n_pages])
        o_ref[0] = acc_ref[...] / l_ref[...]


def _dsa_sample_attend(pt_flat, k_pool, v_pool, layer, q_s, bias, k_new, v_new, *, n_b, n_pages, t_new):
    pp = PAGES_PER_STEP
    n_steps = n_pages // pp
    rows = q_s.shape[1]
    page_specs = [pl.BlockSpec((1, 1, PAGE_SIZE * C_KV_HEADS, HEAD_DIM), _page_index_map(layer, i, n_pages, n_steps, 2))
                  for i in range(pp)]
    grid_spec = pltpu.PrefetchScalarGridSpec(
        num_scalar_prefetch=1,
        grid=(n_b, n_steps + 1),
        in_specs=page_specs + page_specs
                 + [pl.BlockSpec((1, rows, HEAD_DIM), lambda b, j, pt: (b, 0, 0)),
                    pl.BlockSpec((1, n_pages + 1, t_new, LANE), lambda b, j, pt: (b, 0, 0, 0)),
                    pl.BlockSpec((1, PAGE_SIZE, C_KV), lambda b, j, pt: (b, 0, 0)),
                    pl.BlockSpec((1, PAGE_SIZE, C_KV), lambda b, j, pt: (b, 0, 0))],
        out_specs=pl.BlockSpec((1, rows, HEAD_DIM), lambda b, j, pt: (b, 0, 0)),
        scratch_shapes=[pltpu.VMEM((rows, 1), F32), pltpu.VMEM((rows, 1), F32), pltpu.VMEM((rows, HEAD_DIM), F32)])
    return pl.pallas_call(
        functools.partial(_dsa_sample_attend_body, n_pages=n_pages, t_new=t_new),
        out_shape=jax.ShapeDtypeStruct((n_b, rows, HEAD_DIM), F32),
        grid_spec=grid_spec,
        compiler_params=_cparams(("parallel", "arbitrary"), 32 << 20),
        name="branch_c_decode_attend",
    )(pt_flat, *([k_pool] * pp), *([v_pool] * pp), q_s, bias, k_new, v_new)


def _dsa_sample(h_s, lay, pt_flat, k_pool, v_pool, kidx_pool, layer, *, n_b, t_new, n_pages):
    hd = HEAD_DIM
    seg = lambda name, w: h_s[:, lay[name]:lay[name] + w]
    rq = -(-C_GROUP * t_new // 16) * 16
    q = seg("cq", C_Q).reshape(n_b, t_new, C_KV_HEADS, C_GROUP, hd).transpose(0, 2, 3, 1, 4)
    q = q.reshape(n_b, C_KV_HEADS, C_GROUP * t_new, hd)
    q_s = jnp.pad(q, ((0, 0), (0, 0), (0, rq - C_GROUP * t_new), (0, 0))).reshape(n_b, C_KV_HEADS * rq, hd)
    iq_s = seg("iq", IDX_HEADS * IDX_DIM).reshape(n_b, t_new * IDX_HEADS, IDX_DIM)
    iw = h_s[:, lay["small"] + 2 * B_HEADS:lay["small"] + 2 * B_HEADS + IDX_HEADS]
    iw_s = (iw * (IDX_HEADS ** -0.5 * IDX_DIM ** -0.5)).reshape(n_b, t_new * IDX_HEADS, 1)
    padrows = lambda x: jnp.pad(x.reshape(n_b, t_new, -1), ((0, 0), (0, PAGE_SIZE - t_new), (0, 0)))
    ik_new, k_new, v_new = padrows(seg("ik", IDX_DIM)), padrows(seg("ck", C_KV)), padrows(seg("cv", C_KV))
    bias = _dsa_sample_select(pt_flat, kidx_pool, layer, iq_s, iw_s, ik_new, n_b=n_b, n_pages=n_pages, t_new=t_new)
    o = _dsa_sample_attend(pt_flat, k_pool, v_pool, layer, q_s, bias, k_new, v_new,
                           n_b=n_b, n_pages=n_pages, t_new=t_new)
    o = o.reshape(n_b, C_KV_HEADS, rq, hd)[:, :, :C_GROUP * t_new].reshape(n_b, C_KV_HEADS, C_GROUP, t_new, hd)
    return o.transpose(0, 3, 1, 2, 4).reshape(n_b * t_new, C_Q).astype(BF16)


def _merge_body(ya_ref, yb_ref, yc_ref, w_ref, g0_ref, g1_ref, g2_ref, o_ref):
    b0, c0 = A_WIDTH, A_WIDTH + B_WIDTH
    acc = jax.nn.sigmoid(g0_ref[...]) * jnp.dot(ya_ref[...], w_ref[0:b0, :], preferred_element_type=F32)
    acc = acc + jax.nn.sigmoid(g1_ref[...]) * jnp.dot(yb_ref[...], w_ref[b0:c0, :], preferred_element_type=F32)
    acc = acc + jax.nn.sigmoid(g2_ref[...]) * jnp.dot(yc_ref[...], w_ref[c0:, :], preferred_element_type=F32)
    o_ref[...] = acc.astype(o_ref.dtype)


def _merge(ya, yb, yc, w_br, layer, h, lay, d_model):
    m = ya.shape[0]
    tm = _pick(m, 1024)
    tn = _pick(d_model, 512)
    gcb = lay["gate"] // tn
    per = d_model // tn

    def gate_spec(i):
        return pl.BlockSpec((tm, tn), lambda r, c: (r, gcb + i * per + c))

    kw = A_WIDTH + B_WIDTH + C_Q
    vmem = 2 * (tm * kw * 2 + kw * tn * 2 + 3 * tm * tn * 4 + tm * tn * 2) + 4 * tm * tn * 4
    return pl.pallas_call(
        _merge_body,
        out_shape=jax.ShapeDtypeStruct((m, d_model), BF16),
        grid=(m // tm, d_model // tn),
        in_specs=[pl.BlockSpec((tm, A_WIDTH), lambda r, c: (r, 0)),
                  pl.BlockSpec((tm, B_WIDTH), lambda r, c: (r, 0)),
                  pl.BlockSpec((tm, C_Q), lambda r, c: (r, 0)),
                  _layer_spec((kw, tn), lambda r, c: (0, c), layer),
                  gate_spec(0), gate_spec(1), gate_spec(2)],
        out_specs=pl.BlockSpec((tm, tn), lambda r, c: (r, c)),
        compiler_params=_cparams(("parallel", "arbitrary"), vmem + COMPILER_SCRATCH_BYTES),
        name="gated_merge",
    )(ya, yb, yc, w_br, h, h, h)


def _ffn_body(hn_ref, w1_ref, w3_ref, w2_ref, x_ref, o_ref):
    @pl.when(pl.program_id(1) == 0)
    def _():
        o_ref[...] = x_ref[...]

    hn = hn_ref[...]
    a = jnp.dot(hn, w1_ref[...], preferred_element_type=F32)
    b = jnp.dot(hn, w3_ref[...], preferred_element_type=F32)
    act = (jax.nn.silu(a) * b).astype(BF16)
    o_ref[...] += jnp.dot(act, w2_ref[...], preferred_element_type=F32)


def _ffn(hn, w1, w3, w2, layer, x):
    m, d = hn.shape
    d_ff = w1.shape[-1]
    tm = _pick(m, 512)
    tf = FFN_TF
    assert d_ff % tf == 0
    once = pl.Buffered(1)
    vmem = tm * d * 2 + tm * d * 4 + 2 * tm * d * 4 + 2 * 3 * d * tf * 2 + 3 * tm * tf * 4
    return pl.pallas_call(
        _ffn_body,
        out_shape=jax.ShapeDtypeStruct((m, d), F32),
        grid=(m // tm, d_ff // tf),
        in_specs=[pl.BlockSpec((tm, d), lambda i, f: (i, 0), pipeline_mode=once),
                  _layer_spec((d, tf), lambda i, f: (0, f), layer),
                  _layer_spec((d, tf), lambda i, f: (0, f), layer),
                  _layer_spec((tf, d), lambda i, f: (f, 0), layer),
                  pl.BlockSpec((tm, d), lambda i, f: (i, 0), pipeline_mode=once)],
        out_specs=pl.BlockSpec((tm, d), lambda i, f: (i, 0)),
        compiler_params=_cparams(("parallel", "arbitrary"), vmem + COMPILER_SCRATCH_BYTES),
        name="swiglu_ffn",
    )(hn, w1, w3, w2, x)


def _dense_front(x, ln1, w_in_p, layer):
    xn = _rmsnorm(x, ln1, BF16)
    return _matmul(xn, w_in_p, layer=layer, w_transposed=True, tn_pref=768, name="in_proj")


def _dense_back(x, ya, yb, yc, h, lay, wts, layer, d_model):
    w_br, w_o, w1, w3, w2, ln2 = wts
    mix = _merge(ya, yb, yc, w_br, layer, h, lay, d_model)
    x = _matmul(mix, w_o, layer=layer, residual=x, name="out_proj")
    hn = _rmsnorm(x, ln2, BF16)
    return _ffn(hn, w1, w3, w2, layer, x)


def kernel(x_prompt, x_sample, cache_k, cache_v, cache_kidx, state_conv, state_delta, page_table, ln1, w_in,
           a_ln_g, a_ln_b, a_ws, a_bs, b_conv_w, b_a_log, b_dt_bias, b_out_g, w_br, w_o, ln2, ffn_w1, ffn_w3,
           ffn_w2, ln_f):
    n_bp, seq, d_model = x_prompt.shape
    n_bs, t_new, _ = x_sample.shape
    depth = ln1.shape[0]
    n_phys = cache_k.shape[1]
    n_pages = page_table.shape[1]
    assert seq % A_CHUNK == 0 and seq % GDN_CHUNK == 0 and CONV_W - 1 <= t_new <= min(A_CHUNK, GDN_CHUNK)
    lay = _layout(d_model)
    hd = HEAD_DIM

    xp = x_prompt.reshape(n_bp * seq, d_model)
    xs = x_sample.reshape(n_bs * t_new, d_model)
    pt_flat = page_table.reshape(-1).astype(I32)
    k_pool = cache_k.reshape(depth, n_phys, PAGE_SIZE * C_KV_HEADS, hd)
    v_pool = cache_v.reshape(depth, n_phys, PAGE_SIZE * C_KV_HEADS, hd)
    conv0_p = jnp.zeros((n_bp, SUBLANE, 3 * B_WIDTH), F32)
    s0_p = jnp.zeros((n_bp, B_HEADS, hd, hd), F32)
    qkv_off = lay["bq"]

    outs = {k: [] for k in ("pk", "pv", "pik", "pconv", "pdelta", "sk", "sv", "sik", "sconv", "sdelta", "schunk")}
    dense_w = tuple(w.astype(BF16) for w in (w_br, w_o, ffn_w1, ffn_w3, ffn_w2))
    w_in_p = _pack_w_in(w_in, d_model)
    for l in range(depth):
        wts = dense_w + (ln2[l],)

        h = _dense_front(xp, ln1[l], w_in_p, l)
        ya = _branch_a(h, lay, a_ws[l], a_bs[l], a_ln_g[l], a_ln_b[l], chunk=A_CHUNK, tb=A_CHUNK, emit_av=False)
        srcs = [(h, lay[n] // B_WIDTH) for n in ("bq", "bk", "bv", "bz")] + [(h, lay["small"] // LANE)]
        yb, s_new = _gdn(srcs, conv0_p, s0_p, b_conv_w[l], b_a_log[l], b_dt_bias[l], b_out_g[l],
                         n_b=n_bp, t_pad=seq, t_valid=GDN_CHUNK, chunk=GDN_CHUNK)
        yc = _dsa_prompt(h, lay, n_b=n_bp, seq=seq)
        h3 = h.reshape(n_bp, seq, lay["total"])
        outs["pk"].append(h3[:, :, lay["ck"]:lay["ck"] + C_KV].reshape(n_bp, seq, C_KV_HEADS, hd))
        outs["pv"].append(h3[:, :, lay["cv"]:lay["cv"] + C_KV].reshape(n_bp, seq, C_KV_HEADS, hd))
        outs["pik"].append(h3[:, :, lay["ik"]:lay["ik"] + IDX_DIM])
        outs["pconv"].append(h3[:, seq - (CONV_W - 1):, qkv_off:qkv_off + 3 * B_WIDTH])
        outs["pdelta"].append(s_new)
        xp = _dense_back(xp, ya, yb, yc, h, lay, wts, l, d_model)

        hs = _dense_front(xs, ln1[l], w_in_p, l)
        ya, av = _branch_a(hs, lay, a_ws[l], a_bs[l], a_ln_g[l], a_ln_b[l], chunk=t_new, tb=n_bs * t_new,
                           emit_av=True)
        hs3 = hs.reshape(n_bs, t_new, lay["total"])
        padt = lambda x: jnp.pad(x, ((0, 0), (0, GDN_CHUNK - t_new), (0, 0))).reshape(n_bs * GDN_CHUNK, -1)
        srcs = [(padt(hs3[:, :, lay[n]:lay[n] + B_WIDTH]), 0) for n in ("bq", "bk", "bv", "bz")]
        srcs.append((padt(hs3[:, :, lay["small"]:lay["small"] + LANE]), 0))
        conv0_s = jnp.pad(state_conv[l], ((0, 0), (SUBLANE - (CONV_W - 1), 0), (0, 0)))
        yb, s_new = _gdn(srcs, conv0_s, state_delta[l], b_conv_w[l], b_a_log[l], b_dt_bias[l], b_out_g[l],
                         n_b=n_bs, t_pad=GDN_CHUNK, t_valid=t_new, chunk=GDN_CHUNK)
        yb = yb.reshape(n_bs, GDN_CHUNK, B_WIDTH)[:, :t_new].reshape(n_bs * t_new, B_WIDTH)
        yc = _dsa_sample(hs, lay, pt_flat, k_pool, v_pool, cache_kidx, l, n_b=n_bs, t_new=t_new, n_pages=n_pages)
        outs["sk"].append(hs3[:, :, lay["ck"]:lay["ck"] + C_KV].reshape(n_bs, t_new, C_KV_HEADS, hd))
        outs["sv"].append(hs3[:, :, lay["cv"]:lay["cv"] + C_KV].reshape(n_bs, t_new, C_KV_HEADS, hd))
        outs["sik"].append(hs3[:, :, lay["ik"]:lay["ik"] + IDX_DIM])
        outs["sconv"].append(hs3[:, t_new - (CONV_W - 1):, qkv_off:qkv_off + 3 * B_WIDTH])
        outs["sdelta"].append(s_new)
        outs["schunk"].append(av.reshape(n_bs, t_new, A_WIDTH))
        xs = _dense_back(xs, ya, yb, yc, hs, lay, wts, l, d_model)

    y_prompt = _rmsnorm(xp, ln_f, F32).reshape(n_bp, seq, d_model)
    y_sample = _rmsnorm(xs, ln_f, F32).reshape(n_bs, t_new, d_model)
    st = lambda k: jnp.stack(outs[k])
    return (y_prompt, y_sample, st("pk"), st("pv"), st("pik"), st("pconv"), st("pdelta"),
            st("sk"), st("sv"), st("sik"), st("sconv"), st("sdelta"), st("schunk"))
```

```python
import functools

import jax
import jax.numpy as jnp
from jax import lax
from jax.experimental import pallas as pl
from jax.experimental.pallas import tpu as pltpu

F32 = jnp.float32
BF16 = jnp.bfloat16
I32 = jnp.int32

HEAD_DIM = 128
A_GROUPS = 8
A_CHUNK = 128
A_WIDTH = A_GROUPS * HEAD_DIM
B_HEADS = 12
B_WIDTH = B_HEADS * HEAD_DIM
CONV_W = 4
C_HEADS = 12
C_KV_HEADS = 4
C_GROUP = C_HEADS // C_KV_HEADS
C_Q = C_HEADS * HEAD_DIM
C_KV = C_KV_HEADS * HEAD_DIM
IDX_HEADS = 16
IDX_DIM = 128
TOPK_MAX = 256
PAGE_SIZE = 128
N_BRANCH = 3
RMS_EPS = 1e-6
LN_EPS = 1e-5

LANE = 128
SUBLANE = 8
V7X_VMEM_BYTES = 64 * 1024 * 1024
VMEM_BUDGET = 56 * 1024 * 1024
COMPILER_SCRATCH_BYTES = 8 * 1024 * 1024

GDN_CHUNK = 128
DSA_TQ = 256
DSA_TK = 512
DSA_ROW_TILE = 32
FFN_TF = 256
PAGES_PER_STEP = 16
MASK_NEG = -1e30
LOG2E = 1.4426950408889634
INT_MIN = -2147483648
KEY_OF_NEG_INF = -2139095041


def _cparams(semantics, vmem_bytes):
    return pltpu.CompilerParams(dimension_semantics=semantics,
                                vmem_limit_bytes=int(min(max(vmem_bytes, 16 * 1024 * 1024), VMEM_BUDGET)))


def _pick(dim, pref):
    t = pref
    while t >= SUBLANE:
        if dim % t == 0:
            return t
        t //= 2
    return dim


def _layout(d_model):
    segs = [("gate", N_BRANCH * d_model, d_model), ("bq", B_WIDTH, B_WIDTH), ("bk", B_WIDTH, B_WIDTH),
            ("bv", B_WIDTH, B_WIDTH), ("bz", B_WIDTH, B_WIDTH), ("cq", C_Q, C_Q), ("ck", C_KV, C_KV),
            ("au", A_WIDTH, A_WIDTH), ("av", A_WIDTH, A_WIDTH), ("cv", C_KV, C_KV),
            ("iq", IDX_HEADS * IDX_DIM, 4 * IDX_DIM), ("ik", IDX_DIM, IDX_DIM), ("small", LANE, LANE)]
    off, lay = 0, {}
    for name, width, align in segs:
        assert off % align == 0, (name, off, align)
        lay[name] = off
        off += width
    lay["total"] = off
    return lay


def _pack_body(tab_ref, main_ref, ab_ref, iw_ref, o_ref):
    j = pl.program_id(1)
    last = pl.num_programs(1) - 1

    @pl.when(j != last)
    def _():
        o_ref[...] = main_ref[0].astype(o_ref.dtype)

    @pl.when(j == last)
    def _():
        n_ab, n_iw = 2 * B_HEADS, IDX_HEADS
        rest = jnp.zeros((o_ref.shape[0] - n_ab - n_iw, o_ref.shape[1]), F32)
        o_ref[...] = jnp.concatenate([ab_ref[0, 0:n_ab, :], iw_ref[0, 0:n_iw, :], rest], axis=0).astype(o_ref.dtype)


def _pack_w_in(w_in, d_model):
    depth, d, in_width = w_in.shape
    lay = _layout(d_model)
    w_t = jnp.swapaxes(w_in, 1, 2)
    widths = (A_WIDTH, A_WIDTH, 3 * B_WIDTH, B_WIDTH, 2 * B_HEADS, C_Q, C_KV, C_KV,
              IDX_HEADS * IDX_DIM, IDX_HEADS, IDX_DIM, N_BRANCH * d_model)
    dsts = (lay["au"], lay["av"], lay["bq"], lay["bz"], None, lay["cq"], lay["ck"], lay["cv"], lay["iq"], None,
            lay["ik"], lay["gate"])
    n_blocks = lay["total"] // LANE
    table, start, special = [0] * n_blocks, 0, []
    for width, dst in zip(widths, dsts):
        if dst is None:
            special.append(start)
        else:
            assert width % LANE == 0 and dst % LANE == 0 and start % SUBLANE == 0
            for blk in range(width // LANE):
                table[dst // LANE + blk] = (start + blk * LANE) // SUBLANE
        start += width
    assert start == in_width and lay["small"] // LANE == n_blocks - 1
    ab_row, iw_row = special
    assert ab_row % SUBLANE == 0 and iw_row % SUBLANE == 0 and max(ab_row, iw_row) + LANE <= in_width
    window = lambda index_map: pl.BlockSpec((pl.Element(1), pl.Element(LANE), pl.Element(d)), index_map)
    grid_spec = pltpu.PrefetchScalarGridSpec(
        num_scalar_prefetch=1,
        grid=(depth, n_blocks),
        in_specs=[window(lambda l, j, tab: (l, tab[j] * SUBLANE, 0)),
                  window(lambda l, j, tab: (l, ab_row, 0)),
                  window(lambda l, j, tab: (l, iw_row, 0))],
        out_specs=pl.BlockSpec((None, LANE, d), lambda l, j, tab: (l, j, 0)))
    return pl.pallas_call(
        _pack_body,
        out_shape=jax.ShapeDtypeStruct((depth, lay["total"], d), BF16),
        grid_spec=grid_spec,
        compiler_params=_cparams(("parallel", "arbitrary"), 8 * LANE * d * 4),
        name="pack_in_proj_weight",
    )(jnp.asarray(table, I32), w_t, w_t, w_t)


def _rmsnorm_body(x_ref, g_ref, o_ref):
    x = x_ref[...]
    ms = jnp.mean(x * x, axis=-1, keepdims=True)
    o_ref[...] = (x * lax.rsqrt(ms + RMS_EPS) * g_ref[...]).astype(o_ref.dtype)


def _rmsnorm(x, g, out_dtype):
    m, d = x.shape
    tm = _pick(m, 256)
    return pl.pallas_call(
        _rmsnorm_body,
        out_shape=jax.ShapeDtypeStruct((m, d), out_dtype),
        grid=(m // tm,),
        in_specs=[pl.BlockSpec((tm, d), lambda i: (i, 0)), pl.BlockSpec((1, d), lambda i: (0, 0))],
        out_specs=pl.BlockSpec((tm, d), lambda i: (i, 0)),
        compiler_params=_cparams(("parallel",), 6 * tm * d * 4),
        name="rmsnorm",
    )(x, g.reshape(1, d))


def _mm_body(a_ref, w_ref, o_ref):
    o_ref[...] = jnp.dot(a_ref[...], w_ref[...], preferred_element_type=F32).astype(o_ref.dtype)


def _mm_res_body(a_ref, w_ref, r_ref, o_ref):
    o_ref[...] = (r_ref[...] + jnp.dot(a_ref[...], w_ref[...], preferred_element_type=F32)).astype(o_ref.dtype)


def _mm_wt_body(a_ref, wt_ref, o_ref):
    o_ref[...] = lax.dot_general(a_ref[...], wt_ref[...], (((1,), (1,)), ((), ())),
                                 preferred_element_type=F32).astype(o_ref.dtype)


def _layer_spec(block, index_map, layer):
    if layer is None:
        return pl.BlockSpec(block, index_map)
    return pl.BlockSpec((None,) + block, lambda *g: (layer,) + index_map(*g))


def _matmul(a, w, *, layer=None, w_transposed=False, residual=None, out_dtype=F32, tm_pref=1024, tn_pref=1024,
            a_single_buffer=False, name="matmul"):
    m, k = a.shape
    n = w.shape[-2] if w_transposed else w.shape[-1]
    tm = _pick(m, tm_pref)
    tn = tn_pref if n % tn_pref == 0 else _pick(n, tn_pref)
    osz = jnp.dtype(out_dtype).itemsize
    vmem = (1 if a_single_buffer else 2) * tm * k * 2 + 2 * (k * tn * 2 + tm * tn * osz) + tm * tn * 4
    w_spec = (_layer_spec((tn, k), lambda i, j: (j, 0), layer) if w_transposed
              else _layer_spec((k, tn), lambda i, j: (0, j), layer))
    a_mode = dict(pipeline_mode=pl.Buffered(1)) if a_single_buffer else {}
    in_specs = [pl.BlockSpec((tm, k), lambda i, j: (i, 0), **a_mode), w_spec]
    args = [a, w]
    body = _mm_wt_body if w_transposed else _mm_body
    assert residual is None or not w_transposed
    if residual is not None:
        in_specs.append(pl.BlockSpec((tm, tn), lambda i, j: (i, j)))
        args.append(residual)
        body = _mm_res_body
        vmem += 2 * tm * tn * 4
    return pl.pallas_call(
        body,
        out_shape=jax.ShapeDtypeStruct((m, n), out_dtype),
        grid=(m // tm, n // tn),
        in_specs=in_specs,
        out_specs=pl.BlockSpec((tm, tn), lambda i, j: (i, j)),
        compiler_params=_cparams(("parallel", "arbitrary"), vmem + COMPILER_SCRATCH_BYTES),
        name=name,
    )(*args)


def _branch_a_body(u_ref, v_ref, w_ref, bs_ref, g_ref, b_ref, y_ref, *av_ref, chunk):
    tb = u_ref.shape[0]
    u = jax.nn.gelu(u_ref[...])
    v = jax.nn.gelu(v_ref[...])
    mu = jnp.mean(v, axis=-1, keepdims=True)
    var = jnp.mean(jnp.square(v - mu), axis=-1, keepdims=True)
    vn = (v - mu) * lax.rsqrt(var + LN_EPS) * g_ref[...] + b_ref[...]
    if av_ref:
        av_ref[0][...] = vn
    row = lax.broadcasted_iota(I32, (tb, tb), 0)
    col = lax.broadcasted_iota(I32, (tb, tb), 1)
    keep = (col <= row) & ((row // chunk) == (col // chunk))
    vb = vn.astype(BF16)
    for g in range(A_GROUPS):
        sl = slice(g * HEAD_DIM, (g + 1) * HEAD_DIM)
        wg = jnp.where(keep, w_ref[g], 0.0).astype(BF16)
        s = jnp.dot(wg, vb[:, sl], preferred_element_type=F32) + bs_ref[:, g:g + 1]
        y_ref[:, sl] = (u[:, sl] * s).astype(y_ref.dtype)


def _branch_a(h, lay, a_ws, a_bs, ln_g, ln_b, *, chunk, tb, emit_av):
    m = h.shape[0]
    reps = tb // chunk
    wfull = jnp.tile(a_ws[:, :chunk, :chunk], (1, reps, reps))
    bs_t = jnp.tile(a_bs[:, :chunk].T, (reps, 1))
    cu, cv = lay["au"] // A_WIDTH, lay["av"] // A_WIDTH
    out_shape = [jax.ShapeDtypeStruct((m, A_WIDTH), BF16)]
    out_specs = [pl.BlockSpec((tb, A_WIDTH), lambda i: (i, 0))]
    if emit_av:
        out_shape.append(jax.ShapeDtypeStruct((m, A_WIDTH), F32))
        out_specs.append(pl.BlockSpec((tb, A_WIDTH), lambda i: (i, 0)))
    res = pl.pallas_call(
        functools.partial(_branch_a_body, chunk=chunk),
        out_shape=out_shape,
        grid=(m // tb,),
        in_specs=[pl.BlockSpec((tb, A_WIDTH), lambda i: (i, cu)),
                  pl.BlockSpec((tb, A_WIDTH), lambda i: (i, cv)),
                  pl.BlockSpec((A_GROUPS, tb, tb), lambda i: (0, 0, 0)),
                  pl.BlockSpec((tb, A_GROUPS), lambda i: (0, 0)),
                  pl.BlockSpec((1, A_WIDTH), lambda i: (0, 0)),
                  pl.BlockSpec((1, A_WIDTH), lambda i: (0, 0))],
        out_specs=out_specs,
        compiler_params=_cparams(("parallel",), 32 << 20),
        name="branch_a_gmlp",
    )(h, h, wfull, bs_t, ln_g.reshape(1, A_WIDTH), ln_b.reshape(1, A_WIDTH))
    return res if emit_av else res[0]


def _bdot(a, b):
    return lax.dot_general(a.astype(BF16), b.astype(BF16), (((2,), (1,)), ((0,), (0,))),
                           preferred_element_type=F32)


def _bdot_nt(a, b):
    return lax.dot_general(a.astype(BF16), b.astype(BF16), (((2,), (2,)), ((0,), (0,))),
                           preferred_element_type=F32)


def _transpose_rows(x):
    r = x.shape[0]
    if r < LANE:
        x = jnp.concatenate([x, jnp.zeros((LANE - r, LANE), x.dtype)], axis=0)
    return x.T[:, :r]


def _gdn_body(q_ref, k_ref, v_ref, z_ref, sm_ref, c0_ref, s0_ref, cw_ref, al_ref, dt_ref, og_ref,
              y_ref, sout_ref, ext_ref, s_ref, *, chunk, t_valid):
    c = pl.program_id(1)
    nc = pl.num_programs(1)
    hd = HEAD_DIM

    @pl.when(c == 0)
    def _():
        s_ref[...] = s0_ref[0]
        for j in range(3):
            ext_ref[j, 0:SUBLANE, :] = c0_ref[0, :, j * B_WIDTH:(j + 1) * B_WIDTH]

    acts = []
    for j, ref in enumerate((q_ref, k_ref, v_ref)):
        ext_ref[j, SUBLANE:SUBLANE + chunk, :] = ref[...]
        acc = None
        for i in range(CONV_W):
            lo = SUBLANE - (CONV_W - 1) + i
            term = ext_ref[j, lo:lo + chunk, :] * cw_ref[i:i + 1, j * B_WIDTH:(j + 1) * B_WIDTH]
            acc = term if acc is None else acc + term
        acts.append(jax.nn.silu(acc))
        ext_ref[j, 0:SUBLANE, :] = ext_ref[j, chunk:chunk + SUBLANE, :]
    qa, ka, va = acts

    sm = sm_ref[...]
    row1 = lax.broadcasted_iota(I32, (chunk, LANE), 0)
    g_all = -jnp.exp(al_ref[...]) * jax.nn.softplus(sm + dt_ref[...])
    beta_all = jax.nn.sigmoid(sm)
    if t_valid < chunk:
        g_all = jnp.where(row1 < t_valid, g_all, 0.0)
        beta_all = jnp.where(row1 < t_valid, beta_all, 0.0)
    gc_all = g_all
    d = 1
    while d < chunk:
        gc_all = gc_all + jnp.where(row1 >= d, pltpu.roll(gc_all, d, 0), 0.0)
        d *= 2
    gc_t = _transpose_rows(gc_all)

    row = lax.broadcasted_iota(I32, (chunk, chunk), 0)
    col = lax.broadcasted_iota(I32, (chunk, chunk), 1)
    incl = row >= col
    strict = row > col
    eye = jnp.where(row == col, 1.0, 0.0)
    pair_masks = []
    bs = 1
    while bs < chunk:
        pair_masks.append(((row // bs) % 2 == 1) & ((col // bs) == (row // bs) - 1))
        bs *= 2

    heads = range(B_HEADS)
    per_head = lambda x: jnp.stack([x[:, h * hd:(h + 1) * hd] for h in heads], axis=0)
    q3, k3, v3 = per_head(qa), per_head(ka), per_head(va)
    qn = q3 * lax.rsqrt(jnp.sum(q3 * q3, axis=-1, keepdims=True) + 1e-6) * (hd ** -0.5)
    kn = k3 * lax.rsqrt(jnp.sum(k3 * k3, axis=-1, keepdims=True) + 1e-6)
    beta = jnp.stack([beta_all[:, B_HEADS + h:B_HEADS + h + 1] for h in heads], axis=0)
    gcol = jnp.stack([gc_all[:, h:h + 1] for h in heads], axis=0)
    grow = jnp.stack([gc_t[h:h + 1, :] for h in heads], axis=0)
    decay = jnp.exp(jnp.where(incl[None], gcol - grow, -jnp.inf))
    kb = kn * beta
    eg = jnp.exp(gcol)
    lmat = jnp.where(strict[None], _bdot_nt(kb, kn) * decay, 0.0)
    attn = _bdot_nt(qn, kn) * decay
    tinv = eye[None] - jnp.where(pair_masks[0][None], lmat, 0.0)
    for pm in pair_masks[1:]:
        tinv = tinv - _bdot(tinv, _bdot(jnp.where(pm[None], lmat, 0.0), tinv))
    sol = _bdot(tinv, jnp.concatenate([v3 * beta, kb * eg], axis=-1))
    value, kcd = sol[:, :, :hd], sol[:, :, hd:]
    s_old = s_ref[...]
    vnew = value - _bdot(kcd, s_old)
    o = _bdot(qn * eg, s_old) + _bdot(attn, vnew)
    glast = gcol[:, chunk - 1:chunk, :]
    kend = kn * jnp.exp(glast - gcol)
    s_ref[...] = s_old * jnp.exp(glast) + _bdot(jnp.swapaxes(kend, 1, 2), vnew)
    on = o * lax.rsqrt(jnp.mean(o * o, axis=-1, keepdims=True) + RMS_EPS) * og_ref[...]
    for h in heads:
        sl = slice(h * hd, (h + 1) * hd)
        y_ref[:, sl] = (on[h] * jax.nn.silu(z_ref[:, sl])).astype(y_ref.dtype)

    @pl.when(c == nc - 1)
    def _():
        sout_ref[0] = s_ref[...]


def _gdn(srcs, conv0, s0, conv_w, a_log, dt_bias, o_g, *, n_b, t_pad, t_valid, chunk):
    nc = t_pad // chunk
    arrs = [a for a, _ in srcs]
    cbs = [cb for _, cb in srcs]
    widths = [B_WIDTH] * 4 + [LANE]

    def tok_spec(w, cb):
        return pl.BlockSpec((chunk, w), lambda b, c: (b * nc + c, cb))

    pad12 = lambda x: jnp.zeros((1, LANE), F32).at[0, :B_HEADS].set(x)
    return pl.pallas_call(
        functools.partial(_gdn_body, chunk=chunk, t_valid=t_valid),
        out_shape=[jax.ShapeDtypeStruct((n_b * t_pad, B_WIDTH), BF16),
                   jax.ShapeDtypeStruct((n_b, B_HEADS, HEAD_DIM, HEAD_DIM), F32)],
        grid=(n_b, nc),
        in_specs=[tok_spec(w, cb) for w, cb in zip(widths, cbs)] + [
            pl.BlockSpec((1, SUBLANE, 3 * B_WIDTH), lambda b, c: (b, 0, 0)),
            pl.BlockSpec((1, B_HEADS, HEAD_DIM, HEAD_DIM), lambda b, c: (b, 0, 0, 0)),
            pl.BlockSpec((CONV_W, 3 * B_WIDTH), lambda b, c: (0, 0)),
            pl.BlockSpec((1, LANE), lambda b, c: (0, 0)),
            pl.BlockSpec((1, LANE), lambda b, c: (0, 0)),
            pl.BlockSpec((1, HEAD_DIM), lambda b, c: (0, 0))],
        out_specs=[pl.BlockSpec((chunk, B_WIDTH), lambda b, c: (b * nc + c, 0)),
                   pl.BlockSpec((1, B_HEADS, HEAD_DIM, HEAD_DIM), lambda b, c: (b, 0, 0, 0))],
        scratch_shapes=[pltpu.VMEM((3, chunk + SUBLANE, B_WIDTH), F32),
                        pltpu.VMEM((B_HEADS, HEAD_DIM, HEAD_DIM), F32)],
        compiler_params=_cparams(("parallel", "arbitrary"), 40 << 20),
        name="branch_b_gated_delta",
    )(*arrs, conv0, s0, conv_w, pad12(a_log), pad12(dt_bias), o_g.reshape(1, HEAD_DIM))


def _sort_key(x):
    b = lax.bitcast_convert_type(x + 0.0, I32)
    return b ^ ((b >> 31) & 0x7FFFFFFF)


def _select_rule(count, shape, k, nbits):
    kf = jnp.float32(k)
    count_ge = lambda cand: count(lambda key, pos: key >= cand)
    t0 = jnp.where(count_ge(jnp.zeros(shape, I32)) >= kf, 0, INT_MIN).astype(I32)

    def bit_step(i, t):
        cand = t + lax.shift_left(jnp.int32(1), 30 - i)
        return jnp.where(count_ge(cand) >= kf, cand, t)

    t = lax.fori_loop(0, 31, bit_step, t0)
    n_ge = count_ge(t)
    tied = (n_ge > kf) & (t > KEY_OF_NEG_INF)

    def tie_index():
        r = kf - count(lambda key, pos: key > t)

        def idx_step(i, j):
            cand = j + lax.shift_left(jnp.int32(1), nbits - 1 - i)
            below = count(lambda key, pos: (key == t) & (pos <= cand - 1))
            return jnp.where(below < r, cand, j)

        return lax.fori_loop(0, nbits, idx_step, jnp.zeros(shape, I32))

    no_limit = jnp.full(shape, 2 ** 30, I32)
    j = lax.cond(jnp.sum(jnp.where(tied, 1.0, 0.0)) > 0.0,
                 lambda: jnp.where(tied, tie_index(), no_limit), lambda: no_limit)
    return t, j


def _selected(key, pos, t, j):
    return (key > t) | ((key == t) & (pos <= j))


def _dsa_prompt_body(cq_ref, iq0_ref, iq1_ref, iq2_ref, iq3_ref, sm_ref, ik_ref, ck_ref, cv_ref, y_ref,
                     key_ref, iqb_ref, wb_ref, qb_ref, m_ref, l_ref, acc_ref, p_ref, al_ref,
                     *, k_sel, nbits, tq, tk):
    rows3 = C_GROUP * tq
    qi = pl.program_id(1)
    kpg = tk // LANE
    nkb = (qi + 1) * (tq // LANE)
    ngrp = ((qi + 1) * tq + tk - 1) // tk
    hd = HEAD_DIM
    nt = (((1,), (1,)), ((), ()))
    sub = LANE // SUBLANE
    qpos3 = qi * tq + lax.broadcasted_iota(I32, (sub, SUBLANE, tq), 2)
    kofs3 = lax.broadcasted_iota(I32, (sub, SUBLANE, tq), 0) * SUBLANE + lax.broadcasted_iota(I32, (sub, SUBLANE, tq), 1)

    w_all = sm_ref[...] * (IDX_HEADS ** -0.5 * IDX_DIM ** -0.5)
    w_t = jnp.concatenate([w_all[r:r + LANE].T for r in range(0, tq, LANE)], axis=1)
    w_off = 2 * B_HEADS
    for hh in range(IDX_HEADS):
        ref = (iq0_ref, iq1_ref, iq2_ref, iq3_ref)[hh // 4]
        iqb_ref[hh] = ref[:, (hh % 4) * IDX_DIM:(hh % 4 + 1) * IDX_DIM].astype(BF16)
        wb_ref[hh] = jnp.broadcast_to(w_t[w_off + hh:w_off + hh + 1, :], (SUBLANE, tq))

    def score_grp(g, carry):
        ikg = ik_ref[pl.ds(pl.multiple_of(g * tk, tk), tk), :].astype(BF16)
        acc = jnp.zeros((tk // SUBLANE, SUBLANE, tq), F32)
        for hh in range(IDX_HEADS):
            s = lax.dot_general(ikg, iqb_ref[hh], nt, preferred_element_type=F32)
            acc = acc + wb_ref[hh] * jnp.maximum(s, 0.0).reshape(tk // SUBLANE, SUBLANE, tq)
        for t in range(kpg):
            kpos3 = (g * kpg + t) * LANE + kofs3
            blk = acc[t * sub:(t + 1) * sub]
            key_ref[g * kpg + t] = _sort_key(jnp.where(kpos3 <= qpos3, blk, -jnp.inf))
        return carry

    lax.fori_loop(0, ngrp, score_grp, 0)

    bpt = tq // LANE

    def count(pred):
        def trip(i, acc):
            for u in range(bpt):
                b = i * bpt + u
                acc = acc + jnp.sum(jnp.where(pred(key_ref[b], b * LANE + kofs3), 1.0, 0.0), axis=0)
            return acc
        acc = lax.fori_loop(0, qi + 1, trip, jnp.zeros((SUBLANE, tq), F32))
        return jnp.broadcast_to(jnp.sum(acc, axis=0, keepdims=True), (SUBLANE, tq))

    t8, j8 = _select_rule(count, (SUBLANE, tq), k_sel, nbits)

    for kvh in range(C_KV_HEADS):
        for g in range(C_GROUP):
            hsl = slice((kvh * C_GROUP + g) * hd, (kvh * C_GROUP + g + 1) * hd)
            qb_ref[kvh, g * tq:(g + 1) * tq, :] = cq_ref[:, hsl].astype(BF16)
    m_ref[...] = jnp.full(m_ref.shape, MASK_NEG, F32)
    l_ref[...] = jnp.zeros(l_ref.shape, F32)
    acc_ref[...] = jnp.zeros(acc_ref.shape, F32)
    ones = jnp.ones((tk, hd), BF16)

    def attend_grp(g, carry):
        bias_t = []
        for t in range(kpg):
            kpos3 = (g * kpg + t) * LANE + kofs3
            sel = _selected(key_ref[g * kpg + t], kpos3, t8, j8) & (kpos3 <= qpos3)
            kq = jnp.where(sel, 0.0, MASK_NEG).reshape(LANE, tq)
            bias_t.append(jnp.concatenate([kq[:, r:r + LANE].T for r in range(0, tq, LANE)], axis=0))
        bias = jnp.concatenate(bias_t, axis=1)
        start = pl.multiple_of(g * tk, tk)
        kgrp = ck_ref[pl.ds(start, tk), :].astype(BF16)
        vgrp = cv_ref[pl.ds(start, tk), :].astype(BF16)
        for kvh in range(C_KV_HEADS):
            sl = slice(kvh * hd, (kvh + 1) * hd)
            s = lax.dot_general(qb_ref[kvh], kgrp[:, sl], nt, preferred_element_type=F32)
            for r0 in range(0, rows3, DSA_ROW_TILE):
                rs = slice(r0, r0 + DSA_ROW_TILE)
                x = s[rs] * (hd ** -0.5 * LOG2E) + bias[r0 % tq:r0 % tq + DSA_ROW_TILE]
                m_old = m_ref[kvh, rs, :]
                m_new = jnp.maximum(m_old, jnp.max(x, axis=-1, keepdims=True))
                p_ref[rs, :] = jnp.exp2(x - m_new).astype(BF16)
                al_ref[rs, :] = jnp.exp2(m_old - m_new)
                m_ref[kvh, rs, :] = m_new
            pv = jnp.dot(p_ref[...], jnp.concatenate([vgrp[:, sl], ones], axis=1), preferred_element_type=F32)
            alpha = al_ref[...]
            l_ref[kvh] = alpha * l_ref[kvh] + pv[:, hd:hd + 1]
            acc_ref[kvh] = alpha * acc_ref[kvh] + pv[:, :hd]
        return carry

    lax.fori_loop(0, ngrp, attend_grp, 0)

    for kvh in range(C_KV_HEADS):
        o = acc_ref[kvh] / l_ref[kvh]
        for g in range(C_GROUP):
            hsl = slice((kvh * C_GROUP + g) * hd, (kvh * C_GROUP + g + 1) * hd)
            y_ref[:, hsl] = o[g * tq:(g + 1) * tq, :].astype(y_ref.dtype)


def _dsa_prompt(h, lay, *, n_b, seq):
    tq, tk = DSA_TQ, DSA_TK
    assert seq % tq == 0 and seq % tk == 0
    nqb = seq // tq
    k_sel = min(TOPK_MAX, seq // 4)
    nbits = max(1, (seq - 1).bit_length())
    iq_cb = lay["iq"] // (4 * IDX_DIM)
    once = pl.Buffered(1)

    def q_spec(w, cb):
        return pl.BlockSpec((tq, w), lambda b, q: (b * nqb + q, cb))

    def kv_spec(w, cb):
        return pl.BlockSpec((seq, w), lambda b, q: (b, cb), pipeline_mode=once)

    rows3 = C_GROUP * tq
    lane_pad = lambda r: r * LANE * 4
    vmem = (seq * (2 * C_KV + IDX_DIM) * 4 + 2 * tq * (C_Q + IDX_HEADS * IDX_DIM + LANE) * 4 + 2 * tq * C_Q * 2
            + (seq // LANE) * tq * LANE * 4 + IDX_HEADS * tq * LANE * 6 + C_KV_HEADS * rows3 * HEAD_DIM * 6
            + 2 * C_KV_HEADS * lane_pad(rows3) + 4 * rows3 * tk * 4)
    return pl.pallas_call(
        functools.partial(_dsa_prompt_body, k_sel=k_sel, nbits=nbits, tq=tq, tk=tk),
        out_shape=jax.ShapeDtypeStruct((n_b * seq, C_Q), BF16),
        grid=(n_b, nqb),
        in_specs=[q_spec(C_Q, lay["cq"] // C_Q)]
                 + [q_spec(4 * IDX_DIM, iq_cb + i) for i in range(4)]
                 + [q_spec(LANE, lay["small"] // LANE),
                    kv_spec(IDX_DIM, lay["ik"] // IDX_DIM),
                    kv_spec(C_KV, lay["ck"] // C_KV),
                    kv_spec(C_KV, lay["cv"] // C_KV)],
        out_specs=pl.BlockSpec((tq, C_Q), lambda b, q: (b * nqb + q, 0)),
        scratch_shapes=[pltpu.VMEM((seq // LANE, LANE // SUBLANE, SUBLANE, tq), I32),
                        pltpu.VMEM((IDX_HEADS, tq, IDX_DIM), BF16),
                        pltpu.VMEM((IDX_HEADS, SUBLANE, tq), F32),
                        pltpu.VMEM((C_KV_HEADS, rows3, HEAD_DIM), BF16),
                        pltpu.VMEM((C_KV_HEADS, rows3, 1), F32),
                        pltpu.VMEM((C_KV_HEADS, rows3, 1), F32),
                        pltpu.VMEM((C_KV_HEADS, rows3, HEAD_DIM), F32),
                        pltpu.VMEM((rows3, tk), BF16),
                        pltpu.VMEM((rows3, 1), F32)],
        compiler_params=_cparams(("parallel", "arbitrary"), vmem + COMPILER_SCRATCH_BYTES),
        name="branch_c_prompt_dsa",
    )(h, h, h, h, h, h, h, h, h)


def _dsa_sample_select_body(pt_ref, *refs, n_pages, k_sel, nbits, t_new):
    pp = PAGES_PER_STEP
    page_refs = refs[:pp]
    iq_ref, w_ref, ikn_ref, bias_ref, key_ref = refs[pp:]
    j = pl.program_id(1)
    n_steps = n_pages // pp
    rows = t_new
    col = lax.broadcasted_iota(I32, (rows, LANE), 1)
    trow = lax.broadcasted_iota(I32, (rows, LANE), 0)
    iq = iq_ref[0].astype(BF16)

    def scores(keys_f32):
        n = keys_f32.shape[0]
        s = lax.dot_general(iq, keys_f32.astype(BF16), (((1,), (1,)), ((), ())), preferred_element_type=F32)
        r = jnp.maximum(s, 0.0) * w_ref[0]
        return jnp.sum(r.reshape(rows, IDX_HEADS, n), axis=1)

    @pl.when(j < n_steps)
    def _():
        keys = _sort_key(scores(jnp.concatenate([r[0, 0] for r in page_refs], axis=0)))
        for i in range(pp):
            key_ref[j * pp + i] = keys[:, i * LANE:(i + 1) * LANE]

    @pl.when(j == n_steps)
    def _():
        new_ok = (col <= trow) & (col < t_new)
        key_ref[n_pages] = _sort_key(jnp.where(new_ok, scores(ikn_ref[0]), -jnp.inf))
        nblk = n_pages + 1
        grp = SUBLANE
        nblk_pad = key_ref.shape[0]
        if nblk_pad > nblk:
            key_ref[nblk:nblk_pad] = jnp.full((nblk_pad - nblk, rows, LANE), KEY_OF_NEG_INF, I32)
        blk3 = lax.broadcasted_iota(I32, (grp, rows, LANE), 0)
        col3 = lax.broadcasted_iota(I32, (grp, rows, LANE), 2)

        def count(pred):
            def trip(i, acc):
                b0 = pl.multiple_of(i * grp, grp)
                hit = pred(key_ref[pl.ds(b0, grp)], (b0 + blk3) * LANE + col3)
                return acc + jnp.sum(jnp.where(hit, 1.0, 0.0), axis=0)
            acc = lax.fori_loop(0, nblk_pad // grp, trip, jnp.zeros((rows, LANE), F32))
            return jnp.broadcast_to(jnp.sum(acc, axis=1, keepdims=True), (rows, LANE))

        tb, jb = _select_rule(count, (rows, LANE), k_sel, nbits)

        def write_blk(b, carry):
            kpos = b * LANE + col
            sel = _selected(key_ref[b], kpos, tb, jb) & ((kpos < n_pages * PAGE_SIZE) | new_ok)
            bias_ref[0, b] = jnp.where(sel, 0.0, MASK_NEG)
            return carry

        lax.fori_loop(0, nblk, write_blk, 0)


def _page_index_map(layer, i, n_pages, n_steps, trailing):
    def index_map(b, j, pt):
        step = jnp.minimum(j, n_steps - 1)
        return (layer, pt[b * n_pages + step * PAGES_PER_STEP + i]) + (0,) * trailing
    return index_map


def _dsa_sample_select(pt_flat, kidx_pool, layer, iq_s, iw_s, ik_new, *, n_b, n_pages, t_new):
    pp = PAGES_PER_STEP
    assert n_pages % pp == 0
    n_steps = n_pages // pp
    total = n_pages * PAGE_SIZE + t_new
    k_sel = min(TOPK_MAX, total // 4)
    nbits = max(1, ((n_pages + 1) * PAGE_SIZE - 1).bit_length())
    rows_q = t_new * IDX_HEADS
    grid_spec = pltpu.PrefetchScalarGridSpec(
        num_scalar_prefetch=1,
        grid=(n_b, n_steps + 1),
        in_specs=[pl.BlockSpec((1, 1, PAGE_SIZE, IDX_DIM), _page_index_map(layer, i, n_pages, n_steps, 2))
                  for i in range(pp)]
                 + [pl.BlockSpec((1, rows_q, IDX_DIM), lambda b, j, pt: (b, 0, 0)),
                    pl.BlockSpec((1, rows_q, 1), lambda b, j, pt: (b, 0, 0)),
                    pl.BlockSpec((1, PAGE_SIZE, IDX_DIM), lambda b, j, pt: (b, 0, 0))],
        out_specs=pl.BlockSpec((1, n_pages + 1, t_new, LANE), lambda b, j, pt: (b, 0, 0, 0)),
        scratch_shapes=[pltpu.VMEM((-(-(n_pages + 1) // SUBLANE) * SUBLANE, t_new, LANE), I32)])
    return pl.pallas_call(
        functools.partial(_dsa_sample_select_body, n_pages=n_pages, k_sel=k_sel, nbits=nbits, t_new=t_new),
        out_shape=jax.ShapeDtypeStruct((n_b, n_pages + 1, t_new, LANE), F32),
        grid_spec=grid_spec,
        compiler_params=_cparams(("parallel", "arbitrary"), 24 << 20),
        name="branch_c_decode_select",
    )(pt_flat, *([kidx_pool] * pp), iq_s, iw_s, ik_new)


def _dsa_sample_attend_body(pt_ref, *refs, n_pages, t_new):
    pp = PAGES_PER_STEP
    k_refs, v_refs = refs[:pp], refs[pp:2 * pp]
    q_ref, bias_ref, kn_ref, vn_ref, o_ref, m_ref, l_ref, acc_ref = refs[2 * pp:]
    j = pl.program_id(1)
    n_steps = n_pages // pp
    hd = HEAD_DIM
    rq = q_ref.shape[1] // C_KV_HEADS
    reps = rq // t_new

    @pl.when(j == 0)
    def _():
        m_ref[...] = jnp.full(m_ref.shape, MASK_NEG, F32)
        l_ref[...] = jnp.zeros(l_ref.shape, F32)
        acc_ref[...] = jnp.zeros(acc_ref.shape, F32)

    def attend(k_of, v_of, bias_t):
        bias = jnp.concatenate([bias_t] * reps, axis=0)
        for kvh in range(C_KV_HEADS):
            rs = slice(kvh * rq, (kvh + 1) * rq)
            s = lax.dot_general(q_ref[0, rs, :].astype(BF16), k_of(kvh).astype(BF16), (((1,), (1,)), ((), ())),
                                preferred_element_type=F32) * (hd ** -0.5) + bias
            m_old = m_ref[rs, :]
            m_new = jnp.maximum(m_old, jnp.max(s, axis=-1, keepdims=True))
            alpha = jnp.exp(m_old - m_new)
            p = jnp.exp(s - m_new)
            l_ref[rs, :] = alpha * l_ref[rs, :] + jnp.sum(p, axis=-1, keepdims=True)
            acc_ref[rs, :] = alpha * acc_ref[rs, :] + jnp.dot(p.astype(BF16), v_of(kvh).astype(BF16),
                                                              preferred_element_type=F32)
            m_ref[rs, :] = m_new

    @pl.when(j < n_steps)
    def _():
        head_rows = lambda kvh: pl.ds(kvh, PAGE_SIZE, stride=C_KV_HEADS)
        pages = lambda prefs: (lambda kvh: jnp.concatenate([r[0, 0, head_rows(kvh), :] for r in prefs], axis=0))
        bias_t = jnp.concatenate([bias_ref[0, j * pp + i] for i in range(pp)], axis=1)
        attend(pages(k_refs), pages(v_refs), bias_t)

    @pl.when(j == n_steps)
    def _():
        new = lambda ref: (lambda kvh: ref[0, :, kvh * hd:(kvh + 1) * hd])
        attend(new(kn_ref), new(vn_ref), bias_ref[0, n_pages])
        o_ref[0] = acc_ref[...] / l_ref[...]


def _dsa_sample_attend(pt_flat, k_pool, v_pool, layer, q_s, bias, k_new, v_new, *, n_b, n_pages, t_new):
    pp = PAGES_PER_STEP
    n_steps = n_pages // pp
    rows = q_s.shape[1]
    page_specs = [pl.BlockSpec((1, 1, PAGE_SIZE * C_KV_HEADS, HEAD_DIM), _page_index_map(layer, i, n_pages, n_steps, 2))
                  for i in range(pp)]
    grid_spec = pltpu.PrefetchScalarGridSpec(
        num_scalar_prefetch=1,
        grid=(n_b, n_steps + 1),
        in_specs=page_specs + page_specs
                 + [pl.BlockSpec((1, rows, HEAD_DIM), lambda b, j, pt: (b, 0, 0)),
                    pl.BlockSpec((1, n_pages + 1, t_new, LANE), lambda b, j, pt: (b, 0, 0, 0)),
                    pl.BlockSpec((1, PAGE_SIZE, C_KV), lambda b, j, pt: (b, 0, 0)),
                    pl.BlockSpec((1, PAGE_SIZE, C_KV), lambda b, j, pt: (b, 0, 0))],
        out_specs=pl.BlockSpec((1, rows, HEAD_DIM), lambda b, j, pt: (b, 0, 0)),
        scratch_shapes=[pltpu.VMEM((rows, 1), F32), pltpu.VMEM((rows, 1), F32), pltpu.VMEM((rows, HEAD_DIM), F32)])
    return pl.pallas_call(
        functools.partial(_dsa_sample_attend_body, n_pages=n_pages, t_new=t_new),
        out_shape=jax.ShapeDtypeStruct((n_b, rows, HEAD_DIM), F32),
        grid_spec=grid_spec,
        compiler_params=_cparams(("parallel", "arbitrary"), 32 << 20),
        name="branch_c_decode_attend",
    )(pt_flat, *([k_pool] * pp), *([v_pool] * pp), q_s, bias, k_new, v_new)


def _dsa_sample(h_s, lay, pt_flat, k_pool, v_pool, kidx_pool, layer, *, n_b, t_new, n_pages):
    hd = HEAD_DIM
    seg = lambda name, w: h_s[:, lay[name]:lay[name] + w]
    rq = -(-C_GROUP * t_new // 16) * 16
    q = seg("cq", C_Q).reshape(n_b, t_new, C_KV_HEADS, C_GROUP, hd).transpose(0, 2, 3, 1, 4)
    q = q.reshape(n_b, C_KV_HEADS, C_GROUP * t_new, hd)
    q_s = jnp.pad(q, ((0, 0), (0, 0), (0, rq - C_GROUP * t_new), (0, 0))).reshape(n_b, C_KV_HEADS * rq, hd)
    iq_s = seg("iq", IDX_HEADS * IDX_DIM).reshape(n_b, t_new * IDX_HEADS, IDX_DIM)
    iw = h_s[:, lay["small"] + 2 * B_HEADS:lay["small"] + 2 * B_HEADS + IDX_HEADS]
    iw_s = (iw * (IDX_HEADS ** -0.5 * IDX_DIM ** -0.5)).reshape(n_b, t_new * IDX_HEADS, 1)
    padrows = lambda x: jnp.pad(x.reshape(n_b, t_new, -1), ((0, 0), (0, PAGE_SIZE - t_new), (0, 0)))
    ik_new, k_new, v_new = padrows(seg("ik", IDX_DIM)), padrows(seg("ck", C_KV)), padrows(seg("cv", C_KV))
    bias = _dsa_sample_select(pt_flat, kidx_pool, layer, iq_s, iw_s, ik_new, n_b=n_b, n_pages=n_pages, t_new=t_new)
    o = _dsa_sample_attend(pt_flat, k_pool, v_pool, layer, q_s, bias, k_new, v_new,
                           n_b=n_b, n_pages=n_pages, t_new=t_new)
    o = o.reshape(n_b, C_KV_HEADS, rq, hd)[:, :, :C_GROUP * t_new].reshape(n_b, C_KV_HEADS, C_GROUP, t_new, hd)
    return o.transpose(0, 3, 1, 2, 4).reshape(n_b * t_new, C_Q).astype(BF16)


def _merge_body(ya_ref, yb_ref, yc_ref, w_ref, g0_ref, g1_ref, g2_ref, o_ref):
    b0, c0 = A_WIDTH, A_WIDTH + B_WIDTH
    acc = jax.nn.sigmoid(g0_ref[...]) * jnp.dot(ya_ref[...], w_ref[0:b0, :], preferred_element_type=F32)
    acc = acc + jax.nn.sigmoid(g1_ref[...]) * jnp.dot(yb_ref[...], w_ref[b0:c0, :], preferred_element_type=F32)
    acc = acc + jax.nn.sigmoid(g2_ref[...]) * jnp.dot(yc_ref[...], w_ref[c0:, :], preferred_element_type=F32)
    o_ref[...] = acc.astype(o_ref.dtype)


def _merge(ya, yb, yc, w_br, layer, h, lay, d_model):
    m = ya.shape[0]
    tm = _pick(m, 1024)
    tn = _pick(d_model, 512)
    gcb = lay["gate"] // tn
    per = d_model // tn

    def gate_spec(i):
        return pl.BlockSpec((tm, tn), lambda r, c: (r, gcb + i * per + c))

    kw = A_WIDTH + B_WIDTH + C_Q
    vmem = 2 * (tm * kw * 2 + kw * tn * 2 + 3 * tm * tn * 4 + tm * tn * 2) + 4 * tm * tn * 4
    return pl.pallas_call(
        _merge_body,
        out_shape=jax.ShapeDtypeStruct((m, d_model), BF16),
        grid=(m // tm, d_model // tn),
        in_specs=[pl.BlockSpec((tm, A_WIDTH), lambda r, c: (r, 0)),
                  pl.BlockSpec((tm, B_WIDTH), lambda r, c: (r, 0)),
                  pl.BlockSpec((tm, C_Q), lambda r, c: (r, 0)),
                  _layer_spec((kw, tn), lambda r, c: (0, c), layer),
                  gate_spec(0), gate_spec(1), gate_spec(2)],
        out_specs=pl.BlockSpec((tm, tn), lambda r, c: (r, c)),
        compiler_params=_cparams(("parallel", "arbitrary"), vmem + COMPILER_SCRATCH_BYTES),
        name="gated_merge",
    )(ya, yb, yc, w_br, h, h, h)


def _ffn_up_body(hn_ref, w1_ref, w3_ref, o_ref):
    hn = hn_ref[...]
    a = jnp.dot(hn, w1_ref[...], preferred_element_type=F32)
    b = jnp.dot(hn, w3_ref[...], preferred_element_type=F32)
    o_ref[...] = (jax.nn.silu(a) * b).astype(o_ref.dtype)


def _ffn(hn, w1, w3, w2, layer, x):
    m, d = hn.shape
    d_ff = w1.shape[-1]
    tm = _pick(m, 1024)
    tf = FFN_TF
    assert d_ff % tf == 0
    vmem = 2 * (tm * d * 2 + 2 * d * tf * 2 + tm * tf * 2) + 3 * tm * tf * 4
    act = pl.pallas_call(
        _ffn_up_body,
        out_shape=jax.ShapeDtypeStruct((m, d_ff), BF16),
        grid=(m // tm, d_ff // tf),
        in_specs=[pl.BlockSpec((tm, d), lambda i, f: (i, 0)),
                  _layer_spec((d, tf), lambda i, f: (0, f), layer),
                  _layer_spec((d, tf), lambda i, f: (0, f), layer)],
        out_specs=pl.BlockSpec((tm, tf), lambda i, f: (i, f)),
        compiler_params=_cparams(("parallel", "arbitrary"), vmem + COMPILER_SCRATCH_BYTES),
        name="swiglu_up",
    )(hn, w1, w3)
    return _matmul(act, w2, layer=layer, residual=x, tm_pref=512, tn_pref=512, a_single_buffer=True,
                   name="swiglu_down")


def _dense_front(x, ln1, w_in_p, layer):
    xn = _rmsnorm(x, ln1, BF16)
    return _matmul(xn, w_in_p, layer=layer, w_transposed=True, tn_pref=768, name="in_proj")


def _dense_back(x, ya, yb, yc, h, lay, wts, layer, d_model):
    w_br, w_o, w1, w3, w2, ln2 = wts
    mix = _merge(ya, yb, yc, w_br, layer, h, lay, d_model)
    x = _matmul(mix, w_o, layer=layer, residual=x, name="out_proj")
    hn = _rmsnorm(x, ln2, BF16)
    return _ffn(hn, w1, w3, w2, layer, x)


def kernel(x_prompt, x_sample, cache_k, cache_v, cache_kidx, state_conv, state_delta, page_table, ln1, w_in,
           a_ln_g, a_ln_b, a_ws, a_bs, b_conv_w, b_a_log, b_dt_bias, b_out_g, w_br, w_o, ln2, ffn_w1, ffn_w3,
           ffn_w2, ln_f):
    n_bp, seq, d_model = x_prompt.shape
    n_bs, t_new, _ = x_sample.shape
    depth = ln1.shape[0]
    n_phys = cache_k.shape[1]
    n_pages = page_table.shape[1]
    assert seq % A_CHUNK == 0 and seq % GDN_CHUNK == 0 and CONV_W - 1 <= t_new <= min(A_CHUNK, GDN_CHUNK)
    lay = _layout(d_model)
    hd = HEAD_DIM

    xp = x_prompt.reshape(n_bp * seq, d_model)
    xs = x_sample.reshape(n_bs * t_new, d_model)
    pt_flat = page_table.reshape(-1).astype(I32)
    k_pool = cache_k.reshape(depth, n_phys, PAGE_SIZE * C_KV_HEADS, hd)
    v_pool = cache_v.reshape(depth, n_phys, PAGE_SIZE * C_KV_HEADS, hd)
    conv0_p = jnp.zeros((n_bp, SUBLANE, 3 * B_WIDTH), F32)
    s0_p = jnp.zeros((n_bp, B_HEADS, hd, hd), F32)
    qkv_off = lay["bq"]

    outs = {k: [] for k in ("pk", "pv", "pik", "pconv", "pdelta", "sk", "sv", "sik", "sconv", "sdelta", "schunk")}
    dense_w = tuple(w.astype(BF16) for w in (w_br, w_o, ffn_w1, ffn_w3, ffn_w2))
    w_in_p = _pack_w_in(w_in, d_model)
    for l in range(depth):
        wts = dense_w + (ln2[l],)

        h = _dense_front(xp, ln1[l], w_in_p, l)
        ya = _branch_a(h, lay, a_ws[l], a_bs[l], a_ln_g[l], a_ln_b[l], chunk=A_CHUNK, tb=A_CHUNK, emit_av=False)
        srcs = [(h, lay[n] // B_WIDTH) for n in ("bq", "bk", "bv", "bz")] + [(h, lay["small"] // LANE)]
        yb, s_new = _gdn(srcs, conv0_p, s0_p, b_conv_w[l], b_a_log[l], b_dt_bias[l], b_out_g[l],
                         n_b=n_bp, t_pad=seq, t_valid=GDN_CHUNK, chunk=GDN_CHUNK)
        yc = _dsa_prompt(h, lay, n_b=n_bp, seq=seq)
        h3 = h.reshape(n_bp, seq, lay["total"])
        outs["pk"].append(h3[:, :, lay["ck"]:lay["ck"] + C_KV].reshape(n_bp, seq, C_KV_HEADS, hd))
        outs["pv"].append(h3[:, :, lay["cv"]:lay["cv"] + C_KV].reshape(n_bp, seq, C_KV_HEADS, hd))
        outs["pik"].append(h3[:, :, lay["ik"]:lay["ik"] + IDX_DIM])
        outs["pconv"].append(h3[:, seq - (CONV_W - 1):, qkv_off:qkv_off + 3 * B_WIDTH])
        outs["pdelta"].append(s_new)
        xp = _dense_back(xp, ya, yb, yc, h, lay, wts, l, d_model)

        hs = _dense_front(xs, ln1[l], w_in_p, l)
        ya, av = _branch_a(hs, lay, a_ws[l], a_bs[l], a_ln_g[l], a_ln_b[l], chunk=t_new, tb=n_bs * t_new,
                           emit_av=True)
        hs3 = hs.reshape(n_bs, t_new, lay["total"])
        padt = lambda x: jnp.pad(x, ((0, 0), (0, GDN_CHUNK - t_new), (0, 0))).reshape(n_bs * GDN_CHUNK, -1)
        srcs = [(padt(hs3[:, :, lay[n]:lay[n] + B_WIDTH]), 0) for n in ("bq", "bk", "bv", "bz")]
        srcs.append((padt(hs3[:, :, lay["small"]:lay["small"] + LANE]), 0))
        conv0_s = jnp.pad(state_conv[l], ((0, 0), (SUBLANE - (CONV_W - 1), 0), (0, 0)))
        yb, s_new = _gdn(srcs, conv0_s, state_delta[l], b_conv_w[l], b_a_log[l], b_dt_bias[l], b_out_g[l],
                         n_b=n_bs, t_pad=GDN_CHUNK, t_valid=t_new, chunk=GDN_CHUNK)
        yb = yb.reshape(n_bs, GDN_CHUNK, B_WIDTH)[:, :t_new].reshape(n_bs * t_new, B_WIDTH)
        yc = _dsa_sample(hs, lay, pt_flat, k_pool, v_pool, cache_kidx, l, n_b=n_bs, t_new=t_new, n_pages=n_pages)
        outs["sk"].append(hs3[:, :, lay["ck"]:lay["ck"] + C_KV].reshape(n_bs, t_new, C_KV_HEADS, hd))
        outs["sv"].append(hs3[:, :, lay["cv"]:lay["cv"] + C_KV].reshape(n_bs, t_new, C_KV_HEADS, hd))
        outs["sik"].append(hs3[:, :, lay["ik"]:lay["ik"] + IDX_DIM])
        outs["sconv"].append(hs3[:, t_new - (CONV_W - 1):, qkv_off:qkv_off + 3 * B_WIDTH])
        outs["sdelta"].append(s_new)
        outs["schunk"].append(av.reshape(n_bs, t_new, A_WIDTH))
        xs = _dense_back(xs, ya, yb, yc, hs, lay, wts, l, d_model)

    y_prompt = _rmsnorm(xp, ln_f, F32).reshape(n_bp, seq, d_model)
    y_sample = _rmsnorm(xs, ln_f, F32).reshape(n_bs, t_new, d_model)
    st = lambda k: jnp.stack(outs[k])
    return (y_prompt, y_sample, st("pk"), st("pv"), st("pik"), st("pconv"), st("pdelta"),
            st("sk"), st("sv"), st("sik"), st("sconv"), st("sdelta"), st("schunk"))
```

```python
import functools

import jax
import jax.numpy as jnp
from jax import lax
from jax.experimental import pallas as pl
from jax.experimental.pallas import tpu as pltpu

F32 = jnp.float32
BF16 = jnp.bfloat16
I32 = jnp.int32

HEAD_DIM = 128
A_GROUPS = 8
A_CHUNK = 128
A_WIDTH = A_GROUPS * HEAD_DIM
B_HEADS = 12
B_WIDTH = B_HEADS * HEAD_DIM
CONV_W = 4
C_HEADS = 12
C_KV_HEADS = 4
C_GROUP = C_HEADS // C_KV_HEADS
C_Q = C_HEADS * HEAD_DIM
C_KV = C_KV_HEADS * HEAD_DIM
IDX_HEADS = 16
IDX_DIM = 128
TOPK_MAX = 256
PAGE_SIZE = 128
N_BRANCH = 3
RMS_EPS = 1e-6
LN_EPS = 1e-5

LANE = 128
SUBLANE = 8
V7X_VMEM_BYTES = 64 * 1024 * 1024
VMEM_BUDGET = 56 * 1024 * 1024
COMPILER_SCRATCH_BYTES = 8 * 1024 * 1024

GDN_CHUNK = 128
GDN_DECODE_CHUNK = 16
DSA_TQ = 256
DSA_TK = 512
DSA_ROW_TILE = 32
FFN_TF = 256
PAGES_PER_STEP = 16
MASK_NEG = -1e30
LOG2E = 1.4426950408889634
INT_MIN = -2147483648
KEY_OF_NEG_INF = -2139095041


def _cparams(semantics, vmem_bytes):
    return pltpu.CompilerParams(dimension_semantics=semantics,
                                vmem_limit_bytes=int(min(max(vmem_bytes, 16 * 1024 * 1024), VMEM_BUDGET)))


def _pick(dim, pref):
    t = pref
    while t >= SUBLANE:
        if dim % t == 0:
            return t
        t //= 2
    return dim


def _layout(d_model):
    segs = [("gate", N_BRANCH * d_model, d_model), ("bq", B_WIDTH, B_WIDTH), ("bk", B_WIDTH, B_WIDTH),
            ("bv", B_WIDTH, B_WIDTH), ("bz", B_WIDTH, B_WIDTH), ("cq", C_Q, C_Q), ("ck", C_KV, C_KV),
            ("au", A_WIDTH, A_WIDTH), ("av", A_WIDTH, A_WIDTH), ("cv", C_KV, C_KV),
            ("iq", IDX_HEADS * IDX_DIM, 4 * IDX_DIM), ("ik", IDX_DIM, IDX_DIM), ("small", LANE, LANE)]
    off, lay = 0, {}
    for name, width, align in segs:
        assert off % align == 0, (name, off, align)
        lay[name] = off
        off += width
    lay["total"] = off
    return lay


def _pack_body(tab_ref, main_ref, ab_ref, iw_ref, o_ref):
    j = pl.program_id(1)
    last = pl.num_programs(1) - 1

    @pl.when(j != last)
    def _():
        o_ref[...] = main_ref[0].astype(o_ref.dtype)

    @pl.when(j == last)
    def _():
        n_ab, n_iw = 2 * B_HEADS, IDX_HEADS
        rest = jnp.zeros((o_ref.shape[0] - n_ab - n_iw, o_ref.shape[1]), F32)
        o_ref[...] = jnp.concatenate([ab_ref[0, 0:n_ab, :], iw_ref[0, 0:n_iw, :], rest], axis=0).astype(o_ref.dtype)


def _pack_w_in(w_in, d_model):
    depth, d, in_width = w_in.shape
    lay = _layout(d_model)
    w_t = jnp.swapaxes(w_in, 1, 2)
    widths = (A_WIDTH, A_WIDTH, 3 * B_WIDTH, B_WIDTH, 2 * B_HEADS, C_Q, C_KV, C_KV,
              IDX_HEADS * IDX_DIM, IDX_HEADS, IDX_DIM, N_BRANCH * d_model)
    dsts = (lay["au"], lay["av"], lay["bq"], lay["bz"], None, lay["cq"], lay["ck"], lay["cv"], lay["iq"], None,
            lay["ik"], lay["gate"])
    n_blocks = lay["total"] // LANE
    table, start, special = [0] * n_blocks, 0, []
    for width, dst in zip(widths, dsts):
        if dst is None:
            special.append(start)
        else:
            assert width % LANE == 0 and dst % LANE == 0 and start % SUBLANE == 0
            for blk in range(width // LANE):
                table[dst // LANE + blk] = (start + blk * LANE) // SUBLANE
        start += width
    assert start == in_width and lay["small"] // LANE == n_blocks - 1
    ab_row, iw_row = special
    assert ab_row % SUBLANE == 0 and iw_row % SUBLANE == 0 and max(ab_row, iw_row) + LANE <= in_width
    window = lambda index_map: pl.BlockSpec((pl.Element(1), pl.Element(LANE), pl.Element(d)), index_map)
    grid_spec = pltpu.PrefetchScalarGridSpec(
        num_scalar_prefetch=1,
        grid=(depth, n_blocks),
        in_specs=[window(lambda l, j, tab: (l, tab[j] * SUBLANE, 0)),
                  window(lambda l, j, tab: (l, ab_row, 0)),
                  window(lambda l, j, tab: (l, iw_row, 0))],
        out_specs=pl.BlockSpec((None, LANE, d), lambda l, j, tab: (l, j, 0)))
    return pl.pallas_call(
        _pack_body,
        out_shape=jax.ShapeDtypeStruct((depth, lay["total"], d), BF16),
        grid_spec=grid_spec,
        compiler_params=_cparams(("parallel", "arbitrary"), 8 * LANE * d * 4),
        name="pack_in_proj_weight",
    )(jnp.asarray(table, I32), w_t, w_t, w_t)


def _rmsnorm_body(x_ref, g_ref, o_ref):
    x = x_ref[...]
    ms = jnp.mean(x * x, axis=-1, keepdims=True)
    o_ref[...] = (x * lax.rsqrt(ms + RMS_EPS) * g_ref[...]).astype(o_ref.dtype)


def _rmsnorm(x, g, out_dtype):
    m, d = x.shape
    tm = _pick(m, 256)
    return pl.pallas_call(
        _rmsnorm_body,
        out_shape=jax.ShapeDtypeStruct((m, d), out_dtype),
        grid=(m // tm,),
        in_specs=[pl.BlockSpec((tm, d), lambda i: (i, 0)), pl.BlockSpec((1, d), lambda i: (0, 0))],
        out_specs=pl.BlockSpec((tm, d), lambda i: (i, 0)),
        compiler_params=_cparams(("parallel",), 6 * tm * d * 4),
        name="rmsnorm",
    )(x, g.reshape(1, d))


def _mm_body(a_ref, w_ref, o_ref):
    o_ref[...] = jnp.dot(a_ref[...], w_ref[...], preferred_element_type=F32).astype(o_ref.dtype)


def _mm_res_body(a_ref, w_ref, r_ref, o_ref):
    o_ref[...] = (r_ref[...] + jnp.dot(a_ref[...], w_ref[...], preferred_element_type=F32)).astype(o_ref.dtype)


def _mm_wt_body(a_ref, wt_ref, o_ref):
    o_ref[...] = lax.dot_general(a_ref[...], wt_ref[...], (((1,), (1,)), ((), ())),
                                 preferred_element_type=F32).astype(o_ref.dtype)


def _layer_spec(block, index_map, layer):
    if layer is None:
        return pl.BlockSpec(block, index_map)
    return pl.BlockSpec((None,) + block, lambda *g: (layer,) + index_map(*g))


def _matmul(a, w, *, layer=None, w_transposed=False, residual=None, out_dtype=F32, tm_pref=1024, tn_pref=1024,
            a_single_buffer=False, name="matmul"):
    m, k = a.shape
    n = w.shape[-2] if w_transposed else w.shape[-1]
    tm = _pick(m, tm_pref)
    tn = tn_pref if n % tn_pref == 0 else _pick(n, tn_pref)
    osz = jnp.dtype(out_dtype).itemsize
    vmem = (1 if a_single_buffer else 2) * tm * k * 2 + 2 * (k * tn * 2 + tm * tn * osz) + tm * tn * 4
    w_spec = (_layer_spec((tn, k), lambda i, j: (j, 0), layer) if w_transposed
              else _layer_spec((k, tn), lambda i, j: (0, j), layer))
    a_mode = dict(pipeline_mode=pl.Buffered(1)) if a_single_buffer else {}
    in_specs = [pl.BlockSpec((tm, k), lambda i, j: (i, 0), **a_mode), w_spec]
    args = [a, w]
    body = _mm_wt_body if w_transposed else _mm_body
    assert residual is None or not w_transposed
    if residual is not None:
        in_specs.append(pl.BlockSpec((tm, tn), lambda i, j: (i, j)))
        args.append(residual)
        body = _mm_res_body
        vmem += 2 * tm * tn * 4
    return pl.pallas_call(
        body,
        out_shape=jax.ShapeDtypeStruct((m, n), out_dtype),
        grid=(m // tm, n // tn),
        in_specs=in_specs,
        out_specs=pl.BlockSpec((tm, tn), lambda i, j: (i, j)),
        compiler_params=_cparams(("parallel", "arbitrary"), vmem + COMPILER_SCRATCH_BYTES),
        name=name,
    )(*args)


def _branch_a_body(u_ref, v_ref, w_ref, bs_ref, g_ref, b_ref, y_ref, *av_ref, chunk):
    tb = u_ref.shape[0]
    u = jax.nn.gelu(u_ref[...])
    v = jax.nn.gelu(v_ref[...])
    mu = jnp.mean(v, axis=-1, keepdims=True)
    var = jnp.mean(jnp.square(v - mu), axis=-1, keepdims=True)
    vn = (v - mu) * lax.rsqrt(var + LN_EPS) * g_ref[...] + b_ref[...]
    if av_ref:
        av_ref[0][...] = vn
    row = lax.broadcasted_iota(I32, (tb, tb), 0)
    col = lax.broadcasted_iota(I32, (tb, tb), 1)
    keep = (col <= row) & ((row // chunk) == (col // chunk))
    vb = vn.astype(BF16)
    for g in range(A_GROUPS):
        sl = slice(g * HEAD_DIM, (g + 1) * HEAD_DIM)
        wg = jnp.where(keep, w_ref[g], 0.0).astype(BF16)
        s = jnp.dot(wg, vb[:, sl], preferred_element_type=F32) + bs_ref[:, g:g + 1]
        y_ref[:, sl] = (u[:, sl] * s).astype(y_ref.dtype)


def _branch_a(h, lay, a_ws, a_bs, ln_g, ln_b, *, chunk, tb, emit_av):
    m = h.shape[0]
    reps = tb // chunk
    wfull = jnp.tile(a_ws[:, :chunk, :chunk], (1, reps, reps))
    bs_t = jnp.tile(a_bs[:, :chunk].T, (reps, 1))
    cu, cv = lay["au"] // A_WIDTH, lay["av"] // A_WIDTH
    out_shape = [jax.ShapeDtypeStruct((m, A_WIDTH), BF16)]
    out_specs = [pl.BlockSpec((tb, A_WIDTH), lambda i: (i, 0))]
    if emit_av:
        out_shape.append(jax.ShapeDtypeStruct((m, A_WIDTH), F32))
        out_specs.append(pl.BlockSpec((tb, A_WIDTH), lambda i: (i, 0)))
    res = pl.pallas_call(
        functools.partial(_branch_a_body, chunk=chunk),
        out_shape=out_shape,
        grid=(m // tb,),
        in_specs=[pl.BlockSpec((tb, A_WIDTH), lambda i: (i, cu)),
                  pl.BlockSpec((tb, A_WIDTH), lambda i: (i, cv)),
                  pl.BlockSpec((A_GROUPS, tb, tb), lambda i: (0, 0, 0)),
                  pl.BlockSpec((tb, A_GROUPS), lambda i: (0, 0)),
                  pl.BlockSpec((1, A_WIDTH), lambda i: (0, 0)),
                  pl.BlockSpec((1, A_WIDTH), lambda i: (0, 0))],
        out_specs=out_specs,
        compiler_params=_cparams(("parallel",), 32 << 20),
        name="branch_a_gmlp",
    )(h, h, wfull, bs_t, ln_g.reshape(1, A_WIDTH), ln_b.reshape(1, A_WIDTH))
    return res if emit_av else res[0]


def _bdot(a, b):
    return lax.dot_general(a.astype(BF16), b.astype(BF16), (((2,), (1,)), ((0,), (0,))),
                           preferred_element_type=F32)


def _bdot_nt(a, b):
    return lax.dot_general(a.astype(BF16), b.astype(BF16), (((2,), (2,)), ((0,), (0,))),
                           preferred_element_type=F32)


def _transpose_rows(x):
    r = x.shape[0]
    if r < LANE:
        x = jnp.concatenate([x, jnp.zeros((LANE - r, LANE), x.dtype)], axis=0)
    return x.T[:, :r]


def _gdn_body(q_ref, k_ref, v_ref, z_ref, sm_ref, c0_ref, s0_ref, cw_ref, al_ref, dt_ref, og_ref,
              y_ref, sout_ref, ext_ref, s_ref, *, chunk, t_valid):
    c = pl.program_id(1)
    nc = pl.num_programs(1)
    hd = HEAD_DIM

    @pl.when(c == 0)
    def _():
        s_ref[...] = s0_ref[0]
        for j in range(3):
            ext_ref[j, 0:SUBLANE, :] = c0_ref[0, :, j * B_WIDTH:(j + 1) * B_WIDTH]

    acts = []
    for j, ref in enumerate((q_ref, k_ref, v_ref)):
        ext_ref[j, SUBLANE:SUBLANE + chunk, :] = ref[...]
        acc = None
        for i in range(CONV_W):
            lo = SUBLANE - (CONV_W - 1) + i
            term = ext_ref[j, lo:lo + chunk, :] * cw_ref[i:i + 1, j * B_WIDTH:(j + 1) * B_WIDTH]
            acc = term if acc is None else acc + term
        acts.append(jax.nn.silu(acc))
        ext_ref[j, 0:SUBLANE, :] = ext_ref[j, chunk:chunk + SUBLANE, :]
    qa, ka, va = acts

    sm = sm_ref[...]
    row1 = lax.broadcasted_iota(I32, (chunk, LANE), 0)
    g_all = -jnp.exp(al_ref[...]) * jax.nn.softplus(sm + dt_ref[...])
    beta_all = jax.nn.sigmoid(sm)
    if t_valid < chunk:
        g_all = jnp.where(row1 < t_valid, g_all, 0.0)
        beta_all = jnp.where(row1 < t_valid, beta_all, 0.0)
    gc_all = g_all
    d = 1
    while d < chunk:
        gc_all = gc_all + jnp.where(row1 >= d, pltpu.roll(gc_all, d, 0), 0.0)
        d *= 2
    gc_t = _transpose_rows(gc_all)

    row = lax.broadcasted_iota(I32, (chunk, chunk), 0)
    col = lax.broadcasted_iota(I32, (chunk, chunk), 1)
    incl = row >= col
    strict = row > col
    eye = jnp.where(row == col, 1.0, 0.0)
    pair_masks = []
    bs = 1
    while bs < chunk:
        pair_masks.append(((row // bs) % 2 == 1) & ((col // bs) == (row // bs) - 1))
        bs *= 2

    heads = range(B_HEADS)
    per_head = lambda x: jnp.stack([x[:, h * hd:(h + 1) * hd] for h in heads], axis=0)
    q3, k3, v3 = per_head(qa), per_head(ka), per_head(va)
    qn = q3 * lax.rsqrt(jnp.sum(q3 * q3, axis=-1, keepdims=True) + 1e-6) * (hd ** -0.5)
    kn = k3 * lax.rsqrt(jnp.sum(k3 * k3, axis=-1, keepdims=True) + 1e-6)
    beta = jnp.stack([beta_all[:, B_HEADS + h:B_HEADS + h + 1] for h in heads], axis=0)
    gcol = jnp.stack([gc_all[:, h:h + 1] for h in heads], axis=0)
    grow = jnp.stack([gc_t[h:h + 1, :] for h in heads], axis=0)
    decay = jnp.exp(jnp.where(incl[None], gcol - grow, -jnp.inf))
    kb = kn * beta
    eg = jnp.exp(gcol)
    lmat = jnp.where(strict[None], _bdot_nt(kb, kn) * decay, 0.0)
    attn = _bdot_nt(qn, kn) * decay
    tinv = eye[None] - jnp.where(pair_masks[0][None], lmat, 0.0)
    for pm in pair_masks[1:]:
        tinv = tinv - _bdot(tinv, _bdot(jnp.where(pm[None], lmat, 0.0), tinv))
    sol = _bdot(tinv, jnp.concatenate([v3 * beta, kb * eg], axis=-1))
    value, kcd = sol[:, :, :hd], sol[:, :, hd:]
    s_old = s_ref[...]
    vnew = value - _bdot(kcd, s_old)
    o = _bdot(qn * eg, s_old) + _bdot(attn, vnew)
    glast = gcol[:, chunk - 1:chunk, :]
    kend = kn * jnp.exp(glast - gcol)
    s_ref[...] = s_old * jnp.exp(glast) + _bdot(jnp.swapaxes(kend, 1, 2), vnew)
    on = o * lax.rsqrt(jnp.mean(o * o, axis=-1, keepdims=True) + RMS_EPS) * og_ref[...]
    for h in heads:
        sl = slice(h * hd, (h + 1) * hd)
        y_ref[:, sl] = (on[h] * jax.nn.silu(z_ref[:, sl])).astype(y_ref.dtype)

    @pl.when(c == nc - 1)
    def _():
        sout_ref[0] = s_ref[...]


def _gdn(srcs, conv0, s0, conv_w, a_log, dt_bias, o_g, *, n_b, t_pad, t_valid, chunk):
    nc = t_pad // chunk
    arrs = [a for a, _ in srcs]
    cbs = [cb for _, cb in srcs]
    widths = [B_WIDTH] * 4 + [LANE]

    def tok_spec(w, cb):
        return pl.BlockSpec((chunk, w), lambda b, c: (b * nc + c, cb))

    pad12 = lambda x: jnp.zeros((1, LANE), F32).at[0, :B_HEADS].set(x)
    return pl.pallas_call(
        functools.partial(_gdn_body, chunk=chunk, t_valid=t_valid),
        out_shape=[jax.ShapeDtypeStruct((n_b * t_pad, B_WIDTH), BF16),
                   jax.ShapeDtypeStruct((n_b, B_HEADS, HEAD_DIM, HEAD_DIM), F32)],
        grid=(n_b, nc),
        in_specs=[tok_spec(w, cb) for w, cb in zip(widths, cbs)] + [
            pl.BlockSpec((1, SUBLANE, 3 * B_WIDTH), lambda b, c: (b, 0, 0)),
            pl.BlockSpec((1, B_HEADS, HEAD_DIM, HEAD_DIM), lambda b, c: (b, 0, 0, 0)),
            pl.BlockSpec((CONV_W, 3 * B_WIDTH), lambda b, c: (0, 0)),
            pl.BlockSpec((1, LANE), lambda b, c: (0, 0)),
            pl.BlockSpec((1, LANE), lambda b, c: (0, 0)),
            pl.BlockSpec((1, HEAD_DIM), lambda b, c: (0, 0))],
        out_specs=[pl.BlockSpec((chunk, B_WIDTH), lambda b, c: (b * nc + c, 0)),
                   pl.BlockSpec((1, B_HEADS, HEAD_DIM, HEAD_DIM), lambda b, c: (b, 0, 0, 0))],
        scratch_shapes=[pltpu.VMEM((3, chunk + SUBLANE, B_WIDTH), F32),
                        pltpu.VMEM((B_HEADS, HEAD_DIM, HEAD_DIM), F32)],
        compiler_params=_cparams(("parallel", "arbitrary"), 40 << 20),
        name="branch_b_gated_delta",
    )(*arrs, conv0, s0, conv_w, pad12(a_log), pad12(dt_bias), o_g.reshape(1, HEAD_DIM))


def _sort_key(x):
    b = lax.bitcast_convert_type(x + 0.0, I32)
    return b ^ ((b >> 31) & 0x7FFFFFFF)


def _select_rule(count, shape, k, nbits):
    kf = jnp.float32(k)
    count_ge = lambda cand: count(lambda key, pos: key >= cand)
    t0 = jnp.where(count_ge(jnp.zeros(shape, I32)) >= kf, 0, INT_MIN).astype(I32)

    def bit_step(i, t):
        cand = t + lax.shift_left(jnp.int32(1), 30 - i)
        return jnp.where(count_ge(cand) >= kf, cand, t)

    t = lax.fori_loop(0, 31, bit_step, t0)
    n_ge = count_ge(t)
    tied = (n_ge > kf) & (t > KEY_OF_NEG_INF)

    def tie_index():
        r = kf - count(lambda key, pos: key > t)

        def idx_step(i, j):
            cand = j + lax.shift_left(jnp.int32(1), nbits - 1 - i)
            below = count(lambda key, pos: (key == t) & (pos <= cand - 1))
            return jnp.where(below < r, cand, j)

        return lax.fori_loop(0, nbits, idx_step, jnp.zeros(shape, I32))

    no_limit = jnp.full(shape, 2 ** 30, I32)
    j = lax.cond(jnp.sum(jnp.where(tied, 1.0, 0.0)) > 0.0,
                 lambda: jnp.where(tied, tie_index(), no_limit), lambda: no_limit)
    return t, j


def _selected(key, pos, t, j):
    return (key > t) | ((key == t) & (pos <= j))


def _dsa_prompt_body(cq_ref, iq0_ref, iq1_ref, iq2_ref, iq3_ref, sm_ref, ik_ref, ck_ref, cv_ref, y_ref,
                     key_ref, iqb_ref, wb_ref, qb_ref, m_ref, l_ref, acc_ref, p_ref, al_ref,
                     *, k_sel, nbits, tq, tk):
    rows3 = C_GROUP * tq
    qi = pl.program_id(1)
    kpg = tk // LANE
    nkb = (qi + 1) * (tq // LANE)
    ngrp = ((qi + 1) * tq + tk - 1) // tk
    hd = HEAD_DIM
    nt = (((1,), (1,)), ((), ()))
    sub = LANE // SUBLANE
    qpos3 = qi * tq + lax.broadcasted_iota(I32, (sub, SUBLANE, tq), 2)
    kofs3 = lax.broadcasted_iota(I32, (sub, SUBLANE, tq), 0) * SUBLANE + lax.broadcasted_iota(I32, (sub, SUBLANE, tq), 1)

    w_all = sm_ref[...] * (IDX_HEADS ** -0.5 * IDX_DIM ** -0.5)
    w_t = jnp.concatenate([w_all[r:r + LANE].T for r in range(0, tq, LANE)], axis=1)
    w_off = 2 * B_HEADS
    for hh in range(IDX_HEADS):
        ref = (iq0_ref, iq1_ref, iq2_ref, iq3_ref)[hh // 4]
        iqb_ref[hh] = ref[:, (hh % 4) * IDX_DIM:(hh % 4 + 1) * IDX_DIM].astype(BF16)
        wb_ref[hh] = jnp.broadcast_to(w_t[w_off + hh:w_off + hh + 1, :], (SUBLANE, tq))

    def score_grp(g, carry):
        ikg = ik_ref[pl.ds(pl.multiple_of(g * tk, tk), tk), :].astype(BF16)
        acc = jnp.zeros((tk // SUBLANE, SUBLANE, tq), F32)
        for hh in range(IDX_HEADS):
            s = lax.dot_general(ikg, iqb_ref[hh], nt, preferred_element_type=F32)
            acc = acc + wb_ref[hh] * jnp.maximum(s, 0.0).reshape(tk // SUBLANE, SUBLANE, tq)
        for t in range(kpg):
            kpos3 = (g * kpg + t) * LANE + kofs3
            blk = acc[t * sub:(t + 1) * sub]
            key_ref[g * kpg + t] = _sort_key(jnp.where(kpos3 <= qpos3, blk, -jnp.inf))
        return carry

    lax.fori_loop(0, ngrp, score_grp, 0)

    bpt = tq // LANE

    def count(pred):
        def trip(i, acc):
            for u in range(bpt):
                b = i * bpt + u
                acc = acc + jnp.sum(jnp.where(pred(key_ref[b], b * LANE + kofs3), 1.0, 0.0), axis=0)
            return acc
        acc = lax.fori_loop(0, qi + 1, trip, jnp.zeros((SUBLANE, tq), F32))
        return jnp.broadcast_to(jnp.sum(acc, axis=0, keepdims=True), (SUBLANE, tq))

    t8, j8 = _select_rule(count, (SUBLANE, tq), k_sel, nbits)

    for kvh in range(C_KV_HEADS):
        for g in range(C_GROUP):
            hsl = slice((kvh * C_GROUP + g) * hd, (kvh * C_GROUP + g + 1) * hd)
            qb_ref[kvh, g * tq:(g + 1) * tq, :] = cq_ref[:, hsl].astype(BF16)
    m_ref[...] = jnp.full(m_ref.shape, MASK_NEG, F32)
    l_ref[...] = jnp.zeros(l_ref.shape, F32)
    acc_ref[...] = jnp.zeros(acc_ref.shape, F32)
    ones = jnp.ones((tk, hd), BF16)

    def attend_grp(g, carry):
        bias_t = []
        for t in range(kpg):
            kpos3 = (g * kpg + t) * LANE + kofs3
            sel = _selected(key_ref[g * kpg + t], kpos3, t8, j8) & (kpos3 <= qpos3)
            kq = jnp.where(sel, 0.0, MASK_NEG).reshape(LANE, tq)
            bias_t.append(jnp.concatenate([kq[:, r:r + LANE].T for r in range(0, tq, LANE)], axis=0))
        bias = jnp.concatenate(bias_t, axis=1)
        start = pl.multiple_of(g * tk, tk)
        kgrp = ck_ref[pl.ds(start, tk), :].astype(BF16)
        vgrp = cv_ref[pl.ds(start, tk), :].astype(BF16)
        for kvh in range(C_KV_HEADS):
            sl = slice(kvh * hd, (kvh + 1) * hd)
            s = lax.dot_general(qb_ref[kvh], kgrp[:, sl], nt, preferred_element_type=F32)
            for r0 in range(0, rows3, DSA_ROW_TILE):
                rs = slice(r0, r0 + DSA_ROW_TILE)
                x = s[rs] * (hd ** -0.5 * LOG2E) + bias[r0 % tq:r0 % tq + DSA_ROW_TILE]
                m_old = m_ref[kvh, rs, :]
                m_new = jnp.maximum(m_old, jnp.max(x, axis=-1, keepdims=True))
                p_ref[rs, :] = jnp.exp2(x - m_new).astype(BF16)
                al_ref[rs, :] = jnp.exp2(m_old - m_new)
                m_ref[kvh, rs, :] = m_new
            pv = jnp.dot(p_ref[...], jnp.concatenate([vgrp[:, sl], ones], axis=1), preferred_element_type=F32)
            alpha = al_ref[...]
            l_ref[kvh] = alpha * l_ref[kvh] + pv[:, hd:hd + 1]
            acc_ref[kvh] = alpha * acc_ref[kvh] + pv[:, :hd]
        return carry

    lax.fori_loop(0, ngrp, attend_grp, 0)

    for kvh in range(C_KV_HEADS):
        o = acc_ref[kvh] / l_ref[kvh]
        for g in range(C_GROUP):
            hsl = slice((kvh * C_GROUP + g) * hd, (kvh * C_GROUP + g + 1) * hd)
            y_ref[:, hsl] = o[g * tq:(g + 1) * tq, :].astype(y_ref.dtype)


def _dsa_prompt(h, lay, *, n_b, seq):
    tq, tk = DSA_TQ, DSA_TK
    assert seq % tq == 0 and seq % tk == 0
    nqb = seq // tq
    k_sel = min(TOPK_MAX, seq // 4)
    nbits = max(1, (seq - 1).bit_length())
    iq_cb = lay["iq"] // (4 * IDX_DIM)
    once = pl.Buffered(1)

    def q_spec(w, cb):
        return pl.BlockSpec((tq, w), lambda b, q: (b * nqb + q, cb))

    def kv_spec(w, cb):
        return pl.BlockSpec((seq, w), lambda b, q: (b, cb), pipeline_mode=once)

    rows3 = C_GROUP * tq
    lane_pad = lambda r: r * LANE * 4
    vmem = (seq * (2 * C_KV + IDX_DIM) * 4 + 2 * tq * (C_Q + IDX_HEADS * IDX_DIM + LANE) * 4 + 2 * tq * C_Q * 2
            + (seq // LANE) * tq * LANE * 4 + IDX_HEADS * tq * LANE * 6 + C_KV_HEADS * rows3 * HEAD_DIM * 6
            + 2 * C_KV_HEADS * lane_pad(rows3) + 4 * rows3 * tk * 4)
    return pl.pallas_call(
        functools.partial(_dsa_prompt_body, k_sel=k_sel, nbits=nbits, tq=tq, tk=tk),
        out_shape=jax.ShapeDtypeStruct((n_b * seq, C_Q), BF16),
        grid=(n_b, nqb),
        in_specs=[q_spec(C_Q, lay["cq"] // C_Q)]
                 + [q_spec(4 * IDX_DIM, iq_cb + i) for i in range(4)]
                 + [q_spec(LANE, lay["small"] // LANE),
                    kv_spec(IDX_DIM, lay["ik"] // IDX_DIM),
                    kv_spec(C_KV, lay["ck"] // C_KV),
                    kv_spec(C_KV, lay["cv"] // C_KV)],
        out_specs=pl.BlockSpec((tq, C_Q), lambda b, q: (b * nqb + q, 0)),
        scratch_shapes=[pltpu.VMEM((seq // LANE, LANE // SUBLANE, SUBLANE, tq), I32),
                        pltpu.VMEM((IDX_HEADS, tq, IDX_DIM), BF16),
                        pltpu.VMEM((IDX_HEADS, SUBLANE, tq), F32),
                        pltpu.VMEM((C_KV_HEADS, rows3, HEAD_DIM), BF16),
                        pltpu.VMEM((C_KV_HEADS, rows3, 1), F32),
                        pltpu.VMEM((C_KV_HEADS, rows3, 1), F32),
                        pltpu.VMEM((C_KV_HEADS, rows3, HEAD_DIM), F32),
                        pltpu.VMEM((rows3, tk), BF16),
                        pltpu.VMEM((rows3, 1), F32)],
        compiler_params=_cparams(("parallel", "arbitrary"), vmem + COMPILER_SCRATCH_BYTES),
        name="branch_c_prompt_dsa",
    )(h, h, h, h, h, h, h, h, h)


def _dsa_sample_select_body(pt_ref, *refs, n_pages, k_sel, nbits, t_new):
    pp = PAGES_PER_STEP
    page_refs = refs[:pp]
    iq_ref, w_ref, ikn_ref, bias_ref, key_ref = refs[pp:]
    j = pl.program_id(1)
    n_steps = n_pages // pp
    rows = t_new
    col = lax.broadcasted_iota(I32, (rows, LANE), 1)
    trow = lax.broadcasted_iota(I32, (rows, LANE), 0)
    iq = iq_ref[0].astype(BF16)

    def scores(keys_f32):
        n = keys_f32.shape[0]
        s = lax.dot_general(iq, keys_f32.astype(BF16), (((1,), (1,)), ((), ())), preferred_element_type=F32)
        r = jnp.maximum(s, 0.0) * w_ref[0]
        return jnp.sum(r.reshape(rows, IDX_HEADS, n), axis=1)

    @pl.when(j < n_steps)
    def _():
        keys = _sort_key(scores(jnp.concatenate([r[0, 0] for r in page_refs], axis=0)))
        for i in range(pp):
            key_ref[j * pp + i] = keys[:, i * LANE:(i + 1) * LANE]

    @pl.when(j == n_steps)
    def _():
        new_ok = (col <= trow) & (col < t_new)
        key_ref[n_pages] = _sort_key(jnp.where(new_ok, scores(ikn_ref[0]), -jnp.inf))
        nblk = n_pages + 1
        grp = SUBLANE
        nblk_pad = key_ref.shape[0]
        if nblk_pad > nblk:
            key_ref[nblk:nblk_pad] = jnp.full((nblk_pad - nblk, rows, LANE), KEY_OF_NEG_INF, I32)
        blk3 = lax.broadcasted_iota(I32, (grp, rows, LANE), 0)
        col3 = lax.broadcasted_iota(I32, (grp, rows, LANE), 2)

        def count(pred):
            def trip(i, acc):
                b0 = pl.multiple_of(i * grp, grp)
                hit = pred(key_ref[pl.ds(b0, grp)], (b0 + blk3) * LANE + col3)
                return acc + jnp.sum(jnp.where(hit, 1.0, 0.0), axis=0)
            acc = lax.fori_loop(0, nblk_pad // grp, trip, jnp.zeros((rows, LANE), F32))
            return jnp.broadcast_to(jnp.sum(acc, axis=1, keepdims=True), (rows, LANE))

        tb, jb = _select_rule(count, (rows, LANE), k_sel, nbits)

        def write_blk(b, carry):
            kpos = b * LANE + col
            sel = _selected(key_ref[b], kpos, tb, jb) & ((kpos < n_pages * PAGE_SIZE) | new_ok)
            bias_ref[0, b] = jnp.where(sel, 0.0, MASK_NEG)
            return carry

        lax.fori_loop(0, nblk, write_blk, 0)


def _page_index_map(layer, i, n_pages, n_steps, trailing):
    def index_map(b, j, pt):
        step = jnp.minimum(j, n_steps - 1)
        return (layer, pt[b * n_pages + step * PAGES_PER_STEP + i]) + (0,) * trailing
    return index_map


def _dsa_sample_select(pt_flat, kidx_pool, layer, iq_s, iw_s, ik_new, *, n_b, n_pages, t_new):
    pp = PAGES_PER_STEP
    assert n_pages % pp == 0
    n_steps = n_pages // pp
    total = n_pages * PAGE_SIZE + t_new
    k_sel = min(TOPK_MAX, total // 4)
    nbits = max(1, ((n_pages + 1) * PAGE_SIZE - 1).bit_length())
    rows_q = t_new * IDX_HEADS
    grid_spec = pltpu.PrefetchScalarGridSpec(
        num_scalar_prefetch=1,
        grid=(n_b, n_steps + 1),
        in_specs=[pl.BlockSpec((1, 1, PAGE_SIZE, IDX_DIM), _page_index_map(layer, i, n_pages, n_steps, 2))
                  for i in range(pp)]
                 + [pl.BlockSpec((1, rows_q, IDX_DIM), lambda b, j, pt: (b, 0, 0)),
                    pl.BlockSpec((1, rows_q, 1), lambda b, j, pt: (b, 0, 0)),
                    pl.BlockSpec((1, PAGE_SIZE, IDX_DIM), lambda b, j, pt: (b, 0, 0))],
        out_specs=pl.BlockSpec((1, n_pages + 1, t_new, LANE), lambda b, j, pt: (b, 0, 0, 0)),
        scratch_shapes=[pltpu.VMEM((-(-(n_pages + 1) // SUBLANE) * SUBLANE, t_new, LANE), I32)])
    return pl.pallas_call(
        functools.partial(_dsa_sample_select_body, n_pages=n_pages, k_sel=k_sel, nbits=nbits, t_new=t_new),
        out_shape=jax.ShapeDtypeStruct((n_b, n_pages + 1, t_new, LANE), F32),
        grid_spec=grid_spec,
        compiler_params=_cparams(("parallel", "arbitrary"), 24 << 20),
        name="branch_c_decode_select",
    )(pt_flat, *([kidx_pool] * pp), iq_s, iw_s, ik_new)


def _dsa_sample_attend_body(pt_ref, *refs, n_pages, t_new):
    pp = PAGES_PER_STEP
    k_refs, v_refs = refs[:pp], refs[pp:2 * pp]
    q_ref, bias_ref, kn_ref, vn_ref, o_ref, m_ref, l_ref, acc_ref = refs[2 * pp:]
    j = pl.program_id(1)
    n_steps = n_pages // pp
    hd = HEAD_DIM
    rq = q_ref.shape[1] // C_KV_HEADS
    reps = rq // t_new

    @pl.when(j == 0)
    def _():
        m_ref[...] = jnp.full(m_ref.shape, MASK_NEG, F32)
        l_ref[...] = jnp.zeros(l_ref.shape, F32)
        acc_ref[...] = jnp.zeros(acc_ref.shape, F32)

    def attend(k_of, v_of, bias_t):
        bias = jnp.concatenate([bias_t] * reps, axis=0)
        for kvh in range(C_KV_HEADS):
            rs = slice(kvh * rq, (kvh + 1) * rq)
            s = lax.dot_general(q_ref[0, rs, :].astype(BF16), k_of(kvh).astype(BF16), (((1,), (1,)), ((), ())),
                                preferred_element_type=F32) * (hd ** -0.5) + bias
            m_old = m_ref[rs, :]
            m_new = jnp.maximum(m_old, jnp.max(s, axis=-1, keepdims=True))
            alpha = jnp.exp(m_old - m_new)
            p = jnp.exp(s - m_new)
            l_ref[rs, :] = alpha * l_ref[rs, :] + jnp.sum(p, axis=-1, keepdims=True)
            acc_ref[rs, :] = alpha * acc_ref[rs, :] + jnp.dot(p.astype(BF16), v_of(kvh).astype(BF16),
                                                              preferred_element_type=F32)
            m_ref[rs, :] = m_new

    @pl.when(j < n_steps)
    def _():
        head_rows = lambda kvh: pl.ds(kvh, PAGE_SIZE, stride=C_KV_HEADS)
        pages = lambda prefs: (lambda kvh: jnp.concatenate([r[0, 0, head_rows(kvh), :] for r in prefs], axis=0))
        bias_t = jnp.concatenate([bias_ref[0, j * pp + i] for i in range(pp)], axis=1)
        attend(pages(k_refs), pages(v_refs), bias_t)

    @pl.when(j == n_steps)
    def _():
        new = lambda ref: (lambda kvh: ref[0, :, kvh * hd:(kvh + 1) * hd])
        attend(new(kn_ref), new(vn_ref), bias_ref[0, n_pages])
        o_ref[0] = acc_ref[...] / l_ref[...]


def _dsa_sample_attend(pt_flat, k_pool, v_pool, layer, q_s, bias, k_new, v_new, *, n_b, n_pages, t_new):
    pp = PAGES_PER_STEP
    n_steps = n_pages // pp
    rows = q_s.shape[1]
    page_specs = [pl.BlockSpec((1, 1, PAGE_SIZE * C_KV_HEADS, HEAD_DIM), _page_index_map(layer, i, n_pages, n_steps, 2))
                  for i in range(pp)]
    grid_spec = pltpu.PrefetchScalarGridSpec(
        num_scalar_prefetch=1,
        grid=(n_b, n_steps + 1),
        in_specs=page_specs + page_specs
                 + [pl.BlockSpec((1, rows, HEAD_DIM), lambda b, j, pt: (b, 0, 0)),
                    pl.BlockSpec((1, n_pages + 1, t_new, LANE), lambda b, j, pt: (b, 0, 0, 0)),
                    pl.BlockSpec((1, PAGE_SIZE, C_KV), lambda b, j, pt: (b, 0, 0)),
                    pl.BlockSpec((1, PAGE_SIZE, C_KV), lambda b, j, pt: (b, 0, 0))],
        out_specs=pl.BlockSpec((1, rows, HEAD_DIM), lambda b, j, pt: (b, 0, 0)),
        scratch_shapes=[pltpu.VMEM((rows, 1), F32), pltpu.VMEM((rows, 1), F32), pltpu.VMEM((rows, HEAD_DIM), F32)])
    return pl.pallas_call(
        functools.partial(_dsa_sample_attend_body, n_pages=n_pages, t_new=t_new),
        out_shape=jax.ShapeDtypeStruct((n_b, rows, HEAD_DIM), F32),
        grid_spec=grid_spec,
        compiler_params=_cparams(("parallel", "arbitrary"), 32 << 20),
        name="branch_c_decode_attend",
    )(pt_flat, *([k_pool] * pp), *([v_pool] * pp), q_s, bias, k_new, v_new)


def _dsa_sample(h_s, lay, pt_flat, k_pool, v_pool, kidx_pool, layer, *, n_b, t_new, n_pages):
    hd = HEAD_DIM
    seg = lambda name, w: h_s[:, lay[name]:lay[name] + w]
    rq = -(-C_GROUP * t_new // 16) * 16
    q = seg("cq", C_Q).reshape(n_b, t_new, C_KV_HEADS, C_GROUP, hd).transpose(0, 2, 3, 1, 4)
    q = q.reshape(n_b, C_KV_HEADS, C_GROUP * t_new, hd)
    q_s = jnp.pad(q, ((0, 0), (0, 0), (0, rq - C_GROUP * t_new), (0, 0))).reshape(n_b, C_KV_HEADS * rq, hd)
    iq_s = seg("iq", IDX_HEADS * IDX_DIM).reshape(n_b, t_new * IDX_HEADS, IDX_DIM)
    iw = h_s[:, lay["small"] + 2 * B_HEADS:lay["small"] + 2 * B_HEADS + IDX_HEADS]
    iw_s = (iw * (IDX_HEADS ** -0.5 * IDX_DIM ** -0.5)).reshape(n_b, t_new * IDX_HEADS, 1)
    padrows = lambda x: jnp.pad(x.reshape(n_b, t_new, -1), ((0, 0), (0, PAGE_SIZE - t_new), (0, 0)))
    ik_new, k_new, v_new = padrows(seg("ik", IDX_DIM)), padrows(seg("ck", C_KV)), padrows(seg("cv", C_KV))
    bias = _dsa_sample_select(pt_flat, kidx_pool, layer, iq_s, iw_s, ik_new, n_b=n_b, n_pages=n_pages, t_new=t_new)
    o = _dsa_sample_attend(pt_flat, k_pool, v_pool, layer, q_s, bias, k_new, v_new,
                           n_b=n_b, n_pages=n_pages, t_new=t_new)
    o = o.reshape(n_b, C_KV_HEADS, rq, hd)[:, :, :C_GROUP * t_new].reshape(n_b, C_KV_HEADS, C_GROUP, t_new, hd)
    return o.transpose(0, 3, 1, 2, 4).reshape(n_b * t_new, C_Q).astype(BF16)


def _merge_body(ya_ref, yb_ref, yc_ref, w_ref, g0_ref, g1_ref, g2_ref, o_ref):
    b0, c0 = A_WIDTH, A_WIDTH + B_WIDTH
    acc = jax.nn.sigmoid(g0_ref[...]) * jnp.dot(ya_ref[...], w_ref[0:b0, :], preferred_element_type=F32)
    acc = acc + jax.nn.sigmoid(g1_ref[...]) * jnp.dot(yb_ref[...], w_ref[b0:c0, :], preferred_element_type=F32)
    acc = acc + jax.nn.sigmoid(g2_ref[...]) * jnp.dot(yc_ref[...], w_ref[c0:, :], preferred_element_type=F32)
    o_ref[...] = acc.astype(o_ref.dtype)


def _merge(ya, yb, yc, w_br, layer, h, lay, d_model):
    m = ya.shape[0]
    tm = _pick(m, 1024)
    tn = _pick(d_model, 512)
    gcb = lay["gate"] // tn
    per = d_model // tn

    def gate_spec(i):
        return pl.BlockSpec((tm, tn), lambda r, c: (r, gcb + i * per + c))

    kw = A_WIDTH + B_WIDTH + C_Q
    vmem = 2 * (tm * kw * 2 + kw * tn * 2 + 3 * tm * tn * 4 + tm * tn * 2) + 4 * tm * tn * 4
    return pl.pallas_call(
        _merge_body,
        out_shape=jax.ShapeDtypeStruct((m, d_model), BF16),
        grid=(m // tm, d_model // tn),
        in_specs=[pl.BlockSpec((tm, A_WIDTH), lambda r, c: (r, 0)),
                  pl.BlockSpec((tm, B_WIDTH), lambda r, c: (r, 0)),
                  pl.BlockSpec((tm, C_Q), lambda r, c: (r, 0)),
                  _layer_spec((kw, tn), lambda r, c: (0, c), layer),
                  gate_spec(0), gate_spec(1), gate_spec(2)],
        out_specs=pl.BlockSpec((tm, tn), lambda r, c: (r, c)),
        compiler_params=_cparams(("parallel", "arbitrary"), vmem + COMPILER_SCRATCH_BYTES),
        name="gated_merge",
    )(ya, yb, yc, w_br, h, h, h)


def _ffn_up_body(hn_ref, w1_ref, w3_ref, o_ref):
    hn = hn_ref[...]
    a = jnp.dot(hn, w1_ref[...], preferred_element_type=F32)
    b = jnp.dot(hn, w3_ref[...], preferred_element_type=F32)
    o_ref[...] = (jax.nn.silu(a) * b).astype(o_ref.dtype)


def _ffn(hn, w1, w3, w2, layer, x):
    m, d = hn.shape
    d_ff = w1.shape[-1]
    tm = _pick(m, 1024)
    tf = FFN_TF
    assert d_ff % tf == 0
    vmem = 2 * (tm * d * 2 + 2 * d * tf * 2 + tm * tf * 2) + 3 * tm * tf * 4
    act = pl.pallas_call(
        _ffn_up_body,
        out_shape=jax.ShapeDtypeStruct((m, d_ff), BF16),
        grid=(m // tm, d_ff // tf),
        in_specs=[pl.BlockSpec((tm, d), lambda i, f: (i, 0)),
                  _layer_spec((d, tf), lambda i, f: (0, f), layer),
                  _layer_spec((d, tf), lambda i, f: (0, f), layer)],
        out_specs=pl.BlockSpec((tm, tf), lambda i, f: (i, f)),
        compiler_params=_cparams(("parallel", "arbitrary"), vmem + COMPILER_SCRATCH_BYTES),
        name="swiglu_up",
    )(hn, w1, w3)
    return _matmul(act, w2, layer=layer, residual=x, tm_pref=1024, tn_pref=256, a_single_buffer=True,
                   name="swiglu_down")


def _dense_front(x, ln1, w_in_p, layer):
    xn = _rmsnorm(x, ln1, BF16)
    return _matmul(xn, w_in_p, layer=layer, w_transposed=True, tn_pref=768, name="in_proj")


def _dense_back(x, ya, yb, yc, h, lay, wts, layer, d_model):
    w_br, w_o, w1, w3, w2, ln2 = wts
    mix = _merge(ya, yb, yc, w_br, layer, h, lay, d_model)
    x = _matmul(mix, w_o, layer=layer, residual=x, name="out_proj")
    hn = _rmsnorm(x, ln2, BF16)
    return _ffn(hn, w1, w3, w2, layer, x)


def kernel(x_prompt, x_sample, cache_k, cache_v, cache_kidx, state_conv, state_delta, page_table, ln1, w_in,
           a_ln_g, a_ln_b, a_ws, a_bs, b_conv_w, b_a_log, b_dt_bias, b_out_g, w_br, w_o, ln2, ffn_w1, ffn_w3,
           ffn_w2, ln_f):
    n_bp, seq, d_model = x_prompt.shape
    n_bs, t_new, _ = x_sample.shape
    depth = ln1.shape[0]
    n_phys = cache_k.shape[1]
    n_pages = page_table.shape[1]
    assert seq % A_CHUNK == 0 and seq % GDN_CHUNK == 0 and CONV_W - 1 <= t_new <= min(A_CHUNK, GDN_DECODE_CHUNK)
    lay = _layout(d_model)
    hd = HEAD_DIM

    xp = x_prompt.reshape(n_bp * seq, d_model)
    xs = x_sample.reshape(n_bs * t_new, d_model)
    pt_flat = page_table.reshape(-1).astype(I32)
    k_pool = cache_k.reshape(depth, n_phys, PAGE_SIZE * C_KV_HEADS, hd)
    v_pool = cache_v.reshape(depth, n_phys, PAGE_SIZE * C_KV_HEADS, hd)
    conv0_p = jnp.zeros((n_bp, SUBLANE, 3 * B_WIDTH), F32)
    s0_p = jnp.zeros((n_bp, B_HEADS, hd, hd), F32)
    qkv_off = lay["bq"]

    outs = {k: [] for k in ("pk", "pv", "pik", "pconv", "pdelta", "sk", "sv", "sik", "sconv", "sdelta", "schunk")}
    dense_w = tuple(w.astype(BF16) for w in (w_br, w_o, ffn_w1, ffn_w3, ffn_w2))
    w_in_p = _pack_w_in(w_in, d_model)
    for l in range(depth):
        wts = dense_w + (ln2[l],)

        h = _dense_front(xp, ln1[l], w_in_p, l)
        ya = _branch_a(h, lay, a_ws[l], a_bs[l], a_ln_g[l], a_ln_b[l], chunk=A_CHUNK, tb=A_CHUNK, emit_av=False)
        srcs = [(h, lay[n] // B_WIDTH) for n in ("bq", "bk", "bv", "bz")] + [(h, lay["small"] // LANE)]
        yb, s_new = _gdn(srcs, conv0_p, s0_p, b_conv_w[l], b_a_log[l], b_dt_bias[l], b_out_g[l],
                         n_b=n_bp, t_pad=seq, t_valid=GDN_CHUNK, chunk=GDN_CHUNK)
        yc = _dsa_prompt(h, lay, n_b=n_bp, seq=seq)
        h3 = h.reshape(n_bp, seq, lay["total"])
        outs["pk"].append(h3[:, :, lay["ck"]:lay["ck"] + C_KV].reshape(n_bp, seq, C_KV_HEADS, hd))
        outs["pv"].append(h3[:, :, lay["cv"]:lay["cv"] + C_KV].reshape(n_bp, seq, C_KV_HEADS, hd))
        outs["pik"].append(h3[:, :, lay["ik"]:lay["ik"] + IDX_DIM])
        outs["pconv"].append(h3[:, seq - (CONV_W - 1):, qkv_off:qkv_off + 3 * B_WIDTH])
        outs["pdelta"].append(s_new)
        xp = _dense_back(xp, ya, yb, yc, h, lay, wts, l, d_model)

        hs = _dense_front(xs, ln1[l], w_in_p, l)
        ya, av = _branch_a(hs, lay, a_ws[l], a_bs[l], a_ln_g[l], a_ln_b[l], chunk=t_new, tb=n_bs * t_new,
                           emit_av=True)
        hs3 = hs.reshape(n_bs, t_new, lay["total"])
        dc = GDN_DECODE_CHUNK
        padt = lambda x: jnp.pad(x, ((0, 0), (0, dc - t_new), (0, 0))).reshape(n_bs * dc, -1)
        srcs = [(padt(hs3[:, :, lay[n]:lay[n] + B_WIDTH]), 0) for n in ("bq", "bk", "bv", "bz")]
        srcs.append((padt(hs3[:, :, lay["small"]:lay["small"] + LANE]), 0))
        conv0_s = jnp.pad(state_conv[l], ((0, 0), (SUBLANE - (CONV_W - 1), 0), (0, 0)))
        yb, s_new = _gdn(srcs, conv0_s, state_delta[l], b_conv_w[l], b_a_log[l], b_dt_bias[l], b_out_g[l],
                         n_b=n_bs, t_pad=dc, t_valid=t_new, chunk=dc)
        yb = yb.reshape(n_bs, dc, B_WIDTH)[:, :t_new].reshape(n_bs * t_new, B_WIDTH)
        yc = _dsa_sample(hs, lay, pt_flat, k_pool, v_pool, cache_kidx, l, n_b=n_bs, t_new=t_new, n_pages=n_pages)
        outs["sk"].append(hs3[:, :, lay["ck"]:lay["ck"] + C_KV].reshape(n_bs, t_new, C_KV_HEADS, hd))
        outs["sv"].append(hs3[:, :, lay["cv"]:lay["cv"] + C_KV].reshape(n_bs, t_new, C_KV_HEADS, hd))
        outs["sik"].append(hs3[:, :, lay["ik"]:lay["ik"] + IDX_DIM])
        outs["sconv"].append(hs3[:, t_new - (CONV_W - 1):, qkv_off:qkv_off + 3 * B_WIDTH])
        outs["sdelta"].append(s_new)
        outs["schunk"].append(av.reshape(n_bs, t_new, A_WIDTH))
        xs = _dense_back(xs, ya, yb, yc, hs, lay, wts, l, d_model)

    y_prompt = _rmsnorm(xp, ln_f, F32).reshape(n_bp, seq, d_model)
    y_sample = _rmsnorm(xs, ln_f, F32).reshape(n_bs, t_new, d_model)
    st = lambda k: jnp.stack(outs[k])
    return (y_prompt, y_sample, st("pk"), st("pv"), st("pik"), st("pconv"), st("pdelta"),
            st("sk"), st("sv"), st("sik"), st("sconv"), st("sdelta"), st("schunk"))
```

```python
import functools

import jax
import jax.numpy as jnp
from jax import lax
from jax.experimental import pallas as pl
from jax.experimental.pallas import tpu as pltpu

F32 = jnp.float32
BF16 = jnp.bfloat16
I32 = jnp.int32

HEAD_DIM = 128
A_GROUPS = 8
A_CHUNK = 128
A_WIDTH = A_GROUPS * HEAD_DIM
B_HEADS = 12
B_WIDTH = B_HEADS * HEAD_DIM
CONV_W = 4
C_HEADS = 12
C_KV_HEADS = 4
C_GROUP = C_HEADS // C_KV_HEADS
C_Q = C_HEADS * HEAD_DIM
C_KV = C_KV_HEADS * HEAD_DIM
IDX_HEADS = 16
IDX_DIM = 128
TOPK_MAX = 256
PAGE_SIZE = 128
N_BRANCH = 3
RMS_EPS = 1e-6
LN_EPS = 1e-5

LANE = 128
SUBLANE = 8
V7X_VMEM_BYTES = 64 * 1024 * 1024
VMEM_BUDGET = 56 * 1024 * 1024
COMPILER_SCRATCH_BYTES = 8 * 1024 * 1024

GDN_CHUNK = 128
GDN_DECODE_CHUNK = 16
DSA_TQ = 256
DSA_TK = 512
DSA_ROW_TILE = 32
FFN_TF = 256
PAGES_PER_STEP = 16
MASK_NEG = -1e30
LOG2E = 1.4426950408889634
INT_MIN = -2147483648
KEY_OF_NEG_INF = -2139095041


def _cparams(semantics, vmem_bytes):
    return pltpu.CompilerParams(dimension_semantics=semantics,
                                vmem_limit_bytes=int(min(max(vmem_bytes, 16 * 1024 * 1024), VMEM_BUDGET)))


def _pick(dim, pref):
    t = pref
    while t >= SUBLANE:
        if dim % t == 0:
            return t
        t //= 2
    return dim


def _layout(d_model):
    segs = [("gate", N_BRANCH * d_model, d_model), ("bq", B_WIDTH, B_WIDTH), ("bk", B_WIDTH, B_WIDTH),
            ("bv", B_WIDTH, B_WIDTH), ("bz", B_WIDTH, B_WIDTH), ("cq", C_Q, C_Q), ("ck", C_KV, C_KV),
            ("au", A_WIDTH, A_WIDTH), ("av", A_WIDTH, A_WIDTH), ("cv", C_KV, C_KV),
            ("iq", IDX_HEADS * IDX_DIM, 4 * IDX_DIM), ("ik", IDX_DIM, IDX_DIM), ("small", LANE, LANE)]
    off, lay = 0, {}
    for name, width, align in segs:
        assert off % align == 0, (name, off, align)
        lay[name] = off
        off += width
    lay["total"] = off
    return lay


def _pack_body(tab_ref, first_ref, second_ref, ab_ref, iw_ref, o_ref):
    j = pl.program_id(1)
    last = pl.num_programs(1) - 1
    o_ref[0:LANE, :] = first_ref[0].astype(o_ref.dtype)

    @pl.when(j != last)
    def _():
        o_ref[LANE:, :] = second_ref[0].astype(o_ref.dtype)

    @pl.when(j == last)
    def _():
        n_ab, n_iw = 2 * B_HEADS, IDX_HEADS
        rest = jnp.zeros((LANE - n_ab - n_iw, o_ref.shape[1]), F32)
        o_ref[LANE:, :] = jnp.concatenate([ab_ref[0, 0:n_ab, :], iw_ref[0, 0:n_iw, :], rest],
                                          axis=0).astype(o_ref.dtype)


def _pack_w_in(w_in, d_model):
    depth, d, in_width = w_in.shape
    lay = _layout(d_model)
    w_t = jnp.swapaxes(w_in, 1, 2)
    widths = (A_WIDTH, A_WIDTH, 3 * B_WIDTH, B_WIDTH, 2 * B_HEADS, C_Q, C_KV, C_KV,
              IDX_HEADS * IDX_DIM, IDX_HEADS, IDX_DIM, N_BRANCH * d_model)
    dsts = (lay["au"], lay["av"], lay["bq"], lay["bz"], None, lay["cq"], lay["ck"], lay["cv"], lay["iq"], None,
            lay["ik"], lay["gate"])
    n_blocks = lay["total"] // LANE
    table, start, special = [0] * n_blocks, 0, []
    for width, dst in zip(widths, dsts):
        if dst is None:
            special.append(start)
        else:
            assert width % LANE == 0 and dst % LANE == 0 and start % SUBLANE == 0
            for blk in range(width // LANE):
                table[dst // LANE + blk] = (start + blk * LANE) // SUBLANE
        start += width
    assert start == in_width and lay["small"] // LANE == n_blocks - 1 and n_blocks % 2 == 0
    ab_row, iw_row = special
    assert ab_row % SUBLANE == 0 and iw_row % SUBLANE == 0 and max(ab_row, iw_row) + LANE <= in_width
    window = lambda index_map: pl.BlockSpec((pl.Element(1), pl.Element(LANE), pl.Element(d)), index_map)
    grid_spec = pltpu.PrefetchScalarGridSpec(
        num_scalar_prefetch=1,
        grid=(depth, n_blocks // 2),
        in_specs=[window(lambda l, j, tab: (l, tab[2 * j] * SUBLANE, 0)),
                  window(lambda l, j, tab: (l, tab[2 * j + 1] * SUBLANE, 0)),
                  window(lambda l, j, tab: (l, ab_row, 0)),
                  window(lambda l, j, tab: (l, iw_row, 0))],
        out_specs=pl.BlockSpec((None, 2 * LANE, d), lambda l, j, tab: (l, j, 0)))
    return pl.pallas_call(
        _pack_body,
        out_shape=jax.ShapeDtypeStruct((depth, lay["total"], d), BF16),
        grid_spec=grid_spec,
        compiler_params=_cparams(("parallel", "arbitrary"), 14 * LANE * d * 4),
        name="pack_in_proj_weight",
    )(jnp.asarray(table, I32), w_t, w_t, w_t, w_t)


def _rmsnorm_body(x_ref, g_ref, o_ref):
    x = x_ref[...]
    ms = jnp.mean(x * x, axis=-1, keepdims=True)
    o_ref[...] = (x * lax.rsqrt(ms + RMS_EPS) * g_ref[...]).astype(o_ref.dtype)


def _rmsnorm(x, g, out_dtype):
    m, d = x.shape
    tm = _pick(m, 256)
    return pl.pallas_call(
        _rmsnorm_body,
        out_shape=jax.ShapeDtypeStruct((m, d), out_dtype),
        grid=(m // tm,),
        in_specs=[pl.BlockSpec((tm, d), lambda i: (i, 0)), pl.BlockSpec((1, d), lambda i: (0, 0))],
        out_specs=pl.BlockSpec((tm, d), lambda i: (i, 0)),
        compiler_params=_cparams(("parallel",), 6 * tm * d * 4),
        name="rmsnorm",
    )(x, g.reshape(1, d))


def _mm_body(a_ref, w_ref, o_ref):
    o_ref[...] = jnp.dot(a_ref[...], w_ref[...], preferred_element_type=F32).astype(o_ref.dtype)


def _mm_res_body(a_ref, w_ref, r_ref, o_ref):
    o_ref[...] = (r_ref[...] + jnp.dot(a_ref[...], w_ref[...], preferred_element_type=F32)).astype(o_ref.dtype)


def _mm_wt_body(a_ref, wt_ref, o_ref):
    o_ref[...] = lax.dot_general(a_ref[...], wt_ref[...], (((1,), (1,)), ((), ())),
                                 preferred_element_type=F32).astype(o_ref.dtype)


def _layer_spec(block, index_map, layer):
    if layer is None:
        return pl.BlockSpec(block, index_map)
    return pl.BlockSpec((None,) + block, lambda *g: (layer,) + index_map(*g))


def _matmul(a, w, *, layer=None, w_transposed=False, residual=None, out_dtype=F32, tm_pref=1024, tn_pref=1024,
            a_single_buffer=False, name="matmul"):
    m, k = a.shape
    n = w.shape[-2] if w_transposed else w.shape[-1]
    tm = _pick(m, tm_pref)
    tn = tn_pref if n % tn_pref == 0 else _pick(n, tn_pref)
    osz = jnp.dtype(out_dtype).itemsize
    vmem = (1 if a_single_buffer else 2) * tm * k * 2 + 2 * (k * tn * 2 + tm * tn * osz) + tm * tn * 4
    w_spec = (_layer_spec((tn, k), lambda i, j: (j, 0), layer) if w_transposed
              else _layer_spec((k, tn), lambda i, j: (0, j), layer))
    a_mode = dict(pipeline_mode=pl.Buffered(1)) if a_single_buffer else {}
    in_specs = [pl.BlockSpec((tm, k), lambda i, j: (i, 0), **a_mode), w_spec]
    args = [a, w]
    body = _mm_wt_body if w_transposed else _mm_body
    assert residual is None or not w_transposed
    if residual is not None:
        in_specs.append(pl.BlockSpec((tm, tn), lambda i, j: (i, j)))
        args.append(residual)
        body = _mm_res_body
        vmem += 2 * tm * tn * 4
    return pl.pallas_call(
        body,
        out_shape=jax.ShapeDtypeStruct((m, n), out_dtype),
        grid=(m // tm, n // tn),
        in_specs=in_specs,
        out_specs=pl.BlockSpec((tm, tn), lambda i, j: (i, j)),
        compiler_params=_cparams(("parallel", "arbitrary"), vmem + COMPILER_SCRATCH_BYTES),
        name=name,
    )(*args)


def _branch_a_body(u_ref, v_ref, w_ref, bs_ref, g_ref, b_ref, y_ref, *av_ref, chunk):
    tb = u_ref.shape[0]
    u = jax.nn.gelu(u_ref[...])
    v = jax.nn.gelu(v_ref[...])
    mu = jnp.mean(v, axis=-1, keepdims=True)
    var = jnp.mean(jnp.square(v - mu), axis=-1, keepdims=True)
    vn = (v - mu) * lax.rsqrt(var + LN_EPS) * g_ref[...] + b_ref[...]
    if av_ref:
        av_ref[0][...] = vn
    row = lax.broadcasted_iota(I32, (tb, tb), 0)
    col = lax.broadcasted_iota(I32, (tb, tb), 1)
    keep = (col <= row) & ((row // chunk) == (col // chunk))
    vb = vn.astype(BF16)
    for g in range(A_GROUPS):
        sl = slice(g * HEAD_DIM, (g + 1) * HEAD_DIM)
        wg = jnp.where(keep, w_ref[g], 0.0).astype(BF16)
        s = jnp.dot(wg, vb[:, sl], preferred_element_type=F32) + bs_ref[:, g:g + 1]
        y_ref[:, sl] = (u[:, sl] * s).astype(y_ref.dtype)


def _branch_a(h, lay, a_ws, a_bs, ln_g, ln_b, *, chunk, tb, emit_av):
    m = h.shape[0]
    reps = tb // chunk
    wfull = jnp.tile(a_ws[:, :chunk, :chunk], (1, reps, reps))
    bs_t = jnp.tile(a_bs[:, :chunk].T, (reps, 1))
    cu, cv = lay["au"] // A_WIDTH, lay["av"] // A_WIDTH
    out_shape = [jax.ShapeDtypeStruct((m, A_WIDTH), BF16)]
    out_specs = [pl.BlockSpec((tb, A_WIDTH), lambda i: (i, 0))]
    if emit_av:
        out_shape.append(jax.ShapeDtypeStruct((m, A_WIDTH), F32))
        out_specs.append(pl.BlockSpec((tb, A_WIDTH), lambda i: (i, 0)))
    res = pl.pallas_call(
        functools.partial(_branch_a_body, chunk=chunk),
        out_shape=out_shape,
        grid=(m // tb,),
        in_specs=[pl.BlockSpec((tb, A_WIDTH), lambda i: (i, cu)),
                  pl.BlockSpec((tb, A_WIDTH), lambda i: (i, cv)),
                  pl.BlockSpec((A_GROUPS, tb, tb), lambda i: (0, 0, 0)),
                  pl.BlockSpec((tb, A_GROUPS), lambda i: (0, 0)),
                  pl.BlockSpec((1, A_WIDTH), lambda i: (0, 0)),
                  pl.BlockSpec((1, A_WIDTH), lambda i: (0, 0))],
        out_specs=out_specs,
        compiler_params=_cparams(("parallel",), 32 << 20),
        name="branch_a_gmlp",
    )(h, h, wfull, bs_t, ln_g.reshape(1, A_WIDTH), ln_b.reshape(1, A_WIDTH))
    return res if emit_av else res[0]


def _bdot(a, b):
    return lax.dot_general(a.astype(BF16), b.astype(BF16), (((2,), (1,)), ((0,), (0,))),
                           preferred_element_type=F32)


def _bdot_nt(a, b):
    return lax.dot_general(a.astype(BF16), b.astype(BF16), (((2,), (2,)), ((0,), (0,))),
                           preferred_element_type=F32)


def _transpose_rows(x):
    r = x.shape[0]
    if r < LANE:
        x = jnp.concatenate([x, jnp.zeros((LANE - r, LANE), x.dtype)], axis=0)
    return x.T[:, :r]


def _gdn_body(q_ref, k_ref, v_ref, z_ref, sm_ref, c0_ref, s0_ref, cw_ref, al_ref, dt_ref, og_ref,
              y_ref, sout_ref, ext_ref, s_ref, *, chunk, t_valid):
    c = pl.program_id(1)
    nc = pl.num_programs(1)
    hd = HEAD_DIM

    @pl.when(c == 0)
    def _():
        s_ref[...] = s0_ref[0]
        for j in range(3):
            ext_ref[j, 0:SUBLANE, :] = c0_ref[0, :, j * B_WIDTH:(j + 1) * B_WIDTH]

    acts = []
    for j, ref in enumerate((q_ref, k_ref, v_ref)):
        ext_ref[j, SUBLANE:SUBLANE + chunk, :] = ref[...]
        acc = None
        for i in range(CONV_W):
            lo = SUBLANE - (CONV_W - 1) + i
            term = ext_ref[j, lo:lo + chunk, :] * cw_ref[i:i + 1, j * B_WIDTH:(j + 1) * B_WIDTH]
            acc = term if acc is None else acc + term
        acts.append(jax.nn.silu(acc))
        ext_ref[j, 0:SUBLANE, :] = ext_ref[j, chunk:chunk + SUBLANE, :]
    qa, ka, va = acts

    sm = sm_ref[...]
    row1 = lax.broadcasted_iota(I32, (chunk, LANE), 0)
    g_all = -jnp.exp(al_ref[...]) * jax.nn.softplus(sm + dt_ref[...])
    beta_all = jax.nn.sigmoid(sm)
    if t_valid < chunk:
        g_all = jnp.where(row1 < t_valid, g_all, 0.0)
        beta_all = jnp.where(row1 < t_valid, beta_all, 0.0)
    gc_all = g_all
    d = 1
    while d < chunk:
        gc_all = gc_all + jnp.where(row1 >= d, pltpu.roll(gc_all, d, 0), 0.0)
        d *= 2
    gc_t = _transpose_rows(gc_all)

    row = lax.broadcasted_iota(I32, (chunk, chunk), 0)
    col = lax.broadcasted_iota(I32, (chunk, chunk), 1)
    incl = row >= col
    strict = row > col
    eye = jnp.where(row == col, 1.0, 0.0)
    pair_masks = []
    bs = 1
    while bs < chunk:
        pair_masks.append(((row // bs) % 2 == 1) & ((col // bs) == (row // bs) - 1))
        bs *= 2

    heads = range(B_HEADS)
    per_head = lambda x: jnp.stack([x[:, h * hd:(h + 1) * hd] for h in heads], axis=0)
    q3, k3, v3 = per_head(qa), per_head(ka), per_head(va)
    qn = q3 * lax.rsqrt(jnp.sum(q3 * q3, axis=-1, keepdims=True) + 1e-6) * (hd ** -0.5)
    kn = k3 * lax.rsqrt(jnp.sum(k3 * k3, axis=-1, keepdims=True) + 1e-6)
    beta = jnp.stack([beta_all[:, B_HEADS + h:B_HEADS + h + 1] for h in heads], axis=0)
    gcol = jnp.stack([gc_all[:, h:h + 1] for h in heads], axis=0)
    grow = jnp.stack([gc_t[h:h + 1, :] for h in heads], axis=0)
    decay = jnp.exp(jnp.where(incl[None], gcol - grow, -jnp.inf))
    kb = kn * beta
    eg = jnp.exp(gcol)
    lmat = jnp.where(strict[None], _bdot_nt(kb, kn) * decay, 0.0)
    attn = _bdot_nt(qn, kn) * decay
    tinv = eye[None] - jnp.where(pair_masks[0][None], lmat, 0.0)
    for pm in pair_masks[1:]:
        tinv = tinv - _bdot(tinv, _bdot(jnp.where(pm[None], lmat, 0.0), tinv))
    sol = _bdot(tinv, jnp.concatenate([v3 * beta, kb * eg], axis=-1))
    value, kcd = sol[:, :, :hd], sol[:, :, hd:]
    s_old = s_ref[...]
    vnew = value - _bdot(kcd, s_old)
    o = _bdot(qn * eg, s_old) + _bdot(attn, vnew)
    glast = gcol[:, chunk - 1:chunk, :]
    kend = kn * jnp.exp(glast - gcol)
    s_ref[...] = s_old * jnp.exp(glast) + _bdot(jnp.swapaxes(kend, 1, 2), vnew)
    on = o * lax.rsqrt(jnp.mean(o * o, axis=-1, keepdims=True) + RMS_EPS) * og_ref[...]
    for h in heads:
        sl = slice(h * hd, (h + 1) * hd)
        y_ref[:, sl] = (on[h] * jax.nn.silu(z_ref[:, sl])).astype(y_ref.dtype)

    @pl.when(c == nc - 1)
    def _():
        sout_ref[0] = s_ref[...]


def _gdn(srcs, conv0, s0, conv_w, a_log, dt_bias, o_g, *, n_b, t_pad, t_valid, chunk):
    nc = t_pad // chunk
    arrs = [a for a, _ in srcs]
    cbs = [cb for _, cb in srcs]
    widths = [B_WIDTH] * 4 + [LANE]

    def tok_spec(w, cb):
        return pl.BlockSpec((chunk, w), lambda b, c: (b * nc + c, cb))

    pad12 = lambda x: jnp.zeros((1, LANE), F32).at[0, :B_HEADS].set(x)
    return pl.pallas_call(
        functools.partial(_gdn_body, chunk=chunk, t_valid=t_valid),
        out_shape=[jax.ShapeDtypeStruct((n_b * t_pad, B_WIDTH), BF16),
                   jax.ShapeDtypeStruct((n_b, B_HEADS, HEAD_DIM, HEAD_DIM), F32)],
        grid=(n_b, nc),
        in_specs=[tok_spec(w, cb) for w, cb in zip(widths, cbs)] + [
            pl.BlockSpec((1, SUBLANE, 3 * B_WIDTH), lambda b, c: (b, 0, 0)),
            pl.BlockSpec((1, B_HEADS, HEAD_DIM, HEAD_DIM), lambda b, c: (b, 0, 0, 0)),
            pl.BlockSpec((CONV_W, 3 * B_WIDTH), lambda b, c: (0, 0)),
            pl.BlockSpec((1, LANE), lambda b, c: (0, 0)),
            pl.BlockSpec((1, LANE), lambda b, c: (0, 0)),
            pl.BlockSpec((1, HEAD_DIM), lambda b, c: (0, 0))],
        out_specs=[pl.BlockSpec((chunk, B_WIDTH), lambda b, c: (b * nc + c, 0)),
                   pl.BlockSpec((1, B_HEADS, HEAD_DIM, HEAD_DIM), lambda b, c: (b, 0, 0, 0))],
        scratch_shapes=[pltpu.VMEM((3, chunk + SUBLANE, B_WIDTH), F32),
                        pltpu.VMEM((B_HEADS, HEAD_DIM, HEAD_DIM), F32)],
        compiler_params=_cparams(("parallel", "arbitrary"), 40 << 20),
        name="branch_b_gated_delta",
    )(*arrs, conv0, s0, conv_w, pad12(a_log), pad12(dt_bias), o_g.reshape(1, HEAD_DIM))


def _sort_key(x):
    b = lax.bitcast_convert_type(x + 0.0, I32)
    return b ^ ((b >> 31) & 0x7FFFFFFF)


def _select_rule(count, shape, k, nbits):
    kf = jnp.float32(k)
    count_ge = lambda cand: count(lambda key, pos: key >= cand)
    t0 = jnp.where(count_ge(jnp.zeros(shape, I32)) >= kf, 0, INT_MIN).astype(I32)

    def bit_step(i, t):
        cand = t + lax.shift_left(jnp.int32(1), 30 - i)
        return jnp.where(count_ge(cand) >= kf, cand, t)

    t = lax.fori_loop(0, 31, bit_step, t0)
    n_ge = count_ge(t)
    tied = (n_ge > kf) & (t > KEY_OF_NEG_INF)

    def tie_index():
        r = kf - count(lambda key, pos: key > t)

        def idx_step(i, j):
            cand = j + lax.shift_left(jnp.int32(1), nbits - 1 - i)
            below = count(lambda key, pos: (key == t) & (pos <= cand - 1))
            return jnp.where(below < r, cand, j)

        return lax.fori_loop(0, nbits, idx_step, jnp.zeros(shape, I32))

    no_limit = jnp.full(shape, 2 ** 30, I32)
    j = lax.cond(jnp.sum(jnp.where(tied, 1.0, 0.0)) > 0.0,
                 lambda: jnp.where(tied, tie_index(), no_limit), lambda: no_limit)
    return t, j


def _selected(key, pos, t, j):
    return (key > t) | ((key == t) & (pos <= j))


def _dsa_prompt_body(cq_ref, iq0_ref, iq1_ref, iq2_ref, iq3_ref, sm_ref, ik_ref, ck_ref, cv_ref, y_ref,
                     key_ref, iqb_ref, wb_ref, qb_ref, m_ref, l_ref, acc_ref, p_ref, al_ref,
                     *, k_sel, nbits, tq, tk):
    rows3 = C_GROUP * tq
    qi = pl.program_id(1)
    kpg = tk // LANE
    nkb = (qi + 1) * (tq // LANE)
    ngrp = ((qi + 1) * tq + tk - 1) // tk
    hd = HEAD_DIM
    nt = (((1,), (1,)), ((), ()))
    sub = LANE // SUBLANE
    qpos3 = qi * tq + lax.broadcasted_iota(I32, (sub, SUBLANE, tq), 2)
    kofs3 = lax.broadcasted_iota(I32, (sub, SUBLANE, tq), 0) * SUBLANE + lax.broadcasted_iota(I32, (sub, SUBLANE, tq), 1)

    w_all = sm_ref[...] * (IDX_HEADS ** -0.5 * IDX_DIM ** -0.5)
    w_t = jnp.concatenate([w_all[r:r + LANE].T for r in range(0, tq, LANE)], axis=1)
    w_off = 2 * B_HEADS
    for hh in range(IDX_HEADS):
        ref = (iq0_ref, iq1_ref, iq2_ref, iq3_ref)[hh // 4]
        iqb_ref[hh] = ref[:, (hh % 4) * IDX_DIM:(hh % 4 + 1) * IDX_DIM].astype(BF16)
        wb_ref[hh] = jnp.broadcast_to(w_t[w_off + hh:w_off + hh + 1, :], (SUBLANE, tq))

    def score_grp(g, carry):
        ikg = ik_ref[pl.ds(pl.multiple_of(g * tk, tk), tk), :].astype(BF16)
        acc = jnp.zeros((tk // SUBLANE, SUBLANE, tq), F32)
        for hh in range(IDX_HEADS):
            s = lax.dot_general(ikg, iqb_ref[hh], nt, preferred_element_type=F32)
            acc = acc + wb_ref[hh] * jnp.maximum(s, 0.0).reshape(tk // SUBLANE, SUBLANE, tq)
        for t in range(kpg):
            kpos3 = (g * kpg + t) * LANE + kofs3
            blk = acc[t * sub:(t + 1) * sub]
            key_ref[g * kpg + t] = _sort_key(jnp.where(kpos3 <= qpos3, blk, -jnp.inf))
        return carry

    lax.fori_loop(0, ngrp, score_grp, 0)

    bpt = tq // LANE

    def count(pred):
        def trip(i, acc):
            for u in range(bpt):
                b = i * bpt + u
                acc = acc + jnp.sum(jnp.where(pred(key_ref[b], b * LANE + kofs3), 1.0, 0.0), axis=0)
            return acc
        acc = lax.fori_loop(0, qi + 1, trip, jnp.zeros((SUBLANE, tq), F32))
        return jnp.broadcast_to(jnp.sum(acc, axis=0, keepdims=True), (SUBLANE, tq))

    t8, j8 = _select_rule(count, (SUBLANE, tq), k_sel, nbits)

    for kvh in range(C_KV_HEADS):
        for g in range(C_GROUP):
            hsl = slice((kvh * C_GROUP + g) * hd, (kvh * C_GROUP + g + 1) * hd)
            qb_ref[kvh, g * tq:(g + 1) * tq, :] = cq_ref[:, hsl].astype(BF16)
    m_ref[...] = jnp.full(m_ref.shape, MASK_NEG, F32)
    l_ref[...] = jnp.zeros(l_ref.shape, F32)
    acc_ref[...] = jnp.zeros(acc_ref.shape, F32)
    ones = jnp.ones((tk, hd), BF16)

    def attend_grp(g, carry):
        bias_t = []
        for t in range(kpg):
            kpos3 = (g * kpg + t) * LANE + kofs3
            sel = _selected(key_ref[g * kpg + t], kpos3, t8, j8) & (kpos3 <= qpos3)
            kq = jnp.where(sel, 0.0, MASK_NEG).reshape(LANE, tq)
            bias_t.append(jnp.concatenate([kq[:, r:r + LANE].T for r in range(0, tq, LANE)], axis=0))
        bias = jnp.concatenate(bias_t, axis=1)
        start = pl.multiple_of(g * tk, tk)
        kgrp = ck_ref[pl.ds(start, tk), :].astype(BF16)
        vgrp = cv_ref[pl.ds(start, tk), :].astype(BF16)
        for kvh in range(C_KV_HEADS):
            sl = slice(kvh * hd, (kvh + 1) * hd)
            s = lax.dot_general(qb_ref[kvh], kgrp[:, sl], nt, preferred_element_type=F32)
            for r0 in range(0, rows3, DSA_ROW_TILE):
                rs = slice(r0, r0 + DSA_ROW_TILE)
                x = s[rs] * (hd ** -0.5 * LOG2E) + bias[r0 % tq:r0 % tq + DSA_ROW_TILE]
                m_old = m_ref[kvh, rs, :]
                m_new = jnp.maximum(m_old, jnp.max(x, axis=-1, keepdims=True))
                p_ref[rs, :] = jnp.exp2(x - m_new).astype(BF16)
                al_ref[rs, :] = jnp.exp2(m_old - m_new)
                m_ref[kvh, rs, :] = m_new
            pv = jnp.dot(p_ref[...], jnp.concatenate([vgrp[:, sl], ones], axis=1), preferred_element_type=F32)
            alpha = al_ref[...]
            l_ref[kvh] = alpha * l_ref[kvh] + pv[:, hd:hd + 1]
            acc_ref[kvh] = alpha * acc_ref[kvh] + pv[:, :hd]
        return carry

    lax.fori_loop(0, ngrp, attend_grp, 0)

    for kvh in range(C_KV_HEADS):
        o = acc_ref[kvh] / l_ref[kvh]
        for g in range(C_GROUP):
            hsl = slice((kvh * C_GROUP + g) * hd, (kvh * C_GROUP + g + 1) * hd)
            y_ref[:, hsl] = o[g * tq:(g + 1) * tq, :].astype(y_ref.dtype)


def _dsa_prompt(h, lay, *, n_b, seq):
    tq, tk = DSA_TQ, DSA_TK
    assert seq % tq == 0 and seq % tk == 0
    nqb = seq // tq
    k_sel = min(TOPK_MAX, seq // 4)
    nbits = max(1, (seq - 1).bit_length())
    iq_cb = lay["iq"] // (4 * IDX_DIM)
    once = pl.Buffered(1)

    def q_spec(w, cb):
        return pl.BlockSpec((tq, w), lambda b, q: (b * nqb + q, cb))

    def kv_spec(w, cb):
        return pl.BlockSpec((seq, w), lambda b, q: (b, cb), pipeline_mode=once)

    rows3 = C_GROUP * tq
    lane_pad = lambda r: r * LANE * 4
    vmem = (seq * (2 * C_KV + IDX_DIM) * 4 + 2 * tq * (C_Q + IDX_HEADS * IDX_DIM + LANE) * 4 + 2 * tq * C_Q * 2
            + (seq // LANE) * tq * LANE * 4 + IDX_HEADS * tq * LANE * 6 + C_KV_HEADS * rows3 * HEAD_DIM * 6
            + 2 * C_KV_HEADS * lane_pad(rows3) + 4 * rows3 * tk * 4)
    return pl.pallas_call(
        functools.partial(_dsa_prompt_body, k_sel=k_sel, nbits=nbits, tq=tq, tk=tk),
        out_shape=jax.ShapeDtypeStruct((n_b * seq, C_Q), BF16),
        grid=(n_b, nqb),
        in_specs=[q_spec(C_Q, lay["cq"] // C_Q)]
                 + [q_spec(4 * IDX_DIM, iq_cb + i) for i in range(4)]
                 + [q_spec(LANE, lay["small"] // LANE),
                    kv_spec(IDX_DIM, lay["ik"] // IDX_DIM),
                    kv_spec(C_KV, lay["ck"] // C_KV),
                    kv_spec(C_KV, lay["cv"] // C_KV)],
        out_specs=pl.BlockSpec((tq, C_Q), lambda b, q: (b * nqb + q, 0)),
        scratch_shapes=[pltpu.VMEM((seq // LANE, LANE // SUBLANE, SUBLANE, tq), I32),
                        pltpu.VMEM((IDX_HEADS, tq, IDX_DIM), BF16),
                        pltpu.VMEM((IDX_HEADS, SUBLANE, tq), F32),
                        pltpu.VMEM((C_KV_HEADS, rows3, HEAD_DIM), BF16),
                        pltpu.VMEM((C_KV_HEADS, rows3, 1), F32),
                        pltpu.VMEM((C_KV_HEADS, rows3, 1), F32),
                        pltpu.VMEM((C_KV_HEADS, rows3, HEAD_DIM), F32),
                        pltpu.VMEM((rows3, tk), BF16),
                        pltpu.VMEM((rows3, 1), F32)],
        compiler_params=_cparams(("parallel", "arbitrary"), vmem + COMPILER_SCRATCH_BYTES),
        name="branch_c_prompt_dsa",
    )(h, h, h, h, h, h, h, h, h)


def _dsa_sample_select_body(pt_ref, *refs, n_pages, k_sel, nbits, t_new):
    pp = PAGES_PER_STEP
    page_refs = refs[:pp]
    iq_ref, w_ref, ikn_ref, bias_ref, key_ref = refs[pp:]
    j = pl.program_id(1)
    n_steps = n_pages // pp
    rows = t_new
    col = lax.broadcasted_iota(I32, (rows, LANE), 1)
    trow = lax.broadcasted_iota(I32, (rows, LANE), 0)
    iq = iq_ref[0].astype(BF16)

    def scores(keys_f32):
        n = keys_f32.shape[0]
        s = lax.dot_general(iq, keys_f32.astype(BF16), (((1,), (1,)), ((), ())), preferred_element_type=F32)
        r = jnp.maximum(s, 0.0) * w_ref[0]
        return jnp.sum(r.reshape(rows, IDX_HEADS, n), axis=1)

    @pl.when(j < n_steps)
    def _():
        keys = _sort_key(scores(jnp.concatenate([r[0, 0] for r in page_refs], axis=0)))
        for i in range(pp):
            key_ref[j * pp + i] = keys[:, i * LANE:(i + 1) * LANE]

    @pl.when(j == n_steps)
    def _():
        new_ok = (col <= trow) & (col < t_new)
        key_ref[n_pages] = _sort_key(jnp.where(new_ok, scores(ikn_ref[0]), -jnp.inf))
        nblk = n_pages + 1
        grp = SUBLANE
        nblk_pad = key_ref.shape[0]
        if nblk_pad > nblk:
            key_ref[nblk:nblk_pad] = jnp.full((nblk_pad - nblk, rows, LANE), KEY_OF_NEG_INF, I32)
        blk3 = lax.broadcasted_iota(I32, (grp, rows, LANE), 0)
        col3 = lax.broadcasted_iota(I32, (grp, rows, LANE), 2)

        def count(pred):
            def trip(i, acc):
                b0 = pl.multiple_of(i * grp, grp)
                hit = pred(key_ref[pl.ds(b0, grp)], (b0 + blk3) * LANE + col3)
                return acc + jnp.sum(jnp.where(hit, 1.0, 0.0), axis=0)
            acc = lax.fori_loop(0, nblk_pad // grp, trip, jnp.zeros((rows, LANE), F32))
            return jnp.broadcast_to(jnp.sum(acc, axis=1, keepdims=True), (rows, LANE))

        tb, jb = _select_rule(count, (rows, LANE), k_sel, nbits)

        def write_blk(b, carry):
            kpos = b * LANE + col
            sel = _selected(key_ref[b], kpos, tb, jb) & ((kpos < n_pages * PAGE_SIZE) | new_ok)
            bias_ref[0, b] = jnp.where(sel, 0.0, MASK_NEG)
            return carry

        lax.fori_loop(0, nblk, write_blk, 0)


def _page_index_map(layer, i, n_pages, n_steps, trailing):
    def index_map(b, j, pt):
        step = jnp.minimum(j, n_steps - 1)
        return (layer, pt[b * n_pages + step * PAGES_PER_STEP + i]) + (0,) * trailing
    return index_map


def _dsa_sample_select(pt_flat, kidx_pool, layer, iq_s, iw_s, ik_new, *, n_b, n_pages, t_new):
    pp = PAGES_PER_STEP
    assert n_pages % pp == 0
    n_steps = n_pages // pp
    total = n_pages * PAGE_SIZE + t_new
    k_sel = min(TOPK_MAX, total // 4)
    nbits = max(1, ((n_pages + 1) * PAGE_SIZE - 1).bit_length())
    rows_q = t_new * IDX_HEADS
    grid_spec = pltpu.PrefetchScalarGridSpec(
        num_scalar_prefetch=1,
        grid=(n_b, n_steps + 1),
        in_specs=[pl.BlockSpec((1, 1, PAGE_SIZE, IDX_DIM), _page_index_map(layer, i, n_pages, n_steps, 2))
                  for i in range(pp)]
                 + [pl.BlockSpec((1, rows_q, IDX_DIM), lambda b, j, pt: (b, 0, 0)),
                    pl.BlockSpec((1, rows_q, 1), lambda b, j, pt: (b, 0, 0)),
                    pl.BlockSpec((1, PAGE_SIZE, IDX_DIM), lambda b, j, pt: (b, 0, 0))],
        out_specs=pl.BlockSpec((1, n_pages + 1, t_new, LANE), lambda b, j, pt: (b, 0, 0, 0)),
        scratch_shapes=[pltpu.VMEM((-(-(n_pages + 1) // SUBLANE) * SUBLANE, t_new, LANE), I32)])
    return pl.pallas_call(
        functools.partial(_dsa_sample_select_body, n_pages=n_pages, k_sel=k_sel, nbits=nbits, t_new=t_new),
        out_shape=jax.ShapeDtypeStruct((n_b, n_pages + 1, t_new, LANE), F32),
        grid_spec=grid_spec,
        compiler_params=_cparams(("parallel", "arbitrary"), 24 << 20),
        name="branch_c_decode_select",
    )(pt_flat, *([kidx_pool] * pp), iq_s, iw_s, ik_new)


def _dsa_sample_attend_body(pt_ref, *refs, n_pages, t_new):
    pp = PAGES_PER_STEP
    k_refs, v_refs = refs[:pp], refs[pp:2 * pp]
    q_ref, bias_ref, kn_ref, vn_ref, o_ref, m_ref, l_ref, acc_ref = refs[2 * pp:]
    j = pl.program_id(1)
    n_steps = n_pages // pp
    hd = HEAD_DIM
    rq = q_ref.shape[1] // C_KV_HEADS
    reps = rq // t_new

    @pl.when(j == 0)
    def _():
        m_ref[...] = jnp.full(m_ref.shape, MASK_NEG, F32)
        l_ref[...] = jnp.zeros(l_ref.shape, F32)
        acc_ref[...] = jnp.zeros(acc_ref.shape, F32)

    def attend(k_of, v_of, bias_t):
        bias = jnp.concatenate([bias_t] * reps, axis=0)
        for kvh in range(C_KV_HEADS):
            rs = slice(kvh * rq, (kvh + 1) * rq)
            s = lax.dot_general(q_ref[0, rs, :].astype(BF16), k_of(kvh).astype(BF16), (((1,), (1,)), ((), ())),
                                preferred_element_type=F32) * (hd ** -0.5) + bias
            m_old = m_ref[rs, :]
            m_new = jnp.maximum(m_old, jnp.max(s, axis=-1, keepdims=True))
            alpha = jnp.exp(m_old - m_new)
            p = jnp.exp(s - m_new)
            l_ref[rs, :] = alpha * l_ref[rs, :] + jnp.sum(p, axis=-1, keepdims=True)
            acc_ref[rs, :] = alpha * acc_ref[rs, :] + jnp.dot(p.astype(BF16), v_of(kvh).astype(BF16),
                                                              preferred_element_type=F32)
            m_ref[rs, :] = m_new

    @pl.when(j < n_steps)
    def _():
        head_rows = lambda kvh: pl.ds(kvh, PAGE_SIZE, stride=C_KV_HEADS)
        pages = lambda prefs: (lambda kvh: jnp.concatenate([r[0, 0, head_rows(kvh), :] for r in prefs], axis=0))
        bias_t = jnp.concatenate([bias_ref[0, j * pp + i] for i in range(pp)], axis=1)
        attend(pages(k_refs), pages(v_refs), bias_t)

    @pl.when(j == n_steps)
    def _():
        new = lambda ref: (lambda kvh: ref[0, :, kvh * hd:(kvh + 1) * hd])
        attend(new(kn_ref), new(vn_ref), bias_ref[0, n_pages])
        o_ref[0] = acc_ref[...] / l_ref[...]


def _dsa_sample_attend(pt_flat, k_pool, v_pool, layer, q_s, bias, k_new, v_new, *, n_b, n_pages, t_new):
    pp = PAGES_PER_STEP
    n_steps = n_pages // pp
    rows = q_s.shape[1]
    page_specs = [pl.BlockSpec((1, 1, PAGE_SIZE * C_KV_HEADS, HEAD_DIM), _page_index_map(layer, i, n_pages, n_steps, 2))
                  for i in range(pp)]
    grid_spec = pltpu.PrefetchScalarGridSpec(
        num_scalar_prefetch=1,
        grid=(n_b, n_steps + 1),
        in_specs=page_specs + page_specs
                 + [pl.BlockSpec((1, rows, HEAD_DIM), lambda b, j, pt: (b, 0, 0)),
                    pl.BlockSpec((1, n_pages + 1, t_new, LANE), lambda b, j, pt: (b, 0, 0, 0)),
                    pl.BlockSpec((1, PAGE_SIZE, C_KV), lambda b, j, pt: (b, 0, 0)),
                    pl.BlockSpec((1, PAGE_SIZE, C_KV), lambda b, j, pt: (b, 0, 0))],
        out_specs=pl.BlockSpec((1, rows, HEAD_DIM), lambda b, j, pt: (b, 0, 0)),
        scratch_shapes=[pltpu.VMEM((rows, 1), F32), pltpu.VMEM((rows, 1), F32), pltpu.VMEM((rows, HEAD_DIM), F32)])
    return pl.pallas_call(
        functools.partial(_dsa_sample_attend_body, n_pages=n_pages, t_new=t_new),
        out_shape=jax.ShapeDtypeStruct((n_b, rows, HEAD_DIM), F32),
        grid_spec=grid_spec,
        compiler_params=_cparams(("parallel", "arbitrary"), 32 << 20),
        name="branch_c_decode_attend",
    )(pt_flat, *([k_pool] * pp), *([v_pool] * pp), q_s, bias, k_new, v_new)


def _dsa_sample(h_s, lay, pt_flat, k_pool, v_pool, kidx_pool, layer, *, n_b, t_new, n_pages):
    hd = HEAD_DIM
    seg = lambda name, w: h_s[:, lay[name]:lay[name] + w]
    rq = -(-C_GROUP * t_new // 16) * 16
    q = seg("cq", C_Q).reshape(n_b, t_new, C_KV_HEADS, C_GROUP, hd).transpose(0, 2, 3, 1, 4)
    q = q.reshape(n_b, C_KV_HEADS, C_GROUP * t_new, hd)
    q_s = jnp.pad(q, ((0, 0), (0, 0), (0, rq - C_GROUP * t_new), (0, 0))).reshape(n_b, C_KV_HEADS * rq, hd)
    iq_s = seg("iq", IDX_HEADS * IDX_DIM).reshape(n_b, t_new * IDX_HEADS, IDX_DIM)
    iw = h_s[:, lay["small"] + 2 * B_HEADS:lay["small"] + 2 * B_HEADS + IDX_HEADS]
    iw_s = (iw * (IDX_HEADS ** -0.5 * IDX_DIM ** -0.5)).reshape(n_b, t_new * IDX_HEADS, 1)
    padrows = lambda x: jnp.pad(x.reshape(n_b, t_new, -1), ((0, 0), (0, PAGE_SIZE - t_new), (0, 0)))
    ik_new, k_new, v_new = padrows(seg("ik", IDX_DIM)), padrows(seg("ck", C_KV)), padrows(seg("cv", C_KV))
    bias = _dsa_sample_select(pt_flat, kidx_pool, layer, iq_s, iw_s, ik_new, n_b=n_b, n_pages=n_pages, t_new=t_new)
    o = _dsa_sample_attend(pt_flat, k_pool, v_pool, layer, q_s, bias, k_new, v_new,
                           n_b=n_b, n_pages=n_pages, t_new=t_new)
    o = o.reshape(n_b, C_KV_HEADS, rq, hd)[:, :, :C_GROUP * t_new].reshape(n_b, C_KV_HEADS, C_GROUP, t_new, hd)
    return o.transpose(0, 3, 1, 2, 4).reshape(n_b * t_new, C_Q).astype(BF16)


def _merge_body(ya_ref, yb_ref, yc_ref, w_ref, g0_ref, g1_ref, g2_ref, o_ref):
    b0, c0 = A_WIDTH, A_WIDTH + B_WIDTH
    acc = jax.nn.sigmoid(g0_ref[...]) * jnp.dot(ya_ref[...], w_ref[0:b0, :], preferred_element_type=F32)
    acc = acc + jax.nn.sigmoid(g1_ref[...]) * jnp.dot(yb_ref[...], w_ref[b0:c0, :], preferred_element_type=F32)
    acc = acc + jax.nn.sigmoid(g2_ref[...]) * jnp.dot(yc_ref[...], w_ref[c0:, :], preferred_element_type=F32)
    o_ref[...] = acc.astype(o_ref.dtype)


def _merge(ya, yb, yc, w_br, layer, h, lay, d_model):
    m = ya.shape[0]
    tm = _pick(m, 1024)
    tn = _pick(d_model, 512)
    gcb = lay["gate"] // tn
    per = d_model // tn

    def gate_spec(i):
        return pl.BlockSpec((tm, tn), lambda r, c: (r, gcb + i * per + c))

    kw = A_WIDTH + B_WIDTH + C_Q
    vmem = 2 * (tm * kw * 2 + kw * tn * 2 + 3 * tm * tn * 4 + tm * tn * 2) + 4 * tm * tn * 4
    return pl.pallas_call(
        _merge_body,
        out_shape=jax.ShapeDtypeStruct((m, d_model), BF16),
        grid=(m // tm, d_model // tn),
        in_specs=[pl.BlockSpec((tm, A_WIDTH), lambda r, c: (r, 0)),
                  pl.BlockSpec((tm, B_WIDTH), lambda r, c: (r, 0)),
                  pl.BlockSpec((tm, C_Q), lambda r, c: (r, 0)),
                  _layer_spec((kw, tn), lambda r, c: (0, c), layer),
                  gate_spec(0), gate_spec(1), gate_spec(2)],
        out_specs=pl.BlockSpec((tm, tn), lambda r, c: (r, c)),
        compiler_params=_cparams(("parallel", "arbitrary"), vmem + COMPILER_SCRATCH_BYTES),
        name="gated_merge",
    )(ya, yb, yc, w_br, h, h, h)


def _ffn_up_body(hn_ref, w1_ref, w3_ref, o_ref):
    hn = hn_ref[...]
    a = jnp.dot(hn, w1_ref[...], preferred_element_type=F32)
    b = jnp.dot(hn, w3_ref[...], preferred_element_type=F32)
    o_ref[...] = (jax.nn.silu(a) * b).astype(o_ref.dtype)


def _ffn(hn, w1, w3, w2, layer, x):
    m, d = hn.shape
    d_ff = w1.shape[-1]
    tm = _pick(m, 1024)
    tf = FFN_TF
    assert d_ff % tf == 0
    vmem = 2 * (tm * d * 2 + 2 * d * tf * 2 + tm * tf * 2) + 3 * tm * tf * 4
    act = pl.pallas_call(
        _ffn_up_body,
        out_shape=jax.ShapeDtypeStruct((m, d_ff), BF16),
        grid=(m // tm, d_ff // tf),
        in_specs=[pl.BlockSpec((tm, d), lambda i, f: (i, 0)),
                  _layer_spec((d, tf), lambda i, f: (0, f), layer),
                  _layer_spec((d, tf), lambda i, f: (0, f), layer)],
        out_specs=pl.BlockSpec((tm, tf), lambda i, f: (i, f)),
        compiler_params=_cparams(("parallel", "arbitrary"), vmem + COMPILER_SCRATCH_BYTES),
        name="swiglu_up",
    )(hn, w1, w3)
    return _matmul(act, w2, layer=layer, residual=x, tm_pref=1024, tn_pref=256, a_single_buffer=True,
                   name="swiglu_down")


def _dense_front(x, ln1, w_in_p, layer):
    xn = _rmsnorm(x, ln1, BF16)
    return _matmul(xn, w_in_p, layer=layer, w_transposed=True, tn_pref=768, name="in_proj")


def _dense_back(x, ya, yb, yc, h, lay, wts, layer, d_model):
    w_br, w_o, w1, w3, w2, ln2 = wts
    mix = _merge(ya, yb, yc, w_br, layer, h, lay, d_model)
    x = _matmul(mix, w_o, layer=layer, residual=x, name="out_proj")
    hn = _rmsnorm(x, ln2, BF16)
    return _ffn(hn, w1, w3, w2, layer, x)


def kernel(x_prompt, x_sample, cache_k, cache_v, cache_kidx, state_conv, state_delta, page_table, ln1, w_in,
           a_ln_g, a_ln_b, a_ws, a_bs, b_conv_w, b_a_log, b_dt_bias, b_out_g, w_br, w_o, ln2, ffn_w1, ffn_w3,
           ffn_w2, ln_f):
    n_bp, seq, d_model = x_prompt.shape
    n_bs, t_new, _ = x_sample.shape
    depth = ln1.shape[0]
    n_phys = cache_k.shape[1]
    n_pages = page_table.shape[1]
    assert seq % A_CHUNK == 0 and seq % GDN_CHUNK == 0 and CONV_W - 1 <= t_new <= min(A_CHUNK, GDN_DECODE_CHUNK)
    lay = _layout(d_model)
    hd = HEAD_DIM

    xp = x_prompt.reshape(n_bp * seq, d_model)
    xs = x_sample.reshape(n_bs * t_new, d_model)
    pt_flat = page_table.reshape(-1).astype(I32)
    k_pool = cache_k.reshape(depth, n_phys, PAGE_SIZE * C_KV_HEADS, hd)
    v_pool = cache_v.reshape(depth, n_phys, PAGE_SIZE * C_KV_HEADS, hd)
    conv0_p = jnp.zeros((n_bp, SUBLANE, 3 * B_WIDTH), F32)
    s0_p = jnp.zeros((n_bp, B_HEADS, hd, hd), F32)
    qkv_off = lay["bq"]

    outs = {k: [] for k in ("pk", "pv", "pik", "pconv", "pdelta", "sk", "sv", "sik", "sconv", "sdelta", "schunk")}
    dense_w = tuple(w.astype(BF16) for w in (w_br, w_o, ffn_w1, ffn_w3, ffn_w2))
    w_in_p = _pack_w_in(w_in, d_model)
    for l in range(depth):
        wts = dense_w + (ln2[l],)

        h = _dense_front(xp, ln1[l], w_in_p, l)
        ya = _branch_a(h, lay, a_ws[l], a_bs[l], a_ln_g[l], a_ln_b[l], chunk=A_CHUNK, tb=A_CHUNK, emit_av=False)
        srcs = [(h, lay[n] // B_WIDTH) for n in ("bq", "bk", "bv", "bz")] + [(h, lay["small"] // LANE)]
        yb, s_new = _gdn(srcs, conv0_p, s0_p, b_conv_w[l], b_a_log[l], b_dt_bias[l], b_out_g[l],
                         n_b=n_bp, t_pad=seq, t_valid=GDN_CHUNK, chunk=GDN_CHUNK)
        yc = _dsa_prompt(h, lay, n_b=n_bp, seq=seq)
        h3 = h.reshape(n_bp, seq, lay["total"])
        outs["pk"].append(h3[:, :, lay["ck"]:lay["ck"] + C_KV].reshape(n_bp, seq, C_KV_HEADS, hd))
        outs["pv"].append(h3[:, :, lay["cv"]:lay["cv"] + C_KV].reshape(n_bp, seq, C_KV_HEADS, hd))
        outs["pik"].append(h3[:, :, lay["ik"]:lay["ik"] + IDX_DIM])
        outs["pconv"].append(h3[:, seq - (CONV_W - 1):, qkv_off:qkv_off + 3 * B_WIDTH])
        outs["pdelta"].append(s_new)
        xp = _dense_back(xp, ya, yb, yc, h, lay, wts, l, d_model)

        hs = _dense_front(xs, ln1[l], w_in_p, l)
        ya, av = _branch_a(hs, lay, a_ws[l], a_bs[l], a_ln_g[l], a_ln_b[l], chunk=t_new, tb=n_bs * t_new,
                           emit_av=True)
        hs3 = hs.reshape(n_bs, t_new, lay["total"])
        dc = GDN_DECODE_CHUNK
        padt = lambda x: jnp.pad(x, ((0, 0), (0, dc - t_new), (0, 0))).reshape(n_bs * dc, -1)
        srcs = [(padt(hs3[:, :, lay[n]:lay[n] + B_WIDTH]), 0) for n in ("bq", "bk", "bv", "bz")]
        srcs.append((padt(hs3[:, :, lay["small"]:lay["small"] + LANE]), 0))
        conv0_s = jnp.pad(state_conv[l], ((0, 0), (SUBLANE - (CONV_W - 1), 0), (0, 0)))
        yb, s_new = _gdn(srcs, conv0_s, state_delta[l], b_conv_w[l], b_a_log[l], b_dt_bias[l], b_out_g[l],
                         n_b=n_bs, t_pad=dc, t_valid=t_new, chunk=dc)
        yb = yb.reshape(n_bs, dc, B_WIDTH)[:, :t_new].reshape(n_bs * t_new, B_WIDTH)
        yc = _dsa_sample(hs, lay, pt_flat, k_pool, v_pool, cache_kidx, l, n_b=n_bs, t_new=t_new, n_pages=n_pages)
        outs["sk"].append(hs3[:, :, lay["ck"]:lay["ck"] + C_KV].reshape(n_bs, t_new, C_KV_HEADS, hd))
        outs["sv"].append(hs3[:, :, lay["cv"]:lay["cv"] + C_KV].reshape(n_bs, t_new, C_KV_HEADS, hd))
        outs["sik"].append(hs3[:, :, lay["ik"]:lay["ik"] + IDX_DIM])
        outs["sconv"].append(hs3[:, t_new - (CONV_W - 1):, qkv_off:qkv_off + 3 * B_WIDTH])
        outs["sdelta"].append(s_new)
        outs["schunk"].append(av.reshape(n_bs, t_new, A_WIDTH))
        xs = _dense_back(xs, ya, yb, yc, hs, lay, wts, l, d_model)

    y_prompt = _rmsnorm(xp, ln_f, F32).reshape(n_bp, seq, d_model)
    y_sample = _rmsnorm(xs, ln_f, F32).reshape(n_bs, t_new, d_model)
    st = lambda k: jnp.stack(outs[k])
    return (y_prompt, y_sample, st("pk"), st("pv"), st("pik"), st("pconv"), st("pdelta"),
            st("sk"), st("sv"), st("sik"), st("sconv"), st("sdelta"), st("schunk"))
```

```python
import functools

import jax
import jax.numpy as jnp
from jax import lax
from jax.experimental import pallas as pl
from jax.experimental.pallas import tpu as pltpu

F32 = jnp.float32
BF16 = jnp.bfloat16
I32 = jnp.int32

HEAD_DIM = 128
A_GROUPS = 8
A_CHUNK = 128
A_WIDTH = A_GROUPS * HEAD_DIM
B_HEADS = 12
B_WIDTH = B_HEADS * HEAD_DIM
CONV_W = 4
C_HEADS = 12
C_KV_HEADS = 4
C_GROUP = C_HEADS // C_KV_HEADS
C_Q = C_HEADS * HEAD_DIM
C_KV = C_KV_HEADS * HEAD_DIM
IDX_HEADS = 16
IDX_DIM = 128
TOPK_MAX = 256
PAGE_SIZE = 128
N_BRANCH = 3
RMS_EPS = 1e-6
LN_EPS = 1e-5

LANE = 128
SUBLANE = 8
V7X_VMEM_BYTES = 64 * 1024 * 1024
VMEM_BUDGET = 56 * 1024 * 1024
COMPILER_SCRATCH_BYTES = 8 * 1024 * 1024

GDN_CHUNK = 128
GDN_DECODE_CHUNK = 16
DSA_TQ = 256
DSA_TK = 512
DSA_ROW_TILE = 32
FFN_TF = 256
PAGES_PER_STEP = 16
MASK_NEG = -1e30
LOG2E = 1.4426950408889634
INT_MIN = -2147483648
KEY_OF_NEG_INF = -2139095041


def _cparams(semantics, vmem_bytes):
    return pltpu.CompilerParams(dimension_semantics=semantics,
                                vmem_limit_bytes=int(min(max(vmem_bytes, 16 * 1024 * 1024), VMEM_BUDGET)))


def _pick(dim, pref):
    t = pref
    while t >= SUBLANE:
        if dim % t == 0:
            return t
        t //= 2
    return dim


def _layout(d_model):
    segs = [("gate", N_BRANCH * d_model, d_model), ("bq", B_WIDTH, B_WIDTH), ("bk", B_WIDTH, B_WIDTH),
            ("bv", B_WIDTH, B_WIDTH), ("bz", B_WIDTH, B_WIDTH), ("cq", C_Q, C_Q), ("ck", C_KV, C_KV),
            ("au", A_WIDTH, A_WIDTH), ("av", A_WIDTH, A_WIDTH), ("cv", C_KV, C_KV),
            ("iq", IDX_HEADS * IDX_DIM, 4 * IDX_DIM), ("ik", IDX_DIM, IDX_DIM), ("small", LANE, LANE)]
    off, lay = 0, {}
    for name, width, align in segs:
        assert off % align == 0, (name, off, align)
        lay[name] = off
        off += width
    lay["total"] = off
    return lay


def _pack_body(tab_ref, first_ref, second_ref, ab_ref, iw_ref, o_ref):
    j = pl.program_id(1)
    last = pl.num_programs(1) - 1
    o_ref[0:LANE, :] = first_ref[0].astype(o_ref.dtype)

    @pl.when(j != last)
    def _():
        o_ref[LANE:, :] = second_ref[0].astype(o_ref.dtype)

    @pl.when(j == last)
    def _():
        n_ab, n_iw = 2 * B_HEADS, IDX_HEADS
        rest = jnp.zeros((LANE - n_ab - n_iw, o_ref.shape[1]), F32)
        o_ref[LANE:, :] = jnp.concatenate([ab_ref[0, 0:n_ab, :], iw_ref[0, 0:n_iw, :], rest],
                                          axis=0).astype(o_ref.dtype)


def _pack_w_in(w_in, d_model):
    depth, d, in_width = w_in.shape
    lay = _layout(d_model)
    w_t = jnp.swapaxes(w_in, 1, 2)
    widths = (A_WIDTH, A_WIDTH, 3 * B_WIDTH, B_WIDTH, 2 * B_HEADS, C_Q, C_KV, C_KV,
              IDX_HEADS * IDX_DIM, IDX_HEADS, IDX_DIM, N_BRANCH * d_model)
    dsts = (lay["au"], lay["av"], lay["bq"], lay["bz"], None, lay["cq"], lay["ck"], lay["cv"], lay["iq"], None,
            lay["ik"], lay["gate"])
    n_blocks = lay["total"] // LANE
    table, start, special = [0] * n_blocks, 0, []
    for width, dst in zip(widths, dsts):
        if dst is None:
            special.append(start)
        else:
            assert width % LANE == 0 and dst % LANE == 0 and start % SUBLANE == 0
            for blk in range(width // LANE):
                table[dst // LANE + blk] = (start + blk * LANE) // SUBLANE
        start += width
    assert start == in_width and lay["small"] // LANE == n_blocks - 1 and n_blocks % 2 == 0
    ab_row, iw_row = special
    assert ab_row % SUBLANE == 0 and iw_row % SUBLANE == 0 and max(ab_row, iw_row) + LANE <= in_width
    window = lambda index_map: pl.BlockSpec((pl.Element(1), pl.Element(LANE), pl.Element(d)), index_map)
    grid_spec = pltpu.PrefetchScalarGridSpec(
        num_scalar_prefetch=1,
        grid=(depth, n_blocks // 2),
        in_specs=[window(lambda l, j, tab: (l, tab[2 * j] * SUBLANE, 0)),
                  window(lambda l, j, tab: (l, tab[2 * j + 1] * SUBLANE, 0)),
                  window(lambda l, j, tab: (l, ab_row, 0)),
                  window(lambda l, j, tab: (l, iw_row, 0))],
        out_specs=pl.BlockSpec((None, 2 * LANE, d), lambda l, j, tab: (l, j, 0)))
    return pl.pallas_call(
        _pack_body,
        out_shape=jax.ShapeDtypeStruct((depth, lay["total"], d), BF16),
        grid_spec=grid_spec,
        compiler_params=_cparams(("parallel", "arbitrary"), 14 * LANE * d * 4),
        name="pack_in_proj_weight",
    )(jnp.asarray(table, I32), w_t, w_t, w_t, w_t)


def _rmsnorm_body(x_ref, g_ref, o_ref):
    x = x_ref[...]
    ms = jnp.mean(x * x, axis=-1, keepdims=True)
    o_ref[...] = (x * lax.rsqrt(ms + RMS_EPS) * g_ref[...]).astype(o_ref.dtype)


def _rmsnorm(x, g, out_dtype):
    m, d = x.shape
    tm = _pick(m, 256)
    return pl.pallas_call(
        _rmsnorm_body,
        out_shape=jax.ShapeDtypeStruct((m, d), out_dtype),
        grid=(m // tm,),
        in_specs=[pl.BlockSpec((tm, d), lambda i: (i, 0)), pl.BlockSpec((1, d), lambda i: (0, 0))],
        out_specs=pl.BlockSpec((tm, d), lambda i: (i, 0)),
        compiler_params=_cparams(("parallel",), 6 * tm * d * 4),
        name="rmsnorm",
    )(x, g.reshape(1, d))


def _mm_body(a_ref, w_ref, o_ref):
    o_ref[...] = jnp.dot(a_ref[...], w_ref[...], preferred_element_type=F32).astype(o_ref.dtype)


def _mm_res_body(a_ref, w_ref, r_ref, o_ref):
    o_ref[...] = (r_ref[...] + jnp.dot(a_ref[...], w_ref[...], preferred_element_type=F32)).astype(o_ref.dtype)


def _mm_wt_body(a_ref, wt_ref, o_ref):
    o_ref[...] = lax.dot_general(a_ref[...], wt_ref[...], (((1,), (1,)), ((), ())),
                                 preferred_element_type=F32).astype(o_ref.dtype)


def _layer_spec(block, index_map, layer):
    if layer is None:
        return pl.BlockSpec(block, index_map)
    return pl.BlockSpec((None,) + block, lambda *g: (layer,) + index_map(*g))


def _matmul(a, w, *, layer=None, w_transposed=False, residual=None, out_dtype=F32, tm_pref=1024, tn_pref=1024,
            a_single_buffer=False, name="matmul"):
    m, k = a.shape
    n = w.shape[-2] if w_transposed else w.shape[-1]
    tm = _pick(m, tm_pref)
    tn = tn_pref if n % tn_pref == 0 else _pick(n, tn_pref)
    osz = jnp.dtype(out_dtype).itemsize
    vmem = (1 if a_single_buffer else 2) * tm * k * 2 + 2 * (k * tn * 2 + tm * tn * osz) + tm * tn * 4
    w_spec = (_layer_spec((tn, k), lambda i, j: (j, 0), layer) if w_transposed
              else _layer_spec((k, tn), lambda i, j: (0, j), layer))
    a_mode = dict(pipeline_mode=pl.Buffered(1)) if a_single_buffer else {}
    in_specs = [pl.BlockSpec((tm, k), lambda i, j: (i, 0), **a_mode), w_spec]
    args = [a, w]
    body = _mm_wt_body if w_transposed else _mm_body
    assert residual is None or not w_transposed
    if residual is not None:
        in_specs.append(pl.BlockSpec((tm, tn), lambda i, j: (i, j)))
        args.append(residual)
        body = _mm_res_body
        vmem += 2 * tm * tn * 4
    return pl.pallas_call(
        body,
        out_shape=jax.ShapeDtypeStruct((m, n), out_dtype),
        grid=(m // tm, n // tn),
        in_specs=in_specs,
        out_specs=pl.BlockSpec((tm, tn), lambda i, j: (i, j)),
        compiler_params=_cparams(("parallel", "arbitrary"), vmem + COMPILER_SCRATCH_BYTES),
        name=name,
    )(*args)


def _branch_a_body(u_ref, v_ref, w_ref, bs_ref, g_ref, b_ref, y_ref, *av_ref, chunk):
    tb = u_ref.shape[0]
    u = jax.nn.gelu(u_ref[...])
    v = jax.nn.gelu(v_ref[...])
    mu = jnp.mean(v, axis=-1, keepdims=True)
    var = jnp.mean(jnp.square(v - mu), axis=-1, keepdims=True)
    vn = (v - mu) * lax.rsqrt(var + LN_EPS) * g_ref[...] + b_ref[...]
    if av_ref:
        av_ref[0][...] = vn
    row = lax.broadcasted_iota(I32, (tb, tb), 0)
    col = lax.broadcasted_iota(I32, (tb, tb), 1)
    keep = (col <= row) & ((row // chunk) == (col // chunk))
    vb = vn.astype(BF16)
    for g in range(A_GROUPS):
        sl = slice(g * HEAD_DIM, (g + 1) * HEAD_DIM)
        wg = jnp.where(keep, w_ref[g], 0.0).astype(BF16)
        s = jnp.dot(wg, vb[:, sl], preferred_element_type=F32) + bs_ref[:, g:g + 1]
        y_ref[:, sl] = (u[:, sl] * s).astype(y_ref.dtype)


def _branch_a(h, lay, a_ws, a_bs, ln_g, ln_b, *, chunk, tb, emit_av):
    m = h.shape[0]
    reps = tb // chunk
    wfull = jnp.tile(a_ws[:, :chunk, :chunk], (1, reps, reps))
    bs_t = jnp.tile(a_bs[:, :chunk].T, (reps, 1))
    cu, cv = lay["au"] // A_WIDTH, lay["av"] // A_WIDTH
    out_shape = [jax.ShapeDtypeStruct((m, A_WIDTH), BF16)]
    out_specs = [pl.BlockSpec((tb, A_WIDTH), lambda i: (i, 0))]
    if emit_av:
        out_shape.append(jax.ShapeDtypeStruct((m, A_WIDTH), F32))
        out_specs.append(pl.BlockSpec((tb, A_WIDTH), lambda i: (i, 0)))
    res = pl.pallas_call(
        functools.partial(_branch_a_body, chunk=chunk),
        out_shape=out_shape,
        grid=(m // tb,),
        in_specs=[pl.BlockSpec((tb, A_WIDTH), lambda i: (i, cu)),
                  pl.BlockSpec((tb, A_WIDTH), lambda i: (i, cv)),
                  pl.BlockSpec((A_GROUPS, tb, tb), lambda i: (0, 0, 0)),
                  pl.BlockSpec((tb, A_GROUPS), lambda i: (0, 0)),
                  pl.BlockSpec((1, A_WIDTH), lambda i: (0, 0)),
                  pl.BlockSpec((1, A_WIDTH), lambda i: (0, 0))],
        out_specs=out_specs,
        compiler_params=_cparams(("parallel",), 32 << 20),
        name="branch_a_gmlp",
    )(h, h, wfull, bs_t, ln_g.reshape(1, A_WIDTH), ln_b.reshape(1, A_WIDTH))
    return res if emit_av else res[0]


def _bdot(a, b):
    return lax.dot_general(a.astype(BF16), b.astype(BF16), (((2,), (1,)), ((0,), (0,))),
                           preferred_element_type=F32)


def _bdot_nt(a, b):
    return lax.dot_general(a.astype(BF16), b.astype(BF16), (((2,), (2,)), ((0,), (0,))),
                           preferred_element_type=F32)


def _transpose_rows(x):
    r = x.shape[0]
    if r < LANE:
        x = jnp.concatenate([x, jnp.zeros((LANE - r, LANE), x.dtype)], axis=0)
    return x.T[:, :r]


def _gdn_body(q_ref, k_ref, v_ref, z_ref, sm_ref, c0_ref, s0_ref, cw_ref, al_ref, dt_ref, og_ref,
              y_ref, sout_ref, ext_ref, s_ref, *, chunk, t_valid):
    c = pl.program_id(1)
    nc = pl.num_programs(1)
    hd = HEAD_DIM

    @pl.when(c == 0)
    def _():
        s_ref[...] = s0_ref[0]
        for j in range(3):
            ext_ref[j, 0:SUBLANE, :] = c0_ref[0, :, j * B_WIDTH:(j + 1) * B_WIDTH]

    acts = []
    for j, ref in enumerate((q_ref, k_ref, v_ref)):
        ext_ref[j, SUBLANE:SUBLANE + chunk, :] = ref[...]
        acc = None
        for i in range(CONV_W):
            lo = SUBLANE - (CONV_W - 1) + i
            term = ext_ref[j, lo:lo + chunk, :] * cw_ref[i:i + 1, j * B_WIDTH:(j + 1) * B_WIDTH]
            acc = term if acc is None else acc + term
        acts.append(jax.nn.silu(acc))
        ext_ref[j, 0:SUBLANE, :] = ext_ref[j, chunk:chunk + SUBLANE, :]
    qa, ka, va = acts

    sm = sm_ref[...]
    row1 = lax.broadcasted_iota(I32, (chunk, LANE), 0)
    g_all = -jnp.exp(al_ref[...]) * jax.nn.softplus(sm + dt_ref[...])
    beta_all = jax.nn.sigmoid(sm)
    if t_valid < chunk:
        g_all = jnp.where(row1 < t_valid, g_all, 0.0)
        beta_all = jnp.where(row1 < t_valid, beta_all, 0.0)
    gc_all = g_all
    d = 1
    while d < chunk:
        gc_all = gc_all + jnp.where(row1 >= d, pltpu.roll(gc_all, d, 0), 0.0)
        d *= 2
    gc_t = _transpose_rows(gc_all)

    row = lax.broadcasted_iota(I32, (chunk, chunk), 0)
    col = lax.broadcasted_iota(I32, (chunk, chunk), 1)
    incl = row >= col
    strict = row > col
    eye = jnp.where(row == col, 1.0, 0.0)
    pair_masks = []
    bs = 1
    while bs < chunk:
        pair_masks.append(((row // bs) % 2 == 1) & ((col // bs) == (row // bs) - 1))
        bs *= 2

    heads = range(B_HEADS)
    per_head = lambda x: jnp.stack([x[:, h * hd:(h + 1) * hd] for h in heads], axis=0)
    q3, k3, v3 = per_head(qa), per_head(ka), per_head(va)
    qn = q3 * lax.rsqrt(jnp.sum(q3 * q3, axis=-1, keepdims=True) + 1e-6) * (hd ** -0.5)
    kn = k3 * lax.rsqrt(jnp.sum(k3 * k3, axis=-1, keepdims=True) + 1e-6)
    beta = jnp.stack([beta_all[:, B_HEADS + h:B_HEADS + h + 1] for h in heads], axis=0)
    gcol = jnp.stack([gc_all[:, h:h + 1] for h in heads], axis=0)
    grow = jnp.stack([gc_t[h:h + 1, :] for h in heads], axis=0)
    decay = jnp.exp(jnp.where(incl[None], gcol - grow, -jnp.inf))
    kb = kn * beta
    eg = jnp.exp(gcol)
    lmat = jnp.where(strict[None], _bdot_nt(kb, kn) * decay, 0.0)
    attn = _bdot_nt(qn, kn) * decay
    tinv = eye[None] - jnp.where(pair_masks[0][None], lmat, 0.0)
    for pm in pair_masks[1:]:
        tinv = tinv - _bdot(tinv, _bdot(jnp.where(pm[None], lmat, 0.0), tinv))
    sol = _bdot(tinv, jnp.concatenate([v3 * beta, kb * eg], axis=-1))
    value, kcd = sol[:, :, :hd], sol[:, :, hd:]
    s_old = s_ref[...]
    vnew = value - _bdot(kcd, s_old)
    o = _bdot(qn * eg, s_old) + _bdot(attn, vnew)
    glast = gcol[:, chunk - 1:chunk, :]
    kend = kn * jnp.exp(glast - gcol)
    s_ref[...] = s_old * jnp.exp(glast) + _bdot(jnp.swapaxes(kend, 1, 2), vnew)
    on = o * lax.rsqrt(jnp.mean(o * o, axis=-1, keepdims=True) + RMS_EPS) * og_ref[...]
    for h in heads:
        sl = slice(h * hd, (h + 1) * hd)
        y_ref[:, sl] = (on[h] * jax.nn.silu(z_ref[:, sl])).astype(y_ref.dtype)

    @pl.when(c == nc - 1)
    def _():
        sout_ref[0] = s_ref[...]


def _gdn(srcs, conv0, s0, conv_w, a_log, dt_bias, o_g, *, n_b, t_pad, t_valid, chunk):
    nc = t_pad // chunk
    arrs = [a for a, _ in srcs]
    cbs = [cb for _, cb in srcs]
    widths = [B_WIDTH] * 4 + [LANE]

    def tok_spec(w, cb):
        return pl.BlockSpec((chunk, w), lambda b, c: (b * nc + c, cb))

    pad12 = lambda x: jnp.zeros((1, LANE), F32).at[0, :B_HEADS].set(x)
    return pl.pallas_call(
        functools.partial(_gdn_body, chunk=chunk, t_valid=t_valid),
        out_shape=[jax.ShapeDtypeStruct((n_b * t_pad, B_WIDTH), BF16),
                   jax.ShapeDtypeStruct((n_b, B_HEADS, HEAD_DIM, HEAD_DIM), F32)],
        grid=(n_b, nc),
        in_specs=[tok_spec(w, cb) for w, cb in zip(widths, cbs)] + [
            pl.BlockSpec((1, SUBLANE, 3 * B_WIDTH), lambda b, c: (b, 0, 0)),
            pl.BlockSpec((1, B_HEADS, HEAD_DIM, HEAD_DIM), lambda b, c: (b, 0, 0, 0)),
            pl.BlockSpec((CONV_W, 3 * B_WIDTH), lambda b, c: (0, 0)),
            pl.BlockSpec((1, LANE), lambda b, c: (0, 0)),
            pl.BlockSpec((1, LANE), lambda b, c: (0, 0)),
            pl.BlockSpec((1, HEAD_DIM), lambda b, c: (0, 0))],
        out_specs=[pl.BlockSpec((chunk, B_WIDTH), lambda b, c: (b * nc + c, 0)),
                   pl.BlockSpec((1, B_HEADS, HEAD_DIM, HEAD_DIM), lambda b, c: (b, 0, 0, 0))],
        scratch_shapes=[pltpu.VMEM((3, chunk + SUBLANE, B_WIDTH), F32),
                        pltpu.VMEM((B_HEADS, HEAD_DIM, HEAD_DIM), F32)],
        compiler_params=_cparams(("parallel", "arbitrary"), 40 << 20),
        name="branch_b_gated_delta",
    )(*arrs, conv0, s0, conv_w, pad12(a_log), pad12(dt_bias), o_g.reshape(1, HEAD_DIM))


def _sort_key(x):
    b = lax.bitcast_convert_type(x + 0.0, I32)
    return b ^ ((b >> 31) & 0x7FFFFFFF)


def _select_rule(count, shape, k, nbits):
    kf = jnp.float32(k)
    count_ge = lambda cand: count(lambda key, pos: key >= cand)
    t0 = jnp.where(count_ge(jnp.zeros(shape, I32)) >= kf, 0, INT_MIN).astype(I32)

    def bit_step(i, t):
        cand = t + lax.shift_left(jnp.int32(1), 30 - i)
        return jnp.where(count_ge(cand) >= kf, cand, t)

    t = lax.fori_loop(0, 31, bit_step, t0)
    n_ge = count_ge(t)
    tied = (n_ge > kf) & (t > KEY_OF_NEG_INF)

    def tie_index():
        r = kf - count(lambda key, pos: key > t)

        def idx_step(i, j):
            cand = j + lax.shift_left(jnp.int32(1), nbits - 1 - i)
            below = count(lambda key, pos: (key == t) & (pos <= cand - 1))
            return jnp.where(below < r, cand, j)

        return lax.fori_loop(0, nbits, idx_step, jnp.zeros(shape, I32))

    no_limit = jnp.full(shape, 2 ** 30, I32)
    j = lax.cond(jnp.sum(jnp.where(tied, 1.0, 0.0)) > 0.0,
                 lambda: jnp.where(tied, tie_index(), no_limit), lambda: no_limit)
    return t, j


def _selected(key, pos, t, j):
    return (key > t) | ((key == t) & (pos <= j))


def _dsa_prompt_body(cq_ref, iq0_ref, iq1_ref, iq2_ref, iq3_ref, sm_ref, ik_ref, ck_ref, cv_ref, y_ref,
                     key_ref, iqb_ref, wb_ref, qb_ref, m_ref, l_ref, acc_ref, p_ref, al_ref,
                     *, k_sel, nbits, tq, tk):
    rows3 = C_GROUP * tq
    qi = pl.program_id(1)
    kpg = tk // LANE
    nkb = (qi + 1) * (tq // LANE)
    ngrp = ((qi + 1) * tq + tk - 1) // tk
    hd = HEAD_DIM
    nt = (((1,), (1,)), ((), ()))
    sub = LANE // SUBLANE
    qpos3 = qi * tq + lax.broadcasted_iota(I32, (sub, SUBLANE, tq), 2)
    kofs3 = lax.broadcasted_iota(I32, (sub, SUBLANE, tq), 0) * SUBLANE + lax.broadcasted_iota(I32, (sub, SUBLANE, tq), 1)

    w_all = sm_ref[...] * (IDX_HEADS ** -0.5 * IDX_DIM ** -0.5)
    w_t = jnp.concatenate([w_all[r:r + LANE].T for r in range(0, tq, LANE)], axis=1)
    w_off = 2 * B_HEADS
    for hh in range(IDX_HEADS):
        ref = (iq0_ref, iq1_ref, iq2_ref, iq3_ref)[hh // 4]
        iqb_ref[hh] = ref[:, (hh % 4) * IDX_DIM:(hh % 4 + 1) * IDX_DIM].astype(BF16)
        wb_ref[hh] = jnp.broadcast_to(w_t[w_off + hh:w_off + hh + 1, :], (SUBLANE, tq))

    def score_grp(g, carry):
        ikg = ik_ref[pl.ds(pl.multiple_of(g * tk, tk), tk), :].astype(BF16)
        acc = jnp.zeros((tk // SUBLANE, SUBLANE, tq), F32)
        for hh in range(IDX_HEADS):
            s = lax.dot_general(ikg, iqb_ref[hh], nt, preferred_element_type=F32)
            acc = acc + wb_ref[hh] * jnp.maximum(s, 0.0).reshape(tk // SUBLANE, SUBLANE, tq)
        for t in range(kpg):
            kpos3 = (g * kpg + t) * LANE + kofs3
            blk = acc[t * sub:(t + 1) * sub]
            key_ref[g * kpg + t] = _sort_key(jnp.where(kpos3 <= qpos3, blk, -jnp.inf))
        return carry

    lax.fori_loop(0, ngrp, score_grp, 0)

    bpt = tq // LANE

    def count(pred):
        def trip(i, acc):
            for u in range(bpt):
                b = i * bpt + u
                acc = acc + jnp.sum(jnp.where(pred(key_ref[b], b * LANE + kofs3), 1.0, 0.0), axis=0)
            return acc
        acc = lax.fori_loop(0, qi + 1, trip, jnp.zeros((SUBLANE, tq), F32))
        return jnp.broadcast_to(jnp.sum(acc, axis=0, keepdims=True), (SUBLANE, tq))

    t8, j8 = _select_rule(count, (SUBLANE, tq), k_sel, nbits)

    for kvh in range(C_KV_HEADS):
        for g in range(C_GROUP):
            hsl = slice((kvh * C_GROUP + g) * hd, (kvh * C_GROUP + g + 1) * hd)
            qb_ref[kvh, g * tq:(g + 1) * tq, :] = cq_ref[:, hsl].astype(BF16)
    m_ref[...] = jnp.full(m_ref.shape, MASK_NEG, F32)
    l_ref[...] = jnp.zeros(l_ref.shape, F32)
    acc_ref[...] = jnp.zeros(acc_ref.shape, F32)
    ones = jnp.ones((tk, hd), BF16)

    def attend_grp(g, carry):
        bias_t = []
        for t in range(kpg):
            kpos3 = (g * kpg + t) * LANE + kofs3
            sel = _selected(key_ref[g * kpg + t], kpos3, t8, j8) & (kpos3 <= qpos3)
            kq = jnp.where(sel, 0.0, MASK_NEG).reshape(LANE, tq)
            bias_t.append(jnp.concatenate([kq[:, r:r + LANE].T for r in range(0, tq, LANE)], axis=0))
        bias = jnp.concatenate(bias_t, axis=1)
        start = pl.multiple_of(g * tk, tk)
        kgrp = ck_ref[pl.ds(start, tk), :].astype(BF16)
        vgrp = cv_ref[pl.ds(start, tk), :].astype(BF16)
        for kvh in range(C_KV_HEADS):
            sl = slice(kvh * hd, (kvh + 1) * hd)
            s = lax.dot_general(qb_ref[kvh], kgrp[:, sl], nt, preferred_element_type=F32)
            for r0 in range(0, rows3, DSA_ROW_TILE):
                rs = slice(r0, r0 + DSA_ROW_TILE)
                x = s[rs] * (hd ** -0.5 * LOG2E) + bias[r0 % tq:r0 % tq + DSA_ROW_TILE]
                m_old = m_ref[kvh, rs, :]
                m_new = jnp.maximum(m_old, jnp.max(x, axis=-1, keepdims=True))
                p_ref[rs, :] = jnp.exp2(x - m_new).astype(BF16)
                al_ref[rs, :] = jnp.exp2(m_old - m_new)
                m_ref[kvh, rs, :] = m_new
            pv = jnp.dot(p_ref[...], jnp.concatenate([vgrp[:, sl], ones], axis=1), preferred_element_type=F32)
            alpha = al_ref[...]
            l_ref[kvh] = alpha * l_ref[kvh] + pv[:, hd:hd + 1]
            acc_ref[kvh] = alpha * acc_ref[kvh] + pv[:, :hd]
        return carry

    lax.fori_loop(0, ngrp, attend_grp, 0)

    for kvh in range(C_KV_HEADS):
        o = acc_ref[kvh] / l_ref[kvh]
        for g in range(C_GROUP):
            hsl = slice((kvh * C_GROUP + g) * hd, (kvh * C_GROUP + g + 1) * hd)
            y_ref[:, hsl] = o[g * tq:(g + 1) * tq, :].astype(y_ref.dtype)


def _dsa_prompt(h, lay, *, n_b, seq):
    tq, tk = DSA_TQ, DSA_TK
    assert seq % tq == 0 and seq % tk == 0
    nqb = seq // tq
    k_sel = min(TOPK_MAX, seq // 4)
    nbits = max(1, (seq - 1).bit_length())
    iq_cb = lay["iq"] // (4 * IDX_DIM)
    once = pl.Buffered(1)

    def q_spec(w, cb):
        return pl.BlockSpec((tq, w), lambda b, q: (b * nqb + q, cb))

    def kv_spec(w, cb):
        return pl.BlockSpec((seq, w), lambda b, q: (b, cb), pipeline_mode=once)

    rows3 = C_GROUP * tq
    lane_pad = lambda r: r * LANE * 4
    vmem = (seq * (2 * C_KV + IDX_DIM) * 4 + 2 * tq * (C_Q + IDX_HEADS * IDX_DIM + LANE) * 4 + 2 * tq * C_Q * 2
            + (seq // LANE) * tq * LANE * 4 + IDX_HEADS * tq * LANE * 6 + C_KV_HEADS * rows3 * HEAD_DIM * 6
            + 2 * C_KV_HEADS * lane_pad(rows3) + 4 * rows3 * tk * 4)
    return pl.pallas_call(
        functools.partial(_dsa_prompt_body, k_sel=k_sel, nbits=nbits, tq=tq, tk=tk),
        out_shape=jax.ShapeDtypeStruct((n_b * seq, C_Q), BF16),
        grid=(n_b, nqb),
        in_specs=[q_spec(C_Q, lay["cq"] // C_Q)]
                 + [q_spec(4 * IDX_DIM, iq_cb + i) for i in range(4)]
                 + [q_spec(LANE, lay["small"] // LANE),
                    kv_spec(IDX_DIM, lay["ik"] // IDX_DIM),
                    kv_spec(C_KV, lay["ck"] // C_KV),
                    kv_spec(C_KV, lay["cv"] // C_KV)],
        out_specs=pl.BlockSpec((tq, C_Q), lambda b, q: (b * nqb + q, 0)),
        scratch_shapes=[pltpu.VMEM((seq // LANE, LANE // SUBLANE, SUBLANE, tq), I32),
                        pltpu.VMEM((IDX_HEADS, tq, IDX_DIM), BF16),
                        pltpu.VMEM((IDX_HEADS, SUBLANE, tq), F32),
                        pltpu.VMEM((C_KV_HEADS, rows3, HEAD_DIM), BF16),
                        pltpu.VMEM((C_KV_HEADS, rows3, 1), F32),
                        pltpu.VMEM((C_KV_HEADS, rows3, 1), F32),
                        pltpu.VMEM((C_KV_HEADS, rows3, HEAD_DIM), F32),
                        pltpu.VMEM((rows3, tk), BF16),
                        pltpu.VMEM((rows3, 1), F32)],
        compiler_params=_cparams(("parallel", "arbitrary"), vmem + COMPILER_SCRATCH_BYTES),
        name="branch_c_prompt_dsa",
    )(h, h, h, h, h, h, h, h, h)


def _dsa_sample_select_body(pt_ref, *refs, n_pages, k_sel, nbits, t_new):
    pp = PAGES_PER_STEP
    page_refs = refs[:pp]
    iq_ref, w_ref, ikn_ref, bias_ref, key_ref = refs[pp:]
    j = pl.program_id(1)
    n_steps = n_pages // pp
    rows = t_new
    col = lax.broadcasted_iota(I32, (rows, LANE), 1)
    trow = lax.broadcasted_iota(I32, (rows, LANE), 0)
    iq = iq_ref[0].astype(BF16)

    def scores(keys_f32):
        n = keys_f32.shape[0]
        s = lax.dot_general(iq, keys_f32.astype(BF16), (((1,), (1,)), ((), ())), preferred_element_type=F32)
        r = jnp.maximum(s, 0.0) * w_ref[0]
        return jnp.sum(r.reshape(rows, IDX_HEADS, n), axis=1)

    @pl.when(j < n_steps)
    def _():
        keys = _sort_key(scores(jnp.concatenate([r[0, 0] for r in page_refs], axis=0)))
        for i in range(pp):
            key_ref[j * pp + i] = keys[:, i * LANE:(i + 1) * LANE]

    @pl.when(j == n_steps)
    def _():
        new_ok = (col <= trow) & (col < t_new)
        key_ref[n_pages] = _sort_key(jnp.where(new_ok, scores(ikn_ref[0]), -jnp.inf))
        nblk = n_pages + 1
        grp = SUBLANE
        nblk_pad = key_ref.shape[0]
        if nblk_pad > nblk:
            key_ref[nblk:nblk_pad] = jnp.full((nblk_pad - nblk, rows, LANE), KEY_OF_NEG_INF, I32)
        blk3 = lax.broadcasted_iota(I32, (grp, rows, LANE), 0)
        col3 = lax.broadcasted_iota(I32, (grp, rows, LANE), 2)

        def count(pred):
            def trip(i, acc):
                b0 = pl.multiple_of(i * grp, grp)
                hit = pred(key_ref[pl.ds(b0, grp)], (b0 + blk3) * LANE + col3)
                return acc + jnp.sum(jnp.where(hit, 1.0, 0.0), axis=0)
            acc = lax.fori_loop(0, nblk_pad // grp, trip, jnp.zeros((rows, LANE), F32))
            return jnp.broadcast_to(jnp.sum(acc, axis=1, keepdims=True), (rows, LANE))

        tb, jb = _select_rule(count, (rows, LANE), k_sel, nbits)

        def write_blk(b, carry):
            kpos = b * LANE + col
            sel = _selected(key_ref[b], kpos, tb, jb) & ((kpos < n_pages * PAGE_SIZE) | new_ok)
            bias_ref[0, b] = jnp.where(sel, 0.0, MASK_NEG)
            return carry

        lax.fori_loop(0, nblk, write_blk, 0)


def _page_index_map(layer, i, n_pages, n_steps, trailing):
    def index_map(b, j, pt):
        step = jnp.minimum(j, n_steps - 1)
        return (layer, pt[b * n_pages + step * PAGES_PER_STEP + i]) + (0,) * trailing
    return index_map


def _dsa_sample_select(pt_flat, kidx_pool, layer, iq_s, iw_s, ik_new, *, n_b, n_pages, t_new):
    pp = PAGES_PER_STEP
    assert n_pages % pp == 0
    n_steps = n_pages // pp
    total = n_pages * PAGE_SIZE + t_new
    k_sel = min(TOPK_MAX, total // 4)
    nbits = max(1, ((n_pages + 1) * PAGE_SIZE - 1).bit_length())
    rows_q = t_new * IDX_HEADS
    grid_spec = pltpu.PrefetchScalarGridSpec(
        num_scalar_prefetch=1,
        grid=(n_b, n_steps + 1),
        in_specs=[pl.BlockSpec((1, 1, PAGE_SIZE, IDX_DIM), _page_index_map(layer, i, n_pages, n_steps, 2))
                  for i in range(pp)]
                 + [pl.BlockSpec((1, rows_q, IDX_DIM), lambda b, j, pt: (b, 0, 0)),
                    pl.BlockSpec((1, rows_q, 1), lambda b, j, pt: (b, 0, 0)),
                    pl.BlockSpec((1, PAGE_SIZE, IDX_DIM), lambda b, j, pt: (b, 0, 0))],
        out_specs=pl.BlockSpec((1, n_pages + 1, t_new, LANE), lambda b, j, pt: (b, 0, 0, 0)),
        scratch_shapes=[pltpu.VMEM((-(-(n_pages + 1) // SUBLANE) * SUBLANE, t_new, LANE), I32)])
    return pl.pallas_call(
        functools.partial(_dsa_sample_select_body, n_pages=n_pages, k_sel=k_sel, nbits=nbits, t_new=t_new),
        out_shape=jax.ShapeDtypeStruct((n_b, n_pages + 1, t_new, LANE), F32),
        grid_spec=grid_spec,
        compiler_params=_cparams(("parallel", "arbitrary"), 24 << 20),
        name="branch_c_decode_select",
    )(pt_flat, *([kidx_pool] * pp), iq_s, iw_s, ik_new)


def _dsa_sample_attend_body(pt_ref, *refs, n_pages, t_new):
    pp = PAGES_PER_STEP
    k_refs, v_refs = refs[:pp], refs[pp:2 * pp]
    q_ref, bias_ref, kn_ref, vn_ref, o_ref, m_ref, l_ref, acc_ref = refs[2 * pp:]
    j = pl.program_id(1)
    n_steps = n_pages // pp
    hd = HEAD_DIM
    rq = q_ref.shape[1] // C_KV_HEADS
    reps = rq // t_new

    @pl.when(j == 0)
    def _():
        m_ref[...] = jnp.full(m_ref.shape, MASK_NEG, F32)
        l_ref[...] = jnp.zeros(l_ref.shape, F32)
        acc_ref[...] = jnp.zeros(acc_ref.shape, F32)

    def attend(k_of, v_of, bias_t):
        bias = jnp.concatenate([bias_t] * reps, axis=0)
        for kvh in range(C_KV_HEADS):
            rs = slice(kvh * rq, (kvh + 1) * rq)
            s = lax.dot_general(q_ref[0, rs, :].astype(BF16), k_of(kvh).astype(BF16), (((1,), (1,)), ((), ())),
                                preferred_element_type=F32) * (hd ** -0.5) + bias
            m_old = m_ref[rs, :]
            m_new = jnp.maximum(m_old, jnp.max(s, axis=-1, keepdims=True))
            alpha = jnp.exp(m_old - m_new)
            p = jnp.exp(s - m_new)
            l_ref[rs, :] = alpha * l_ref[rs, :] + jnp.sum(p, axis=-1, keepdims=True)
            acc_ref[rs, :] = alpha * acc_ref[rs, :] + jnp.dot(p.astype(BF16), v_of(kvh).astype(BF16),
                                                              preferred_element_type=F32)
            m_ref[rs, :] = m_new

    @pl.when(j < n_steps)
    def _():
        head_rows = lambda kvh: pl.ds(kvh, PAGE_SIZE, stride=C_KV_HEADS)
        pages = lambda prefs: (lambda kvh: jnp.concatenate([r[0, 0, head_rows(kvh), :] for r in prefs], axis=0))
        bias_t = jnp.concatenate([bias_ref[0, j * pp + i] for i in range(pp)], axis=1)
        attend(pages(k_refs), pages(v_refs), bias_t)

    @pl.when(j == n_steps)
    def _():
        new = lambda ref: (lambda kvh: ref[0, :, kvh * hd:(kvh + 1) * hd])
        attend(new(kn_ref), new(vn_ref), bias_ref[0, n_pages])
        o_ref[0] = acc_ref[...] / l_ref[...]


def _dsa_sample_attend(pt_flat, k_pool, v_pool, layer, q_s, bias, k_new, v_new, *, n_b, n_pages, t_new):
    pp = PAGES_PER_STEP
    n_steps = n_pages // pp
    rows = q_s.shape[1]
    page_specs = [pl.BlockSpec((1, 1, PAGE_SIZE * C_KV_HEADS, HEAD_DIM), _page_index_map(layer, i, n_pages, n_steps, 2))
                  for i in range(pp)]
    grid_spec = pltpu.PrefetchScalarGridSpec(
        num_scalar_prefetch=1,
        grid=(n_b, n_steps + 1),
        in_specs=page_specs + page_specs
                 + [pl.BlockSpec((1, rows, HEAD_DIM), lambda b, j, pt: (b, 0, 0)),
                    pl.BlockSpec((1, n_pages + 1, t_new, LANE), lambda b, j, pt: (b, 0, 0, 0)),
                    pl.BlockSpec((1, PAGE_SIZE, C_KV), lambda b, j, pt: (b, 0, 0)),
                    pl.BlockSpec((1, PAGE_SIZE, C_KV), lambda b, j, pt: (b, 0, 0))],
        out_specs=pl.BlockSpec((1, rows, HEAD_DIM), lambda b, j, pt: (b, 0, 0)),
        scratch_shapes=[pltpu.VMEM((rows, 1), F32), pltpu.VMEM((rows, 1), F32), pltpu.VMEM((rows, HEAD_DIM), F32)])
    return pl.pallas_call(
        functools.partial(_dsa_sample_attend_body, n_pages=n_pages, t_new=t_new),
        out_shape=jax.ShapeDtypeStruct((n_b, rows, HEAD_DIM), F32),
        grid_spec=grid_spec,
        compiler_params=_cparams(("parallel", "arbitrary"), 32 << 20),
        name="branch_c_decode_attend",
    )(pt_flat, *([k_pool] * pp), *([v_pool] * pp), q_s, bias, k_new, v_new)


def _dsa_sample(h_s, lay, pt_flat, k_pool, v_pool, kidx_pool, layer, *, n_b, t_new, n_pages):
    hd = HEAD_DIM
    seg = lambda name, w: h_s[:, lay[name]:lay[name] + w]
    rq = -(-C_GROUP * t_new // 16) * 16
    q = seg("cq", C_Q).reshape(n_b, t_new, C_KV_HEADS, C_GROUP, hd).transpose(0, 2, 3, 1, 4)
    q = q.reshape(n_b, C_KV_HEADS, C_GROUP * t_new, hd)
    q_s = jnp.pad(q, ((0, 0), (0, 0), (0, rq - C_GROUP * t_new), (0, 0))).reshape(n_b, C_KV_HEADS * rq, hd)
    iq_s = seg("iq", IDX_HEADS * IDX_DIM).reshape(n_b, t_new * IDX_HEADS, IDX_DIM)
    iw = h_s[:, lay["small"] + 2 * B_HEADS:lay["small"] + 2 * B_HEADS + IDX_HEADS]
    iw_s = (iw * (IDX_HEADS ** -0.5 * IDX_DIM ** -0.5)).reshape(n_b, t_new * IDX_HEADS, 1)
    padrows = lambda x: jnp.pad(x.reshape(n_b, t_new, -1), ((0, 0), (0, PAGE_SIZE - t_new), (0, 0)))
    ik_new, k_new, v_new = padrows(seg("ik", IDX_DIM)), padrows(seg("ck", C_KV)), padrows(seg("cv", C_KV))
    bias = _dsa_sample_select(pt_flat, kidx_pool, layer, iq_s, iw_s, ik_new, n_b=n_b, n_pages=n_pages, t_new=t_new)
    o = _dsa_sample_attend(pt_flat, k_pool, v_pool, layer, q_s, bias, k_new, v_new,
                           n_b=n_b, n_pages=n_pages, t_new=t_new)
    o = o.reshape(n_b, C_KV_HEADS, rq, hd)[:, :, :C_GROUP * t_new].reshape(n_b, C_KV_HEADS, C_GROUP, t_new, hd)
    return o.transpose(0, 3, 1, 2, 4).reshape(n_b * t_new, C_Q).astype(BF16)


def _merge_body(ya_ref, yb_ref, yc_ref, w_ref, g0_ref, g1_ref, g2_ref, o_ref):
    b0, c0 = A_WIDTH, A_WIDTH + B_WIDTH
    acc = jax.nn.sigmoid(g0_ref[...]) * jnp.dot(ya_ref[...], w_ref[0:b0, :], preferred_element_type=F32)
    acc = acc + jax.nn.sigmoid(g1_ref[...]) * jnp.dot(yb_ref[...], w_ref[b0:c0, :], preferred_element_type=F32)
    acc = acc + jax.nn.sigmoid(g2_ref[...]) * jnp.dot(yc_ref[...], w_ref[c0:, :], preferred_element_type=F32)
    o_ref[...] = acc.astype(o_ref.dtype)


def _merge(ya, yb, yc, w_br, layer, h, lay, d_model):
    m = ya.shape[0]
    tm = _pick(m, 1024)
    tn = _pick(d_model, 512)
    gcb = lay["gate"] // tn
    per = d_model // tn

    def gate_spec(i):
        return pl.BlockSpec((tm, tn), lambda r, c: (r, gcb + i * per + c))

    kw = A_WIDTH + B_WIDTH + C_Q
    vmem = 2 * (tm * kw * 2 + kw * tn * 2 + 3 * tm * tn * 4 + tm * tn * 2) + 4 * tm * tn * 4
    return pl.pallas_call(
        _merge_body,
        out_shape=jax.ShapeDtypeStruct((m, d_model), BF16),
        grid=(m // tm, d_model // tn),
        in_specs=[pl.BlockSpec((tm, A_WIDTH), lambda r, c: (r, 0)),
                  pl.BlockSpec((tm, B_WIDTH), lambda r, c: (r, 0)),
                  pl.BlockSpec((tm, C_Q), lambda r, c: (r, 0)),
                  _layer_spec((kw, tn), lambda r, c: (0, c), layer),
                  gate_spec(0), gate_spec(1), gate_spec(2)],
        out_specs=pl.BlockSpec((tm, tn), lambda r, c: (r, c)),
        compiler_params=_cparams(("parallel", "arbitrary"), vmem + COMPILER_SCRATCH_BYTES),
        name="gated_merge",
    )(ya, yb, yc, w_br, h, h, h)


def _ffn_up_body(hn_ref, w1_ref, w3_ref, o_ref):
    hn = hn_ref[...]
    a = jnp.dot(hn, w1_ref[...].astype(BF16), preferred_element_type=F32)
    b = jnp.dot(hn, w3_ref[...].astype(BF16), preferred_element_type=F32)
    o_ref[...] = (jax.nn.silu(a) * b).astype(o_ref.dtype)


def _ffn(hn, w1, w3, w2, layer, x):
    m, d = hn.shape
    d_ff = w1.shape[-1]
    tm = _pick(m, 1024)
    tf = FFN_TF
    assert d_ff % tf == 0
    vmem = 2 * (tm * d * 2 + 2 * d * tf * 4 + tm * tf * 2) + 2 * d * tf * 2 + 3 * tm * tf * 4
    act = pl.pallas_call(
        _ffn_up_body,
        out_shape=jax.ShapeDtypeStruct((m, d_ff), BF16),
        grid=(m // tm, d_ff // tf),
        in_specs=[pl.BlockSpec((tm, d), lambda i, f: (i, 0)),
                  _layer_spec((d, tf), lambda i, f: (0, f), layer),
                  _layer_spec((d, tf), lambda i, f: (0, f), layer)],
        out_specs=pl.BlockSpec((tm, tf), lambda i, f: (i, f)),
        compiler_params=_cparams(("parallel", "arbitrary"), vmem + COMPILER_SCRATCH_BYTES),
        name="swiglu_up",
    )(hn, w1, w3)
    return _matmul(act, w2, layer=layer, residual=x, tm_pref=1024, tn_pref=256, a_single_buffer=True,
                   name="swiglu_down")


def _dense_front(x, ln1, w_in_p, layer):
    xn = _rmsnorm(x, ln1, BF16)
    return _matmul(xn, w_in_p, layer=layer, w_transposed=True, tn_pref=768, name="in_proj")


def _dense_back(x, ya, yb, yc, h, lay, wts, layer, d_model):
    w_br, w_o, w1, w3, w2, ln2 = wts
    mix = _merge(ya, yb, yc, w_br, layer, h, lay, d_model)
    x = _matmul(mix, w_o, layer=layer, residual=x, name="out_proj")
    hn = _rmsnorm(x, ln2, BF16)
    return _ffn(hn, w1, w3, w2, layer, x)


def kernel(x_prompt, x_sample, cache_k, cache_v, cache_kidx, state_conv, state_delta, page_table, ln1, w_in,
           a_ln_g, a_ln_b, a_ws, a_bs, b_conv_w, b_a_log, b_dt_bias, b_out_g, w_br, w_o, ln2, ffn_w1, ffn_w3,
           ffn_w2, ln_f):
    n_bp, seq, d_model = x_prompt.shape
    n_bs, t_new, _ = x_sample.shape
    depth = ln1.shape[0]
    n_phys = cache_k.shape[1]
    n_pages = page_table.shape[1]
    assert seq % A_CHUNK == 0 and seq % GDN_CHUNK == 0 and CONV_W - 1 <= t_new <= min(A_CHUNK, GDN_DECODE_CHUNK)
    lay = _layout(d_model)
    hd = HEAD_DIM

    xp = x_prompt.reshape(n_bp * seq, d_model)
    xs = x_sample.reshape(n_bs * t_new, d_model)
    pt_flat = page_table.reshape(-1).astype(I32)
    k_pool = cache_k.reshape(depth, n_phys, PAGE_SIZE * C_KV_HEADS, hd)
    v_pool = cache_v.reshape(depth, n_phys, PAGE_SIZE * C_KV_HEADS, hd)
    conv0_p = jnp.zeros((n_bp, SUBLANE, 3 * B_WIDTH), F32)
    s0_p = jnp.zeros((n_bp, B_HEADS, hd, hd), F32)
    qkv_off = lay["bq"]

    outs = {k: [] for k in ("pk", "pv", "pik", "pconv", "pdelta", "sk", "sv", "sik", "sconv", "sdelta", "schunk")}
    dense_w = (w_br.astype(BF16), w_o.astype(BF16), ffn_w1, ffn_w3, ffn_w2.astype(BF16))
    w_in_p = _pack_w_in(w_in, d_model)
    for l in range(depth):
        wts = dense_w + (ln2[l],)

        h = _dense_front(xp, ln1[l], w_in_p, l)
        ya = _branch_a(h, lay, a_ws[l], a_bs[l], a_ln_g[l], a_ln_b[l], chunk=A_CHUNK, tb=A_CHUNK, emit_av=False)
        srcs = [(h, lay[n] // B_WIDTH) for n in ("bq", "bk", "bv", "bz")] + [(h, lay["small"] // LANE)]
        yb, s_new = _gdn(srcs, conv0_p, s0_p, b_conv_w[l], b_a_log[l], b_dt_bias[l], b_out_g[l],
                         n_b=n_bp, t_pad=seq, t_valid=GDN_CHUNK, chunk=GDN_CHUNK)
        yc = _dsa_prompt(h, lay, n_b=n_bp, seq=seq)
        h3 = h.reshape(n_bp, seq, lay["total"])
        outs["pk"].append(h3[:, :, lay["ck"]:lay["ck"] + C_KV].reshape(n_bp, seq, C_KV_HEADS, hd))
        outs["pv"].append(h3[:, :, lay["cv"]:lay["cv"] + C_KV].reshape(n_bp, seq, C_KV_HEADS, hd))
        outs["pik"].append(h3[:, :, lay["ik"]:lay["ik"] + IDX_DIM])
        outs["pconv"].append(h3[:, seq - (CONV_W - 1):, qkv_off:qkv_off + 3 * B_WIDTH])
        outs["pdelta"].append(s_new)
        xp = _dense_back(xp, ya, yb, yc, h, lay, wts, l, d_model)

        hs = _dense_front(xs, ln1[l], w_in_p, l)
        ya, av = _branch_a(hs, lay, a_ws[l], a_bs[l], a_ln_g[l], a_ln_b[l], chunk=t_new, tb=n_bs * t_new,
                           emit_av=True)
        hs3 = hs.reshape(n_bs, t_new, lay["total"])
        dc = GDN_DECODE_CHUNK
        padt = lambda x: jnp.pad(x, ((0, 0), (0, dc - t_new), (0, 0))).reshape(n_bs * dc, -1)
        srcs = [(padt(hs3[:, :, lay[n]:lay[n] + B_WIDTH]), 0) for n in ("bq", "bk", "bv", "bz")]
        srcs.append((padt(hs3[:, :, lay["small"]:lay["small"] + LANE]), 0))
        conv0_s = jnp.pad(state_conv[l], ((0, 0), (SUBLANE - (CONV_W - 1), 0), (0, 0)))
        yb, s_new = _gdn(srcs, conv0_s, state_delta[l], b_conv_w[l], b_a_log[l], b_dt_bias[l], b_out_g[l],
                         n_b=n_bs, t_pad=dc, t_valid=t_new, chunk=dc)
        yb = yb.reshape(n_bs, dc, B_WIDTH)[:, :t_new].reshape(n_bs * t_new, B_WIDTH)
        yc = _dsa_sample(hs, lay, pt_flat, k_pool, v_pool, cache_kidx, l, n_b=n_bs, t_new=t_new, n_pages=n_pages)
        outs["sk"].append(hs3[:, :, lay["ck"]:lay["ck"] + C_KV].reshape(n_bs, t_new, C_KV_HEADS, hd))
        outs["sv"].append(hs3[:, :, lay["cv"]:lay["cv"] + C_KV].reshape(n_bs, t_new, C_KV_HEADS, hd))
        outs["sik"].append(hs3[:, :, lay["ik"]:lay["ik"] + IDX_DIM])
        outs["sconv"].append(hs3[:, t_new - (CONV_W - 1):, qkv_off:qkv_off + 3 * B_WIDTH])
        outs["sdelta"].append(s_new)
        outs["schunk"].append(av.reshape(n_bs, t_new, A_WIDTH))
        xs = _dense_back(xs, ya, yb, yc, hs, lay, wts, l, d_model)

    y_prompt = _rmsnorm(xp, ln_f, F32).reshape(n_bp, seq, d_model)
    y_sample = _rmsnorm(xs, ln_f, F32).reshape(n_bs, t_new, d_model)
    st = lambda k: jnp.stack(outs[k])
    return (y_prompt, y_sample, st("pk"), st("pv"), st("pik"), st("pconv"), st("pdelta"),
            st("sk"), st("sv"), st("sik"), st("sconv"), st("sdelta"), st("schunk"))
```

```python
import functools

import jax
import jax.numpy as jnp
from jax import lax
from jax.experimental import pallas as pl
from jax.experimental.pallas import tpu as pltpu

F32 = jnp.float32
BF16 = jnp.bfloat16
I32 = jnp.int32

HEAD_DIM = 128
A_GROUPS = 8
A_CHUNK = 128
A_WIDTH = A_GROUPS * HEAD_DIM
B_HEADS = 12
B_WIDTH = B_HEADS * HEAD_DIM
CONV_W = 4
C_HEADS = 12
C_KV_HEADS = 4
C_GROUP = C_HEADS // C_KV_HEADS
C_Q = C_HEADS * HEAD_DIM
C_KV = C_KV_HEADS * HEAD_DIM
IDX_HEADS = 16
IDX_DIM = 128
TOPK_MAX = 256
PAGE_SIZE = 128
N_BRANCH = 3
RMS_EPS = 1e-6
LN_EPS = 1e-5

LANE = 128
SUBLANE = 8
V7X_VMEM_BYTES = 64 * 1024 * 1024
VMEM_BUDGET = 56 * 1024 * 1024
COMPILER_SCRATCH_BYTES = 8 * 1024 * 1024

GDN_CHUNK = 128
GDN_DECODE_CHUNK = 16
DSA_TQ = 256
DSA_TK = 512
DSA_ROW_TILE = 32
FFN_TF = 256
PAGES_PER_STEP = 16
SELECT_PAGES_PER_STEP = 32
MASK_NEG = -1e30
LOG2E = 1.4426950408889634
INT_MIN = -2147483648
KEY_OF_NEG_INF = -2139095041


def _cparams(semantics, vmem_bytes):
    return pltpu.CompilerParams(dimension_semantics=semantics,
                                vmem_limit_bytes=int(min(max(vmem_bytes, 16 * 1024 * 1024), VMEM_BUDGET)))


def _pick(dim, pref):
    t = pref
    while t >= SUBLANE:
        if dim % t == 0:
            return t
        t //= 2
    return dim


def _layout(d_model):
    segs = [("gate", N_BRANCH * d_model, d_model), ("bq", B_WIDTH, B_WIDTH), ("bk", B_WIDTH, B_WIDTH),
            ("bv", B_WIDTH, B_WIDTH), ("bz", B_WIDTH, B_WIDTH), ("cq", C_Q, C_Q), ("ck", C_KV, C_KV),
            ("au", A_WIDTH, A_WIDTH), ("av", A_WIDTH, A_WIDTH), ("cv", C_KV, C_KV),
            ("iq", IDX_HEADS * IDX_DIM, 4 * IDX_DIM), ("ik", IDX_DIM, IDX_DIM), ("small", LANE, LANE)]
    off, lay = 0, {}
    for name, width, align in segs:
        assert off % align == 0, (name, off, align)
        lay[name] = off
        off += width
    lay["total"] = off
    return lay


def _pack_body(tab_ref, first_ref, second_ref, ab_ref, iw_ref, o_ref):
    j = pl.program_id(1)
    last = pl.num_programs(1) - 1
    o_ref[0:LANE, :] = first_ref[0].astype(o_ref.dtype)

    @pl.when(j != last)
    def _():
        o_ref[LANE:, :] = second_ref[0].astype(o_ref.dtype)

    @pl.when(j == last)
    def _():
        n_ab, n_iw = 2 * B_HEADS, IDX_HEADS
        rest = jnp.zeros((LANE - n_ab - n_iw, o_ref.shape[1]), F32)
        o_ref[LANE:, :] = jnp.concatenate([ab_ref[0, 0:n_ab, :], iw_ref[0, 0:n_iw, :], rest],
                                          axis=0).astype(o_ref.dtype)


def _pack_w_in(w_in, d_model):
    depth, d, in_width = w_in.shape
    lay = _layout(d_model)
    w_t = jnp.swapaxes(w_in, 1, 2)
    widths = (A_WIDTH, A_WIDTH, 3 * B_WIDTH, B_WIDTH, 2 * B_HEADS, C_Q, C_KV, C_KV,
              IDX_HEADS * IDX_DIM, IDX_HEADS, IDX_DIM, N_BRANCH * d_model)
    dsts = (lay["au"], lay["av"], lay["bq"], lay["bz"], None, lay["cq"], lay["ck"], lay["cv"], lay["iq"], None,
            lay["ik"], lay["gate"])
    n_blocks = lay["total"] // LANE
    table, start, special = [0] * n_blocks, 0, []
    for width, dst in zip(widths, dsts):
        if dst is None:
            special.append(start)
        else:
            assert width % LANE == 0 and dst % LANE == 0 and start % SUBLANE == 0
            for blk in range(width // LANE):
                table[dst // LANE + blk] = (start + blk * LANE) // SUBLANE
        start += width
    assert start == in_width and lay["small"] // LANE == n_blocks - 1 and n_blocks % 2 == 0
    ab_row, iw_row = special
    assert ab_row % SUBLANE == 0 and iw_row % SUBLANE == 0 and max(ab_row, iw_row) + LANE <= in_width
    window = lambda index_map: pl.BlockSpec((pl.Element(1), pl.Element(LANE), pl.Element(d)), index_map)
    grid_spec = pltpu.PrefetchScalarGridSpec(
        num_scalar_prefetch=1,
        grid=(depth, n_blocks // 2),
        in_specs=[window(lambda l, j, tab: (l, tab[2 * j] * SUBLANE, 0)),
                  window(lambda l, j, tab: (l, tab[2 * j + 1] * SUBLANE, 0)),
                  window(lambda l, j, tab: (l, ab_row, 0)),
                  window(lambda l, j, tab: (l, iw_row, 0))],
        out_specs=pl.BlockSpec((None, 2 * LANE, d), lambda l, j, tab: (l, j, 0)))
    return pl.pallas_call(
        _pack_body,
        out_shape=jax.ShapeDtypeStruct((depth, lay["total"], d), BF16),
        grid_spec=grid_spec,
        compiler_params=_cparams(("parallel", "arbitrary"), 14 * LANE * d * 4),
        name="pack_in_proj_weight",
    )(jnp.asarray(table, I32), w_t, w_t, w_t, w_t)


def _rmsnorm_body(x_ref, g_ref, o_ref):
    x = x_ref[...]
    ms = jnp.mean(x * x, axis=-1, keepdims=True)
    o_ref[...] = (x * lax.rsqrt(ms + RMS_EPS) * g_ref[...]).astype(o_ref.dtype)


def _rmsnorm(x, g, out_dtype):
    m, d = x.shape
    tm = _pick(m, 256)
    return pl.pallas_call(
        _rmsnorm_body,
        out_shape=jax.ShapeDtypeStruct((m, d), out_dtype),
        grid=(m // tm,),
        in_specs=[pl.BlockSpec((tm, d), lambda i: (i, 0)), pl.BlockSpec((1, d), lambda i: (0, 0))],
        out_specs=pl.BlockSpec((tm, d), lambda i: (i, 0)),
        compiler_params=_cparams(("parallel",), 6 * tm * d * 4),
        name="rmsnorm",
    )(x, g.reshape(1, d))


def _mm_body(a_ref, w_ref, o_ref):
    o_ref[...] = jnp.dot(a_ref[...], w_ref[...], preferred_element_type=F32).astype(o_ref.dtype)


def _mm_res_body(a_ref, w_ref, r_ref, o_ref):
    o_ref[...] = (r_ref[...] + jnp.dot(a_ref[...], w_ref[...], preferred_element_type=F32)).astype(o_ref.dtype)


def _mm_wt_body(a_ref, wt_ref, o_ref):
    o_ref[...] = lax.dot_general(a_ref[...], wt_ref[...], (((1,), (1,)), ((), ())),
                                 preferred_element_type=F32).astype(o_ref.dtype)


def _layer_spec(block, index_map, layer):
    if layer is None:
        return pl.BlockSpec(block, index_map)
    return pl.BlockSpec((None,) + block, lambda *g: (layer,) + index_map(*g))


def _matmul(a, w, *, layer=None, w_transposed=False, residual=None, out_dtype=F32, tm_pref=1024, tn_pref=1024,
            a_single_buffer=False, name="matmul"):
    m, k = a.shape
    n = w.shape[-2] if w_transposed else w.shape[-1]
    tm = _pick(m, tm_pref)
    tn = tn_pref if n % tn_pref == 0 else _pick(n, tn_pref)
    osz = jnp.dtype(out_dtype).itemsize
    vmem = (1 if a_single_buffer else 2) * tm * k * 2 + 2 * (k * tn * 2 + tm * tn * osz) + tm * tn * 4
    w_spec = (_layer_spec((tn, k), lambda i, j: (j, 0), layer) if w_transposed
              else _layer_spec((k, tn), lambda i, j: (0, j), layer))
    a_mode = dict(pipeline_mode=pl.Buffered(1)) if a_single_buffer else {}
    in_specs = [pl.BlockSpec((tm, k), lambda i, j: (i, 0), **a_mode), w_spec]
    args = [a, w]
    body = _mm_wt_body if w_transposed else _mm_body
    assert residual is None or not w_transposed
    if residual is not None:
        in_specs.append(pl.BlockSpec((tm, tn), lambda i, j: (i, j)))
        args.append(residual)
        body = _mm_res_body
        vmem += 2 * tm * tn * 4
    return pl.pallas_call(
        body,
        out_shape=jax.ShapeDtypeStruct((m, n), out_dtype),
        grid=(m // tm, n // tn),
        in_specs=in_specs,
        out_specs=pl.BlockSpec((tm, tn), lambda i, j: (i, j)),
        compiler_params=_cparams(("parallel", "arbitrary"), vmem + COMPILER_SCRATCH_BYTES),
        name=name,
    )(*args)


def _branch_a_body(u_ref, v_ref, w_ref, bs_ref, g_ref, b_ref, y_ref, *av_ref, chunk):
    tb = u_ref.shape[0]
    u = jax.nn.gelu(u_ref[...])
    v = jax.nn.gelu(v_ref[...])
    mu = jnp.mean(v, axis=-1, keepdims=True)
    var = jnp.mean(jnp.square(v - mu), axis=-1, keepdims=True)
    vn = (v - mu) * lax.rsqrt(var + LN_EPS) * g_ref[...] + b_ref[...]
    if av_ref:
        av_ref[0][...] = vn
    row = lax.broadcasted_iota(I32, (tb, tb), 0)
    col = lax.broadcasted_iota(I32, (tb, tb), 1)
    keep = (col <= row) & ((row // chunk) == (col // chunk))
    vb = vn.astype(BF16)
    for g in range(A_GROUPS):
        sl = slice(g * HEAD_DIM, (g + 1) * HEAD_DIM)
        wg = jnp.where(keep, w_ref[g], 0.0).astype(BF16)
        s = jnp.dot(wg, vb[:, sl], preferred_element_type=F32) + bs_ref[:, g:g + 1]
        y_ref[:, sl] = (u[:, sl] * s).astype(y_ref.dtype)


def _branch_a(h, lay, a_ws, a_bs, ln_g, ln_b, *, chunk, tb, emit_av):
    m = h.shape[0]
    reps = tb // chunk
    wfull = jnp.tile(a_ws[:, :chunk, :chunk], (1, reps, reps))
    bs_t = jnp.tile(a_bs[:, :chunk].T, (reps, 1))
    cu, cv = lay["au"] // A_WIDTH, lay["av"] // A_WIDTH
    out_shape = [jax.ShapeDtypeStruct((m, A_WIDTH), BF16)]
    out_specs = [pl.BlockSpec((tb, A_WIDTH), lambda i: (i, 0))]
    if emit_av:
        out_shape.append(jax.ShapeDtypeStruct((m, A_WIDTH), F32))
        out_specs.append(pl.BlockSpec((tb, A_WIDTH), lambda i: (i, 0)))
    res = pl.pallas_call(
        functools.partial(_branch_a_body, chunk=chunk),
        out_shape=out_shape,
        grid=(m // tb,),
        in_specs=[pl.BlockSpec((tb, A_WIDTH), lambda i: (i, cu)),
                  pl.BlockSpec((tb, A_WIDTH), lambda i: (i, cv)),
                  pl.BlockSpec((A_GROUPS, tb, tb), lambda i: (0, 0, 0)),
                  pl.BlockSpec((tb, A_GROUPS), lambda i: (0, 0)),
                  pl.BlockSpec((1, A_WIDTH), lambda i: (0, 0)),
                  pl.BlockSpec((1, A_WIDTH), lambda i: (0, 0))],
        out_specs=out_specs,
        compiler_params=_cparams(("parallel",), 32 << 20),
        name="branch_a_gmlp",
    )(h, h, wfull, bs_t, ln_g.reshape(1, A_WIDTH), ln_b.reshape(1, A_WIDTH))
    return res if emit_av else res[0]


def _bdot(a, b):
    return lax.dot_general(a.astype(BF16), b.astype(BF16), (((2,), (1,)), ((0,), (0,))),
                           preferred_element_type=F32)


def _bdot_nt(a, b):
    return lax.dot_general(a.astype(BF16), b.astype(BF16), (((2,), (2,)), ((0,), (0,))),
                           preferred_element_type=F32)


def _transpose_rows(x):
    r = x.shape[0]
    if r < LANE:
        x = jnp.concatenate([x, jnp.zeros((LANE - r, LANE), x.dtype)], axis=0)
    return x.T[:, :r]


def _gdn_body(q_ref, k_ref, v_ref, z_ref, sm_ref, c0_ref, s0_ref, cw_ref, al_ref, dt_ref, og_ref,
              y_ref, sout_ref, ext_ref, s_ref, *, chunk, t_valid):
    c = pl.program_id(1)
    nc = pl.num_programs(1)
    hd = HEAD_DIM

    @pl.when(c == 0)
    def _():
        s_ref[...] = s0_ref[0]
        for j in range(3):
            ext_ref[j, 0:SUBLANE, :] = c0_ref[0, :, j * B_WIDTH:(j + 1) * B_WIDTH]

    acts = []
    for j, ref in enumerate((q_ref, k_ref, v_ref)):
        ext_ref[j, SUBLANE:SUBLANE + chunk, :] = ref[...]
        acc = None
        for i in range(CONV_W):
            lo = SUBLANE - (CONV_W - 1) + i
            term = ext_ref[j, lo:lo + chunk, :] * cw_ref[i:i + 1, j * B_WIDTH:(j + 1) * B_WIDTH]
            acc = term if acc is None else acc + term
        acts.append(jax.nn.silu(acc))
        ext_ref[j, 0:SUBLANE, :] = ext_ref[j, chunk:chunk + SUBLANE, :]
    qa, ka, va = acts

    sm = sm_ref[...]
    row1 = lax.broadcasted_iota(I32, (chunk, LANE), 0)
    g_all = -jnp.exp(al_ref[...]) * jax.nn.softplus(sm + dt_ref[...])
    beta_all = jax.nn.sigmoid(sm)
    if t_valid < chunk:
        g_all = jnp.where(row1 < t_valid, g_all, 0.0)
        beta_all = jnp.where(row1 < t_valid, beta_all, 0.0)
    gc_all = g_all
    d = 1
    while d < chunk:
        gc_all = gc_all + jnp.where(row1 >= d, pltpu.roll(gc_all, d, 0), 0.0)
        d *= 2
    gc_t = _transpose_rows(gc_all)

    row = lax.broadcasted_iota(I32, (chunk, chunk), 0)
    col = lax.broadcasted_iota(I32, (chunk, chunk), 1)
    incl = row >= col
    strict = row > col
    eye = jnp.where(row == col, 1.0, 0.0)
    pair_masks = []
    bs = 1
    while bs < chunk:
        pair_masks.append(((row // bs) % 2 == 1) & ((col // bs) == (row // bs) - 1))
        bs *= 2

    heads = range(B_HEADS)
    per_head = lambda x: jnp.stack([x[:, h * hd:(h + 1) * hd] for h in heads], axis=0)
    q3, k3, v3 = per_head(qa), per_head(ka), per_head(va)
    qn = q3 * lax.rsqrt(jnp.sum(q3 * q3, axis=-1, keepdims=True) + 1e-6) * (hd ** -0.5)
    kn = k3 * lax.rsqrt(jnp.sum(k3 * k3, axis=-1, keepdims=True) + 1e-6)
    beta = jnp.stack([beta_all[:, B_HEADS + h:B_HEADS + h + 1] for h in heads], axis=0)
    gcol = jnp.stack([gc_all[:, h:h + 1] for h in heads], axis=0)
    grow = jnp.stack([gc_t[h:h + 1, :] for h in heads], axis=0)
    decay = jnp.exp(jnp.where(incl[None], gcol - grow, -jnp.inf))
    kb = kn * beta
    eg = jnp.exp(gcol)
    lmat = jnp.where(strict[None], _bdot_nt(kb, kn) * decay, 0.0)
    attn = _bdot_nt(qn, kn) * decay
    tinv = eye[None] - jnp.where(pair_masks[0][None], lmat, 0.0)
    for pm in pair_masks[1:]:
        tinv = tinv - _bdot(tinv, _bdot(jnp.where(pm[None], lmat, 0.0), tinv))
    sol = _bdot(tinv, jnp.concatenate([v3 * beta, kb * eg], axis=-1))
    value, kcd = sol[:, :, :hd], sol[:, :, hd:]
    s_old = s_ref[...]
    vnew = value - _bdot(kcd, s_old)
    o = _bdot(qn * eg, s_old) + _bdot(attn, vnew)
    glast = gcol[:, chunk - 1:chunk, :]
    kend = kn * jnp.exp(glast - gcol)
    s_ref[...] = s_old * jnp.exp(glast) + _bdot(jnp.swapaxes(kend, 1, 2), vnew)
    on = o * lax.rsqrt(jnp.mean(o * o, axis=-1, keepdims=True) + RMS_EPS) * og_ref[...]
    for h in heads:
        sl = slice(h * hd, (h + 1) * hd)
        y_ref[:, sl] = (on[h] * jax.nn.silu(z_ref[:, sl])).astype(y_ref.dtype)

    @pl.when(c == nc - 1)
    def _():
        sout_ref[0] = s_ref[...]


def _gdn(srcs, conv0, s0, conv_w, a_log, dt_bias, o_g, *, n_b, t_pad, t_valid, chunk):
    nc = t_pad // chunk
    arrs = [a for a, _ in srcs]
    cbs = [cb for _, cb in srcs]
    widths = [B_WIDTH] * 4 + [LANE]

    def tok_spec(w, cb):
        return pl.BlockSpec((chunk, w), lambda b, c: (b * nc + c, cb))

    pad12 = lambda x: jnp.zeros((1, LANE), F32).at[0, :B_HEADS].set(x)
    return pl.pallas_call(
        functools.partial(_gdn_body, chunk=chunk, t_valid=t_valid),
        out_shape=[jax.ShapeDtypeStruct((n_b * t_pad, B_WIDTH), BF16),
                   jax.ShapeDtypeStruct((n_b, B_HEADS, HEAD_DIM, HEAD_DIM), F32)],
        grid=(n_b, nc),
        in_specs=[tok_spec(w, cb) for w, cb in zip(widths, cbs)] + [
            pl.BlockSpec((1, SUBLANE, 3 * B_WIDTH), lambda b, c: (b, 0, 0)),
            pl.BlockSpec((1, B_HEADS, HEAD_DIM, HEAD_DIM), lambda b, c: (b, 0, 0, 0)),
            pl.BlockSpec((CONV_W, 3 * B_WIDTH), lambda b, c: (0, 0)),
            pl.BlockSpec((1, LANE), lambda b, c: (0, 0)),
            pl.BlockSpec((1, LANE), lambda b, c: (0, 0)),
            pl.BlockSpec((1, HEAD_DIM), lambda b, c: (0, 0))],
        out_specs=[pl.BlockSpec((chunk, B_WIDTH), lambda b, c: (b * nc + c, 0)),
                   pl.BlockSpec((1, B_HEADS, HEAD_DIM, HEAD_DIM), lambda b, c: (b, 0, 0, 0))],
        scratch_shapes=[pltpu.VMEM((3, chunk + SUBLANE, B_WIDTH), F32),
                        pltpu.VMEM((B_HEADS, HEAD_DIM, HEAD_DIM), F32)],
        compiler_params=_cparams(("parallel", "arbitrary"), 40 << 20),
        name="branch_b_gated_delta",
    )(*arrs, conv0, s0, conv_w, pad12(a_log), pad12(dt_bias), o_g.reshape(1, HEAD_DIM))


def _sort_key(x):
    b = lax.bitcast_convert_type(x + 0.0, I32)
    return b ^ ((b >> 31) & 0x7FFFFFFF)


def _select_rule(count, shape, k, nbits):
    kf = jnp.float32(k)
    count_ge = lambda cand: count(lambda key, pos: key >= cand)
    t0 = jnp.where(count_ge(jnp.zeros(shape, I32)) >= kf, 0, INT_MIN).astype(I32)

    def bit_step(i, t):
        cand = t + lax.shift_left(jnp.int32(1), 30 - i)
        return jnp.where(count_ge(cand) >= kf, cand, t)

    t = lax.fori_loop(0, 31, bit_step, t0)
    n_ge = count_ge(t)
    tied = (n_ge > kf) & (t > KEY_OF_NEG_INF)

    def tie_index():
        r = kf - count(lambda key, pos: key > t)

        def idx_step(i, j):
            cand = j + lax.shift_left(jnp.int32(1), nbits - 1 - i)
            below = count(lambda key, pos: (key == t) & (pos <= cand - 1))
            return jnp.where(below < r, cand, j)

        return lax.fori_loop(0, nbits, idx_step, jnp.zeros(shape, I32))

    no_limit = jnp.full(shape, 2 ** 30, I32)
    j = lax.cond(jnp.sum(jnp.where(tied, 1.0, 0.0)) > 0.0,
                 lambda: jnp.where(tied, tie_index(), no_limit), lambda: no_limit)
    return t, j


def _selected(key, pos, t, j):
    return (key > t) | ((key == t) & (pos <= j))


def _dsa_prompt_body(cq_ref, iq0_ref, iq1_ref, iq2_ref, iq3_ref, sm_ref, ik_ref, ck_ref, cv_ref, y_ref,
                     key_ref, iqb_ref, wb_ref, qb_ref, m_ref, l_ref, acc_ref, p_ref, al_ref,
                     *, k_sel, nbits, tq, tk):
    rows3 = C_GROUP * tq
    qi = pl.program_id(1)
    kpg = tk // LANE
    nkb = (qi + 1) * (tq // LANE)
    ngrp = ((qi + 1) * tq + tk - 1) // tk
    hd = HEAD_DIM
    nt = (((1,), (1,)), ((), ()))
    sub = LANE // SUBLANE
    qpos3 = qi * tq + lax.broadcasted_iota(I32, (sub, SUBLANE, tq), 2)
    kofs3 = lax.broadcasted_iota(I32, (sub, SUBLANE, tq), 0) * SUBLANE + lax.broadcasted_iota(I32, (sub, SUBLANE, tq), 1)

    w_all = sm_ref[...] * (IDX_HEADS ** -0.5 * IDX_DIM ** -0.5)
    w_t = jnp.concatenate([w_all[r:r + LANE].T for r in range(0, tq, LANE)], axis=1)
    w_off = 2 * B_HEADS
    for hh in range(IDX_HEADS):
        ref = (iq0_ref, iq1_ref, iq2_ref, iq3_ref)[hh // 4]
        iqb_ref[hh] = ref[:, (hh % 4) * IDX_DIM:(hh % 4 + 1) * IDX_DIM].astype(BF16)
        wb_ref[hh] = jnp.broadcast_to(w_t[w_off + hh:w_off + hh + 1, :], (SUBLANE, tq))

    def score_grp(g, carry):
        ikg = ik_ref[pl.ds(pl.multiple_of(g * tk, tk), tk), :].astype(BF16)
        acc = jnp.zeros((tk // SUBLANE, SUBLANE, tq), F32)
        for hh in range(IDX_HEADS):
            s = lax.dot_general(ikg, iqb_ref[hh], nt, preferred_element_type=F32)
            acc = acc + wb_ref[hh] * jnp.maximum(s, 0.0).reshape(tk // SUBLANE, SUBLANE, tq)
        for t in range(kpg):
            kpos3 = (g * kpg + t) * LANE + kofs3
            blk = acc[t * sub:(t + 1) * sub]
            key_ref[g * kpg + t] = _sort_key(jnp.where(kpos3 <= qpos3, blk, -jnp.inf))
        return carry

    lax.fori_loop(0, ngrp, score_grp, 0)

    bpt = tq // LANE

    def count(pred):
        def trip(i, acc):
            for u in range(bpt):
                b = i * bpt + u
                acc = acc + jnp.sum(jnp.where(pred(key_ref[b], b * LANE + kofs3), 1.0, 0.0), axis=0)
            return acc
        acc = lax.fori_loop(0, qi + 1, trip, jnp.zeros((SUBLANE, tq), F32))
        return jnp.broadcast_to(jnp.sum(acc, axis=0, keepdims=True), (SUBLANE, tq))

    t8, j8 = _select_rule(count, (SUBLANE, tq), k_sel, nbits)

    for kvh in range(C_KV_HEADS):
        for g in range(C_GROUP):
            hsl = slice((kvh * C_GROUP + g) * hd, (kvh * C_GROUP + g + 1) * hd)
            qb_ref[kvh, g * tq:(g + 1) * tq, :] = cq_ref[:, hsl].astype(BF16)
    m_ref[...] = jnp.full(m_ref.shape, MASK_NEG, F32)
    l_ref[...] = jnp.zeros(l_ref.shape, F32)
    acc_ref[...] = jnp.zeros(acc_ref.shape, F32)
    ones = jnp.ones((tk, hd), BF16)

    def attend_grp(g, carry):
        bias_t = []
        for t in range(kpg):
            kpos3 = (g * kpg + t) * LANE + kofs3
            sel = _selected(key_ref[g * kpg + t], kpos3, t8, j8) & (kpos3 <= qpos3)
            kq = jnp.where(sel, 0.0, MASK_NEG).reshape(LANE, tq)
            bias_t.append(jnp.concatenate([kq[:, r:r + LANE].T for r in range(0, tq, LANE)], axis=0))
        bias = jnp.concatenate(bias_t, axis=1)
        start = pl.multiple_of(g * tk, tk)
        kgrp = ck_ref[pl.ds(start, tk), :].astype(BF16)
        vgrp = cv_ref[pl.ds(start, tk), :].astype(BF16)
        for kvh in range(C_KV_HEADS):
            sl = slice(kvh * hd, (kvh + 1) * hd)
            s = lax.dot_general(qb_ref[kvh], kgrp[:, sl], nt, preferred_element_type=F32)
            for r0 in range(0, rows3, DSA_ROW_TILE):
                rs = slice(r0, r0 + DSA_ROW_TILE)
                x = s[rs] * (hd ** -0.5 * LOG2E) + bias[r0 % tq:r0 % tq + DSA_ROW_TILE]
                m_old = m_ref[kvh, rs, :]
                m_new = jnp.maximum(m_old, jnp.max(x, axis=-1, keepdims=True))
                p_ref[rs, :] = jnp.exp2(x - m_new).astype(BF16)
                al_ref[rs, :] = jnp.exp2(m_old - m_new)
                m_ref[kvh, rs, :] = m_new
            pv = jnp.dot(p_ref[...], jnp.concatenate([vgrp[:, sl], ones], axis=1), preferred_element_type=F32)
            alpha = al_ref[...]
            l_ref[kvh] = alpha * l_ref[kvh] + pv[:, hd:hd + 1]
            acc_ref[kvh] = alpha * acc_ref[kvh] + pv[:, :hd]
        return carry

    lax.fori_loop(0, ngrp, attend_grp, 0)

    for kvh in range(C_KV_HEADS):
        o = acc_ref[kvh] / l_ref[kvh]
        for g in range(C_GROUP):
            hsl = slice((kvh * C_GROUP + g) * hd, (kvh * C_GROUP + g + 1) * hd)
            y_ref[:, hsl] = o[g * tq:(g + 1) * tq, :].astype(y_ref.dtype)


def _dsa_prompt(h, lay, *, n_b, seq):
    tq, tk = DSA_TQ, DSA_TK
    assert seq % tq == 0 and seq % tk == 0
    nqb = seq // tq
    k_sel = min(TOPK_MAX, seq // 4)
    nbits = max(1, (seq - 1).bit_length())
    iq_cb = lay["iq"] // (4 * IDX_DIM)
    once = pl.Buffered(1)

    def q_spec(w, cb):
        return pl.BlockSpec((tq, w), lambda b, q: (b * nqb + q, cb))

    def kv_spec(w, cb):
        return pl.BlockSpec((seq, w), lambda b, q: (b, cb), pipeline_mode=once)

    rows3 = C_GROUP * tq
    lane_pad = lambda r: r * LANE * 4
    vmem = (seq * (2 * C_KV + IDX_DIM) * 4 + 2 * tq * (C_Q + IDX_HEADS * IDX_DIM + LANE) * 4 + 2 * tq * C_Q * 2
            + (seq // LANE) * tq * LANE * 4 + IDX_HEADS * tq * LANE * 6 + C_KV_HEADS * rows3 * HEAD_DIM * 6
            + 2 * C_KV_HEADS * lane_pad(rows3) + 4 * rows3 * tk * 4)
    return pl.pallas_call(
        functools.partial(_dsa_prompt_body, k_sel=k_sel, nbits=nbits, tq=tq, tk=tk),
        out_shape=jax.ShapeDtypeStruct((n_b * seq, C_Q), BF16),
        grid=(n_b, nqb),
        in_specs=[q_spec(C_Q, lay["cq"] // C_Q)]
                 + [q_spec(4 * IDX_DIM, iq_cb + i) for i in range(4)]
                 + [q_spec(LANE, lay["small"] // LANE),
                    kv_spec(IDX_DIM, lay["ik"] // IDX_DIM),
                    kv_spec(C_KV, lay["ck"] // C_KV),
                    kv_spec(C_KV, lay["cv"] // C_KV)],
        out_specs=pl.BlockSpec((tq, C_Q), lambda b, q: (b * nqb + q, 0)),
        scratch_shapes=[pltpu.VMEM((seq // LANE, LANE // SUBLANE, SUBLANE, tq), I32),
                        pltpu.VMEM((IDX_HEADS, tq, IDX_DIM), BF16),
                        pltpu.VMEM((IDX_HEADS, SUBLANE, tq), F32),
                        pltpu.VMEM((C_KV_HEADS, rows3, HEAD_DIM), BF16),
                        pltpu.VMEM((C_KV_HEADS, rows3, 1), F32),
                        pltpu.VMEM((C_KV_HEADS, rows3, 1), F32),
                        pltpu.VMEM((C_KV_HEADS, rows3, HEAD_DIM), F32),
                        pltpu.VMEM((rows3, tk), BF16),
                        pltpu.VMEM((rows3, 1), F32)],
        compiler_params=_cparams(("parallel", "arbitrary"), vmem + COMPILER_SCRATCH_BYTES),
        name="branch_c_prompt_dsa",
    )(h, h, h, h, h, h, h, h, h)


def _dsa_sample_select_body(pt_ref, *refs, n_pages, k_sel, nbits, t_new):
    pp = SELECT_PAGES_PER_STEP
    page_refs = refs[:pp]
    iq_ref, w_ref, ikn_ref, bias_ref, key_ref = refs[pp:]
    j = pl.program_id(1)
    n_steps = n_pages // pp
    rows = t_new
    col = lax.broadcasted_iota(I32, (rows, LANE), 1)
    trow = lax.broadcasted_iota(I32, (rows, LANE), 0)
    iq = iq_ref[0].astype(BF16)

    def scores(keys_f32):
        n = keys_f32.shape[0]
        s = lax.dot_general(iq, keys_f32.astype(BF16), (((1,), (1,)), ((), ())), preferred_element_type=F32)
        r = jnp.maximum(s, 0.0) * w_ref[0]
        return jnp.sum(r.reshape(rows, IDX_HEADS, n), axis=1)

    @pl.when(j < n_steps)
    def _():
        keys = _sort_key(scores(jnp.concatenate([r[0, 0] for r in page_refs], axis=0)))
        for i in range(pp):
            key_ref[j * pp + i] = keys[:, i * LANE:(i + 1) * LANE]

    @pl.when(j == n_steps)
    def _():
        new_ok = (col <= trow) & (col < t_new)
        key_ref[n_pages] = _sort_key(jnp.where(new_ok, scores(ikn_ref[0]), -jnp.inf))
        nblk = n_pages + 1
        grp = SUBLANE
        nblk_pad = key_ref.shape[0]
        if nblk_pad > nblk:
            key_ref[nblk:nblk_pad] = jnp.full((nblk_pad - nblk, rows, LANE), KEY_OF_NEG_INF, I32)
        blk3 = lax.broadcasted_iota(I32, (grp, rows, LANE), 0)
        col3 = lax.broadcasted_iota(I32, (grp, rows, LANE), 2)

        def count(pred):
            def trip(i, acc):
                b0 = pl.multiple_of(i * grp, grp)
                hit = pred(key_ref[pl.ds(b0, grp)], (b0 + blk3) * LANE + col3)
                return acc + jnp.sum(jnp.where(hit, 1.0, 0.0), axis=0)
            acc = lax.fori_loop(0, nblk_pad // grp, trip, jnp.zeros((rows, LANE), F32))
            return jnp.broadcast_to(jnp.sum(acc, axis=1, keepdims=True), (rows, LANE))

        tb, jb = _select_rule(count, (rows, LANE), k_sel, nbits)

        def write_blk(b, carry):
            kpos = b * LANE + col
            sel = _selected(key_ref[b], kpos, tb, jb) & ((kpos < n_pages * PAGE_SIZE) | new_ok)
            bias_ref[0, b] = jnp.where(sel, 0.0, MASK_NEG)
            return carry

        lax.fori_loop(0, nblk, write_blk, 0)


def _page_index_map(layer, i, n_pages, n_steps, trailing, pp):
    def index_map(b, j, pt):
        step = jnp.minimum(j, n_steps - 1)
        return (layer, pt[b * n_pages + step * pp + i]) + (0,) * trailing
    return index_map


def _dsa_sample_select(pt_flat, kidx_pool, layer, iq_s, iw_s, ik_new, *, n_b, n_pages, t_new):
    pp = SELECT_PAGES_PER_STEP
    assert n_pages % pp == 0
    n_steps = n_pages // pp
    total = n_pages * PAGE_SIZE + t_new
    k_sel = min(TOPK_MAX, total // 4)
    nbits = max(1, ((n_pages + 1) * PAGE_SIZE - 1).bit_length())
    rows_q = t_new * IDX_HEADS
    grid_spec = pltpu.PrefetchScalarGridSpec(
        num_scalar_prefetch=1,
        grid=(n_b, n_steps + 1),
        in_specs=[pl.BlockSpec((1, 1, PAGE_SIZE, IDX_DIM), _page_index_map(layer, i, n_pages, n_steps, 2, pp))
                  for i in range(pp)]
                 + [pl.BlockSpec((1, rows_q, IDX_DIM), lambda b, j, pt: (b, 0, 0)),
                    pl.BlockSpec((1, rows_q, 1), lambda b, j, pt: (b, 0, 0)),
                    pl.BlockSpec((1, PAGE_SIZE, IDX_DIM), lambda b, j, pt: (b, 0, 0))],
        out_specs=pl.BlockSpec((1, n_pages + 1, t_new, LANE), lambda b, j, pt: (b, 0, 0, 0)),
        scratch_shapes=[pltpu.VMEM((-(-(n_pages + 1) // SUBLANE) * SUBLANE, t_new, LANE), I32)])
    return pl.pallas_call(
        functools.partial(_dsa_sample_select_body, n_pages=n_pages, k_sel=k_sel, nbits=nbits, t_new=t_new),
        out_shape=jax.ShapeDtypeStruct((n_b, n_pages + 1, t_new, LANE), F32),
        grid_spec=grid_spec,
        compiler_params=_cparams(("parallel", "arbitrary"), 24 << 20),
        name="branch_c_decode_select",
    )(pt_flat, *([kidx_pool] * pp), iq_s, iw_s, ik_new)


def _dsa_sample_attend_body(pt_ref, *refs, n_pages, t_new):
    pp = PAGES_PER_STEP
    k_refs, v_refs = refs[:pp], refs[pp:2 * pp]
    q_ref, bias_ref, kn_ref, vn_ref, o_ref, m_ref, l_ref, acc_ref = refs[2 * pp:]
    j = pl.program_id(1)
    n_steps = n_pages // pp
    hd = HEAD_DIM
    rq = q_ref.shape[1] // C_KV_HEADS
    reps = rq // t_new

    @pl.when(j == 0)
    def _():
        m_ref[...] = jnp.full(m_ref.shape, MASK_NEG, F32)
        l_ref[...] = jnp.zeros(l_ref.shape, F32)
        acc_ref[...] = jnp.zeros(acc_ref.shape, F32)

    def attend(k_of, v_of, bias_t):
        bias = jnp.concatenate([bias_t] * reps, axis=0)
        for kvh in range(C_KV_HEADS):
            rs = slice(kvh * rq, (kvh + 1) * rq)
            s = lax.dot_general(q_ref[0, rs, :].astype(BF16), k_of(kvh).astype(BF16), (((1,), (1,)), ((), ())),
                                preferred_element_type=F32) * (hd ** -0.5) + bias
            m_old = m_ref[rs, :]
            m_new = jnp.maximum(m_old, jnp.max(s, axis=-1, keepdims=True))
            alpha = jnp.exp(m_old - m_new)
            p = jnp.exp(s - m_new)
            l_ref[rs, :] = alpha * l_ref[rs, :] + jnp.sum(p, axis=-1, keepdims=True)
            acc_ref[rs, :] = alpha * acc_ref[rs, :] + jnp.dot(p.astype(BF16), v_of(kvh).astype(BF16),
                                                              preferred_element_type=F32)
            m_ref[rs, :] = m_new

    @pl.when(j < n_steps)
    def _():
        head_rows = lambda kvh: pl.ds(kvh, PAGE_SIZE, stride=C_KV_HEADS)
        pages = lambda prefs: (lambda kvh: jnp.concatenate([r[0, 0, head_rows(kvh), :] for r in prefs], axis=0))
        bias_t = jnp.concatenate([bias_ref[0, j * pp + i] for i in range(pp)], axis=1)
        attend(pages(k_refs), pages(v_refs), bias_t)

    @pl.when(j == n_steps)
    def _():
        new = lambda ref: (lambda kvh: ref[0, :, kvh * hd:(kvh + 1) * hd])
        attend(new(kn_ref), new(vn_ref), bias_ref[0, n_pages])
        o_ref[0] = acc_ref[...] / l_ref[...]


def _dsa_sample_attend(pt_flat, k_pool, v_pool, layer, q_s, bias, k_new, v_new, *, n_b, n_pages, t_new):
    pp = PAGES_PER_STEP
    n_steps = n_pages // pp
    rows = q_s.shape[1]
    page_specs = [pl.BlockSpec((1, 1, PAGE_SIZE * C_KV_HEADS, HEAD_DIM),
                               _page_index_map(layer, i, n_pages, n_steps, 2, pp)) for i in range(pp)]
    grid_spec = pltpu.PrefetchScalarGridSpec(
        num_scalar_prefetch=1,
        grid=(n_b, n_steps + 1),
        in_specs=page_specs + page_specs
                 + [pl.BlockSpec((1, rows, HEAD_DIM), lambda b, j, pt: (b, 0, 0)),
                    pl.BlockSpec((1, n_pages + 1, t_new, LANE), lambda b, j, pt: (b, 0, 0, 0)),
                    pl.BlockSpec((1, PAGE_SIZE, C_KV), lambda b, j, pt: (b, 0, 0)),
                    pl.BlockSpec((1, PAGE_SIZE, C_KV), lambda b, j, pt: (b, 0, 0))],
        out_specs=pl.BlockSpec((1, rows, HEAD_DIM), lambda b, j, pt: (b, 0, 0)),
        scratch_shapes=[pltpu.VMEM((rows, 1), F32), pltpu.VMEM((rows, 1), F32), pltpu.VMEM((rows, HEAD_DIM), F32)])
    return pl.pallas_call(
        functools.partial(_dsa_sample_attend_body, n_pages=n_pages, t_new=t_new),
        out_shape=jax.ShapeDtypeStruct((n_b, rows, HEAD_DIM), F32),
        grid_spec=grid_spec,
        compiler_params=_cparams(("parallel", "arbitrary"), 32 << 20),
        name="branch_c_decode_attend",
    )(pt_flat, *([k_pool] * pp), *([v_pool] * pp), q_s, bias, k_new, v_new)


def _dsa_sample(h_s, lay, pt_flat, k_pool, v_pool, kidx_pool, layer, *, n_b, t_new, n_pages):
    hd = HEAD_DIM
    seg = lambda name, w: h_s[:, lay[name]:lay[name] + w]
    rq = -(-C_GROUP * t_new // 16) * 16
    q = seg("cq", C_Q).reshape(n_b, t_new, C_KV_HEADS, C_GROUP, hd).transpose(0, 2, 3, 1, 4)
    q = q.reshape(n_b, C_KV_HEADS, C_GROUP * t_new, hd)
    q_s = jnp.pad(q, ((0, 0), (0, 0), (0, rq - C_GROUP * t_new), (0, 0))).reshape(n_b, C_KV_HEADS * rq, hd)
    iq_s = seg("iq", IDX_HEADS * IDX_DIM).reshape(n_b, t_new * IDX_HEADS, IDX_DIM)
    iw = h_s[:, lay["small"] + 2 * B_HEADS:lay["small"] + 2 * B_HEADS + IDX_HEADS]
    iw_s = (iw * (IDX_HEADS ** -0.5 * IDX_DIM ** -0.5)).reshape(n_b, t_new * IDX_HEADS, 1)
    padrows = lambda x: jnp.pad(x.reshape(n_b, t_new, -1), ((0, 0), (0, PAGE_SIZE - t_new), (0, 0)))
    ik_new, k_new, v_new = padrows(seg("ik", IDX_DIM)), padrows(seg("ck", C_KV)), padrows(seg("cv", C_KV))
    bias = _dsa_sample_select(pt_flat, kidx_pool, layer, iq_s, iw_s, ik_new, n_b=n_b, n_pages=n_pages, t_new=t_new)
    o = _dsa_sample_attend(pt_flat, k_pool, v_pool, layer, q_s, bias, k_new, v_new,
                           n_b=n_b, n_pages=n_pages, t_new=t_new)
    o = o.reshape(n_b, C_KV_HEADS, rq, hd)[:, :, :C_GROUP * t_new].reshape(n_b, C_KV_HEADS, C_GROUP, t_new, hd)
    return o.transpose(0, 3, 1, 2, 4).reshape(n_b * t_new, C_Q).astype(BF16)


def _merge_body(ya_ref, yb_ref, yc_ref, w_ref, g0_ref, g1_ref, g2_ref, o_ref):
    b0, c0 = A_WIDTH, A_WIDTH + B_WIDTH
    acc = jax.nn.sigmoid(g0_ref[...]) * jnp.dot(ya_ref[...], w_ref[0:b0, :], preferred_element_type=F32)
    acc = acc + jax.nn.sigmoid(g1_ref[...]) * jnp.dot(yb_ref[...], w_ref[b0:c0, :], preferred_element_type=F32)
    acc = acc + jax.nn.sigmoid(g2_ref[...]) * jnp.dot(yc_ref[...], w_ref[c0:, :], preferred_element_type=F32)
    o_ref[...] = acc.astype(o_ref.dtype)


def _merge(ya, yb, yc, w_br, layer, h, lay, d_model):
    m = ya.shape[0]
    tm = _pick(m, 1024)
    tn = _pick(d_model, 512)
    gcb = lay["gate"] // tn
    per = d_model // tn

    def gate_spec(i):
        return pl.BlockSpec((tm, tn), lambda r, c: (r, gcb + i * per + c))

    kw = A_WIDTH + B_WIDTH + C_Q
    vmem = 2 * (tm * kw * 2 + kw * tn * 2 + 3 * tm * tn * 4 + tm * tn * 2) + 4 * tm * tn * 4
    return pl.pallas_call(
        _merge_body,
        out_shape=jax.ShapeDtypeStruct((m, d_model), BF16),
        grid=(m // tm, d_model // tn),
        in_specs=[pl.BlockSpec((tm, A_WIDTH), lambda r, c: (r, 0)),
                  pl.BlockSpec((tm, B_WIDTH), lambda r, c: (r, 0)),
                  pl.BlockSpec((tm, C_Q), lambda r, c: (r, 0)),
                  _layer_spec((kw, tn), lambda r, c: (0, c), layer),
                  gate_spec(0), gate_spec(1), gate_spec(2)],
        out_specs=pl.BlockSpec((tm, tn), lambda r, c: (r, c)),
        compiler_params=_cparams(("parallel", "arbitrary"), vmem + COMPILER_SCRATCH_BYTES),
        name="gated_merge",
    )(ya, yb, yc, w_br, h, h, h)


def _ffn_up_body(hn_ref, w1_ref, w3_ref, o_ref):
    hn = hn_ref[...]
    a = jnp.dot(hn, w1_ref[...].astype(BF16), preferred_element_type=F32)
    b = jnp.dot(hn, w3_ref[...].astype(BF16), preferred_element_type=F32)
    o_ref[...] = (jax.nn.silu(a) * b).astype(o_ref.dtype)


def _ffn(hn, w1, w3, w2, layer, x):
    m, d = hn.shape
    d_ff = w1.shape[-1]
    tm = _pick(m, 1024)
    tf = FFN_TF
    assert d_ff % tf == 0
    vmem = 2 * (tm * d * 2 + 2 * d * tf * 4 + tm * tf * 2) + 2 * d * tf * 2 + 3 * tm * tf * 4
    act = pl.pallas_call(
        _ffn_up_body,
        out_shape=jax.ShapeDtypeStruct((m, d_ff), BF16),
        grid=(m // tm, d_ff // tf),
        in_specs=[pl.BlockSpec((tm, d), lambda i, f: (i, 0)),
                  _layer_spec((d, tf), lambda i, f: (0, f), layer),
                  _layer_spec((d, tf), lambda i, f: (0, f), layer)],
        out_specs=pl.BlockSpec((tm, tf), lambda i, f: (i, f)),
        compiler_params=_cparams(("parallel", "arbitrary"), vmem + COMPILER_SCRATCH_BYTES),
        name="swiglu_up",
    )(hn, w1, w3)
    return _matmul(act, w2, layer=layer, residual=x, tm_pref=1024, tn_pref=256, a_single_buffer=True,
                   name="swiglu_down")


def _dense_front(x, ln1, w_in_p, layer):
    xn = _rmsnorm(x, ln1, BF16)
    return _matmul(xn, w_in_p, layer=layer, w_transposed=True, tn_pref=768, name="in_proj")


def _dense_back(x, ya, yb, yc, h, lay, wts, layer, d_model):
    w_br, w_o, w1, w3, w2, ln2 = wts
    mix = _merge(ya, yb, yc, w_br, layer, h, lay, d_model)
    x = _matmul(mix, w_o, layer=layer, residual=x, name="out_proj")
    hn = _rmsnorm(x, ln2, BF16)
    return _ffn(hn, w1, w3, w2, layer, x)


def kernel(x_prompt, x_sample, cache_k, cache_v, cache_kidx, state_conv, state_delta, page_table, ln1, w_in,
           a_ln_g, a_ln_b, a_ws, a_bs, b_conv_w, b_a_log, b_dt_bias, b_out_g, w_br, w_o, ln2, ffn_w1, ffn_w3,
           ffn_w2, ln_f):
    n_bp, seq, d_model = x_prompt.shape
    n_bs, t_new, _ = x_sample.shape
    depth = ln1.shape[0]
    n_phys = cache_k.shape[1]
    n_pages = page_table.shape[1]
    assert seq % A_CHUNK == 0 and seq % GDN_CHUNK == 0 and CONV_W - 1 <= t_new <= min(A_CHUNK, GDN_DECODE_CHUNK)
    lay = _layout(d_model)
    hd = HEAD_DIM

    xp = x_prompt.reshape(n_bp * seq, d_model)
    xs = x_sample.reshape(n_bs * t_new, d_model)
    pt_flat = page_table.reshape(-1).astype(I32)
    k_pool = cache_k.reshape(depth, n_phys, PAGE_SIZE * C_KV_HEADS, hd)
    v_pool = cache_v.reshape(depth, n_phys, PAGE_SIZE * C_KV_HEADS, hd)
    conv0_p = jnp.zeros((n_bp, SUBLANE, 3 * B_WIDTH), F32)
    s0_p = jnp.zeros((n_bp, B_HEADS, hd, hd), F32)
    qkv_off = lay["bq"]

    outs = {k: [] for k in ("pk", "pv", "pik", "pconv", "pdelta", "sk", "sv", "sik", "sconv", "sdelta", "schunk")}
    dense_w = (w_br.astype(BF16), w_o.astype(BF16), ffn_w1, ffn_w3, ffn_w2.astype(BF16))
    w_in_p = _pack_w_in(w_in, d_model)
    for l in range(depth):
        wts = dense_w + (ln2[l],)

        h = _dense_front(xp, ln1[l], w_in_p, l)
        ya = _branch_a(h, lay, a_ws[l], a_bs[l], a_ln_g[l], a_ln_b[l], chunk=A_CHUNK, tb=A_CHUNK, emit_av=False)
        srcs = [(h, lay[n] // B_WIDTH) for n in ("bq", "bk", "bv", "bz")] + [(h, lay["small"] // LANE)]
        yb, s_new = _gdn(srcs, conv0_p, s0_p, b_conv_w[l], b_a_log[l], b_dt_bias[l], b_out_g[l],
                         n_b=n_bp, t_pad=seq, t_valid=GDN_CHUNK, chunk=GDN_CHUNK)
        yc = _dsa_prompt(h, lay, n_b=n_bp, seq=seq)
        h3 = h.reshape(n_bp, seq, lay["total"])
        outs["pk"].append(h3[:, :, lay["ck"]:lay["ck"] + C_KV].reshape(n_bp, seq, C_KV_HEADS, hd))
        outs["pv"].append(h3[:, :, lay["cv"]:lay["cv"] + C_KV].reshape(n_bp, seq, C_KV_HEADS, hd))
        outs["pik"].append(h3[:, :, lay["ik"]:lay["ik"] + IDX_DIM])
        outs["pconv"].append(h3[:, seq - (CONV_W - 1):, qkv_off:qkv_off + 3 * B_WIDTH])
        outs["pdelta"].append(s_new)
        xp = _dense_back(xp, ya, yb, yc, h, lay, wts, l, d_model)

        hs = _dense_front(xs, ln1[l], w_in_p, l)
        ya, av = _branch_a(hs, lay, a_ws[l], a_bs[l], a_ln_g[l], a_ln_b[l], chunk=t_new, tb=n_bs * t_new,
                           emit_av=True)
        hs3 = hs.reshape(n_bs, t_new, lay["total"])
        dc = GDN_DECODE_CHUNK
        padt = lambda x: jnp.pad(x, ((0, 0), (0, dc - t_new), (0, 0))).reshape(n_bs * dc, -1)
        srcs = [(padt(hs3[:, :, lay[n]:lay[n] + B_WIDTH]), 0) for n in ("bq", "bk", "bv", "bz")]
        srcs.append((padt(hs3[:, :, lay["small"]:lay["small"] + LANE]), 0))
        conv0_s = jnp.pad(state_conv[l], ((0, 0), (SUBLANE - (CONV_W - 1), 0), (0, 0)))
        yb, s_new = _gdn(srcs, conv0_s, state_delta[l], b_conv_w[l], b_a_log[l], b_dt_bias[l], b_out_g[l],
                         n_b=n_bs, t_pad=dc, t_valid=t_new, chunk=dc)
        yb = yb.reshape(n_bs, dc, B_WIDTH)[:, :t_new].reshape(n_bs * t_new, B_WIDTH)
        yc = _dsa_sample(hs, lay, pt_flat, k_pool, v_pool, cache_kidx, l, n_b=n_bs, t_new=t_new, n_pages=n_pages)
        outs["sk"].append(hs3[:, :, lay["ck"]:lay["ck"] + C_KV].reshape(n_bs, t_new, C_KV_HEADS, hd))
        outs["sv"].append(hs3[:, :, lay["cv"]:lay["cv"] + C_KV].reshape(n_bs, t_new, C_KV_HEADS, hd))
        outs["sik"].append(hs3[:, :, lay["ik"]:lay["ik"] + IDX_DIM])
        outs["sconv"].append(hs3[:, t_new - (CONV_W - 1):, qkv_off:qkv_off + 3 * B_WIDTH])
        outs["sdelta"].append(s_new)
        outs["schunk"].append(av.reshape(n_bs, t_new, A_WIDTH))
        xs = _dense_back(xs, ya, yb, yc, hs, lay, wts, l, d_model)

    y_prompt = _rmsnorm(xp, ln_f, F32).reshape(n_bp, seq, d_model)
    y_sample = _rmsnorm(xs, ln_f, F32).reshape(n_bs, t_new, d_model)
    st = lambda k: jnp.stack(outs[k])
    return (y_prompt, y_sample, st("pk"), st("pv"), st("pik"), st("pconv"), st("pdelta"),
            st("sk"), st("sv"), st("sik"), st("sconv"), st("sdelta"), st("schunk"))
```

```python
import functools

import jax
import jax.numpy as jnp
from jax import lax
from jax.experimental import pallas as pl
from jax.experimental.pallas import tpu as pltpu

F32 = jnp.float32
BF16 = jnp.bfloat16
I32 = jnp.int32

HEAD_DIM = 128
A_GROUPS = 8
A_CHUNK = 128
A_WIDTH = A_GROUPS * HEAD_DIM
B_HEADS = 12
B_WIDTH = B_HEADS * HEAD_DIM
CONV_W = 4
C_HEADS = 12
C_KV_HEADS = 4
C_GROUP = C_HEADS // C_KV_HEADS
C_Q = C_HEADS * HEAD_DIM
C_KV = C_KV_HEADS * HEAD_DIM
IDX_HEADS = 16
IDX_DIM = 128
TOPK_MAX = 256
PAGE_SIZE = 128
N_BRANCH = 3
RMS_EPS = 1e-6
LN_EPS = 1e-5

LANE = 128
SUBLANE = 8
V7X_VMEM_BYTES = 64 * 1024 * 1024
VMEM_BUDGET = 56 * 1024 * 1024
COMPILER_SCRATCH_BYTES = 8 * 1024 * 1024

GDN_CHUNK = 128
GDN_DECODE_CHUNK = 16
DSA_TQ = 256
DSA_TK = 512
DSA_ROW_TILE = 32
FFN_TF = 256
PAGES_PER_STEP = 16
SELECT_PAGES_PER_STEP = 32
MASK_NEG = -1e30
LOG2E = 1.4426950408889634
INT_MIN = -2147483648
KEY_OF_NEG_INF = -2139095041


def _cparams(semantics, vmem_bytes):
    return pltpu.CompilerParams(dimension_semantics=semantics,
                                vmem_limit_bytes=int(min(max(vmem_bytes, 16 * 1024 * 1024), VMEM_BUDGET)))


def _row_tile(m, pref):
    n = max(1, round(m / pref))
    if m % n == 0 and (m // n) % 16 == 0:
        return m // n
    return _pick(m, pref)


def _pick(dim, pref):
    t = pref
    while t >= SUBLANE:
        if dim % t == 0:
            return t
        t //= 2
    return dim


def _layout(d_model):
    segs = [("gate", N_BRANCH * d_model, d_model), ("bq", B_WIDTH, B_WIDTH), ("bk", B_WIDTH, B_WIDTH),
            ("bv", B_WIDTH, B_WIDTH), ("bz", B_WIDTH, B_WIDTH), ("cq", C_Q, C_Q), ("ck", C_KV, C_KV),
            ("au", A_WIDTH, A_WIDTH), ("av", A_WIDTH, A_WIDTH), ("cv", C_KV, C_KV),
            ("iq", IDX_HEADS * IDX_DIM, 4 * IDX_DIM), ("ik", IDX_DIM, IDX_DIM), ("small", LANE, LANE)]
    off, lay = 0, {}
    for name, width, align in segs:
        assert off % align == 0, (name, off, align)
        lay[name] = off
        off += width
    lay["total"] = off
    return lay


def _pack_body(tab_ref, first_ref, second_ref, ab_ref, iw_ref, o_ref):
    j = pl.program_id(1)
    last = pl.num_programs(1) - 1
    o_ref[0:LANE, :] = first_ref[0].astype(o_ref.dtype)

    @pl.when(j != last)
    def _():
        o_ref[LANE:, :] = second_ref[0].astype(o_ref.dtype)

    @pl.when(j == last)
    def _():
        n_ab, n_iw = 2 * B_HEADS, IDX_HEADS
        rest = jnp.zeros((LANE - n_ab - n_iw, o_ref.shape[1]), F32)
        o_ref[LANE:, :] = jnp.concatenate([ab_ref[0, 0:n_ab, :], iw_ref[0, 0:n_iw, :], rest],
                                          axis=0).astype(o_ref.dtype)


def _pack_w_in(w_in, d_model):
    depth, d, in_width = w_in.shape
    lay = _layout(d_model)
    w_t = jnp.swapaxes(w_in, 1, 2)
    widths = (A_WIDTH, A_WIDTH, 3 * B_WIDTH, B_WIDTH, 2 * B_HEADS, C_Q, C_KV, C_KV,
              IDX_HEADS * IDX_DIM, IDX_HEADS, IDX_DIM, N_BRANCH * d_model)
    dsts = (lay["au"], lay["av"], lay["bq"], lay["bz"], None, lay["cq"], lay["ck"], lay["cv"], lay["iq"], None,
            lay["ik"], lay["gate"])
    n_blocks = lay["total"] // LANE
    table, start, special = [0] * n_blocks, 0, []
    for width, dst in zip(widths, dsts):
        if dst is None:
            special.append(start)
        else:
            assert width % LANE == 0 and dst % LANE == 0 and start % SUBLANE == 0
            for blk in range(width // LANE):
                table[dst // LANE + blk] = (start + blk * LANE) // SUBLANE
        start += width
    assert start == in_width and lay["small"] // LANE == n_blocks - 1 and n_blocks % 2 == 0
    ab_row, iw_row = special
    assert ab_row % SUBLANE == 0 and iw_row % SUBLANE == 0 and max(ab_row, iw_row) + LANE <= in_width
    window = lambda index_map: pl.BlockSpec((pl.Element(1), pl.Element(LANE), pl.Element(d)), index_map)
    grid_spec = pltpu.PrefetchScalarGridSpec(
        num_scalar_prefetch=1,
        grid=(depth, n_blocks // 2),
        in_specs=[window(lambda l, j, tab: (l, tab[2 * j] * SUBLANE, 0)),
                  window(lambda l, j, tab: (l, tab[2 * j + 1] * SUBLANE, 0)),
                  window(lambda l, j, tab: (l, ab_row, 0)),
                  window(lambda l, j, tab: (l, iw_row, 0))],
        out_specs=pl.BlockSpec((None, 2 * LANE, d), lambda l, j, tab: (l, j, 0)))
    return pl.pallas_call(
        _pack_body,
        out_shape=jax.ShapeDtypeStruct((depth, lay["total"], d), BF16),
        grid_spec=grid_spec,
        compiler_params=_cparams(("parallel", "arbitrary"), 14 * LANE * d * 4),
        name="pack_in_proj_weight",
    )(jnp.asarray(table, I32), w_t, w_t, w_t, w_t)


def _rmsnorm_body(x_ref, g_ref, o_ref):
    x = x_ref[...]
    ms = jnp.mean(x * x, axis=-1, keepdims=True)
    o_ref[...] = (x * lax.rsqrt(ms + RMS_EPS) * g_ref[...]).astype(o_ref.dtype)


def _rmsnorm(x, g, out_dtype, rows=None):
    m, d = (x.shape[0] if rows is None else rows), x.shape[1]
    tm = _pick(m, 256)
    return pl.pallas_call(
        _rmsnorm_body,
        out_shape=jax.ShapeDtypeStruct((m, d), out_dtype),
        grid=(m // tm,),
        in_specs=[pl.BlockSpec((tm, d), lambda i: (i, 0)), pl.BlockSpec((1, d), lambda i: (0, 0))],
        out_specs=pl.BlockSpec((tm, d), lambda i: (i, 0)),
        compiler_params=_cparams(("parallel",), 6 * tm * d * 4),
        name="rmsnorm",
    )(x, g.reshape(1, d))


def _mm_body(a_ref, w_ref, o_ref):
    o_ref[...] = jnp.dot(a_ref[...], w_ref[...], preferred_element_type=F32).astype(o_ref.dtype)


def _mm_res_body(a_ref, w_ref, r_ref, o_ref):
    o_ref[...] = (r_ref[...] + jnp.dot(a_ref[...], w_ref[...], preferred_element_type=F32)).astype(o_ref.dtype)


def _mm_wt_body(a_ref, wt_ref, o_ref):
    o_ref[...] = lax.dot_general(a_ref[...], wt_ref[...], (((1,), (1,)), ((), ())),
                                 preferred_element_type=F32).astype(o_ref.dtype)


def _layer_spec(block, index_map, layer):
    if layer is None:
        return pl.BlockSpec(block, index_map)
    return pl.BlockSpec((None,) + block, lambda *g: (layer,) + index_map(*g))


def _matmul(a, w, *, layer=None, w_transposed=False, residual=None, out_dtype=F32, tm_pref=1024, tn_pref=1024,
            a_single_buffer=False, name="matmul"):
    m, k = a.shape
    n = w.shape[-2] if w_transposed else w.shape[-1]
    tm = _row_tile(m, tm_pref)
    tn = tn_pref if n % tn_pref == 0 else _pick(n, tn_pref)
    osz = jnp.dtype(out_dtype).itemsize
    vmem = (1 if a_single_buffer else 2) * tm * k * 2 + 2 * (k * tn * 2 + tm * tn * osz) + tm * tn * 4
    w_spec = (_layer_spec((tn, k), lambda i, j: (j, 0), layer) if w_transposed
              else _layer_spec((k, tn), lambda i, j: (0, j), layer))
    a_mode = dict(pipeline_mode=pl.Buffered(1)) if a_single_buffer else {}
    in_specs = [pl.BlockSpec((tm, k), lambda i, j: (i, 0), **a_mode), w_spec]
    args = [a, w]
    body = _mm_wt_body if w_transposed else _mm_body
    assert residual is None or not w_transposed
    if residual is not None:
        in_specs.append(pl.BlockSpec((tm, tn), lambda i, j: (i, j)))
        args.append(residual)
        body = _mm_res_body
        vmem += 2 * tm * tn * 4
    return pl.pallas_call(
        body,
        out_shape=jax.ShapeDtypeStruct((m, n), out_dtype),
        grid=(m // tm, n // tn),
        in_specs=in_specs,
        out_specs=pl.BlockSpec((tm, tn), lambda i, j: (i, j)),
        compiler_params=_cparams(("parallel", "arbitrary"), vmem + COMPILER_SCRATCH_BYTES),
        name=name,
    )(*args)


def _branch_a_body(u_ref, v_ref, w_ref, bs_ref, g_ref, b_ref, y_ref, *av_ref, chunk):
    tb = u_ref.shape[0]
    u = jax.nn.gelu(u_ref[...])
    v = jax.nn.gelu(v_ref[...])
    mu = jnp.mean(v, axis=-1, keepdims=True)
    var = jnp.mean(jnp.square(v - mu), axis=-1, keepdims=True)
    vn = (v - mu) * lax.rsqrt(var + LN_EPS) * g_ref[...] + b_ref[...]
    if av_ref:
        av_ref[0][...] = vn
    row = lax.broadcasted_iota(I32, (tb, tb), 0)
    col = lax.broadcasted_iota(I32, (tb, tb), 1)
    keep = (col <= row) & ((row // chunk) == (col // chunk))
    vb = vn.astype(BF16)
    for g in range(A_GROUPS):
        sl = slice(g * HEAD_DIM, (g + 1) * HEAD_DIM)
        wg = jnp.where(keep, w_ref[g], 0.0).astype(BF16)
        s = jnp.dot(wg, vb[:, sl], preferred_element_type=F32) + bs_ref[:, g:g + 1]
        y_ref[:, sl] = (u[:, sl] * s).astype(y_ref.dtype)


def _branch_a(h, lay, a_ws, a_bs, ln_g, ln_b, *, chunk, tb, emit_av, rows=None):
    m = h.shape[0]
    rows = m if rows is None else rows
    reps = tb // chunk
    wfull = jnp.tile(a_ws[:, :chunk, :chunk], (1, reps, reps))
    bs_t = jnp.tile(a_bs[:, :chunk].T, (reps, 1))
    cu, cv = lay["au"] // A_WIDTH, lay["av"] // A_WIDTH
    out_shape = [jax.ShapeDtypeStruct((m, A_WIDTH), BF16)]
    out_specs = [pl.BlockSpec((tb, A_WIDTH), lambda i: (i, 0))]
    if emit_av:
        out_shape.append(jax.ShapeDtypeStruct((m, A_WIDTH), F32))
        out_specs.append(pl.BlockSpec((tb, A_WIDTH), lambda i: (i, 0)))
    res = pl.pallas_call(
        functools.partial(_branch_a_body, chunk=chunk),
        out_shape=out_shape,
        grid=(rows // tb,),
        in_specs=[pl.BlockSpec((tb, A_WIDTH), lambda i: (i, cu)),
                  pl.BlockSpec((tb, A_WIDTH), lambda i: (i, cv)),
                  pl.BlockSpec((A_GROUPS, tb, tb), lambda i: (0, 0, 0)),
                  pl.BlockSpec((tb, A_GROUPS), lambda i: (0, 0)),
                  pl.BlockSpec((1, A_WIDTH), lambda i: (0, 0)),
                  pl.BlockSpec((1, A_WIDTH), lambda i: (0, 0))],
        out_specs=out_specs,
        compiler_params=_cparams(("parallel",), 32 << 20),
        name="branch_a_gmlp",
    )(h, h, wfull, bs_t, ln_g.reshape(1, A_WIDTH), ln_b.reshape(1, A_WIDTH))
    return res if emit_av else res[0]


def _bdot(a, b):
    return lax.dot_general(a.astype(BF16), b.astype(BF16), (((2,), (1,)), ((0,), (0,))),
                           preferred_element_type=F32)


def _bdot_nt(a, b):
    return lax.dot_general(a.astype(BF16), b.astype(BF16), (((2,), (2,)), ((0,), (0,))),
                           preferred_element_type=F32)


def _transpose_rows(x):
    r = x.shape[0]
    if r < LANE:
        x = jnp.concatenate([x, jnp.zeros((LANE - r, LANE), x.dtype)], axis=0)
    return x.T[:, :r]


def _gdn_body(q_ref, k_ref, v_ref, z_ref, sm_ref, c0_ref, s0_ref, cw_ref, al_ref, dt_ref, og_ref,
              y_ref, sout_ref, ext_ref, s_ref, *, chunk, t_valid):
    c = pl.program_id(1)
    nc = pl.num_programs(1)
    hd = HEAD_DIM

    @pl.when(c == 0)
    def _():
        s_ref[...] = s0_ref[0]
        for j in range(3):
            ext_ref[j, 0:SUBLANE, :] = c0_ref[0, :, j * B_WIDTH:(j + 1) * B_WIDTH]

    acts = []
    for j, ref in enumerate((q_ref, k_ref, v_ref)):
        ext_ref[j, SUBLANE:SUBLANE + chunk, :] = ref[...]
        acc = None
        for i in range(CONV_W):
            lo = SUBLANE - (CONV_W - 1) + i
            term = ext_ref[j, lo:lo + chunk, :] * cw_ref[i:i + 1, j * B_WIDTH:(j + 1) * B_WIDTH]
            acc = term if acc is None else acc + term
        acts.append(jax.nn.silu(acc))
        ext_ref[j, 0:SUBLANE, :] = ext_ref[j, chunk:chunk + SUBLANE, :]
    qa, ka, va = acts

    sm = sm_ref[...]
    row1 = lax.broadcasted_iota(I32, (chunk, LANE), 0)
    g_all = -jnp.exp(al_ref[...]) * jax.nn.softplus(sm + dt_ref[...])
    beta_all = jax.nn.sigmoid(sm)
    if t_valid < chunk:
        g_all = jnp.where(row1 < t_valid, g_all, 0.0)
        beta_all = jnp.where(row1 < t_valid, beta_all, 0.0)
    gc_all = g_all
    d = 1
    while d < chunk:
        gc_all = gc_all + jnp.where(row1 >= d, pltpu.roll(gc_all, d, 0), 0.0)
        d *= 2
    gc_t = _transpose_rows(gc_all)

    row = lax.broadcasted_iota(I32, (chunk, chunk), 0)
    col = lax.broadcasted_iota(I32, (chunk, chunk), 1)
    incl = row >= col
    strict = row > col
    eye = jnp.where(row == col, 1.0, 0.0)
    pair_masks = []
    bs = 1
    while bs < chunk:
        pair_masks.append(((row // bs) % 2 == 1) & ((col // bs) == (row // bs) - 1))
        bs *= 2

    heads = range(B_HEADS)
    per_head = lambda x: jnp.stack([x[:, h * hd:(h + 1) * hd] for h in heads], axis=0)
    q3, k3, v3 = per_head(qa), per_head(ka), per_head(va)
    qn = q3 * lax.rsqrt(jnp.sum(q3 * q3, axis=-1, keepdims=True) + 1e-6) * (hd ** -0.5)
    kn = k3 * lax.rsqrt(jnp.sum(k3 * k3, axis=-1, keepdims=True) + 1e-6)
    beta = jnp.stack([beta_all[:, B_HEADS + h:B_HEADS + h + 1] for h in heads], axis=0)
    gcol = jnp.stack([gc_all[:, h:h + 1] for h in heads], axis=0)
    grow = jnp.stack([gc_t[h:h + 1, :] for h in heads], axis=0)
    decay = jnp.exp(jnp.where(incl[None], gcol - grow, -jnp.inf))
    kb = kn * beta
    eg = jnp.exp(gcol)
    lmat = jnp.where(strict[None], _bdot_nt(kb, kn) * decay, 0.0)
    attn = _bdot_nt(qn, kn) * decay
    tinv = eye[None] - jnp.where(pair_masks[0][None], lmat, 0.0)
    for pm in pair_masks[1:]:
        tinv = tinv - _bdot(tinv, _bdot(jnp.where(pm[None], lmat, 0.0), tinv))
    sol = _bdot(tinv, jnp.concatenate([v3 * beta, kb * eg], axis=-1))
    value, kcd = sol[:, :, :hd], sol[:, :, hd:]
    s_old = s_ref[...]
    vnew = value - _bdot(kcd, s_old)
    o = _bdot(qn * eg, s_old) + _bdot(attn, vnew)
    glast = gcol[:, chunk - 1:chunk, :]
    kend = kn * jnp.exp(glast - gcol)
    s_ref[...] = s_old * jnp.exp(glast) + _bdot(jnp.swapaxes(kend, 1, 2), vnew)
    on = o * lax.rsqrt(jnp.mean(o * o, axis=-1, keepdims=True) + RMS_EPS) * og_ref[...]
    for h in heads:
        sl = slice(h * hd, (h + 1) * hd)
        y_ref[:, sl] = (on[h] * jax.nn.silu(z_ref[:, sl])).astype(y_ref.dtype)

    @pl.when(c == nc - 1)
    def _():
        sout_ref[0] = s_ref[...]


def _gdn(srcs, conv0, s0, conv_w, a_log, dt_bias, o_g, *, n_b, t_pad, t_valid, chunk, out_rows=None):
    nc = t_pad // chunk
    arrs = [a for a, _ in srcs]
    cbs = [cb for _, cb in srcs]
    widths = [B_WIDTH] * 4 + [LANE]

    def tok_spec(w, cb):
        return pl.BlockSpec((chunk, w), lambda b, c: (b * nc + c, cb))

    pad12 = lambda x: jnp.zeros((1, LANE), F32).at[0, :B_HEADS].set(x)
    return pl.pallas_call(
        functools.partial(_gdn_body, chunk=chunk, t_valid=t_valid),
        out_shape=[jax.ShapeDtypeStruct((out_rows or n_b * t_pad, B_WIDTH), BF16),
                   jax.ShapeDtypeStruct((n_b, B_HEADS, HEAD_DIM, HEAD_DIM), F32)],
        grid=(n_b, nc),
        in_specs=[tok_spec(w, cb) for w, cb in zip(widths, cbs)] + [
            pl.BlockSpec((1, SUBLANE, 3 * B_WIDTH), lambda b, c: (b, 0, 0)),
            pl.BlockSpec((1, B_HEADS, HEAD_DIM, HEAD_DIM), lambda b, c: (b, 0, 0, 0)),
            pl.BlockSpec((CONV_W, 3 * B_WIDTH), lambda b, c: (0, 0)),
            pl.BlockSpec((1, LANE), lambda b, c: (0, 0)),
            pl.BlockSpec((1, LANE), lambda b, c: (0, 0)),
            pl.BlockSpec((1, HEAD_DIM), lambda b, c: (0, 0))],
        out_specs=[pl.BlockSpec((chunk, B_WIDTH), lambda b, c: (b * nc + c, 0)),
                   pl.BlockSpec((1, B_HEADS, HEAD_DIM, HEAD_DIM), lambda b, c: (b, 0, 0, 0))],
        scratch_shapes=[pltpu.VMEM((3, chunk + SUBLANE, B_WIDTH), F32),
                        pltpu.VMEM((B_HEADS, HEAD_DIM, HEAD_DIM), F32)],
        compiler_params=_cparams(("parallel", "arbitrary"), 40 << 20),
        name="branch_b_gated_delta",
    )(*arrs, conv0, s0, conv_w, pad12(a_log), pad12(dt_bias), o_g.reshape(1, HEAD_DIM))


def _sort_key(x):
    b = lax.bitcast_convert_type(x + 0.0, I32)
    return b ^ ((b >> 31) & 0x7FFFFFFF)


def _select_rule(count, shape, k, nbits):
    kf = jnp.float32(k)
    count_ge = lambda cand: count(lambda key, pos: key >= cand)
    t0 = jnp.where(count_ge(jnp.zeros(shape, I32)) >= kf, 0, INT_MIN).astype(I32)

    def bit_step(i, t):
        cand = t + lax.shift_left(jnp.int32(1), 30 - i)
        return jnp.where(count_ge(cand) >= kf, cand, t)

    t = lax.fori_loop(0, 31, bit_step, t0)
    n_ge = count_ge(t)
    tied = (n_ge > kf) & (t > KEY_OF_NEG_INF)

    def tie_index():
        r = kf - count(lambda key, pos: key > t)

        def idx_step(i, j):
            cand = j + lax.shift_left(jnp.int32(1), nbits - 1 - i)
            below = count(lambda key, pos: (key == t) & (pos <= cand - 1))
            return jnp.where(below < r, cand, j)

        return lax.fori_loop(0, nbits, idx_step, jnp.zeros(shape, I32))

    no_limit = jnp.full(shape, 2 ** 30, I32)
    j = lax.cond(jnp.sum(jnp.where(tied, 1.0, 0.0)) > 0.0,
                 lambda: jnp.where(tied, tie_index(), no_limit), lambda: no_limit)
    return t, j


def _selected(key, pos, t, j):
    return (key > t) | ((key == t) & (pos <= j))


def _dsa_prompt_body(cq_ref, iq0_ref, iq1_ref, iq2_ref, iq3_ref, sm_ref, ik_ref, ck_ref, cv_ref, y_ref,
                     key_ref, iqb_ref, wb_ref, qb_ref, m_ref, l_ref, acc_ref, p_ref, al_ref,
                     *, k_sel, nbits, tq, tk):
    rows3 = C_GROUP * tq
    qi = pl.program_id(1)
    kpg = tk // LANE
    nkb = (qi + 1) * (tq // LANE)
    ngrp = ((qi + 1) * tq + tk - 1) // tk
    hd = HEAD_DIM
    nt = (((1,), (1,)), ((), ()))
    sub = LANE // SUBLANE
    qpos3 = qi * tq + lax.broadcasted_iota(I32, (sub, SUBLANE, tq), 2)
    kofs3 = lax.broadcasted_iota(I32, (sub, SUBLANE, tq), 0) * SUBLANE + lax.broadcasted_iota(I32, (sub, SUBLANE, tq), 1)

    w_all = sm_ref[...] * (IDX_HEADS ** -0.5 * IDX_DIM ** -0.5)
    w_t = jnp.concatenate([w_all[r:r + LANE].T for r in range(0, tq, LANE)], axis=1)
    w_off = 2 * B_HEADS
    for hh in range(IDX_HEADS):
        ref = (iq0_ref, iq1_ref, iq2_ref, iq3_ref)[hh // 4]
        iqb_ref[hh] = ref[:, (hh % 4) * IDX_DIM:(hh % 4 + 1) * IDX_DIM].astype(BF16)
        wb_ref[hh] = jnp.broadcast_to(w_t[w_off + hh:w_off + hh + 1, :], (SUBLANE, tq))

    def score_grp(g, carry):
        ikg = ik_ref[pl.ds(pl.multiple_of(g * tk, tk), tk), :].astype(BF16)
        acc = jnp.zeros((tk // SUBLANE, SUBLANE, tq), F32)
        for hh in range(IDX_HEADS):
            s = lax.dot_general(ikg, iqb_ref[hh], nt, preferred_element_type=F32)
            acc = acc + wb_ref[hh] * jnp.maximum(s, 0.0).reshape(tk // SUBLANE, SUBLANE, tq)
        for t in range(kpg):
            kpos3 = (g * kpg + t) * LANE + kofs3
            blk = acc[t * sub:(t + 1) * sub]
            key_ref[g * kpg + t] = _sort_key(jnp.where(kpos3 <= qpos3, blk, -jnp.inf))
        return carry

    lax.fori_loop(0, ngrp, score_grp, 0)

    bpt = tq // LANE

    def count(pred):
        def trip(i, acc):
            for u in range(bpt):
                b = i * bpt + u
                acc = acc + jnp.sum(jnp.where(pred(key_ref[b], b * LANE + kofs3), 1.0, 0.0), axis=0)
            return acc
        acc = lax.fori_loop(0, qi + 1, trip, jnp.zeros((SUBLANE, tq), F32))
        return jnp.broadcast_to(jnp.sum(acc, axis=0, keepdims=True), (SUBLANE, tq))

    t8, j8 = _select_rule(count, (SUBLANE, tq), k_sel, nbits)

    for kvh in range(C_KV_HEADS):
        for g in range(C_GROUP):
            hsl = slice((kvh * C_GROUP + g) * hd, (kvh * C_GROUP + g + 1) * hd)
            qb_ref[kvh, g * tq:(g + 1) * tq, :] = cq_ref[:, hsl].astype(BF16)
    m_ref[...] = jnp.full(m_ref.shape, MASK_NEG, F32)
    l_ref[...] = jnp.zeros(l_ref.shape, F32)
    acc_ref[...] = jnp.zeros(acc_ref.shape, F32)
    ones = jnp.ones((tk, hd), BF16)

    def attend_grp(g, carry):
        bias_t = []
        for t in range(kpg):
            kpos3 = (g * kpg + t) * LANE + kofs3
            sel = _selected(key_ref[g * kpg + t], kpos3, t8, j8) & (kpos3 <= qpos3)
            kq = jnp.where(sel, 0.0, MASK_NEG).reshape(LANE, tq)
            bias_t.append(jnp.concatenate([kq[:, r:r + LANE].T for r in range(0, tq, LANE)], axis=0))
        bias = jnp.concatenate(bias_t, axis=1)
        start = pl.multiple_of(g * tk, tk)
        kgrp = ck_ref[pl.ds(start, tk), :].astype(BF16)
        vgrp = cv_ref[pl.ds(start, tk), :].astype(BF16)
        for kvh in range(C_KV_HEADS):
            sl = slice(kvh * hd, (kvh + 1) * hd)
            s = lax.dot_general(qb_ref[kvh], kgrp[:, sl], nt, preferred_element_type=F32)
            for r0 in range(0, rows3, DSA_ROW_TILE):
                rs = slice(r0, r0 + DSA_ROW_TILE)
                x = s[rs] * (hd ** -0.5 * LOG2E) + bias[r0 % tq:r0 % tq + DSA_ROW_TILE]
                m_old = m_ref[kvh, rs, :]
                m_new = jnp.maximum(m_old, jnp.max(x, axis=-1, keepdims=True))
                p_ref[rs, :] = jnp.exp2(x - m_new).astype(BF16)
                al_ref[rs, :] = jnp.exp2(m_old - m_new)
                m_ref[kvh, rs, :] = m_new
            pv = jnp.dot(p_ref[...], jnp.concatenate([vgrp[:, sl], ones], axis=1), preferred_element_type=F32)
            alpha = al_ref[...]
            l_ref[kvh] = alpha * l_ref[kvh] + pv[:, hd:hd + 1]
            acc_ref[kvh] = alpha * acc_ref[kvh] + pv[:, :hd]
        return carry

    lax.fori_loop(0, ngrp, attend_grp, 0)

    for kvh in range(C_KV_HEADS):
        o = acc_ref[kvh] / l_ref[kvh]
        for g in range(C_GROUP):
            hsl = slice((kvh * C_GROUP + g) * hd, (kvh * C_GROUP + g + 1) * hd)
            y_ref[:, hsl] = o[g * tq:(g + 1) * tq, :].astype(y_ref.dtype)


def _dsa_prompt(h, lay, *, n_b, seq, out_rows=None):
    tq, tk = DSA_TQ, DSA_TK
    assert seq % tq == 0 and seq % tk == 0
    nqb = seq // tq
    k_sel = min(TOPK_MAX, seq // 4)
    nbits = max(1, (seq - 1).bit_length())
    iq_cb = lay["iq"] // (4 * IDX_DIM)
    once = pl.Buffered(1)

    def q_spec(w, cb):
        return pl.BlockSpec((tq, w), lambda b, q: (b * nqb + q, cb))

    def kv_spec(w, cb):
        return pl.BlockSpec((seq, w), lambda b, q: (b, cb), pipeline_mode=once)

    rows3 = C_GROUP * tq
    lane_pad = lambda r: r * LANE * 4
    vmem = (seq * (2 * C_KV + IDX_DIM) * 4 + 2 * tq * (C_Q + IDX_HEADS * IDX_DIM + LANE) * 4 + 2 * tq * C_Q * 2
            + (seq // LANE) * tq * LANE * 4 + IDX_HEADS * tq * LANE * 6 + C_KV_HEADS * rows3 * HEAD_DIM * 6
            + 2 * C_KV_HEADS * lane_pad(rows3) + 4 * rows3 * tk * 4)
    return pl.pallas_call(
        functools.partial(_dsa_prompt_body, k_sel=k_sel, nbits=nbits, tq=tq, tk=tk),
        out_shape=jax.ShapeDtypeStruct((out_rows or n_b * seq, C_Q), BF16),
        grid=(n_b, nqb),
        in_specs=[q_spec(C_Q, lay["cq"] // C_Q)]
                 + [q_spec(4 * IDX_DIM, iq_cb + i) for i in range(4)]
                 + [q_spec(LANE, lay["small"] // LANE),
                    kv_spec(IDX_DIM, lay["ik"] // IDX_DIM),
                    kv_spec(C_KV, lay["ck"] // C_KV),
                    kv_spec(C_KV, lay["cv"] // C_KV)],
        out_specs=pl.BlockSpec((tq, C_Q), lambda b, q: (b * nqb + q, 0)),
        scratch_shapes=[pltpu.VMEM((seq // LANE, LANE // SUBLANE, SUBLANE, tq), I32),
                        pltpu.VMEM((IDX_HEADS, tq, IDX_DIM), BF16),
                        pltpu.VMEM((IDX_HEADS, SUBLANE, tq), F32),
                        pltpu.VMEM((C_KV_HEADS, rows3, HEAD_DIM), BF16),
                        pltpu.VMEM((C_KV_HEADS, rows3, 1), F32),
                        pltpu.VMEM((C_KV_HEADS, rows3, 1), F32),
                        pltpu.VMEM((C_KV_HEADS, rows3, HEAD_DIM), F32),
                        pltpu.VMEM((rows3, tk), BF16),
                        pltpu.VMEM((rows3, 1), F32)],
        compiler_params=_cparams(("parallel", "arbitrary"), vmem + COMPILER_SCRATCH_BYTES),
        name="branch_c_prompt_dsa",
    )(h, h, h, h, h, h, h, h, h)


def _dsa_sample_select_body(pt_ref, *refs, n_pages, k_sel, nbits, t_new):
    pp = SELECT_PAGES_PER_STEP
    page_refs = refs[:pp]
    iq_ref, w_ref, ikn_ref, bias_ref, key_ref = refs[pp:]
    j = pl.program_id(1)
    n_steps = n_pages // pp
    rows = t_new
    col = lax.broadcasted_iota(I32, (rows, LANE), 1)
    trow = lax.broadcasted_iota(I32, (rows, LANE), 0)
    iq = iq_ref[0].astype(BF16)

    def scores(keys_f32):
        n = keys_f32.shape[0]
        s = lax.dot_general(iq, keys_f32.astype(BF16), (((1,), (1,)), ((), ())), preferred_element_type=F32)
        r = jnp.maximum(s, 0.0) * w_ref[0]
        return jnp.sum(r.reshape(rows, IDX_HEADS, n), axis=1)

    @pl.when(j < n_steps)
    def _():
        keys = _sort_key(scores(jnp.concatenate([r[0, 0] for r in page_refs], axis=0)))
        for i in range(pp):
            key_ref[j * pp + i] = keys[:, i * LANE:(i + 1) * LANE]

    @pl.when(j == n_steps)
    def _():
        new_ok = (col <= trow) & (col < t_new)
        key_ref[n_pages] = _sort_key(jnp.where(new_ok, scores(ikn_ref[0]), -jnp.inf))
        nblk = n_pages + 1
        grp = SUBLANE
        nblk_pad = key_ref.shape[0]
        if nblk_pad > nblk:
            key_ref[nblk:nblk_pad] = jnp.full((nblk_pad - nblk, rows, LANE), KEY_OF_NEG_INF, I32)
        blk3 = lax.broadcasted_iota(I32, (grp, rows, LANE), 0)
        col3 = lax.broadcasted_iota(I32, (grp, rows, LANE), 2)

        def count(pred):
            def trip(i, acc):
                b0 = pl.multiple_of(i * grp, grp)
                hit = pred(key_ref[pl.ds(b0, grp)], (b0 + blk3) * LANE + col3)
                return acc + jnp.sum(jnp.where(hit, 1.0, 0.0), axis=0)
            acc = lax.fori_loop(0, nblk_pad // grp, trip, jnp.zeros((rows, LANE), F32))
            return jnp.broadcast_to(jnp.sum(acc, axis=1, keepdims=True), (rows, LANE))

        tb, jb = _select_rule(count, (rows, LANE), k_sel, nbits)

        def write_blk(b, carry):
            kpos = b * LANE + col
            sel = _selected(key_ref[b], kpos, tb, jb) & ((kpos < n_pages * PAGE_SIZE) | new_ok)
            bias_ref[0, b] = jnp.where(sel, 0.0, MASK_NEG)
            return carry

        lax.fori_loop(0, nblk, write_blk, 0)


def _page_index_map(layer, i, n_pages, n_steps, trailing, pp):
    def index_map(b, j, pt):
        step = jnp.minimum(j, n_steps - 1)
        return (layer, pt[b * n_pages + step * pp + i]) + (0,) * trailing
    return index_map


def _dsa_sample_select(pt_flat, kidx_pool, layer, iq_s, iw_s, ik_new, *, n_b, n_pages, t_new):
    pp = SELECT_PAGES_PER_STEP
    assert n_pages % pp == 0
    n_steps = n_pages // pp
    total = n_pages * PAGE_SIZE + t_new
    k_sel = min(TOPK_MAX, total // 4)
    nbits = max(1, ((n_pages + 1) * PAGE_SIZE - 1).bit_length())
    rows_q = t_new * IDX_HEADS
    grid_spec = pltpu.PrefetchScalarGridSpec(
        num_scalar_prefetch=1,
        grid=(n_b, n_steps + 1),
        in_specs=[pl.BlockSpec((1, 1, PAGE_SIZE, IDX_DIM), _page_index_map(layer, i, n_pages, n_steps, 2, pp))
                  for i in range(pp)]
                 + [pl.BlockSpec((1, rows_q, IDX_DIM), lambda b, j, pt: (b, 0, 0)),
                    pl.BlockSpec((1, rows_q, 1), lambda b, j, pt: (b, 0, 0)),
                    pl.BlockSpec((1, PAGE_SIZE, IDX_DIM), lambda b, j, pt: (b, 0, 0))],
        out_specs=pl.BlockSpec((1, n_pages + 1, t_new, LANE), lambda b, j, pt: (b, 0, 0, 0)),
        scratch_shapes=[pltpu.VMEM((-(-(n_pages + 1) // SUBLANE) * SUBLANE, t_new, LANE), I32)])
    return pl.pallas_call(
        functools.partial(_dsa_sample_select_body, n_pages=n_pages, k_sel=k_sel, nbits=nbits, t_new=t_new),
        out_shape=jax.ShapeDtypeStruct((n_b, n_pages + 1, t_new, LANE), F32),
        grid_spec=grid_spec,
        compiler_params=_cparams(("parallel", "arbitrary"), 24 << 20),
        name="branch_c_decode_select",
    )(pt_flat, *([kidx_pool] * pp), iq_s, iw_s, ik_new)


def _dsa_sample_attend_body(pt_ref, *refs, n_pages, t_new):
    pp = PAGES_PER_STEP
    k_refs, v_refs = refs[:pp], refs[pp:2 * pp]
    q_ref, bias_ref, kn_ref, vn_ref, o_ref, m_ref, l_ref, acc_ref = refs[2 * pp:]
    j = pl.program_id(1)
    n_steps = n_pages // pp
    hd = HEAD_DIM
    rq = q_ref.shape[1] // C_KV_HEADS
    reps = rq // t_new

    @pl.when(j == 0)
    def _():
        m_ref[...] = jnp.full(m_ref.shape, MASK_NEG, F32)
        l_ref[...] = jnp.zeros(l_ref.shape, F32)
        acc_ref[...] = jnp.zeros(acc_ref.shape, F32)

    def attend(k_of, v_of, bias_t):
        bias = jnp.concatenate([bias_t] * reps, axis=0)
        for kvh in range(C_KV_HEADS):
            rs = slice(kvh * rq, (kvh + 1) * rq)
            s = lax.dot_general(q_ref[0, rs, :].astype(BF16), k_of(kvh).astype(BF16), (((1,), (1,)), ((), ())),
                                preferred_element_type=F32) * (hd ** -0.5) + bias
            m_old = m_ref[rs, :]
            m_new = jnp.maximum(m_old, jnp.max(s, axis=-1, keepdims=True))
            alpha = jnp.exp(m_old - m_new)
            p = jnp.exp(s - m_new)
            l_ref[rs, :] = alpha * l_ref[rs, :] + jnp.sum(p, axis=-1, keepdims=True)
            acc_ref[rs, :] = alpha * acc_ref[rs, :] + jnp.dot(p.astype(BF16), v_of(kvh).astype(BF16),
                                                              preferred_element_type=F32)
            m_ref[rs, :] = m_new

    @pl.when(j < n_steps)
    def _():
        head_rows = lambda kvh: pl.ds(kvh, PAGE_SIZE, stride=C_KV_HEADS)
        pages = lambda prefs: (lambda kvh: jnp.concatenate([r[0, 0, head_rows(kvh), :] for r in prefs], axis=0))
        bias_t = jnp.concatenate([bias_ref[0, j * pp + i] for i in range(pp)], axis=1)
        attend(pages(k_refs), pages(v_refs), bias_t)

    @pl.when(j == n_steps)
    def _():
        new = lambda ref: (lambda kvh: ref[0, :, kvh * hd:(kvh + 1) * hd])
        attend(new(kn_ref), new(vn_ref), bias_ref[0, n_pages])
        o_ref[0] = acc_ref[...] / l_ref[...]


def _dsa_sample_attend(pt_flat, k_pool, v_pool, layer, q_s, bias, k_new, v_new, *, n_b, n_pages, t_new):
    pp = PAGES_PER_STEP
    n_steps = n_pages // pp
    rows = q_s.shape[1]
    page_specs = [pl.BlockSpec((1, 1, PAGE_SIZE * C_KV_HEADS, HEAD_DIM),
                               _page_index_map(layer, i, n_pages, n_steps, 2, pp)) for i in range(pp)]
    grid_spec = pltpu.PrefetchScalarGridSpec(
        num_scalar_prefetch=1,
        grid=(n_b, n_steps + 1),
        in_specs=page_specs + page_specs
                 + [pl.BlockSpec((1, rows, HEAD_DIM), lambda b, j, pt: (b, 0, 0)),
                    pl.BlockSpec((1, n_pages + 1, t_new, LANE), lambda b, j, pt: (b, 0, 0, 0)),
                    pl.BlockSpec((1, PAGE_SIZE, C_KV), lambda b, j, pt: (b, 0, 0)),
                    pl.BlockSpec((1, PAGE_SIZE, C_KV), lambda b, j, pt: (b, 0, 0))],
        out_specs=pl.BlockSpec((1, rows, HEAD_DIM), lambda b, j, pt: (b, 0, 0)),
        scratch_shapes=[pltpu.VMEM((rows, 1), F32), pltpu.VMEM((rows, 1), F32), pltpu.VMEM((rows, HEAD_DIM), F32)])
    return pl.pallas_call(
        functools.partial(_dsa_sample_attend_body, n_pages=n_pages, t_new=t_new),
        out_shape=jax.ShapeDtypeStruct((n_b, rows, HEAD_DIM), F32),
        grid_spec=grid_spec,
        compiler_params=_cparams(("parallel", "arbitrary"), 32 << 20),
        name="branch_c_decode_attend",
    )(pt_flat, *([k_pool] * pp), *([v_pool] * pp), q_s, bias, k_new, v_new)


def _dsa_sample(h_s, lay, pt_flat, k_pool, v_pool, kidx_pool, layer, *, n_b, t_new, n_pages):
    hd = HEAD_DIM
    seg = lambda name, w: h_s[:, lay[name]:lay[name] + w]
    rq = -(-C_GROUP * t_new // 16) * 16
    q = seg("cq", C_Q).reshape(n_b, t_new, C_KV_HEADS, C_GROUP, hd).transpose(0, 2, 3, 1, 4)
    q = q.reshape(n_b, C_KV_HEADS, C_GROUP * t_new, hd)
    q_s = jnp.pad(q, ((0, 0), (0, 0), (0, rq - C_GROUP * t_new), (0, 0))).reshape(n_b, C_KV_HEADS * rq, hd)
    iq_s = seg("iq", IDX_HEADS * IDX_DIM).reshape(n_b, t_new * IDX_HEADS, IDX_DIM)
    iw = h_s[:, lay["small"] + 2 * B_HEADS:lay["small"] + 2 * B_HEADS + IDX_HEADS]
    iw_s = (iw * (IDX_HEADS ** -0.5 * IDX_DIM ** -0.5)).reshape(n_b, t_new * IDX_HEADS, 1)
    padrows = lambda x: jnp.pad(x.reshape(n_b, t_new, -1), ((0, 0), (0, PAGE_SIZE - t_new), (0, 0)))
    ik_new, k_new, v_new = padrows(seg("ik", IDX_DIM)), padrows(seg("ck", C_KV)), padrows(seg("cv", C_KV))
    bias = _dsa_sample_select(pt_flat, kidx_pool, layer, iq_s, iw_s, ik_new, n_b=n_b, n_pages=n_pages, t_new=t_new)
    o = _dsa_sample_attend(pt_flat, k_pool, v_pool, layer, q_s, bias, k_new, v_new,
                           n_b=n_b, n_pages=n_pages, t_new=t_new)
    o = o.reshape(n_b, C_KV_HEADS, rq, hd)[:, :, :C_GROUP * t_new].reshape(n_b, C_KV_HEADS, C_GROUP, t_new, hd)
    return o.transpose(0, 3, 1, 2, 4).reshape(n_b * t_new, C_Q).astype(BF16)


def _merge_body(ya_ref, yb_ref, yc_ref, w_ref, g0_ref, g1_ref, g2_ref, o_ref):
    b0, c0 = A_WIDTH, A_WIDTH + B_WIDTH
    acc = jax.nn.sigmoid(g0_ref[...]) * jnp.dot(ya_ref[...], w_ref[0:b0, :], preferred_element_type=F32)
    acc = acc + jax.nn.sigmoid(g1_ref[...]) * jnp.dot(yb_ref[...], w_ref[b0:c0, :], preferred_element_type=F32)
    acc = acc + jax.nn.sigmoid(g2_ref[...]) * jnp.dot(yc_ref[...], w_ref[c0:, :], preferred_element_type=F32)
    o_ref[...] = acc.astype(o_ref.dtype)


def _merge(ya, yb, yc, w_br, layer, h, lay, d_model):
    m = ya.shape[0]
    tm = _row_tile(m, 1024)
    tn = _pick(d_model, 512)
    gcb = lay["gate"] // tn
    per = d_model // tn

    def gate_spec(i):
        return pl.BlockSpec((tm, tn), lambda r, c: (r, gcb + i * per + c))

    kw = A_WIDTH + B_WIDTH + C_Q
    vmem = 2 * (tm * kw * 2 + kw * tn * 2 + 3 * tm * tn * 4 + tm * tn * 2) + 4 * tm * tn * 4
    return pl.pallas_call(
        _merge_body,
        out_shape=jax.ShapeDtypeStruct((m, d_model), BF16),
        grid=(m // tm, d_model // tn),
        in_specs=[pl.BlockSpec((tm, A_WIDTH), lambda r, c: (r, 0)),
                  pl.BlockSpec((tm, B_WIDTH), lambda r, c: (r, 0)),
                  pl.BlockSpec((tm, C_Q), lambda r, c: (r, 0)),
                  _layer_spec((kw, tn), lambda r, c: (0, c), layer),
                  gate_spec(0), gate_spec(1), gate_spec(2)],
        out_specs=pl.BlockSpec((tm, tn), lambda r, c: (r, c)),
        compiler_params=_cparams(("parallel", "arbitrary"), vmem + COMPILER_SCRATCH_BYTES),
        name="gated_merge",
    )(ya, yb, yc, w_br, h, h, h)


def _ffn_up_body(hn_ref, w1_ref, w3_ref, o_ref):
    hn = hn_ref[...]
    a = jnp.dot(hn, w1_ref[...].astype(BF16), preferred_element_type=F32)
    b = jnp.dot(hn, w3_ref[...].astype(BF16), preferred_element_type=F32)
    o_ref[...] = (jax.nn.silu(a) * b).astype(o_ref.dtype)


def _ffn(hn, w1, w3, w2, layer, x):
    m, d = hn.shape
    d_ff = w1.shape[-1]
    tm = _row_tile(m, 1024)
    tf = FFN_TF
    assert d_ff % tf == 0
    vmem = 2 * (tm * d * 2 + 2 * d * tf * 4 + tm * tf * 2) + 2 * d * tf * 2 + 3 * tm * tf * 4
    act = pl.pallas_call(
        _ffn_up_body,
        out_shape=jax.ShapeDtypeStruct((m, d_ff), BF16),
        grid=(m // tm, d_ff // tf),
        in_specs=[pl.BlockSpec((tm, d), lambda i, f: (i, 0)),
                  _layer_spec((d, tf), lambda i, f: (0, f), layer),
                  _layer_spec((d, tf), lambda i, f: (0, f), layer)],
        out_specs=pl.BlockSpec((tm, tf), lambda i, f: (i, f)),
        compiler_params=_cparams(("parallel", "arbitrary"), vmem + COMPILER_SCRATCH_BYTES),
        name="swiglu_up",
    )(hn, w1, w3)
    return _matmul(act, w2, layer=layer, residual=x, tm_pref=1024, tn_pref=256, a_single_buffer=True,
                   name="swiglu_down")


def _dense_front(x, ln1, w_in_p, layer):
    xn = _rmsnorm(x, ln1, BF16)
    return _matmul(xn, w_in_p, layer=layer, w_transposed=True, tn_pref=768, name="in_proj")


def _dense_back(x, ya, yb, yc, h, lay, wts, layer, d_model):
    w_br, w_o, w1, w3, w2, ln2 = wts
    mix = _merge(ya, yb, yc, w_br, layer, h, lay, d_model)
    x = _matmul(mix, w_o, layer=layer, residual=x, tn_pref=512, name="out_proj")
    hn = _rmsnorm(x, ln2, BF16)
    return _ffn(hn, w1, w3, w2, layer, x)


def kernel(x_prompt, x_sample, cache_k, cache_v, cache_kidx, state_conv, state_delta, page_table, ln1, w_in,
           a_ln_g, a_ln_b, a_ws, a_bs, b_conv_w, b_a_log, b_dt_bias, b_out_g, w_br, w_o, ln2, ffn_w1, ffn_w3,
           ffn_w2, ln_f):
    n_bp, seq, d_model = x_prompt.shape
    n_bs, t_new, _ = x_sample.shape
    depth = ln1.shape[0]
    n_phys = cache_k.shape[1]
    n_pages = page_table.shape[1]
    assert seq % A_CHUNK == 0 and seq % GDN_CHUNK == 0 and CONV_W - 1 <= t_new <= min(A_CHUNK, GDN_DECODE_CHUNK)
    lay = _layout(d_model)
    hd = HEAD_DIM

    m_p, m_s = n_bp * seq, n_bs * t_new
    m_all = m_p + -(-m_s // LANE) * LANE
    x_all = jnp.concatenate([x_prompt.reshape(m_p, d_model), x_sample.reshape(m_s, d_model),
                             jnp.zeros((m_all - m_p - m_s, d_model), F32)], axis=0)
    pt_flat = page_table.reshape(-1).astype(I32)
    k_pool = cache_k.reshape(depth, n_phys, PAGE_SIZE * C_KV_HEADS, hd)
    v_pool = cache_v.reshape(depth, n_phys, PAGE_SIZE * C_KV_HEADS, hd)
    conv0_p = jnp.zeros((n_bp, SUBLANE, 3 * B_WIDTH), F32)
    s0_p = jnp.zeros((n_bp, B_HEADS, hd, hd), F32)
    qkv_off = lay["bq"]

    outs = {k: [] for k in ("pk", "pv", "pik", "pconv", "pdelta", "sk", "sv", "sik", "sconv", "sdelta", "schunk")}
    dense_w = (w_br.astype(BF16), w_o.astype(BF16), ffn_w1, ffn_w3, ffn_w2.astype(BF16))
    w_in_p = _pack_w_in(w_in, d_model)
    for l in range(depth):
        wts = dense_w + (ln2[l],)

        h = _dense_front(x_all, ln1[l], w_in_p, l)

        ya = _branch_a(h, lay, a_ws[l], a_bs[l], a_ln_g[l], a_ln_b[l], chunk=A_CHUNK, tb=A_CHUNK, emit_av=False,
                       rows=m_p)
        srcs = [(h, lay[n] // B_WIDTH) for n in ("bq", "bk", "bv", "bz")] + [(h, lay["small"] // LANE)]
        yb, s_new = _gdn(srcs, conv0_p, s0_p, b_conv_w[l], b_a_log[l], b_dt_bias[l], b_out_g[l],
                         n_b=n_bp, t_pad=seq, t_valid=GDN_CHUNK, chunk=GDN_CHUNK, out_rows=m_all)
        yc = _dsa_prompt(h, lay, n_b=n_bp, seq=seq, out_rows=m_all)
        h3 = h[:m_p].reshape(n_bp, seq, lay["total"])
        outs["pk"].append(h3[:, :, lay["ck"]:lay["ck"] + C_KV].reshape(n_bp, seq, C_KV_HEADS, hd))
        outs["pv"].append(h3[:, :, lay["cv"]:lay["cv"] + C_KV].reshape(n_bp, seq, C_KV_HEADS, hd))
        outs["pik"].append(h3[:, :, lay["ik"]:lay["ik"] + IDX_DIM])
        outs["pconv"].append(h3[:, seq - (CONV_W - 1):, qkv_off:qkv_off + 3 * B_WIDTH])
        outs["pdelta"].append(s_new)
        ya_p, yb_p, yc_p = ya, yb, yc

        hs = h[m_p:m_p + m_s]
        ya, av = _branch_a(hs, lay, a_ws[l], a_bs[l], a_ln_g[l], a_ln_b[l], chunk=t_new, tb=n_bs * t_new,
                           emit_av=True)
        hs3 = hs.reshape(n_bs, t_new, lay["total"])
        dc = GDN_DECODE_CHUNK
        padt = lambda x: jnp.pad(x, ((0, 0), (0, dc - t_new), (0, 0))).reshape(n_bs * dc, -1)
        srcs = [(padt(hs3[:, :, lay[n]:lay[n] + B_WIDTH]), 0) for n in ("bq", "bk", "bv", "bz")]
        srcs.append((padt(hs3[:, :, lay["small"]:lay["small"] + LANE]), 0))
        conv0_s = jnp.pad(state_conv[l], ((0, 0), (SUBLANE - (CONV_W - 1), 0), (0, 0)))
        yb, s_new = _gdn(srcs, conv0_s, state_delta[l], b_conv_w[l], b_a_log[l], b_dt_bias[l], b_out_g[l],
                         n_b=n_bs, t_pad=dc, t_valid=t_new, chunk=dc)
        yb = yb.reshape(n_bs, dc, B_WIDTH)[:, :t_new].reshape(n_bs * t_new, B_WIDTH)
        yc = _dsa_sample(hs, lay, pt_flat, k_pool, v_pool, cache_kidx, l, n_b=n_bs, t_new=t_new, n_pages=n_pages)
        outs["sk"].append(hs3[:, :, lay["ck"]:lay["ck"] + C_KV].reshape(n_bs, t_new, C_KV_HEADS, hd))
        outs["sv"].append(hs3[:, :, lay["cv"]:lay["cv"] + C_KV].reshape(n_bs, t_new, C_KV_HEADS, hd))
        outs["sik"].append(hs3[:, :, lay["ik"]:lay["ik"] + IDX_DIM])
        outs["sconv"].append(hs3[:, t_new - (CONV_W - 1):, qkv_off:qkv_off + 3 * B_WIDTH])
        outs["sdelta"].append(s_new)
        outs["schunk"].append(av.reshape(n_bs, t_new, A_WIDTH))

        tail = lambda full, part: full.at[m_p:].set(jnp.pad(part, ((0, m_all - m_p - m_s), (0, 0))))
        x_all = _dense_back(x_all, tail(ya_p, ya), tail(yb_p, yb), tail(yc_p, yc), h, lay, wts, l, d_model)

    y_prompt = _rmsnorm(x_all, ln_f, F32, rows=m_p).reshape(n_bp, seq, d_model)
    y_sample = _rmsnorm(x_all[m_p:m_p + m_s], ln_f, F32).reshape(n_bs, t_new, d_model)
    st = lambda k: jnp.stack(outs[k])
    return (y_prompt, y_sample, st("pk"), st("pv"), st("pik"), st("pconv"), st("pdelta"),
            st("sk"), st("sv"), st("sik"), st("sconv"), st("sdelta"), st("schunk"))
```

```python
import functools

import jax
import jax.numpy as jnp
from jax import lax
from jax.experimental import pallas as pl
from jax.experimental.pallas import tpu as pltpu

F32 = jnp.float32
BF16 = jnp.bfloat16
I32 = jnp.int32

HEAD_DIM = 128
A_GROUPS = 8
A_CHUNK = 128
A_WIDTH = A_GROUPS * HEAD_DIM
B_HEADS = 12
B_WIDTH = B_HEADS * HEAD_DIM
CONV_W = 4
C_HEADS = 12
C_KV_HEADS = 4
C_GROUP = C_HEADS // C_KV_HEADS
C_Q = C_HEADS * HEAD_DIM
C_KV = C_KV_HEADS * HEAD_DIM
IDX_HEADS = 16
IDX_DIM = 128
TOPK_MAX = 256
PAGE_SIZE = 128
N_BRANCH = 3
RMS_EPS = 1e-6
LN_EPS = 1e-5

LANE = 128
SUBLANE = 8
V7X_VMEM_BYTES = 64 * 1024 * 1024
VMEM_BUDGET = 56 * 1024 * 1024
COMPILER_SCRATCH_BYTES = 8 * 1024 * 1024

GDN_CHUNK = 128
GDN_DECODE_CHUNK = 16
DSA_TQ = 256
DSA_TK = 512
DSA_ROW_TILE = 32
FFN_TF = 256
PAGES_PER_STEP = 16
SELECT_PAGES_PER_STEP = 32
MASK_NEG = -1e30
LOG2E = 1.4426950408889634
INT_MIN = -2147483648
KEY_OF_NEG_INF = -2139095041


def _cparams(semantics, vmem_bytes):
    return pltpu.CompilerParams(dimension_semantics=semantics,
                                vmem_limit_bytes=int(min(max(vmem_bytes, 16 * 1024 * 1024), VMEM_BUDGET)))


def _row_tile(m, pref):
    n = max(1, round(m / pref))
    if m % n == 0 and (m // n) % 16 == 0:
        return m // n
    return _pick(m, pref)


def _pick(dim, pref):
    t = pref
    while t >= SUBLANE:
        if dim % t == 0:
            return t
        t //= 2
    return dim


def _layout(d_model):
    segs = [("gate", N_BRANCH * d_model, d_model), ("bq", B_WIDTH, B_WIDTH), ("bk", B_WIDTH, B_WIDTH),
            ("bv", B_WIDTH, B_WIDTH), ("bz", B_WIDTH, B_WIDTH), ("cq", C_Q, C_Q), ("ck", C_KV, C_KV),
            ("au", A_WIDTH, A_WIDTH), ("av", A_WIDTH, A_WIDTH), ("cv", C_KV, C_KV),
            ("iq", IDX_HEADS * IDX_DIM, 4 * IDX_DIM), ("ik", IDX_DIM, IDX_DIM), ("small", LANE, LANE)]
    off, lay = 0, {}
    for name, width, align in segs:
        assert off % align == 0, (name, off, align)
        lay[name] = off
        off += width
    lay["total"] = off
    return lay


def _pack_body(tab_ref, first_ref, second_ref, ab_ref, iw_ref, o_ref):
    j = pl.program_id(1)
    last = pl.num_programs(1) - 1
    o_ref[0:LANE, :] = first_ref[0].astype(o_ref.dtype)

    @pl.when(j != last)
    def _():
        o_ref[LANE:, :] = second_ref[0].astype(o_ref.dtype)

    @pl.when(j == last)
    def _():
        n_ab, n_iw = 2 * B_HEADS, IDX_HEADS
        rest = jnp.zeros((LANE - n_ab - n_iw, o_ref.shape[1]), F32)
        o_ref[LANE:, :] = jnp.concatenate([ab_ref[0, 0:n_ab, :], iw_ref[0, 0:n_iw, :], rest],
                                          axis=0).astype(o_ref.dtype)


def _pack_w_in(w_in, d_model):
    depth, d, in_width = w_in.shape
    lay = _layout(d_model)
    w_t = jnp.swapaxes(w_in, 1, 2)
    widths = (A_WIDTH, A_WIDTH, 3 * B_WIDTH, B_WIDTH, 2 * B_HEADS, C_Q, C_KV, C_KV,
              IDX_HEADS * IDX_DIM, IDX_HEADS, IDX_DIM, N_BRANCH * d_model)
    dsts = (lay["au"], lay["av"], lay["bq"], lay["bz"], None, lay["cq"], lay["ck"], lay["cv"], lay["iq"], None,
            lay["ik"], lay["gate"])
    n_blocks = lay["total"] // LANE
    table, start, special = [0] * n_blocks, 0, []
    for width, dst in zip(widths, dsts):
        if dst is None:
            special.append(start)
        else:
            assert width % LANE == 0 and dst % LANE == 0 and start % SUBLANE == 0
            for blk in range(width // LANE):
                table[dst // LANE + blk] = (start + blk * LANE) // SUBLANE
        start += width
    assert start == in_width and lay["small"] // LANE == n_blocks - 1 and n_blocks % 2 == 0
    ab_row, iw_row = special
    assert ab_row % SUBLANE == 0 and iw_row % SUBLANE == 0 and max(ab_row, iw_row) + LANE <= in_width
    window = lambda index_map: pl.BlockSpec((pl.Element(1), pl.Element(LANE), pl.Element(d)), index_map)
    grid_spec = pltpu.PrefetchScalarGridSpec(
        num_scalar_prefetch=1,
        grid=(depth, n_blocks // 2),
        in_specs=[window(lambda l, j, tab: (l, tab[2 * j] * SUBLANE, 0)),
                  window(lambda l, j, tab: (l, tab[2 * j + 1] * SUBLANE, 0)),
                  window(lambda l, j, tab: (l, ab_row, 0)),
                  window(lambda l, j, tab: (l, iw_row, 0))],
        out_specs=pl.BlockSpec((None, 2 * LANE, d), lambda l, j, tab: (l, j, 0)))
    return pl.pallas_call(
        _pack_body,
        out_shape=jax.ShapeDtypeStruct((depth, lay["total"], d), BF16),
        grid_spec=grid_spec,
        compiler_params=_cparams(("parallel", "arbitrary"), 14 * LANE * d * 4),
        name="pack_in_proj_weight",
    )(jnp.asarray(table, I32), w_t, w_t, w_t, w_t)


def _rmsnorm_body(x_ref, g_ref, o_ref):
    x = x_ref[...]
    ms = jnp.mean(x * x, axis=-1, keepdims=True)
    o_ref[...] = (x * lax.rsqrt(ms + RMS_EPS) * g_ref[...]).astype(o_ref.dtype)


def _rmsnorm(x, g, out_dtype, rows=None):
    m, d = (x.shape[0] if rows is None else rows), x.shape[1]
    tm = _pick(m, 256)
    return pl.pallas_call(
        _rmsnorm_body,
        out_shape=jax.ShapeDtypeStruct((m, d), out_dtype),
        grid=(m // tm,),
        in_specs=[pl.BlockSpec((tm, d), lambda i: (i, 0)), pl.BlockSpec((1, d), lambda i: (0, 0))],
        out_specs=pl.BlockSpec((tm, d), lambda i: (i, 0)),
        compiler_params=_cparams(("parallel",), 6 * tm * d * 4),
        name="rmsnorm",
    )(x, g.reshape(1, d))


def _mm_body(a_ref, w_ref, o_ref):
    o_ref[...] = jnp.dot(a_ref[...], w_ref[...], preferred_element_type=F32).astype(o_ref.dtype)


def _mm_res_body(a_ref, w_ref, r_ref, o_ref):
    o_ref[...] = (r_ref[...] + jnp.dot(a_ref[...], w_ref[...], preferred_element_type=F32)).astype(o_ref.dtype)


def _mm_wt_body(a_ref, wt_ref, o_ref):
    o_ref[...] = lax.dot_general(a_ref[...], wt_ref[...], (((1,), (1,)), ((), ())),
                                 preferred_element_type=F32).astype(o_ref.dtype)


def _layer_spec(block, index_map, layer):
    if layer is None:
        return pl.BlockSpec(block, index_map)
    return pl.BlockSpec((None,) + block, lambda *g: (layer,) + index_map(*g))


def _matmul(a, w, *, layer=None, w_transposed=False, residual=None, out_dtype=F32, tm_pref=1024, tn_pref=1024,
            a_single_buffer=False, name="matmul"):
    m, k = a.shape
    n = w.shape[-2] if w_transposed else w.shape[-1]
    tm = _row_tile(m, tm_pref)
    tn = tn_pref if n % tn_pref == 0 else _pick(n, tn_pref)
    osz = jnp.dtype(out_dtype).itemsize
    vmem = (1 if a_single_buffer else 2) * tm * k * 2 + 2 * (k * tn * 2 + tm * tn * osz) + tm * tn * 4
    w_spec = (_layer_spec((tn, k), lambda i, j: (j, 0), layer) if w_transposed
              else _layer_spec((k, tn), lambda i, j: (0, j), layer))
    a_mode = dict(pipeline_mode=pl.Buffered(1)) if a_single_buffer else {}
    in_specs = [pl.BlockSpec((tm, k), lambda i, j: (i, 0), **a_mode), w_spec]
    args = [a, w]
    body = _mm_wt_body if w_transposed else _mm_body
    assert residual is None or not w_transposed
    if residual is not None:
        in_specs.append(pl.BlockSpec((tm, tn), lambda i, j: (i, j)))
        args.append(residual)
        body = _mm_res_body
        vmem += 2 * tm * tn * 4
    return pl.pallas_call(
        body,
        out_shape=jax.ShapeDtypeStruct((m, n), out_dtype),
        grid=(m // tm, n // tn),
        in_specs=in_specs,
        out_specs=pl.BlockSpec((tm, tn), lambda i, j: (i, j)),
        compiler_params=_cparams(("parallel", "arbitrary"), vmem + COMPILER_SCRATCH_BYTES),
        name=name,
    )(*args)


def _branch_a_body(u_ref, v_ref, w_ref, bs_ref, g_ref, b_ref, y_ref, *av_ref, chunk):
    tb = u_ref.shape[0]
    u = jax.nn.gelu(u_ref[...])
    v = jax.nn.gelu(v_ref[...])
    mu = jnp.mean(v, axis=-1, keepdims=True)
    var = jnp.mean(jnp.square(v - mu), axis=-1, keepdims=True)
    vn = (v - mu) * lax.rsqrt(var + LN_EPS) * g_ref[...] + b_ref[...]
    if av_ref:
        av_ref[0][...] = vn
    row = lax.broadcasted_iota(I32, (tb, tb), 0)
    col = lax.broadcasted_iota(I32, (tb, tb), 1)
    keep = (col <= row) & ((row // chunk) == (col // chunk))
    vb = vn.astype(BF16)
    for g in range(A_GROUPS):
        sl = slice(g * HEAD_DIM, (g + 1) * HEAD_DIM)
        wg = jnp.where(keep, w_ref[g], 0.0).astype(BF16)
        s = jnp.dot(wg, vb[:, sl], preferred_element_type=F32) + bs_ref[:, g:g + 1]
        y_ref[:, sl] = (u[:, sl] * s).astype(y_ref.dtype)


def _branch_a(h, lay, a_ws, a_bs, ln_g, ln_b, *, chunk, tb, emit_av, rows=None):
    m = h.shape[0]
    rows = m if rows is None else rows
    reps = tb // chunk
    wfull = jnp.tile(a_ws[:, :chunk, :chunk], (1, reps, reps))
    bs_t = jnp.tile(a_bs[:, :chunk].T, (reps, 1))
    cu, cv = lay["au"] // A_WIDTH, lay["av"] // A_WIDTH
    out_shape = [jax.ShapeDtypeStruct((m, A_WIDTH), BF16)]
    out_specs = [pl.BlockSpec((tb, A_WIDTH), lambda i: (i, 0))]
    if emit_av:
        out_shape.append(jax.ShapeDtypeStruct((m, A_WIDTH), F32))
        out_specs.append(pl.BlockSpec((tb, A_WIDTH), lambda i: (i, 0)))
    res = pl.pallas_call(
        functools.partial(_branch_a_body, chunk=chunk),
        out_shape=out_shape,
        grid=(rows // tb,),
        in_specs=[pl.BlockSpec((tb, A_WIDTH), lambda i: (i, cu)),
                  pl.BlockSpec((tb, A_WIDTH), lambda i: (i, cv)),
                  pl.BlockSpec((A_GROUPS, tb, tb), lambda i: (0, 0, 0)),
                  pl.BlockSpec((tb, A_GROUPS), lambda i: (0, 0)),
                  pl.BlockSpec((1, A_WIDTH), lambda i: (0, 0)),
                  pl.BlockSpec((1, A_WIDTH), lambda i: (0, 0))],
        out_specs=out_specs,
        compiler_params=_cparams(("parallel",), 32 << 20),
        name="branch_a_gmlp",
    )(h, h, wfull, bs_t, ln_g.reshape(1, A_WIDTH), ln_b.reshape(1, A_WIDTH))
    return res if emit_av else res[0]


def _bdot(a, b):
    return lax.dot_general(a.astype(BF16), b.astype(BF16), (((2,), (1,)), ((0,), (0,))),
                           preferred_element_type=F32)


def _bdot_nt(a, b):
    return lax.dot_general(a.astype(BF16), b.astype(BF16), (((2,), (2,)), ((0,), (0,))),
                           preferred_element_type=F32)


def _transpose_rows(x):
    r = x.shape[0]
    if r < LANE:
        x = jnp.concatenate([x, jnp.zeros((LANE - r, LANE), x.dtype)], axis=0)
    return x.T[:, :r]


def _gdn_body(q_ref, k_ref, v_ref, z_ref, sm_ref, c0_ref, s0_ref, cw_ref, al_ref, dt_ref, og_ref,
              y_ref, sout_ref, ext_ref, s_ref, *, chunk, t_valid):
    c = pl.program_id(1)
    nc = pl.num_programs(1)
    hd = HEAD_DIM

    @pl.when(c == 0)
    def _():
        s_ref[...] = s0_ref[0]
        for j in range(3):
            ext_ref[j, 0:SUBLANE, :] = c0_ref[0, :, j * B_WIDTH:(j + 1) * B_WIDTH]

    acts = []
    for j, ref in enumerate((q_ref, k_ref, v_ref)):
        ext_ref[j, SUBLANE:SUBLANE + chunk, :] = ref[...]
        acc = None
        for i in range(CONV_W):
            lo = SUBLANE - (CONV_W - 1) + i
            term = ext_ref[j, lo:lo + chunk, :] * cw_ref[i:i + 1, j * B_WIDTH:(j + 1) * B_WIDTH]
            acc = term if acc is None else acc + term
        acts.append(jax.nn.silu(acc))
        ext_ref[j, 0:SUBLANE, :] = ext_ref[j, chunk:chunk + SUBLANE, :]
    qa, ka, va = acts

    sm = sm_ref[...]
    row1 = lax.broadcasted_iota(I32, (chunk, LANE), 0)
    g_all = -jnp.exp(al_ref[...]) * jax.nn.softplus(sm + dt_ref[...])
    beta_all = jax.nn.sigmoid(sm)
    if t_valid < chunk:
        g_all = jnp.where(row1 < t_valid, g_all, 0.0)
        beta_all = jnp.where(row1 < t_valid, beta_all, 0.0)
    gc_all = g_all
    d = 1
    while d < chunk:
        gc_all = gc_all + jnp.where(row1 >= d, pltpu.roll(gc_all, d, 0), 0.0)
        d *= 2
    gc_t = _transpose_rows(gc_all)

    row = lax.broadcasted_iota(I32, (chunk, chunk), 0)
    col = lax.broadcasted_iota(I32, (chunk, chunk), 1)
    incl = row >= col
    strict = row > col
    eye = jnp.where(row == col, 1.0, 0.0)
    pair_masks = []
    bs = 1
    while bs < chunk:
        pair_masks.append(((row // bs) % 2 == 1) & ((col // bs) == (row // bs) - 1))
        bs *= 2

    heads = range(B_HEADS)
    per_head = lambda x: jnp.stack([x[:, h * hd:(h + 1) * hd] for h in heads], axis=0)
    q3, k3, v3 = per_head(qa), per_head(ka), per_head(va)
    qn = q3 * lax.rsqrt(jnp.sum(q3 * q3, axis=-1, keepdims=True) + 1e-6) * (hd ** -0.5)
    kn = k3 * lax.rsqrt(jnp.sum(k3 * k3, axis=-1, keepdims=True) + 1e-6)
    beta = jnp.stack([beta_all[:, B_HEADS + h:B_HEADS + h + 1] for h in heads], axis=0)
    gcol = jnp.stack([gc_all[:, h:h + 1] for h in heads], axis=0)
    grow = jnp.stack([gc_t[h:h + 1, :] for h in heads], axis=0)
    decay = jnp.exp(jnp.where(incl[None], gcol - grow, -jnp.inf))
    kb = kn * beta
    eg = jnp.exp(gcol)
    lmat = jnp.where(strict[None], _bdot_nt(kb, kn) * decay, 0.0)
    attn = _bdot_nt(qn, kn) * decay
    tinv = eye[None] - jnp.where(pair_masks[0][None], lmat, 0.0)
    for pm in pair_masks[1:]:
        tinv = tinv - _bdot(tinv, _bdot(jnp.where(pm[None], lmat, 0.0), tinv))
    sol = _bdot(tinv, jnp.concatenate([v3 * beta, kb * eg], axis=-1))
    value, kcd = sol[:, :, :hd], sol[:, :, hd:]
    s_old = s_ref[...]
    vnew = value - _bdot(kcd, s_old)
    o = _bdot(qn * eg, s_old) + _bdot(attn, vnew)
    glast = gcol[:, chunk - 1:chunk, :]
    kend = kn * jnp.exp(glast - gcol)
    s_ref[...] = s_old * jnp.exp(glast) + _bdot(jnp.swapaxes(kend, 1, 2), vnew)
    on = o * lax.rsqrt(jnp.mean(o * o, axis=-1, keepdims=True) + RMS_EPS) * og_ref[...]
    for h in heads:
        sl = slice(h * hd, (h + 1) * hd)
        y_ref[:, sl] = (on[h] * jax.nn.silu(z_ref[:, sl])).astype(y_ref.dtype)

    @pl.when(c == nc - 1)
    def _():
        sout_ref[0] = s_ref[...]


def _gdn(srcs, conv0, s0, conv_w, a_log, dt_bias, o_g, *, n_b, t_pad, t_valid, chunk, out_rows=None):
    nc = t_pad // chunk
    arrs = [a for a, _ in srcs]
    cbs = [cb for _, cb in srcs]
    widths = [B_WIDTH] * 4 + [LANE]

    def tok_spec(w, cb):
        return pl.BlockSpec((chunk, w), lambda b, c: (b * nc + c, cb))

    pad12 = lambda x: jnp.zeros((1, LANE), F32).at[0, :B_HEADS].set(x)
    return pl.pallas_call(
        functools.partial(_gdn_body, chunk=chunk, t_valid=t_valid),
        out_shape=[jax.ShapeDtypeStruct((out_rows or n_b * t_pad, B_WIDTH), BF16),
                   jax.ShapeDtypeStruct((n_b, B_HEADS, HEAD_DIM, HEAD_DIM), F32)],
        grid=(n_b, nc),
        in_specs=[tok_spec(w, cb) for w, cb in zip(widths, cbs)] + [
            pl.BlockSpec((1, SUBLANE, 3 * B_WIDTH), lambda b, c: (b, 0, 0)),
            pl.BlockSpec((1, B_HEADS, HEAD_DIM, HEAD_DIM), lambda b, c: (b, 0, 0, 0)),
            pl.BlockSpec((CONV_W, 3 * B_WIDTH), lambda b, c: (0, 0)),
            pl.BlockSpec((1, LANE), lambda b, c: (0, 0)),
            pl.BlockSpec((1, LANE), lambda b, c: (0, 0)),
            pl.BlockSpec((1, HEAD_DIM), lambda b, c: (0, 0))],
        out_specs=[pl.BlockSpec((chunk, B_WIDTH), lambda b, c: (b * nc + c, 0)),
                   pl.BlockSpec((1, B_HEADS, HEAD_DIM, HEAD_DIM), lambda b, c: (b, 0, 0, 0))],
        scratch_shapes=[pltpu.VMEM((3, chunk + SUBLANE, B_WIDTH), F32),
                        pltpu.VMEM((B_HEADS, HEAD_DIM, HEAD_DIM), F32)],
        compiler_params=_cparams(("parallel", "arbitrary"), 40 << 20),
        name="branch_b_gated_delta",
    )(*arrs, conv0, s0, conv_w, pad12(a_log), pad12(dt_bias), o_g.reshape(1, HEAD_DIM))


def _sort_key(x):
    b = lax.bitcast_convert_type(x + 0.0, I32)
    return b ^ ((b >> 31) & 0x7FFFFFFF)


def _select_rule(count, shape, k, nbits):
    kf = jnp.float32(k)
    count_ge = lambda cand: count(lambda key, pos: key >= cand)
    t0 = jnp.where(count_ge(jnp.zeros(shape, I32)) >= kf, 0, INT_MIN).astype(I32)

    def bit_step(i, t):
        cand = t + lax.shift_left(jnp.int32(1), 30 - i)
        return jnp.where(count_ge(cand) >= kf, cand, t)

    t = lax.fori_loop(0, 31, bit_step, t0)
    n_ge = count_ge(t)
    tied = (n_ge > kf) & (t > KEY_OF_NEG_INF)

    def tie_index():
        r = kf - count(lambda key, pos: key > t)

        def idx_step(i, j):
            cand = j + lax.shift_left(jnp.int32(1), nbits - 1 - i)
            below = count(lambda key, pos: (key == t) & (pos <= cand - 1))
            return jnp.where(below < r, cand, j)

        return lax.fori_loop(0, nbits, idx_step, jnp.zeros(shape, I32))

    no_limit = jnp.full(shape, 2 ** 30, I32)
    j = lax.cond(jnp.sum(jnp.where(tied, 1.0, 0.0)) > 0.0,
                 lambda: jnp.where(tied, tie_index(), no_limit), lambda: no_limit)
    return t, j


def _selected(key, pos, t, j):
    return (key > t) | ((key == t) & (pos <= j))


def _dsa_prompt_body(cq_ref, iq0_ref, iq1_ref, iq2_ref, iq3_ref, sm_ref, ik_ref, ck_ref, cv_ref, y_ref,
                     key_ref, iqb_ref, wb_ref, qb_ref, m_ref, l_ref, acc_ref, p_ref, al_ref,
                     *, k_sel, nbits, tq, tk):
    rows3 = C_GROUP * tq
    qi = pl.program_id(1)
    kpg = tk // LANE
    nkb = (qi + 1) * (tq // LANE)
    ngrp = ((qi + 1) * tq + tk - 1) // tk
    hd = HEAD_DIM
    nt = (((1,), (1,)), ((), ()))
    sub = LANE // SUBLANE
    qpos3 = qi * tq + lax.broadcasted_iota(I32, (sub, SUBLANE, tq), 2)
    kofs3 = lax.broadcasted_iota(I32, (sub, SUBLANE, tq), 0) * SUBLANE + lax.broadcasted_iota(I32, (sub, SUBLANE, tq), 1)

    w_all = sm_ref[...] * (IDX_HEADS ** -0.5 * IDX_DIM ** -0.5)
    w_t = jnp.concatenate([w_all[r:r + LANE].T for r in range(0, tq, LANE)], axis=1)
    w_off = 2 * B_HEADS
    for hh in range(IDX_HEADS):
        ref = (iq0_ref, iq1_ref, iq2_ref, iq3_ref)[hh // 4]
        iqb_ref[hh] = ref[:, (hh % 4) * IDX_DIM:(hh % 4 + 1) * IDX_DIM].astype(BF16)
        wb_ref[hh] = jnp.broadcast_to(w_t[w_off + hh:w_off + hh + 1, :], (SUBLANE, tq))

    def score_grp(g, carry):
        ikg = ik_ref[pl.ds(pl.multiple_of(g * tk, tk), tk), :].astype(BF16)
        acc = jnp.zeros((tk // SUBLANE, SUBLANE, tq), F32)
        for hh in range(IDX_HEADS):
            s = lax.dot_general(ikg, iqb_ref[hh], nt, preferred_element_type=F32)
            acc = acc + wb_ref[hh] * jnp.maximum(s, 0.0).reshape(tk // SUBLANE, SUBLANE, tq)
        for t in range(kpg):
            kpos3 = (g * kpg + t) * LANE + kofs3
            blk = acc[t * sub:(t + 1) * sub]
            key_ref[g * kpg + t] = _sort_key(jnp.where(kpos3 <= qpos3, blk, -jnp.inf))
        return carry

    lax.fori_loop(0, ngrp, score_grp, 0)

    bpt = tq // LANE

    def count(pred):
        def trip(i, acc):
            for u in range(bpt):
                b = i * bpt + u
                acc = acc + jnp.sum(jnp.where(pred(key_ref[b], b * LANE + kofs3), 1.0, 0.0), axis=0)
            return acc
        acc = lax.fori_loop(0, qi + 1, trip, jnp.zeros((SUBLANE, tq), F32))
        return jnp.broadcast_to(jnp.sum(acc, axis=0, keepdims=True), (SUBLANE, tq))

    t8, j8 = _select_rule(count, (SUBLANE, tq), k_sel, nbits)

    for kvh in range(C_KV_HEADS):
        for g in range(C_GROUP):
            hsl = slice((kvh * C_GROUP + g) * hd, (kvh * C_GROUP + g + 1) * hd)
            qb_ref[kvh, g * tq:(g + 1) * tq, :] = cq_ref[:, hsl].astype(BF16)
    m_ref[...] = jnp.full(m_ref.shape, MASK_NEG, F32)
    l_ref[...] = jnp.zeros(l_ref.shape, F32)
    acc_ref[...] = jnp.zeros(acc_ref.shape, F32)
    ones = jnp.ones((tk, hd), BF16)

    def attend_grp(g, carry):
        bias_t = []
        for t in range(kpg):
            kpos3 = (g * kpg + t) * LANE + kofs3
            sel = _selected(key_ref[g * kpg + t], kpos3, t8, j8) & (kpos3 <= qpos3)
            kq = jnp.where(sel, 0.0, MASK_NEG).reshape(LANE, tq)
            bias_t.append(jnp.concatenate([kq[:, r:r + LANE].T for r in range(0, tq, LANE)], axis=0))
        bias = jnp.concatenate(bias_t, axis=1)
        start = pl.multiple_of(g * tk, tk)
        kgrp = ck_ref[pl.ds(start, tk), :].astype(BF16)
        vgrp = cv_ref[pl.ds(start, tk), :].astype(BF16)
        for kvh in range(C_KV_HEADS):
            sl = slice(kvh * hd, (kvh + 1) * hd)
            s = lax.dot_general(qb_ref[kvh], kgrp[:, sl], nt, preferred_element_type=F32)
            for r0 in range(0, rows3, DSA_ROW_TILE):
                rs = slice(r0, r0 + DSA_ROW_TILE)
                x = s[rs] * (hd ** -0.5 * LOG2E) + bias[r0 % tq:r0 % tq + DSA_ROW_TILE]
                m_old = m_ref[kvh, rs, :]
                m_new = jnp.maximum(m_old, jnp.max(x, axis=-1, keepdims=True))
                p_ref[rs, :] = jnp.exp2(x - m_new).astype(BF16)
                al_ref[rs, :] = jnp.exp2(m_old - m_new)
                m_ref[kvh, rs, :] = m_new
            pv = jnp.dot(p_ref[...], jnp.concatenate([vgrp[:, sl], ones], axis=1), preferred_element_type=F32)
            alpha = al_ref[...]
            l_ref[kvh] = alpha * l_ref[kvh] + pv[:, hd:hd + 1]
            acc_ref[kvh] = alpha * acc_ref[kvh] + pv[:, :hd]
        return carry

    lax.fori_loop(0, ngrp, attend_grp, 0)

    for kvh in range(C_KV_HEADS):
        o = acc_ref[kvh] / l_ref[kvh]
        for g in range(C_GROUP):
            hsl = slice((kvh * C_GROUP + g) * hd, (kvh * C_GROUP + g + 1) * hd)
            y_ref[:, hsl] = o[g * tq:(g + 1) * tq, :].astype(y_ref.dtype)


def _dsa_prompt(h, lay, *, n_b, seq, out_rows=None):
    tq, tk = DSA_TQ, DSA_TK
    assert seq % tq == 0 and seq % tk == 0
    nqb = seq // tq
    k_sel = min(TOPK_MAX, seq // 4)
    nbits = max(1, (seq - 1).bit_length())
    iq_cb = lay["iq"] // (4 * IDX_DIM)
    once = pl.Buffered(1)

    def q_spec(w, cb):
        return pl.BlockSpec((tq, w), lambda b, q: (b * nqb + q, cb))

    def kv_spec(w, cb):
        return pl.BlockSpec((seq, w), lambda b, q: (b, cb), pipeline_mode=once)

    rows3 = C_GROUP * tq
    lane_pad = lambda r: r * LANE * 4
    vmem = (seq * (2 * C_KV + IDX_DIM) * 4 + 2 * tq * (C_Q + IDX_HEADS * IDX_DIM + LANE) * 4 + 2 * tq * C_Q * 2
            + (seq // LANE) * tq * LANE * 4 + IDX_HEADS * tq * LANE * 6 + C_KV_HEADS * rows3 * HEAD_DIM * 6
            + 2 * C_KV_HEADS * lane_pad(rows3) + 4 * rows3 * tk * 4)
    return pl.pallas_call(
        functools.partial(_dsa_prompt_body, k_sel=k_sel, nbits=nbits, tq=tq, tk=tk),
        out_shape=jax.ShapeDtypeStruct((out_rows or n_b * seq, C_Q), BF16),
        grid=(n_b, nqb),
        in_specs=[q_spec(C_Q, lay["cq"] // C_Q)]
                 + [q_spec(4 * IDX_DIM, iq_cb + i) for i in range(4)]
                 + [q_spec(LANE, lay["small"] // LANE),
                    kv_spec(IDX_DIM, lay["ik"] // IDX_DIM),
                    kv_spec(C_KV, lay["ck"] // C_KV),
                    kv_spec(C_KV, lay["cv"] // C_KV)],
        out_specs=pl.BlockSpec((tq, C_Q), lambda b, q: (b * nqb + q, 0)),
        scratch_shapes=[pltpu.VMEM((seq // LANE, LANE // SUBLANE, SUBLANE, tq), I32),
                        pltpu.VMEM((IDX_HEADS, tq, IDX_DIM), BF16),
                        pltpu.VMEM((IDX_HEADS, SUBLANE, tq), F32),
                        pltpu.VMEM((C_KV_HEADS, rows3, HEAD_DIM), BF16),
                        pltpu.VMEM((C_KV_HEADS, rows3, 1), F32),
                        pltpu.VMEM((C_KV_HEADS, rows3, 1), F32),
                        pltpu.VMEM((C_KV_HEADS, rows3, HEAD_DIM), F32),
                        pltpu.VMEM((rows3, tk), BF16),
                        pltpu.VMEM((rows3, 1), F32)],
        compiler_params=_cparams(("parallel", "arbitrary"), vmem + COMPILER_SCRATCH_BYTES),
        name="branch_c_prompt_dsa",
    )(h, h, h, h, h, h, h, h, h)


def _dsa_sample_select_body(pt_ref, *refs, n_pages, k_sel, nbits, t_new):
    pp = SELECT_PAGES_PER_STEP
    page_refs = refs[:pp]
    iq_ref, w_ref, ikn_ref, bias_ref, key_ref = refs[pp:]
    j = pl.program_id(1)
    n_steps = n_pages // pp
    rows = t_new
    col = lax.broadcasted_iota(I32, (rows, LANE), 1)
    trow = lax.broadcasted_iota(I32, (rows, LANE), 0)
    iq = iq_ref[0].astype(BF16)

    def scores(keys_f32):
        n = keys_f32.shape[0]
        s = lax.dot_general(iq, keys_f32.astype(BF16), (((1,), (1,)), ((), ())), preferred_element_type=F32)
        r = jnp.maximum(s, 0.0) * w_ref[0]
        return jnp.sum(r.reshape(rows, IDX_HEADS, n), axis=1)

    @pl.when(j < n_steps)
    def _():
        keys = _sort_key(scores(jnp.concatenate([r[0, 0] for r in page_refs], axis=0)))
        for i in range(pp):
            key_ref[j * pp + i] = keys[:, i * LANE:(i + 1) * LANE]

    @pl.when(j == n_steps)
    def _():
        new_ok = (col <= trow) & (col < t_new)
        key_ref[n_pages] = _sort_key(jnp.where(new_ok, scores(ikn_ref[0]), -jnp.inf))
        nblk = n_pages + 1
        grp = SUBLANE
        nblk_pad = key_ref.shape[0]
        if nblk_pad > nblk:
            key_ref[nblk:nblk_pad] = jnp.full((nblk_pad - nblk, rows, LANE), KEY_OF_NEG_INF, I32)
        blk3 = lax.broadcasted_iota(I32, (grp, rows, LANE), 0)
        col3 = lax.broadcasted_iota(I32, (grp, rows, LANE), 2)

        def count(pred):
            def trip(i, acc):
                b0 = pl.multiple_of(i * grp, grp)
                hit = pred(key_ref[pl.ds(b0, grp)], (b0 + blk3) * LANE + col3)
                return acc + jnp.sum(jnp.where(hit, 1.0, 0.0), axis=0)
            acc = lax.fori_loop(0, nblk_pad // grp, trip, jnp.zeros((rows, LANE), F32))
            return jnp.broadcast_to(jnp.sum(acc, axis=1, keepdims=True), (rows, LANE))

        tb, jb = _select_rule(count, (rows, LANE), k_sel, nbits)

        def write_blk(b, carry):
            kpos = b * LANE + col
            sel = _selected(key_ref[b], kpos, tb, jb) & ((kpos < n_pages * PAGE_SIZE) | new_ok)
            bias_ref[0, b] = jnp.where(sel, 0.0, MASK_NEG)
            return carry

        lax.fori_loop(0, nblk, write_blk, 0)


def _page_index_map(layer, i, n_pages, n_steps, trailing, pp):
    def index_map(b, j, pt):
        step = jnp.minimum(j, n_steps - 1)
        return (layer, pt[b * n_pages + step * pp + i]) + (0,) * trailing
    return index_map


def _dsa_sample_select(pt_flat, kidx_pool, layer, iq_s, iw_s, ik_new, *, n_b, n_pages, t_new):
    pp = SELECT_PAGES_PER_STEP
    assert n_pages % pp == 0
    n_steps = n_pages // pp
    total = n_pages * PAGE_SIZE + t_new
    k_sel = min(TOPK_MAX, total // 4)
    nbits = max(1, ((n_pages + 1) * PAGE_SIZE - 1).bit_length())
    rows_q = t_new * IDX_HEADS
    grid_spec = pltpu.PrefetchScalarGridSpec(
        num_scalar_prefetch=1,
        grid=(n_b, n_steps + 1),
        in_specs=[pl.BlockSpec((1, 1, PAGE_SIZE, IDX_DIM), _page_index_map(layer, i, n_pages, n_steps, 2, pp))
                  for i in range(pp)]
                 + [pl.BlockSpec((1, rows_q, IDX_DIM), lambda b, j, pt: (b, 0, 0)),
                    pl.BlockSpec((1, rows_q, 1), lambda b, j, pt: (b, 0, 0)),
                    pl.BlockSpec((1, PAGE_SIZE, IDX_DIM), lambda b, j, pt: (b, 0, 0))],
        out_specs=pl.BlockSpec((1, n_pages + 1, t_new, LANE), lambda b, j, pt: (b, 0, 0, 0)),
        scratch_shapes=[pltpu.VMEM((-(-(n_pages + 1) // SUBLANE) * SUBLANE, t_new, LANE), I32)])
    return pl.pallas_call(
        functools.partial(_dsa_sample_select_body, n_pages=n_pages, k_sel=k_sel, nbits=nbits, t_new=t_new),
        out_shape=jax.ShapeDtypeStruct((n_b, n_pages + 1, t_new, LANE), F32),
        grid_spec=grid_spec,
        compiler_params=_cparams(("parallel", "arbitrary"), 24 << 20),
        name="branch_c_decode_select",
    )(pt_flat, *([kidx_pool] * pp), iq_s, iw_s, ik_new)


def _dsa_sample_attend_body(pt_ref, *refs, n_pages, t_new):
    pp = PAGES_PER_STEP
    k_refs, v_refs = refs[:pp], refs[pp:2 * pp]
    q_ref, bias_ref, kn_ref, vn_ref, o_ref, m_ref, l_ref, acc_ref = refs[2 * pp:]
    j = pl.program_id(1)
    n_steps = n_pages // pp
    hd = HEAD_DIM
    rq = q_ref.shape[1] // C_KV_HEADS
    reps = rq // t_new

    @pl.when(j == 0)
    def _():
        m_ref[...] = jnp.full(m_ref.shape, MASK_NEG, F32)
        l_ref[...] = jnp.zeros(l_ref.shape, F32)
        acc_ref[...] = jnp.zeros(acc_ref.shape, F32)

    def attend(k_of, v_of, bias_t):
        bias = jnp.concatenate([bias_t] * reps, axis=0)
        for kvh in range(C_KV_HEADS):
            rs = slice(kvh * rq, (kvh + 1) * rq)
            s = lax.dot_general(q_ref[0, rs, :].astype(BF16), k_of(kvh).astype(BF16), (((1,), (1,)), ((), ())),
                                preferred_element_type=F32) * (hd ** -0.5) + bias
            m_old = m_ref[rs, :]
            m_new = jnp.maximum(m_old, jnp.max(s, axis=-1, keepdims=True))
            alpha = jnp.exp(m_old - m_new)
            p = jnp.exp(s - m_new)
            l_ref[rs, :] = alpha * l_ref[rs, :] + jnp.sum(p, axis=-1, keepdims=True)
            acc_ref[rs, :] = alpha * acc_ref[rs, :] + jnp.dot(p.astype(BF16), v_of(kvh).astype(BF16),
                                                              preferred_element_type=F32)
            m_ref[rs, :] = m_new

    @pl.when(j < n_steps)
    def _():
        head_rows = lambda kvh: pl.ds(kvh, PAGE_SIZE, stride=C_KV_HEADS)
        pages = lambda prefs: (lambda kvh: jnp.concatenate([r[0, 0, head_rows(kvh), :] for r in prefs], axis=0))
        bias_t = jnp.concatenate([bias_ref[0, j * pp + i] for i in range(pp)], axis=1)
        attend(pages(k_refs), pages(v_refs), bias_t)

    @pl.when(j == n_steps)
    def _():
        new = lambda ref: (lambda kvh: ref[0, :, kvh * hd:(kvh + 1) * hd])
        attend(new(kn_ref), new(vn_ref), bias_ref[0, n_pages])
        o_ref[0] = acc_ref[...] / l_ref[...]


def _dsa_sample_attend(pt_flat, k_pool, v_pool, layer, q_s, bias, k_new, v_new, *, n_b, n_pages, t_new):
    pp = PAGES_PER_STEP
    n_steps = n_pages // pp
    rows = q_s.shape[1]
    page_specs = [pl.BlockSpec((1, 1, PAGE_SIZE * C_KV_HEADS, HEAD_DIM),
                               _page_index_map(layer, i, n_pages, n_steps, 2, pp)) for i in range(pp)]
    grid_spec = pltpu.PrefetchScalarGridSpec(
        num_scalar_prefetch=1,
        grid=(n_b, n_steps + 1),
        in_specs=page_specs + page_specs
                 + [pl.BlockSpec((1, rows, HEAD_DIM), lambda b, j, pt: (b, 0, 0)),
                    pl.BlockSpec((1, n_pages + 1, t_new, LANE), lambda b, j, pt: (b, 0, 0, 0)),
                    pl.BlockSpec((1, PAGE_SIZE, C_KV), lambda b, j, pt: (b, 0, 0)),
                    pl.BlockSpec((1, PAGE_SIZE, C_KV), lambda b, j, pt: (b, 0, 0))],
        out_specs=pl.BlockSpec((1, rows, HEAD_DIM), lambda b, j, pt: (b, 0, 0)),
        scratch_shapes=[pltpu.VMEM((rows, 1), F32), pltpu.VMEM((rows, 1), F32), pltpu.VMEM((rows, HEAD_DIM), F32)])
    return pl.pallas_call(
        functools.partial(_dsa_sample_attend_body, n_pages=n_pages, t_new=t_new),
        out_shape=jax.ShapeDtypeStruct((n_b, rows, HEAD_DIM), F32),
        grid_spec=grid_spec,
        compiler_params=_cparams(("parallel", "arbitrary"), 32 << 20),
        name="branch_c_decode_attend",
    )(pt_flat, *([k_pool] * pp), *([v_pool] * pp), q_s, bias, k_new, v_new)


def _dsa_sample(h_s, lay, pt_flat, k_pool, v_pool, kidx_pool, layer, *, n_b, t_new, n_pages):
    hd = HEAD_DIM
    seg = lambda name, w: h_s[:, lay[name]:lay[name] + w]
    rq = -(-C_GROUP * t_new // 16) * 16
    q = seg("cq", C_Q).reshape(n_b, t_new, C_KV_HEADS, C_GROUP, hd).transpose(0, 2, 3, 1, 4)
    q = q.reshape(n_b, C_KV_HEADS, C_GROUP * t_new, hd)
    q_s = jnp.pad(q, ((0, 0), (0, 0), (0, rq - C_GROUP * t_new), (0, 0))).reshape(n_b, C_KV_HEADS * rq, hd)
    iq_s = seg("iq", IDX_HEADS * IDX_DIM).reshape(n_b, t_new * IDX_HEADS, IDX_DIM)
    iw = h_s[:, lay["small"] + 2 * B_HEADS:lay["small"] + 2 * B_HEADS + IDX_HEADS]
    iw_s = (iw * (IDX_HEADS ** -0.5 * IDX_DIM ** -0.5)).reshape(n_b, t_new * IDX_HEADS, 1)
    padrows = lambda x: jnp.pad(x.reshape(n_b, t_new, -1), ((0, 0), (0, PAGE_SIZE - t_new), (0, 0)))
    ik_new, k_new, v_new = padrows(seg("ik", IDX_DIM)), padrows(seg("ck", C_KV)), padrows(seg("cv", C_KV))
    bias = _dsa_sample_select(pt_flat, kidx_pool, layer, iq_s, iw_s, ik_new, n_b=n_b, n_pages=n_pages, t_new=t_new)
    o = _dsa_sample_attend(pt_flat, k_pool, v_pool, layer, q_s, bias, k_new, v_new,
                           n_b=n_b, n_pages=n_pages, t_new=t_new)
    o = o.reshape(n_b, C_KV_HEADS, rq, hd)[:, :, :C_GROUP * t_new].reshape(n_b, C_KV_HEADS, C_GROUP, t_new, hd)
    return o.transpose(0, 3, 1, 2, 4).reshape(n_b * t_new, C_Q).astype(BF16)


def _merge_body(ya_ref, yb_ref, yc_ref, w_ref, g0_ref, g1_ref, g2_ref, o_ref):
    b0, c0 = A_WIDTH, A_WIDTH + B_WIDTH
    acc = jax.nn.sigmoid(g0_ref[...]) * jnp.dot(ya_ref[...], w_ref[0:b0, :], preferred_element_type=F32)
    acc = acc + jax.nn.sigmoid(g1_ref[...]) * jnp.dot(yb_ref[...], w_ref[b0:c0, :], preferred_element_type=F32)
    acc = acc + jax.nn.sigmoid(g2_ref[...]) * jnp.dot(yc_ref[...], w_ref[c0:, :], preferred_element_type=F32)
    o_ref[...] = acc.astype(o_ref.dtype)


def _merge(ya, yb, yc, w_br, layer, h, lay, d_model):
    m = ya.shape[0]
    tm = _row_tile(m, 1024)
    tn = _pick(d_model, 512)
    gcb = lay["gate"] // tn
    per = d_model // tn

    def gate_spec(i):
        return pl.BlockSpec((tm, tn), lambda r, c: (r, gcb + i * per + c))

    kw = A_WIDTH + B_WIDTH + C_Q
    vmem = 2 * (tm * kw * 2 + kw * tn * 2 + 3 * tm * tn * 4 + tm * tn * 2) + 4 * tm * tn * 4
    return pl.pallas_call(
        _merge_body,
        out_shape=jax.ShapeDtypeStruct((m, d_model), BF16),
        grid=(m // tm, d_model // tn),
        in_specs=[pl.BlockSpec((tm, A_WIDTH), lambda r, c: (r, 0)),
                  pl.BlockSpec((tm, B_WIDTH), lambda r, c: (r, 0)),
                  pl.BlockSpec((tm, C_Q), lambda r, c: (r, 0)),
                  _layer_spec((kw, tn), lambda r, c: (0, c), layer),
                  gate_spec(0), gate_spec(1), gate_spec(2)],
        out_specs=pl.BlockSpec((tm, tn), lambda r, c: (r, c)),
        compiler_params=_cparams(("parallel", "arbitrary"), vmem + COMPILER_SCRATCH_BYTES),
        name="gated_merge",
    )(ya, yb, yc, w_br, h, h, h)


def _ffn_up_body(hn_ref, w1_ref, w3_ref, o_ref):
    hn = hn_ref[...]
    a = jnp.dot(hn, w1_ref[...].astype(BF16), preferred_element_type=F32)
    b = jnp.dot(hn, w3_ref[...].astype(BF16), preferred_element_type=F32)
    o_ref[...] = (jax.nn.silu(a) * b).astype(o_ref.dtype)


def _ffn(hn, w1, w3, w2, layer, x):
    m, d = hn.shape
    d_ff = w1.shape[-1]
    tm = _row_tile(m, 1024)
    tf = FFN_TF
    assert d_ff % tf == 0
    vmem = 2 * (tm * d * 2 + 2 * d * tf * 4 + tm * tf * 2) + 2 * d * tf * 2 + 3 * tm * tf * 4
    act = pl.pallas_call(
        _ffn_up_body,
        out_shape=jax.ShapeDtypeStruct((m, d_ff), BF16),
        grid=(m // tm, d_ff // tf),
        in_specs=[pl.BlockSpec((tm, d), lambda i, f: (i, 0)),
                  _layer_spec((d, tf), lambda i, f: (0, f), layer),
                  _layer_spec((d, tf), lambda i, f: (0, f), layer)],
        out_specs=pl.BlockSpec((tm, tf), lambda i, f: (i, f)),
        compiler_params=_cparams(("parallel", "arbitrary"), vmem + COMPILER_SCRATCH_BYTES),
        name="swiglu_up",
    )(hn, w1, w3)
    return _matmul(act, w2, layer=layer, residual=x, tm_pref=1024, tn_pref=256, a_single_buffer=True,
                   name="swiglu_down")


def _dense_front(x, ln1, w_in_p, layer):
    xn = _rmsnorm(x, ln1, BF16)
    return _matmul(xn, w_in_p, layer=layer, w_transposed=True, tn_pref=768, name="in_proj")


def _dense_back(x, ya, yb, yc, h, lay, wts, layer, d_model):
    w_br, w_o, w1, w3, w2, ln2 = wts
    mix = _merge(ya, yb, yc, w_br, layer, h, lay, d_model)
    x = _matmul(mix, w_o, layer=layer, residual=x, tn_pref=512, name="out_proj")
    hn = _rmsnorm(x, ln2, BF16)
    return _ffn(hn, w1, w3, w2, layer, x)


def kernel(x_prompt, x_sample, cache_k, cache_v, cache_kidx, state_conv, state_delta, page_table, ln1, w_in,
           a_ln_g, a_ln_b, a_ws, a_bs, b_conv_w, b_a_log, b_dt_bias, b_out_g, w_br, w_o, ln2, ffn_w1, ffn_w3,
           ffn_w2, ln_f):
    n_bp, seq, d_model = x_prompt.shape
    n_bs, t_new, _ = x_sample.shape
    depth = ln1.shape[0]
    n_phys = cache_k.shape[1]
    n_pages = page_table.shape[1]
    assert seq % A_CHUNK == 0 and seq % GDN_CHUNK == 0 and CONV_W - 1 <= t_new <= min(A_CHUNK, GDN_DECODE_CHUNK)
    lay = _layout(d_model)
    hd = HEAD_DIM

    m_p, m_s = n_bp * seq, n_bs * t_new
    m_all = m_p + -(-m_s // LANE) * LANE
    x_all = jnp.concatenate([x_prompt.reshape(m_p, d_model), x_sample.reshape(m_s, d_model),
                             jnp.zeros((m_all - m_p - m_s, d_model), F32)], axis=0)
    pt_flat = page_table.reshape(-1).astype(I32)
    k_pool = cache_k.reshape(depth, n_phys, PAGE_SIZE * C_KV_HEADS, hd)
    v_pool = cache_v.reshape(depth, n_phys, PAGE_SIZE * C_KV_HEADS, hd)
    conv0_p = jnp.zeros((n_bp, SUBLANE, 3 * B_WIDTH), F32)
    s0_p = jnp.zeros((n_bp, B_HEADS, hd, hd), F32)
    qkv_off = lay["bq"]

    outs = {k: [] for k in ("pk", "pv", "pik", "pconv", "pdelta", "sk", "sv", "sik", "sconv", "sdelta", "schunk")}
    dense_w = (w_br.astype(BF16), w_o.astype(BF16), ffn_w1, ffn_w3, ffn_w2.astype(BF16))
    w_in_p = _pack_w_in(w_in, d_model)
    for l in range(depth):
        wts = dense_w + (ln2[l],)

        h = _dense_front(x_all, ln1[l], w_in_p, l)

        ya = _branch_a(h, lay, a_ws[l], a_bs[l], a_ln_g[l], a_ln_b[l], chunk=A_CHUNK, tb=A_CHUNK, emit_av=False,
                       rows=m_p)
        srcs = [(h, lay[n] // B_WIDTH) for n in ("bq", "bk", "bv", "bz")] + [(h, lay["small"] // LANE)]
        yb, s_new = _gdn(srcs, conv0_p, s0_p, b_conv_w[l], b_a_log[l], b_dt_bias[l], b_out_g[l],
                         n_b=n_bp, t_pad=seq, t_valid=GDN_CHUNK, chunk=GDN_CHUNK, out_rows=m_all)
        yc = _dsa_prompt(h, lay, n_b=n_bp, seq=seq, out_rows=m_all)
        seg_p = lambda off, w: h[:m_p, off:off + w]
        outs["pk"].append(seg_p(lay["ck"], C_KV).reshape(n_bp, seq, C_KV_HEADS, hd))
        outs["pv"].append(seg_p(lay["cv"], C_KV).reshape(n_bp, seq, C_KV_HEADS, hd))
        outs["pik"].append(seg_p(lay["ik"], IDX_DIM).reshape(n_bp, seq, IDX_DIM))
        outs["pconv"].append(jnp.stack([h[(b + 1) * seq - (CONV_W - 1):(b + 1) * seq, qkv_off:qkv_off + 3 * B_WIDTH]
                                        for b in range(n_bp)]))
        outs["pdelta"].append(s_new)
        ya_p, yb_p, yc_p = ya, yb, yc

        hs = h[m_p:m_p + m_s]
        ya, av = _branch_a(hs, lay, a_ws[l], a_bs[l], a_ln_g[l], a_ln_b[l], chunk=t_new, tb=n_bs * t_new,
                           emit_av=True)
        hs3 = hs.reshape(n_bs, t_new, lay["total"])
        dc = GDN_DECODE_CHUNK
        padt = lambda x: jnp.pad(x, ((0, 0), (0, dc - t_new), (0, 0))).reshape(n_bs * dc, -1)
        srcs = [(padt(hs3[:, :, lay[n]:lay[n] + B_WIDTH]), 0) for n in ("bq", "bk", "bv", "bz")]
        srcs.append((padt(hs3[:, :, lay["small"]:lay["small"] + LANE]), 0))
        conv0_s = jnp.pad(state_conv[l], ((0, 0), (SUBLANE - (CONV_W - 1), 0), (0, 0)))
        yb, s_new = _gdn(srcs, conv0_s, state_delta[l], b_conv_w[l], b_a_log[l], b_dt_bias[l], b_out_g[l],
                         n_b=n_bs, t_pad=dc, t_valid=t_new, chunk=dc)
        yb = yb.reshape(n_bs, dc, B_WIDTH)[:, :t_new].reshape(n_bs * t_new, B_WIDTH)
        yc = _dsa_sample(hs, lay, pt_flat, k_pool, v_pool, cache_kidx, l, n_b=n_bs, t_new=t_new, n_pages=n_pages)
        outs["sk"].append(hs3[:, :, lay["ck"]:lay["ck"] + C_KV].reshape(n_bs, t_new, C_KV_HEADS, hd))
        outs["sv"].append(hs3[:, :, lay["cv"]:lay["cv"] + C_KV].reshape(n_bs, t_new, C_KV_HEADS, hd))
        outs["sik"].append(hs3[:, :, lay["ik"]:lay["ik"] + IDX_DIM])
        outs["sconv"].append(hs3[:, t_new - (CONV_W - 1):, qkv_off:qkv_off + 3 * B_WIDTH])
        outs["sdelta"].append(s_new)
        outs["schunk"].append(av.reshape(n_bs, t_new, A_WIDTH))

        tail = lambda full, part: full.at[m_p:].set(jnp.pad(part, ((0, m_all - m_p - m_s), (0, 0))))
        x_all = _dense_back(x_all, tail(ya_p, ya), tail(yb_p, yb), tail(yc_p, yc), h, lay, wts, l, d_model)

    y_prompt = _rmsnorm(x_all, ln_f, F32, rows=m_p).reshape(n_bp, seq, d_model)
    y_sample = _rmsnorm(x_all[m_p:m_p + m_s], ln_f, F32).reshape(n_bs, t_new, d_model)
    st = lambda k: jnp.stack(outs[k])
    return (y_prompt, y_sample, st("pk"), st("pv"), st("pik"), st("pconv"), st("pdelta"),
            st("sk"), st("sv"), st("sik"), st("sconv"), st("sdelta"), st("schunk"))
```
